```python
import math, functools
import jax, jax.numpy as jnp
from jax import lax
import numpy as np

D_MODEL = 2048
BATCH = 8
SEQ = 2048
DEPTH = 2
DEC_BATCH = 128
DEC_SEQ = 1
PAST_LEN = 2048
PAGE_SIZE = 128

N_META = 16
N_BRANCH = 4
W_MIX = D_MODEL // 4
RWKV_HEAD_DIM = 64
RWKV_HEADS = W_MIX // RWKV_HEAD_DIM
LORA_DECAY = max(32, int(round(1.8 * W_MIX ** 0.5 / 32)) * 32)
LORA_ICLR = LORA_DECAY
LORA_GATE = max(32, int(round(0.6 * W_MIX ** 0.8 / 32)) * 32)
RWKV_GN_EPS = 64e-5
GDN_HEADS = 4
GDN_HEAD_DIM = W_MIX // GDN_HEADS
CONV_W = 4
GDN_CHUNK = 64
ATT_HEADS = 4
ATT_HEAD_DIM = W_MIX // ATT_HEADS
IDX_HEADS = 8
IDX_DIM = 64
TOPK_MAX = 256
Q_BLOCK = 64
MLSTM_HEADS = 4
MLSTM_HEAD_DIM = W_MIX // MLSTM_HEADS
MLSTM_CHUNK = 64
N_GROUPS = 4
EXPERTS_PER_GROUP = 8
N_EXPERTS = N_GROUPS * EXPERTS_PER_GROUP
TOP_K_INNER = 2
D_EXPERT = D_MODEL // 4
MOE_BLOCK = 128
ALPHA = (2 * DEPTH) ** 0.25
N_IN = (W_MIX + (4 * W_MIX + 2 * GDN_HEADS) + (3 * W_MIX + IDX_HEADS * IDX_DIM + IDX_DIM + IDX_HEADS)
        + (4 * W_MIX + 2 * MLSTM_HEADS) + N_BRANCH * D_MODEL)

kernel_name = 'hybrid_rwkv7_gdn_dsa_mlstm_hmoe_step'


def _layer_norm(x, g, b, eps=1e-5):
    xf = x.astype(jnp.float32)
    mu = jnp.mean(xf, -1, keepdims=True)
    var = jnp.mean(jnp.square(xf - mu), -1, keepdims=True)
    return ((xf - mu) * lax.rsqrt(var + eps) * g + b).astype(x.dtype)


def _rms_norm(x, g, eps=1e-6):
    xf = x.astype(jnp.float32)
    return (xf * lax.rsqrt(jnp.mean(xf * xf, -1, keepdims=True) + eps) * g).astype(x.dtype)


def _l2norm(x, eps=1e-6):
    xf = x.astype(jnp.float32)
    return xf * lax.rsqrt(jnp.sum(xf * xf, -1, keepdims=True) + eps)


def _split_cols(h):
    sizes = [W_MIX,
             3 * W_MIX, GDN_HEADS, GDN_HEADS, W_MIX,
             W_MIX, W_MIX, W_MIX, IDX_HEADS * IDX_DIM, IDX_DIM, IDX_HEADS,
             3 * W_MIX, MLSTM_HEADS, MLSTM_HEADS, W_MIX,
             N_BRANCH * D_MODEL]
    return jnp.split(h, np.cumsum(sizes)[:-1].tolist(), axis=-1)


def _take_rows(rows, idx):
    return jax.vmap(lambda r, i: r[i])(rows, idx)


def _short_conv(u, buf, w):
    T = u.shape[1]
    ext = jnp.concatenate([buf.astype(jnp.float32), u.astype(jnp.float32)], axis=1)
    y = sum(ext[:, j:j + T] * w[j] for j in range(CONV_W))
    return jax.nn.silu(y), ext[:, ext.shape[1] - (CONV_W - 1):]


def _run_chunks(chunk_fn, state, xs, is_prompt, chunk):
    if not is_prompt:
        state, out = chunk_fn(state, xs)
        return out, state
    state, out_meta = chunk_fn(state, tuple(t[:, :, :N_META] for t in xs))
    real = tuple(t[:, :, N_META:] for t in xs)
    n = real[0].shape[2] // chunk

    def to_chunks(t):
        return jnp.moveaxis(t.reshape(t.shape[:2] + (n, chunk) + t.shape[3:]), 2, 0)

    state, out = lax.scan(chunk_fn, state, tuple(to_chunks(t) for t in real))
    out = jnp.moveaxis(out, 0, 2)
    out = out.reshape(out.shape[:2] + (n * chunk,) + out.shape[4:])
    return jnp.concatenate([out_meta, out], axis=2), state


def _rwkv7(u, S0, shift0, mu, w_rkv, w0, w1, w2, a0, a1, a2, g1, g2, k_k, k_a, r_k, lnx_g, lnx_b):
    B, T, _ = u.shape
    uf = u.astype(jnp.float32)
    prev = jnp.concatenate([shift0[:, None].astype(jnp.float32), uf[:, :-1]], axis=1)
    xx = prev - uf
    xr, xw, xk, xv, xa, xg = (uf + xx * mu[j] for j in range(6))
    r = xr @ w_rkv[0]
    k = xk @ w_rkv[1]
    v = xv @ w_rkv[2]
    w = -jax.nn.softplus(-(w0 + jnp.tanh(xw @ w1) @ w2)) - 0.5
    a = jax.nn.sigmoid(a0 + (xa @ a1) @ a2)
    g = jax.nn.sigmoid(xg @ g1) @ g2
    hd = lambda t: t.reshape(B, T, RWKV_HEADS, RWKV_HEAD_DIM)
    kk = _l2norm(hd(k * k_k))
    k = hd(k * (1.0 + (a - 1.0) * k_a))
    r, v, a = hd(r), hd(v), hd(a)
    decay = jnp.exp(-jnp.exp(hd(w)))

    def step(S, inp):
        r_t, d_t, k_t, v_t, kk_t, a_t = inp
        sa = jnp.einsum('bhvk,bhk->bhv', S, kk_t)
        S = (S * d_t[:, :, None, :] - sa[..., None] * (kk_t * a_t)[:, :, None, :]
             + v_t[..., None] * k_t[:, :, None, :])
        return S, jnp.einsum('bhvk,bhk->bhv', S, r_t)

    seq = tuple(jnp.swapaxes(t, 0, 1) for t in (r, decay, k, v, kk, a))
    S, y = lax.scan(step, S0.astype(jnp.float32), seq)
    y = jnp.swapaxes(y, 0, 1)
    mean = jnp.mean(y, -1, keepdims=True)
    var = jnp.mean(jnp.square(y - mean), -1, keepdims=True)
    y = ((y - mean) * lax.rsqrt(var + RWKV_GN_EPS)).reshape(B, T, W_MIX) * lnx_g + lnx_b
    bonus = jnp.sum(r * k * r_k, -1, keepdims=True) * v
    y = (y + bonus.reshape(B, T, W_MIX)) * g
    return y.astype(u.dtype), S, u[:, -1]


def _gdn_chunk(S, xs):
    q, k, v, g, beta = xs
    L = q.shape[2]
    tri = jnp.tril(jnp.ones((L, L), bool))
    stri = jnp.tril(jnp.ones((L, L), bool), -1)
    gc = jnp.cumsum(g, axis=-1)
    dec = jnp.exp(jnp.where(tri, gc[..., :, None] - gc[..., None, :], -jnp.inf))
    kb = k * beta[..., None]
    A = jnp.where(stri, jnp.einsum('bhid,bhjd->bhij', kb, k) * dec, 0.0)
    rhs = jnp.concatenate([v * beta[..., None], kb * jnp.exp(gc)[..., None]], axis=-1)
    sol = lax.linalg.triangular_solve(A + jnp.eye(L, dtype=A.dtype), rhs, left_side=True,
                                      lower=True, unit_diagonal=True)
    u_, w_ = sol[..., :GDN_HEAD_DIM], sol[..., GDN_HEAD_DIM:]
    v_new = u_ - jnp.einsum('bhld,bhde->bhle', w_, S)
    attn = jnp.where(tri, jnp.einsum('bhid,bhjd->bhij', q, k) * dec, 0.0)
    o = (jnp.einsum('bhld,bhde->bhle', q * jnp.exp(gc)[..., None], S)
         + jnp.einsum('bhij,bhje->bhie', attn, v_new))
    g_last = gc[..., -1]
    S = (S * jnp.exp(g_last)[..., None, None]
         + jnp.einsum('bhld,bhle->bhde', k * jnp.exp(g_last[..., None] - gc)[..., None], v_new))
    return S, o


def _gdn(qkv, a_pre, b_pre, z, S0, buf0, conv_w, A_log, dt_bias, norm_g, is_prompt):
    B, T, _ = qkv.shape
    y, new_buf = _short_conv(qkv, buf0, conv_w)
    q, k, v = jnp.split(y, 3, axis=-1)
    hd = lambda t: t.reshape(B, T, GDN_HEADS, GDN_HEAD_DIM).transpose(0, 2, 1, 3)
    q = _l2norm(hd(q)) * GDN_HEAD_DIM ** -0.5
    k = _l2norm(hd(k))
    v = hd(v)
    g = (-jnp.exp(A_log.astype(jnp.float32)) * jax.nn.softplus(a_pre.astype(jnp.float32) + dt_bias)).transpose(0, 2, 1)
    beta = jax.nn.sigmoid(b_pre.astype(jnp.float32)).transpose(0, 2, 1)
    o, S = _run_chunks(_gdn_chunk, S0.astype(jnp.float32), (q, k, v, g, beta), is_prompt, GDN_CHUNK)
    o = _rms_norm(o.transpose(0, 2, 1, 3), norm_g)
    o = o * jax.nn.silu(z.astype(jnp.float32).reshape(B, T, GDN_HEADS, GDN_HEAD_DIM))
    return o.reshape(B, T, W_MIX).astype(qkv.dtype), S, new_buf


def _alibi_slopes():
    return jnp.asarray(2.0 ** (-8.0 * np.arange(1, ATT_HEADS + 1) / ATT_HEADS), jnp.float32)


def _dsa_inputs(c_q, c_k, c_v, c_qi, c_ki, c_wi, ln_g, ln_b):
    B, T, _ = c_q.shape
    hd = lambda t: t.reshape(B, T, ATT_HEADS, ATT_HEAD_DIM)
    qi = c_qi.reshape(B, T, IDX_HEADS, IDX_DIM)
    ki = _layer_norm(c_ki, ln_g, ln_b)
    wi = c_wi * IDX_HEADS ** -0.5
    return hd(c_q), hd(c_k), hd(c_v), qi, ki, wi


def _dsa_block(q, qi, wi, q_pos, ki, gather, k_top):
    S = ki.shape[1]
    s_pos = jnp.arange(S, dtype=jnp.int32)
    sc = jnp.einsum('bqhd,bsd->bqhs', qi.astype(jnp.float32), ki.astype(jnp.float32)) * IDX_DIM ** -0.5
    isc = jnp.einsum('bqhs,bqh->bqs', jax.nn.relu(sc), wi.astype(jnp.float32))
    isc = jnp.where((s_pos[None, :] <= q_pos[:, None])[None], isc, -jnp.inf)
    _, sel = lax.top_k(isc, k_top)
    valid = sel <= q_pos[None, :, None]
    k_sel, v_sel = gather(sel)
    logits = jnp.einsum('bqhd,bqkhd->bqhk', q.astype(jnp.float32), k_sel.astype(jnp.float32)) * ATT_HEAD_DIM ** -0.5
    dist = (q_pos[None, :, None] - sel).astype(jnp.float32)
    logits = logits - _alibi_slopes()[None, None, :, None] * dist[:, :, None, :]
    logits = jnp.where(valid[:, :, None, :], logits, -jnp.inf)
    p = jax.nn.softmax(logits, axis=-1)
    return jnp.einsum('bqhk,bqkhd->bqhd', p, v_sel.astype(jnp.float32)).astype(q.dtype)


def _dsa_prompt(q, k, v, qi, ki, wi):
    B, T = q.shape[:2]
    k_top = min(TOPK_MAX, T // 4)
    n_blk = -(-T // Q_BLOCK)
    t_pad = n_blk * Q_BLOCK

    def blocks(t):
        t = jnp.pad(t, [(0, 0), (0, t_pad - T)] + [(0, 0)] * (t.ndim - 2))
        return jnp.swapaxes(t.reshape((B, n_blk, Q_BLOCK) + t.shape[2:]), 0, 1)

    def gather(sel):
        return _take_rows(k, sel), _take_rows(v, sel)

    def body(args):
        q_b, qi_b, wi_b, pos_b = args
        return _dsa_block(q_b, qi_b, wi_b, pos_b, ki, gather, k_top)

    pos = jnp.arange(t_pad, dtype=jnp.int32).reshape(n_blk, Q_BLOCK)
    out = lax.map(body, (blocks(q), blocks(qi), blocks(wi), pos))
    return jnp.swapaxes(out, 0, 1).reshape(B, t_pad, ATT_HEADS, ATT_HEAD_DIM)[:, :T]


def _dsa_sample(q, k, v, qi, ki, wi, k_pool, v_pool, i_pool, page_table):
    nb, Q = q.shape[:2]
    ki_past = i_pool[page_table].reshape(nb, PAST_LEN, IDX_DIM)
    ki_all = jnp.concatenate([ki_past.astype(jnp.float32), ki.astype(jnp.float32)], axis=1)
    k_top = min(TOPK_MAX, (PAST_LEN + Q) // 4)

    def gather(sel):
        in_past = (sel < PAST_LEN)[..., None, None]
        sp = jnp.minimum(sel, PAST_LEN - 1)
        phys = page_table[jnp.arange(nb)[:, None, None], sp // PAGE_SIZE]
        off = sp % PAGE_SIZE
        sn = jnp.clip(sel - PAST_LEN, 0, Q - 1)
        return (jnp.where(in_past, k_pool[phys, off], _take_rows(k, sn)),
                jnp.where(in_past, v_pool[phys, off], _take_rows(v, sn)))

    q_pos = PAST_LEN + jnp.arange(Q, dtype=jnp.int32)
    return _dsa_block(q, qi, wi, q_pos, ki_all, gather, k_top)


def _mlstm_chunk(state, xs):
    C, n, m = state
    q, k, v, li, lf = xs
    L = q.shape[2]
    tri = jnp.tril(jnp.ones((L, L), bool))
    b = jnp.cumsum(lf, axis=-1)
    a = li - b
    m_t = b + jnp.maximum(m[..., None], lax.cummax(a, axis=2))
    dmat = jnp.exp(jnp.where(tri, b[..., :, None] + a[..., None, :] - m_t[..., :, None], -jnp.inf))
    inter = jnp.exp(b + m[..., None] - m_t)
    s = jnp.einsum('bhtd,bhsd->bhts', q, k) * dmat
    num = inter[..., None] * jnp.einsum('bhtd,bhde->bhte', q, C) + jnp.einsum('bhts,bhse->bhte', s, v)
    den = inter * jnp.einsum('bhtd,bhd->bht', q, n) + jnp.sum(s, axis=-1)
    h = num / jnp.maximum(jnp.abs(den), jnp.exp(-m_t))[..., None]
    m_new = m_t[..., -1]
    wend = jnp.exp(b[..., -1:] + a - m_new[..., None])
    carry = jnp.exp(b[..., -1] + m - m_new)
    C = carry[..., None, None] * C + jnp.einsum('bhs,bhsd,bhse->bhde', wend, k, v)
    n = carry[..., None] * n + jnp.einsum('bhs,bhsd->bhd', wend, k)
    return (C, n, m_new), h


def _mlstm(qkv, i_pre, f_pre, o_pre, C0, n0, m0, b_i, b_f, norm_g, is_prompt):
    B, T, _ = qkv.shape
    q, k, v = jnp.split(qkv.astype(jnp.float32), 3, axis=-1)
    hd = lambda t: t.reshape(B, T, MLSTM_HEADS, MLSTM_HEAD_DIM).transpose(0, 2, 1, 3)
    q, k, v = hd(q), hd(k) * MLSTM_HEAD_DIM ** -0.5, hd(v)
    li = (i_pre.astype(jnp.float32) + b_i).transpose(0, 2, 1)
    lf = jax.nn.log_sigmoid(f_pre.astype(jnp.float32) + b_f).transpose(0, 2, 1)
    st0 = (C0.astype(jnp.float32), n0.astype(jnp.float32), m0.astype(jnp.float32))
    h, (C, n, m) = _run_chunks(_mlstm_chunk, st0, (q, k, v, li, lf), is_prompt, MLSTM_CHUNK)
    h = _rms_norm(h.transpose(0, 2, 1, 3), norm_g).reshape(B, T, W_MIX)
    h = jax.nn.sigmoid(o_pre.astype(jnp.float32)) * h
    return h.astype(qkv.dtype), C, n, m


def _expert_dispatch(xt, eid, gate, w_g, w_u, w_d):
    N = xt.shape[0]
    A = N * TOP_K_INNER
    e_flat = eid.reshape(-1)
    tok = jnp.arange(A, dtype=jnp.int32) // TOP_K_INNER
    order = jnp.argsort(e_flat)
    e_sorted = e_flat[order]
    counts = jnp.bincount(e_flat, length=N_EXPERTS)
    padded = (counts + MOE_BLOCK - 1) // MOE_BLOCK * MOE_BLOCK
    pad_end = jnp.cumsum(padded)
    pad_start = pad_end - padded
    start = jnp.cumsum(counts) - counts
    dest = pad_start[e_sorted] + jnp.arange(A, dtype=jnp.int32) - start[e_sorted]
    n_blocks = -(-A // MOE_BLOCK) + N_EXPERTS
    slot_tok = jnp.full((n_blocks * MOE_BLOCK,), N, jnp.int32).at[dest].set(tok[order])
    blk_exp = jnp.minimum(jnp.searchsorted(pad_end, jnp.arange(n_blocks) * MOE_BLOCK, side='right'), N_EXPERTS - 1)
    x_pad = jnp.concatenate([xt, jnp.zeros((1, xt.shape[1]), xt.dtype)], axis=0)
    xb = x_pad[slot_tok].reshape(n_blocks, MOE_BLOCK, xt.shape[1])

    def expert(args):
        xe, e = args
        return (jax.nn.silu(xe @ w_g[e]) * (xe @ w_u[e])) @ w_d[e]

    yb = lax.map(expert, (xb, blk_exp)).reshape(n_blocks * MOE_BLOCK, -1)
    y_sorted = yb[dest]
    y_assign = jnp.zeros_like(y_sorted).at[order].set(y_sorted)
    return jnp.einsum('nkd,nk->nd', y_assign.reshape(N, TOP_K_INNER, -1), gate.astype(y_assign.dtype))


def _moe(x, w_group, w_expert, w_g, w_u, w_d):
    shp = x.shape
    xt = x.reshape(-1, D_MODEL)
    N = xt.shape[0]
    xf = xt.astype(jnp.float32)
    pg = jax.nn.softmax(xf @ w_group.astype(jnp.float32), axis=-1)
    p_top, g_sel = lax.top_k(pg, 1)
    le = (xf @ w_expert.astype(jnp.float32)).reshape(N, N_GROUPS, EXPERTS_PER_GROUP)
    le_g = le[jnp.arange(N), g_sel[:, 0]]
    v2, j2 = lax.top_k(le_g, TOP_K_INNER)
    gate = jax.nn.softmax(v2, axis=-1) * p_top
    eid = g_sel * EXPERTS_PER_GROUP + j2
    return _expert_dispatch(xt, eid, gate, w_g, w_u, w_d).reshape(shp).astype(x.dtype)


def _layer(x, W, l, st, dsa_fn, is_prompt):
    B, T, _ = x.shape
    (a_u, b_qkv, b_a, b_b, b_z, c_q, c_k, c_v, c_qi, c_ki, c_wi,
     d_qkv, d_i, d_f, d_o, gates) = _split_cols(x @ W['w_in'][l])
    rwkv_S0, rwkv_shift0, gdn_S0, gdn_conv0, mC0, mn0, mm0 = st
    y_a, rwkv_S, rwkv_shift = _rwkv7(
        a_u, rwkv_S0, rwkv_shift0, W['rwkv_mu'][l], W['rwkv_w_rkv'][l], W['rwkv_w0'][l], W['rwkv_w1'][l],
        W['rwkv_w2'][l], W['rwkv_a0'][l], W['rwkv_a1'][l], W['rwkv_a2'][l], W['rwkv_g1'][l], W['rwkv_g2'][l],
        W['rwkv_k_k'][l], W['rwkv_k_a'][l], W['rwkv_r_k'][l], W['rwkv_lnx_g'][l], W['rwkv_lnx_b'][l])
    y_b, gdn_S, gdn_conv = _gdn(b_qkv, b_a, b_b, b_z, gdn_S0, gdn_conv0, W['gdn_conv_w'][l], W['gdn_A_log'][l],
                                W['gdn_dt_bias'][l], W['gdn_norm_g'][l], is_prompt)
    q, k, v, qi, ki, wi = _dsa_inputs(c_q, c_k, c_v, c_qi, c_ki, c_wi, W['idx_ln_g'][l], W['idx_ln_b'][l])
    y_c = dsa_fn(q, k, v, qi, ki, wi).reshape(B, T, W_MIX)
    y_d, mC, mn, mm = _mlstm(d_qkv, d_i, d_f, d_o, mC0, mn0, mm0, W['mlstm_b_i'][l], W['mlstm_b_f'][l],
                             W['mlstm_norm_g'][l], is_prompt)
    merged = 0.0
    for i, y_i in enumerate((y_a, y_b, y_c, y_d)):
        gate_i = jax.nn.sigmoid(gates[..., i * D_MODEL:(i + 1) * D_MODEL].astype(jnp.float32))
        merged = merged + gate_i * (y_i @ W['w_branch'][l, i])
    mix = merged.astype(x.dtype) @ W['w_out'][l]
    x = _layer_norm(ALPHA * x + mix, W['ln1_g'][l], W['ln1_b'][l])
    ffn = _moe(x, W['moe_w_group'][l], W['moe_w_expert'][l], W['moe_w_gate'][l], W['moe_w_up'][l], W['moe_w_down'][l])
    x = _layer_norm(ALPHA * x + ffn, W['ln2_g'][l], W['ln2_b'][l])
    return x, (k, v, ki, rwkv_S, rwkv_shift, gdn_S, gdn_conv, mC, mn, mm)


def setup_inputs(seed: int = 0) -> dict:
    key = jax.random.key(seed)
    keys = list(jax.random.split(key, 80))
    f32 = jnp.float32

    def nrm(shape, scale=1.0):
        return jax.random.normal(keys.pop(), shape, f32) * scale

    def unif(shape, lo, hi):
        return jax.random.uniform(keys.pop(), shape, f32, lo, hi)

    beta = (8.0 * DEPTH) ** -0.25
    n_pages = PAST_LEN // PAGE_SIZE
    n_used = DEC_BATCH * n_pages
    n_pool = n_used + max(1, n_used // 4)
    page_table = jax.random.permutation(keys.pop(), n_pool)[:n_used].reshape(DEC_BATCH, n_pages).astype(jnp.int32)
    Wm, D = W_MIX, D_MODEL
    dt = jnp.exp(unif((DEPTH, GDN_HEADS), math.log(1e-3), math.log(1e-1)))
    return {
        'x_prompt': nrm((BATCH, SEQ, D)),
        'x_sample': nrm((DEC_BATCH, DEC_SEQ, D)),
        'cache_k': nrm((DEPTH, n_pool, PAGE_SIZE, ATT_HEADS, ATT_HEAD_DIM)),
        'cache_v': nrm((DEPTH, n_pool, PAGE_SIZE, ATT_HEADS, ATT_HEAD_DIM)),
        'cache_idx_k': nrm((DEPTH, n_pool, PAGE_SIZE, IDX_DIM)),
        'page_table': page_table,
        'state_rwkv_S': nrm((DEPTH, DEC_BATCH, RWKV_HEADS, RWKV_HEAD_DIM, RWKV_HEAD_DIM), 0.3),
        'state_rwkv_shift': nrm((DEPTH, DEC_BATCH, Wm)),
        'state_gdn_S': nrm((DEPTH, DEC_BATCH, GDN_HEADS, GDN_HEAD_DIM, GDN_HEAD_DIM), 0.1),
        'state_gdn_conv': nrm((DEPTH, DEC_BATCH, CONV_W - 1, 3 * Wm)),
        'state_mlstm_C': nrm((DEPTH, DEC_BATCH, MLSTM_HEADS, MLSTM_HEAD_DIM, MLSTM_HEAD_DIM), 0.3),
        'state_mlstm_n': nrm((DEPTH, DEC_BATCH, MLSTM_HEADS, MLSTM_HEAD_DIM), 0.3),
        'state_mlstm_m': nrm((DEPTH, DEC_BATCH, MLSTM_HEADS), 0.5),
        'meta_tokens': nrm((N_META, D)),
        'ln_in_g': 1.0 + nrm((D,), 0.02),
        'ln_in_b': nrm((D,), 0.02),
        'w_in': nrm((DEPTH, D, N_IN), D ** -0.5),
        'rwkv_mu': unif((DEPTH, 6, Wm), 0.0, 1.0),
        'rwkv_w_rkv': nrm((DEPTH, 3, Wm, Wm), Wm ** -0.5),
        'rwkv_w0': unif((DEPTH, Wm), -6.0, -1.0),
        'rwkv_w1': nrm((DEPTH, Wm, LORA_DECAY), Wm ** -0.5),
        'rwkv_w2': nrm((DEPTH, LORA_DECAY, Wm), 0.1 * LORA_DECAY ** -0.5),
        'rwkv_a0': nrm((DEPTH, Wm), 0.1),
        'rwkv_a1': nrm((DEPTH, Wm, LORA_ICLR), Wm ** -0.5),
        'rwkv_a2': nrm((DEPTH, LORA_ICLR, Wm), 0.1 * LORA_ICLR ** -0.5),
        'rwkv_g1': nrm((DEPTH, Wm, LORA_GATE), Wm ** -0.5),
        'rwkv_g2': nrm((DEPTH, LORA_GATE, Wm), LORA_GATE ** -0.5),
        'rwkv_k_k': 0.85 + nrm((DEPTH, Wm), 0.02),
        'rwkv_k_a': 1.0 + nrm((DEPTH, Wm), 0.02),
        'rwkv_r_k': nrm((DEPTH, RWKV_HEADS, RWKV_HEAD_DIM), 0.1),
        'rwkv_lnx_g': 1.0 + nrm((DEPTH, Wm), 0.02),
        'rwkv_lnx_b': nrm((DEPTH, Wm), 0.02),
        'gdn_conv_w': nrm((DEPTH, CONV_W, 3 * Wm), CONV_W ** -0.5),
        'gdn_A_log': jnp.log(unif((DEPTH, GDN_HEADS), 1.0, 16.0)),
        'gdn_dt_bias': dt + jnp.log(-jnp.expm1(-dt)),
        'gdn_norm_g': 1.0 + nrm((DEPTH, GDN_HEAD_DIM), 0.02),
        'idx_ln_g': 1.0 + nrm((DEPTH, IDX_DIM), 0.02),
        'idx_ln_b': nrm((DEPTH, IDX_DIM), 0.02),
        'mlstm_b_i': nrm((DEPTH, MLSTM_HEADS), 0.1),
        'mlstm_b_f': jnp.linspace(3.0, 6.0, MLSTM_HEADS, dtype=f32)[None] + nrm((DEPTH, MLSTM_HEADS), 0.1),
        'mlstm_norm_g': 1.0 + nrm((DEPTH, MLSTM_HEAD_DIM), 0.02),
        'w_branch': nrm((DEPTH, N_BRANCH, Wm, D), beta * Wm ** -0.5),
        'w_out': nrm((DEPTH, D, D), beta * D ** -0.5),
        'ln1_g': 1.0 + nrm((DEPTH, D), 0.02),
        'ln1_b': nrm((DEPTH, D), 0.02),
        'ln2_g': 1.0 + nrm((DEPTH, D), 0.02),
        'ln2_b': nrm((DEPTH, D), 0.02),
        'moe_w_group': nrm((DEPTH, D, N_GROUPS), D ** -0.5),
        'moe_w_expert': nrm((DEPTH, D, N_EXPERTS), D ** -0.5),
        'moe_w_gate': nrm((DEPTH, N_EXPERTS, D, D_EXPERT), D ** -0.5),
        'moe_w_up': nrm((DEPTH, N_EXPERTS, D, D_EXPERT), D ** -0.5),
        'moe_w_down': nrm((DEPTH, N_EXPERTS, D_EXPERT, D), beta * D_EXPERT ** -0.5),
    }


def reference(x_prompt, x_sample, cache_k, cache_v, cache_idx_k, page_table, state_rwkv_S, state_rwkv_shift,
              state_gdn_S, state_gdn_conv, state_mlstm_C, state_mlstm_n, state_mlstm_m, meta_tokens, ln_in_g,
              ln_in_b, w_in, rwkv_mu, rwkv_w_rkv, rwkv_w0, rwkv_w1, rwkv_w2, rwkv_a0, rwkv_a1, rwkv_a2, rwkv_g1,
              rwkv_g2, rwkv_k_k, rwkv_k_a, rwkv_r_k, rwkv_lnx_g, rwkv_lnx_b, gdn_conv_w, gdn_A_log, gdn_dt_bias,
              gdn_norm_g, idx_ln_g, idx_ln_b, mlstm_b_i, mlstm_b_f, mlstm_norm_g, w_branch, w_out, ln1_g, ln1_b,
              ln2_g, ln2_b, moe_w_group, moe_w_expert, moe_w_gate, moe_w_up, moe_w_down):
    W = dict(w_in=w_in, rwkv_mu=rwkv_mu, rwkv_w_rkv=rwkv_w_rkv, rwkv_w0=rwkv_w0, rwkv_w1=rwkv_w1,
             rwkv_w2=rwkv_w2, rwkv_a0=rwkv_a0, rwkv_a1=rwkv_a1, rwkv_a2=rwkv_a2, rwkv_g1=rwkv_g1,
             rwkv_g2=rwkv_g2, rwkv_k_k=rwkv_k_k, rwkv_k_a=rwkv_k_a, rwkv_r_k=rwkv_r_k, rwkv_lnx_g=rwkv_lnx_g,
             rwkv_lnx_b=rwkv_lnx_b, gdn_conv_w=gdn_conv_w, gdn_A_log=gdn_A_log, gdn_dt_bias=gdn_dt_bias,
             gdn_norm_g=gdn_norm_g, idx_ln_g=idx_ln_g, idx_ln_b=idx_ln_b, mlstm_b_i=mlstm_b_i,
             mlstm_b_f=mlstm_b_f, mlstm_norm_g=mlstm_norm_g, w_branch=w_branch, w_out=w_out, ln1_g=ln1_g,
             ln1_b=ln1_b, ln2_g=ln2_g, ln2_b=ln2_b, moe_w_group=moe_w_group, moe_w_expert=moe_w_expert,
             moe_w_gate=moe_w_gate, moe_w_up=moe_w_up, moe_w_down=moe_w_down)
    f32 = jnp.float32
    B = x_prompt.shape[0]
    meta = jnp.broadcast_to(meta_tokens.astype(x_prompt.dtype)[None], (B, N_META, D_MODEL))
    hp = _layer_norm(jnp.concatenate([meta, x_prompt], axis=1), ln_in_g, ln_in_b)
    hs = _layer_norm(x_sample, ln_in_g, ln_in_b)
    zero_state = (jnp.zeros((B, RWKV_HEADS, RWKV_HEAD_DIM, RWKV_HEAD_DIM), f32),
                  jnp.zeros((B, W_MIX), f32),
                  jnp.zeros((B, GDN_HEADS, GDN_HEAD_DIM, GDN_HEAD_DIM), f32),
                  jnp.zeros((B, CONV_W - 1, 3 * W_MIX), f32),
                  jnp.zeros((B, MLSTM_HEADS, MLSTM_HEAD_DIM, MLSTM_HEAD_DIM), f32),
                  jnp.zeros((B, MLSTM_HEADS, MLSTM_HEAD_DIM), f32),
                  jnp.zeros((B, MLSTM_HEADS), f32))
    new_p, new_s = [], []
    for l in range(DEPTH):
        hp, sp = _layer(hp, W, l, zero_state, _dsa_prompt, True)
        dsa_s = functools.partial(_dsa_sample, k_pool=cache_k[l], v_pool=cache_v[l], i_pool=cache_idx_k[l],
                                  page_table=page_table)
        st_s = (state_rwkv_S[l], state_rwkv_shift[l], state_gdn_S[l], state_gdn_conv[l],
                state_mlstm_C[l], state_mlstm_n[l], state_mlstm_m[l])
        hs, ss = _layer(hs, W, l, st_s, dsa_s, False)
        new_p.append(sp)
        new_s.append(ss)

    def stack(rows, j):
        return jnp.stack([r[j] for r in rows])

    k_p, v_p, ik_p, rS_p, rsh_p, gS_p, gc_p, mC_p, mn_p, mm_p = (stack(new_p, j) for j in range(10))
    k_s, v_s, ik_s, rS_s, rsh_s, gS_s, gc_s, mC_s, mn_s, mm_s = (stack(new_s, j) for j in range(10))
    y_prompt = hp[:, N_META:]
    y_sample = hs
    return (y_prompt, y_sample, k_p, v_p, ik_p, k_s, v_s, ik_s, rS_p, rS_s, rsh_p, rsh_s,
            gS_p, gS_s, gc_p, gc_s, mC_p, mC_s, mn_p, mn_s, mm_p, mm_s)
```

```python
import math, functools
import jax, jax.numpy as jnp
from jax import lax
import numpy as np
from jax.experimental import pallas as pl
from jax.experimental.pallas import tpu as pltpu

D_MODEL = 2048
N_META = 16
N_BRANCH = 4
W_MIX = D_MODEL // 4
RWKV_HEAD_DIM = 64
RWKV_HEADS = W_MIX // RWKV_HEAD_DIM
RWKV_GN_EPS = 64e-5
GDN_HEADS = 4
GDN_HEAD_DIM = W_MIX // GDN_HEADS
CONV_W = 4
GDN_CHUNK = 64
ATT_HEADS = 4
ATT_HEAD_DIM = W_MIX // ATT_HEADS
IDX_HEADS = 8
IDX_DIM = 64
TOPK_MAX = 256
Q_BLOCK = 64
MLSTM_HEADS = 4
MLSTM_HEAD_DIM = W_MIX // MLSTM_HEADS
MLSTM_CHUNK = 64
N_GROUPS = 4
EXPERTS_PER_GROUP = 8
N_EXPERTS = N_GROUPS * EXPERTS_PER_GROUP
TOP_K_INNER = 2
D_EXPERT = D_MODEL // 4
MOE_BLOCK = 128
PAGE_SIZE = 128

VMEM_LIMIT_BYTES = 56 * 1024 * 1024


def _mm_body(x_ref, w_ref, o_ref):
    o_ref[...] = jnp.dot(x_ref[...].astype(jnp.bfloat16), w_ref[...].astype(jnp.bfloat16),
                         preferred_element_type=jnp.float32)


def _mm(x, w, tm=512, tn=1024):
    M, K = x.shape
    N = w.shape[1]
    tm = min(tm, M)
    tn = min(tn, N)
    return pl.pallas_call(
        _mm_body,
        grid=(pl.cdiv(N, tn), pl.cdiv(M, tm)),
        in_specs=[pl.BlockSpec((tm, K), lambda j, i: (i, 0)),
                  pl.BlockSpec((K, tn), lambda j, i: (0, j))],
        out_specs=pl.BlockSpec((tm, tn), lambda j, i: (i, j)),
        out_shape=jax.ShapeDtypeStruct((M, N), jnp.float32),
        compiler_params=pltpu.CompilerParams(dimension_semantics=("arbitrary", "arbitrary"),
                                             vmem_limit_bytes=VMEM_LIMIT_BYTES),
        name="proj_mm",
    )(x, w)


def _layer_norm(x, g, b, eps=1e-5):
    xf = x.astype(jnp.float32)
    mu = jnp.mean(xf, -1, keepdims=True)
    var = jnp.mean(jnp.square(xf - mu), -1, keepdims=True)
    return ((xf - mu) * lax.rsqrt(var + eps) * g + b).astype(x.dtype)


def _rms_norm(x, g, eps=1e-6):
    xf = x.astype(jnp.float32)
    return (xf * lax.rsqrt(jnp.mean(xf * xf, -1, keepdims=True) + eps) * g).astype(x.dtype)


def _l2norm(x, eps=1e-6):
    xf = x.astype(jnp.float32)
    return xf * lax.rsqrt(jnp.sum(xf * xf, -1, keepdims=True) + eps)


def _split_cols(h):
    sizes = [W_MIX,
             3 * W_MIX, GDN_HEADS, GDN_HEADS, W_MIX,
             W_MIX, W_MIX, W_MIX, IDX_HEADS * IDX_DIM, IDX_DIM, IDX_HEADS,
             3 * W_MIX, MLSTM_HEADS, MLSTM_HEADS, W_MIX,
             N_BRANCH * D_MODEL]
    return jnp.split(h, np.cumsum(sizes)[:-1].tolist(), axis=-1)


def _take_rows(rows, idx):
    return jax.vmap(lambda r, i: r[i])(rows, idx)


def _short_conv(u, buf, w):
    T = u.shape[1]
    ext = jnp.concatenate([buf.astype(jnp.float32), u.astype(jnp.float32)], axis=1)
    y = sum(ext[:, j:j + T] * w[j] for j in range(CONV_W))
    return jax.nn.silu(y), ext[:, ext.shape[1] - (CONV_W - 1):]


def _run_chunks(chunk_fn, state, xs, is_prompt, chunk):
    if not is_prompt:
        state, out = chunk_fn(state, xs)
        return out, state
    state, out_meta = chunk_fn(state, tuple(t[:, :, :N_META] for t in xs))
    real = tuple(t[:, :, N_META:] for t in xs)
    n = real[0].shape[2] // chunk

    def to_chunks(t):
        return jnp.moveaxis(t.reshape(t.shape[:2] + (n, chunk) + t.shape[3:]), 2, 0)

    state, out = lax.scan(chunk_fn, state, tuple(to_chunks(t) for t in real))
    out = jnp.moveaxis(out, 0, 2)
    out = out.reshape(out.shape[:2] + (n * chunk,) + out.shape[4:])
    return jnp.concatenate([out_meta, out], axis=2), state


def _rwkv7(u, S0, shift0, mu, w_rkv, w0, w1, w2, a0, a1, a2, g1, g2, k_k, k_a, r_k, lnx_g, lnx_b):
    B, T, _ = u.shape
    uf = u.astype(jnp.float32)
    prev = jnp.concatenate([shift0[:, None].astype(jnp.float32), uf[:, :-1]], axis=1)
    xx = prev - uf
    xr, xw, xk, xv, xa, xg = (uf + xx * mu[j] for j in range(6))
    r = xr @ w_rkv[0]
    k = xk @ w_rkv[1]
    v = xv @ w_rkv[2]
    w = -jax.nn.softplus(-(w0 + jnp.tanh(xw @ w1) @ w2)) - 0.5
    a = jax.nn.sigmoid(a0 + (xa @ a1) @ a2)
    g = jax.nn.sigmoid(xg @ g1) @ g2
    hd = lambda t: t.reshape(B, T, RWKV_HEADS, RWKV_HEAD_DIM)
    kk = _l2norm(hd(k * k_k))
    k = hd(k * (1.0 + (a - 1.0) * k_a))
    r, v, a = hd(r), hd(v), hd(a)
    decay = jnp.exp(-jnp.exp(hd(w)))

    def step(S, inp):
        r_t, d_t, k_t, v_t, kk_t, a_t = inp
        sa = jnp.einsum('bhvk,bhk->bhv', S, kk_t)
        S = (S * d_t[:, :, None, :] - sa[..., None] * (kk_t * a_t)[:, :, None, :]
             + v_t[..., None] * k_t[:, :, None, :])
        return S, jnp.einsum('bhvk,bhk->bhv', S, r_t)

    seq = tuple(jnp.swapaxes(t, 0, 1) for t in (r, decay, k, v, kk, a))
    S, y = lax.scan(step, S0.astype(jnp.float32), seq)
    y = jnp.swapaxes(y, 0, 1)
    mean = jnp.mean(y, -1, keepdims=True)
    var = jnp.mean(jnp.square(y - mean), -1, keepdims=True)
    y = ((y - mean) * lax.rsqrt(var + RWKV_GN_EPS)).reshape(B, T, W_MIX) * lnx_g + lnx_b
    bonus = jnp.sum(r * k * r_k, -1, keepdims=True) * v
    y = (y + bonus.reshape(B, T, W_MIX)) * g
    return y.astype(u.dtype), S, u[:, -1]


def _gdn_chunk(S, xs):
    q, k, v, g, beta = xs
    L = q.shape[2]
    tri = jnp.tril(jnp.ones((L, L), bool))
    stri = jnp.tril(jnp.ones((L, L), bool), -1)
    gc = jnp.cumsum(g, axis=-1)
    dec = jnp.exp(jnp.where(tri, gc[..., :, None] - gc[..., None, :], -jnp.inf))
    kb = k * beta[..., None]
    A = jnp.where(stri, jnp.einsum('bhid,bhjd->bhij', kb, k) * dec, 0.0)
    rhs = jnp.concatenate([v * beta[..., None], kb * jnp.exp(gc)[..., None]], axis=-1)
    sol = lax.linalg.triangular_solve(A + jnp.eye(L, dtype=A.dtype), rhs, left_side=True,
                                      lower=True, unit_diagonal=True)
    u_, w_ = sol[..., :GDN_HEAD_DIM], sol[..., GDN_HEAD_DIM:]
    v_new = u_ - jnp.einsum('bhld,bhde->bhle', w_, S)
    attn = jnp.where(tri, jnp.einsum('bhid,bhjd->bhij', q, k) * dec, 0.0)
    o = (jnp.einsum('bhld,bhde->bhle', q * jnp.exp(gc)[..., None], S)
         + jnp.einsum('bhij,bhje->bhie', attn, v_new))
    g_last = gc[..., -1]
    S = (S * jnp.exp(g_last)[..., None, None]
         + jnp.einsum('bhld,bhle->bhde', k * jnp.exp(g_last[..., None] - gc)[..., None], v_new))
    return S, o


def _gdn(qkv, a_pre, b_pre, z, S0, buf0, conv_w, A_log, dt_bias, norm_g, is_prompt):
    B, T, _ = qkv.shape
    y, new_buf = _short_conv(qkv, buf0, conv_w)
    q, k, v = jnp.split(y, 3, axis=-1)
    hd = lambda t: t.reshape(B, T, GDN_HEADS, GDN_HEAD_DIM).transpose(0, 2, 1, 3)
    q = _l2norm(hd(q)) * GDN_HEAD_DIM ** -0.5
    k = _l2norm(hd(k))
    v = hd(v)
    g = (-jnp.exp(A_log.astype(jnp.float32)) * jax.nn.softplus(a_pre.astype(jnp.float32) + dt_bias)).transpose(0, 2, 1)
    beta = jax.nn.sigmoid(b_pre.astype(jnp.float32)).transpose(0, 2, 1)
    o, S = _run_chunks(_gdn_chunk, S0.astype(jnp.float32), (q, k, v, g, beta), is_prompt, GDN_CHUNK)
    o = _rms_norm(o.transpose(0, 2, 1, 3), norm_g)
    o = o * jax.nn.silu(z.astype(jnp.float32).reshape(B, T, GDN_HEADS, GDN_HEAD_DIM))
    return o.reshape(B, T, W_MIX).astype(qkv.dtype), S, new_buf


def _alibi_slopes():
    return jnp.asarray(2.0 ** (-8.0 * np.arange(1, ATT_HEADS + 1) / ATT_HEADS), jnp.float32)


def _dsa_inputs(c_q, c_k, c_v, c_qi, c_ki, c_wi, ln_g, ln_b):
    B, T, _ = c_q.shape
    hd = lambda t: t.reshape(B, T, ATT_HEADS, ATT_HEAD_DIM)
    qi = c_qi.reshape(B, T, IDX_HEADS, IDX_DIM)
    ki = _layer_norm(c_ki, ln_g, ln_b)
    wi = c_wi * IDX_HEADS ** -0.5
    return hd(c_q), hd(c_k), hd(c_v), qi, ki, wi


def _dsa_block(q, qi, wi, q_pos, ki, gather, k_top):
    S = ki.shape[1]
    s_pos = jnp.arange(S, dtype=jnp.int32)
    sc = jnp.einsum('bqhd,bsd->bqhs', qi.astype(jnp.float32), ki.astype(jnp.float32)) * IDX_DIM ** -0.5
    isc = jnp.einsum('bqhs,bqh->bqs', jax.nn.relu(sc), wi.astype(jnp.float32))
    isc = jnp.where((s_pos[None, :] <= q_pos[:, None])[None], isc, -jnp.inf)
    _, sel = lax.top_k(isc, k_top)
    valid = sel <= q_pos[None, :, None]
    k_sel, v_sel = gather(sel)
    logits = jnp.einsum('bqhd,bqkhd->bqhk', q.astype(jnp.float32), k_sel.astype(jnp.float32)) * ATT_HEAD_DIM ** -0.5
    dist = (q_pos[None, :, None] - sel).astype(jnp.float32)
    logits = logits - _alibi_slopes()[None, None, :, None] * dist[:, :, None, :]
    logits = jnp.where(valid[:, :, None, :], logits, -jnp.inf)
    p = jax.nn.softmax(logits, axis=-1)
    return jnp.einsum('bqhk,bqkhd->bqhd', p, v_sel.astype(jnp.float32)).astype(q.dtype)


def _dsa_prompt(q, k, v, qi, ki, wi):
    B, T = q.shape[:2]
    k_top = min(TOPK_MAX, T // 4)
    n_blk = -(-T // Q_BLOCK)
    t_pad = n_blk * Q_BLOCK

    def blocks(t):
        t = jnp.pad(t, [(0, 0), (0, t_pad - T)] + [(0, 0)] * (t.ndim - 2))
        return jnp.swapaxes(t.reshape((B, n_blk, Q_BLOCK) + t.shape[2:]), 0, 1)

    def gather(sel):
        return _take_rows(k, sel), _take_rows(v, sel)

    def body(args):
        q_b, qi_b, wi_b, pos_b = args
        return _dsa_block(q_b, qi_b, wi_b, pos_b, ki, gather, k_top)

    pos = jnp.arange(t_pad, dtype=jnp.int32).reshape(n_blk, Q_BLOCK)
    out = lax.map(body, (blocks(q), blocks(qi), blocks(wi), pos))
    return jnp.swapaxes(out, 0, 1).reshape(B, t_pad, ATT_HEADS, ATT_HEAD_DIM)[:, :T]


def _dsa_sample(q, k, v, qi, ki, wi, k_pool, v_pool, i_pool, page_table):
    nb, Q = q.shape[:2]
    past_len = page_table.shape[1] * PAGE_SIZE
    ki_past = i_pool[page_table].reshape(nb, past_len, IDX_DIM)
    ki_all = jnp.concatenate([ki_past.astype(jnp.float32), ki.astype(jnp.float32)], axis=1)
    k_top = min(TOPK_MAX, (past_len + Q) // 4)

    def gather(sel):
        in_past = (sel < past_len)[..., None, None]
        sp = jnp.minimum(sel, past_len - 1)
        phys = page_table[jnp.arange(nb)[:, None, None], sp // PAGE_SIZE]
        off = sp % PAGE_SIZE
        sn = jnp.clip(sel - past_len, 0, Q - 1)
        return (jnp.where(in_past, k_pool[phys, off], _take_rows(k, sn)),
                jnp.where(in_past, v_pool[phys, off], _take_rows(v, sn)))

    q_pos = past_len + jnp.arange(Q, dtype=jnp.int32)
    return _dsa_block(q, qi, wi, q_pos, ki_all, gather, k_top)


def _mlstm_chunk(state, xs):
    C, n, m = state
    q, k, v, li, lf = xs
    L = q.shape[2]
    tri = jnp.tril(jnp.ones((L, L), bool))
    b = jnp.cumsum(lf, axis=-1)
    a = li - b
    m_t = b + jnp.maximum(m[..., None], lax.cummax(a, axis=2))
    dmat = jnp.exp(jnp.where(tri, b[..., :, None] + a[..., None, :] - m_t[..., :, None], -jnp.inf))
    inter = jnp.exp(b + m[..., None] - m_t)
    s = jnp.einsum('bhtd,bhsd->bhts', q, k) * dmat
    num = inter[..., None] * jnp.einsum('bhtd,bhde->bhte', q, C) + jnp.einsum('bhts,bhse->bhte', s, v)
    den = inter * jnp.einsum('bhtd,bhd->bht', q, n) + jnp.sum(s, axis=-1)
    h = num / jnp.maximum(jnp.abs(den), jnp.exp(-m_t))[..., None]
    m_new = m_t[..., -1]
    wend = jnp.exp(b[..., -1:] + a - m_new[..., None])
    carry = jnp.exp(b[..., -1] + m - m_new)
    C = carry[..., None, None] * C + jnp.einsum('bhs,bhsd,bhse->bhde', wend, k, v)
    n = carry[..., None] * n + jnp.einsum('bhs,bhsd->bhd', wend, k)
    return (C, n, m_new), h


def _mlstm(qkv, i_pre, f_pre, o_pre, C0, n0, m0, b_i, b_f, norm_g, is_prompt):
    B, T, _ = qkv.shape
    q, k, v = jnp.split(qkv.astype(jnp.float32), 3, axis=-1)
    hd = lambda t: t.reshape(B, T, MLSTM_HEADS, MLSTM_HEAD_DIM).transpose(0, 2, 1, 3)
    q, k, v = hd(q), hd(k) * MLSTM_HEAD_DIM ** -0.5, hd(v)
    li = (i_pre.astype(jnp.float32) + b_i).transpose(0, 2, 1)
    lf = jax.nn.log_sigmoid(f_pre.astype(jnp.float32) + b_f).transpose(0, 2, 1)
    st0 = (C0.astype(jnp.float32), n0.astype(jnp.float32), m0.astype(jnp.float32))
    h, (C, n, m) = _run_chunks(_mlstm_chunk, st0, (q, k, v, li, lf), is_prompt, MLSTM_CHUNK)
    h = _rms_norm(h.transpose(0, 2, 1, 3), norm_g).reshape(B, T, W_MIX)
    h = jax.nn.sigmoid(o_pre.astype(jnp.float32)) * h
    return h.astype(qkv.dtype), C, n, m


def _expert_dispatch(xt, eid, gate, w_g, w_u, w_d):
    N = xt.shape[0]
    A = N * TOP_K_INNER
    e_flat = eid.reshape(-1)
    tok = jnp.arange(A, dtype=jnp.int32) // TOP_K_INNER
    order = jnp.argsort(e_flat)
    e_sorted = e_flat[order]
    counts = jnp.bincount(e_flat, length=N_EXPERTS)
    padded = (counts + MOE_BLOCK - 1) // MOE_BLOCK * MOE_BLOCK
    pad_end = jnp.cumsum(padded)
    pad_start = pad_end - padded
    start = jnp.cumsum(counts) - counts
    dest = pad_start[e_sorted] + jnp.arange(A, dtype=jnp.int32) - start[e_sorted]
    n_blocks = -(-A // MOE_BLOCK) + N_EXPERTS
    slot_tok = jnp.full((n_blocks * MOE_BLOCK,), N, jnp.int32).at[dest].set(tok[order])
    blk_exp = jnp.minimum(jnp.searchsorted(pad_end, jnp.arange(n_blocks) * MOE_BLOCK, side='right'), N_EXPERTS - 1)
    x_pad = jnp.concatenate([xt, jnp.zeros((1, xt.shape[1]), xt.dtype)], axis=0)
    xb = x_pad[slot_tok].reshape(n_blocks, MOE_BLOCK, xt.shape[1])

    def expert(args):
        xe, e = args
        return (jax.nn.silu(xe @ w_g[e]) * (xe @ w_u[e])) @ w_d[e]

    yb = lax.map(expert, (xb, blk_exp)).reshape(n_blocks * MOE_BLOCK, -1)
    y_sorted = yb[dest]
    y_assign = jnp.zeros_like(y_sorted).at[order].set(y_sorted)
    return jnp.einsum('nkd,nk->nd', y_assign.reshape(N, TOP_K_INNER, -1), gate.astype(y_assign.dtype))


def _moe(x, w_group, w_expert, w_g, w_u, w_d):
    shp = x.shape
    xt = x.reshape(-1, D_MODEL)
    N = xt.shape[0]
    xf = xt.astype(jnp.float32)
    pg = jax.nn.softmax(xf @ w_group.astype(jnp.float32), axis=-1)
    p_top, g_sel = lax.top_k(pg, 1)
    le = (xf @ w_expert.astype(jnp.float32)).reshape(N, N_GROUPS, EXPERTS_PER_GROUP)
    le_g = le[jnp.arange(N), g_sel[:, 0]]
    v2, j2 = lax.top_k(le_g, TOP_K_INNER)
    gate = jax.nn.softmax(v2, axis=-1) * p_top
    eid = g_sel * EXPERTS_PER_GROUP + j2
    return _expert_dispatch(xt, eid, gate, w_g, w_u, w_d).reshape(shp).astype(x.dtype)


def _layer(x, W, l, st, dsa_fn, is_prompt, alpha):
    B, T, _ = x.shape
    h = _mm(x.reshape(B * T, D_MODEL), W['w_in'][l]).reshape(B, T, -1)
    (a_u, b_qkv, b_a, b_b, b_z, c_q, c_k, c_v, c_qi, c_ki, c_wi,
     d_qkv, d_i, d_f, d_o, gates) = _split_cols(h)
    rwkv_S0, rwkv_shift0, gdn_S0, gdn_conv0, mC0, mn0, mm0 = st
    y_a, rwkv_S, rwkv_shift = _rwkv7(
        a_u, rwkv_S0, rwkv_shift0, W['rwkv_mu'][l], W['rwkv_w_rkv'][l], W['rwkv_w0'][l], W['rwkv_w1'][l],
        W['rwkv_w2'][l], W['rwkv_a0'][l], W['rwkv_a1'][l], W['rwkv_a2'][l], W['rwkv_g1'][l], W['rwkv_g2'][l],
        W['rwkv_k_k'][l], W['rwkv_k_a'][l], W['rwkv_r_k'][l], W['rwkv_lnx_g'][l], W['rwkv_lnx_b'][l])
    y_b, gdn_S, gdn_conv = _gdn(b_qkv, b_a, b_b, b_z, gdn_S0, gdn_conv0, W['gdn_conv_w'][l], W['gdn_A_log'][l],
                                W['gdn_dt_bias'][l], W['gdn_norm_g'][l], is_prompt)
    q, k, v, qi, ki, wi = _dsa_inputs(c_q, c_k, c_v, c_qi, c_ki, c_wi, W['idx_ln_g'][l], W['idx_ln_b'][l])
    y_c = dsa_fn(q, k, v, qi, ki, wi).reshape(B, T, W_MIX)
    y_d, mC, mn, mm = _mlstm(d_qkv, d_i, d_f, d_o, mC0, mn0, mm0, W['mlstm_b_i'][l], W['mlstm_b_f'][l],
                             W['mlstm_norm_g'][l], is_prompt)
    merged = 0.0
    for i, y_i in enumerate((y_a, y_b, y_c, y_d)):
        gate_i = jax.nn.sigmoid(gates[..., i * D_MODEL:(i + 1) * D_MODEL].astype(jnp.float32))
        merged = merged + gate_i * (y_i @ W['w_branch'][l, i])
    mix = merged.astype(x.dtype) @ W['w_out'][l]
    x = _layer_norm(alpha * x + mix, W['ln1_g'][l], W['ln1_b'][l])
    ffn = _moe(x, W['moe_w_group'][l], W['moe_w_expert'][l], W['moe_w_gate'][l], W['moe_w_up'][l], W['moe_w_down'][l])
    x = _layer_norm(alpha * x + ffn, W['ln2_g'][l], W['ln2_b'][l])
    return x, (k, v, ki, rwkv_S, rwkv_shift, gdn_S, gdn_conv, mC, mn, mm)


def kernel(x_prompt, x_sample, cache_k, cache_v, cache_idx_k, page_table, state_rwkv_S, state_rwkv_shift,
           state_gdn_S, state_gdn_conv, state_mlstm_C, state_mlstm_n, state_mlstm_m, meta_tokens, ln_in_g,
           ln_in_b, w_in, rwkv_mu, rwkv_w_rkv, rwkv_w0, rwkv_w1, rwkv_w2, rwkv_a0, rwkv_a1, rwkv_a2, rwkv_g1,
           rwkv_g2, rwkv_k_k, rwkv_k_a, rwkv_r_k, rwkv_lnx_g, rwkv_lnx_b, gdn_conv_w, gdn_A_log, gdn_dt_bias,
           gdn_norm_g, idx_ln_g, idx_ln_b, mlstm_b_i, mlstm_b_f, mlstm_norm_g, w_branch, w_out, ln1_g, ln1_b,
           ln2_g, ln2_b, moe_w_group, moe_w_expert, moe_w_gate, moe_w_up, moe_w_down):
    W = dict(w_in=w_in, rwkv_mu=rwkv_mu, rwkv_w_rkv=rwkv_w_rkv, rwkv_w0=rwkv_w0, rwkv_w1=rwkv_w1,
             rwkv_w2=rwkv_w2, rwkv_a0=rwkv_a0, rwkv_a1=rwkv_a1, rwkv_a2=rwkv_a2, rwkv_g1=rwkv_g1,
             rwkv_g2=rwkv_g2, rwkv_k_k=rwkv_k_k, rwkv_k_a=rwkv_k_a, rwkv_r_k=rwkv_r_k, rwkv_lnx_g=rwkv_lnx_g,
             rwkv_lnx_b=rwkv_lnx_b, gdn_conv_w=gdn_conv_w, gdn_A_log=gdn_A_log, gdn_dt_bias=gdn_dt_bias,
             gdn_norm_g=gdn_norm_g, idx_ln_g=idx_ln_g, idx_ln_b=idx_ln_b, mlstm_b_i=mlstm_b_i,
             mlstm_b_f=mlstm_b_f, mlstm_norm_g=mlstm_norm_g, w_branch=w_branch, w_out=w_out, ln1_g=ln1_g,
             ln1_b=ln1_b, ln2_g=ln2_g, ln2_b=ln2_b, moe_w_group=moe_w_group, moe_w_expert=moe_w_expert,
             moe_w_gate=moe_w_gate, moe_w_up=moe_w_up, moe_w_down=moe_w_down)
    f32 = jnp.float32
    depth = w_in.shape[0]
    alpha = (2 * depth) ** 0.25
    B = x_prompt.shape[0]
    meta = jnp.broadcast_to(meta_tokens.astype(x_prompt.dtype)[None], (B, N_META, D_MODEL))
    hp = _layer_norm(jnp.concatenate([meta, x_prompt], axis=1), ln_in_g, ln_in_b)
    hs = _layer_norm(x_sample, ln_in_g, ln_in_b)
    zero_state = (jnp.zeros((B, RWKV_HEADS, RWKV_HEAD_DIM, RWKV_HEAD_DIM), f32),
                  jnp.zeros((B, W_MIX), f32),
                  jnp.zeros((B, GDN_HEADS, GDN_HEAD_DIM, GDN_HEAD_DIM), f32),
                  jnp.zeros((B, CONV_W - 1, 3 * W_MIX), f32),
                  jnp.zeros((B, MLSTM_HEADS, MLSTM_HEAD_DIM, MLSTM_HEAD_DIM), f32),
                  jnp.zeros((B, MLSTM_HEADS, MLSTM_HEAD_DIM), f32),
                  jnp.zeros((B, MLSTM_HEADS), f32))
    new_p, new_s = [], []
    for l in range(depth):
        hp, sp = _layer(hp, W, l, zero_state, _dsa_prompt, True, alpha)
        dsa_s = functools.partial(_dsa_sample, k_pool=cache_k[l], v_pool=cache_v[l], i_pool=cache_idx_k[l],
                                  page_table=page_table)
        st_s = (state_rwkv_S[l], state_rwkv_shift[l], state_gdn_S[l], state_gdn_conv[l],
                state_mlstm_C[l], state_mlstm_n[l], state_mlstm_m[l])
        hs, ss = _layer(hs, W, l, st_s, dsa_s, False, alpha)
        new_p.append(sp)
        new_s.append(ss)

    def stack(rows, j):
        return jnp.stack([r[j] for r in rows])

    k_p, v_p, ik_p, rS_p, rsh_p, gS_p, gc_p, mC_p, mn_p, mm_p = (stack(new_p, j) for j in range(10))
    k_s, v_s, ik_s, rS_s, rsh_s, gS_s, gc_s, mC_s, mn_s, mm_s = (stack(new_s, j) for j in range(10))
    y_prompt = hp[:, N_META:]
    y_sample = hs
    return (y_prompt, y_sample, k_p, v_p, ik_p, k_s, v_s, ik_s, rS_p, rS_s, rsh_p, rsh_s,
            gS_p, gS_s, gc_p, gc_s, mC_p, mC_s, mn_p, mn_s, mm_p, mm_s)
```

```python
import math, functools
import jax, jax.numpy as jnp
from jax import lax
import numpy as np
from jax.experimental import pallas as pl
from jax.experimental.pallas import tpu as pltpu

D_MODEL = 2048
N_META = 16
N_BRANCH = 4
W_MIX = D_MODEL // 4
RWKV_HEAD_DIM = 64
RWKV_HEADS = W_MIX // RWKV_HEAD_DIM
RWKV_GN_EPS = 64e-5
GDN_HEADS = 4
GDN_HEAD_DIM = W_MIX // GDN_HEADS
CONV_W = 4
GDN_CHUNK = 64
ATT_HEADS = 4
ATT_HEAD_DIM = W_MIX // ATT_HEADS
IDX_HEADS = 8
IDX_DIM = 64
TOPK_MAX = 256
Q_BLOCK = 64
MLSTM_HEADS = 4
MLSTM_HEAD_DIM = W_MIX // MLSTM_HEADS
MLSTM_CHUNK = 64
N_GROUPS = 4
EXPERTS_PER_GROUP = 8
N_EXPERTS = N_GROUPS * EXPERTS_PER_GROUP
TOP_K_INNER = 2
D_EXPERT = D_MODEL // 4
MOE_BLOCK = 128
PAGE_SIZE = 128

VMEM_LIMIT_BYTES = 56 * 1024 * 1024


def _mm_body(x_ref, w_ref, o_ref):
    o_ref[...] = jnp.dot(x_ref[...].astype(jnp.bfloat16), w_ref[...].astype(jnp.bfloat16),
                         preferred_element_type=jnp.float32)


def _mm(x, w, tm=512, tn=1024):
    M, K = x.shape
    N = w.shape[1]
    tm = min(tm, M)
    tn = min(tn, N)
    return pl.pallas_call(
        _mm_body,
        grid=(pl.cdiv(N, tn), pl.cdiv(M, tm)),
        in_specs=[pl.BlockSpec((tm, K), lambda j, i: (i, 0)),
                  pl.BlockSpec((K, tn), lambda j, i: (0, j))],
        out_specs=pl.BlockSpec((tm, tn), lambda j, i: (i, j)),
        out_shape=jax.ShapeDtypeStruct((M, N), jnp.float32),
        compiler_params=pltpu.CompilerParams(dimension_semantics=("arbitrary", "arbitrary"),
                                             vmem_limit_bytes=VMEM_LIMIT_BYTES),
        name="proj_mm",
    )(x, w)


LANES = 128
INT_MIN = -2 ** 31
DSA_TQ = 128


def _round_bf16(x):
    return x.astype(jnp.bfloat16).astype(jnp.float32)


def _dsa_prompt_body(q_ref, qi_ref, wi_ref, kT_ref, v_ref, kiT_ref, o_ref, key_ref, *, k_top):
    tq, t_pad = key_ref.shape
    i = pl.program_id(1)
    row = lax.broadcasted_iota(jnp.int32, (tq, t_pad), 0) + i * tq
    col = lax.broadcasted_iota(jnp.int32, (tq, t_pad), 1)
    valid = col <= row

    wi = _round_bf16(wi_ref[0])
    isc = jnp.zeros((tq, t_pad), jnp.float32)
    for h in range(IDX_HEADS):
        sc = jnp.dot(qi_ref[0, h].astype(jnp.bfloat16), kiT_ref[0],
                     preferred_element_type=jnp.float32) * IDX_DIM ** -0.5
        isc = isc + _round_bf16(jnp.maximum(sc, 0.0)) * wi[:, h:h + 1]

    bits = lax.bitcast_convert_type(isc, jnp.int32)
    key = bits ^ ((bits >> 31) & 0x7FFFFFFF)
    key_ref[...] = jnp.where(valid, key, INT_MIN)

    def count_ge(c):
        return jnp.sum(jnp.where(key_ref[...] >= c, 1.0, 0.0), axis=1, keepdims=True)

    kf = float(k_top)
    thr = jnp.where(count_ge(jnp.zeros((tq, 1), jnp.int32)) >= kf, 0, INT_MIN).astype(jnp.int32)

    def bit_step(j, thr):
        cand = thr | jnp.left_shift(jnp.int32(1), 30 - j)
        return jnp.where(count_ge(cand) >= kf, cand, thr)

    thr = lax.fori_loop(0, 31, bit_step, thr)

    key = key_ref[...]
    gt = key > thr
    eq = key == thr
    need = kf - jnp.sum(jnp.where(gt, 1.0, 0.0), axis=1, keepdims=True)
    upper = (lax.broadcasted_iota(jnp.int32, (LANES, LANES), 0)
             < lax.broadcasted_iota(jnp.int32, (LANES, LANES), 1)).astype(jnp.bfloat16)
    eqf = jnp.where(eq, 1.0, 0.0)
    off = jnp.zeros((tq, 1), jnp.float32)
    pre = []
    for c in range(t_pad // LANES):
        e_c = eqf[:, c * LANES:(c + 1) * LANES]
        pre.append(jnp.dot(e_c.astype(jnp.bfloat16), upper, preferred_element_type=jnp.float32) + off)
        off = off + jnp.sum(e_c, axis=1, keepdims=True)
    prefix = jnp.concatenate(pre, axis=1)
    sel = valid & (gt | (eq & (prefix < need)))

    dist = (row - col).astype(jnp.float32)
    for h in range(ATT_HEADS):
        hs = slice(h * ATT_HEAD_DIM, (h + 1) * ATT_HEAD_DIM)
        slope = 2.0 ** (-8.0 * (h + 1) / ATT_HEADS)
        lg = jnp.dot(q_ref[0, :, hs].astype(jnp.bfloat16), kT_ref[0, hs, :],
                     preferred_element_type=jnp.float32) * ATT_HEAD_DIM ** -0.5 - slope * dist
        lg = jnp.where(sel, lg, -jnp.inf)
        e = jnp.exp(lg - jnp.max(lg, axis=1, keepdims=True))
        p = e * (1.0 / jnp.sum(e, axis=1, keepdims=True))
        o_ref[0, :, hs] = jnp.dot(p.astype(jnp.bfloat16), v_ref[0, :, hs], preferred_element_type=jnp.float32)


def _dsa_prompt_pallas(q, k, v, qi, ki, wi):
    B, T = q.shape[:2]
    k_top = min(TOPK_MAX, T // 4)
    tq = DSA_TQ
    t_pad = pl.cdiv(T, LANES) * LANES
    padt = lambda t: jnp.pad(t, [(0, 0), (0, t_pad - T)] + [(0, 0)] * (t.ndim - 2))
    q2 = padt(q.reshape(B, T, W_MIX))
    kT = jnp.swapaxes(padt(k.reshape(B, T, W_MIX)), 1, 2).astype(jnp.bfloat16)
    v2 = padt(v.reshape(B, T, W_MIX)).astype(jnp.bfloat16)
    qi2 = jnp.swapaxes(padt(qi), 1, 2)
    kiT = jnp.swapaxes(padt(ki), 1, 2).astype(jnp.bfloat16)
    wi2 = padt(wi)
    out = pl.pallas_call(
        functools.partial(_dsa_prompt_body, k_top=k_top),
        grid=(B, t_pad // tq),
        in_specs=[pl.BlockSpec((1, tq, W_MIX), lambda b, i: (b, i, 0)),
                  pl.BlockSpec((1, IDX_HEADS, tq, IDX_DIM), lambda b, i: (b, 0, i, 0)),
                  pl.BlockSpec((1, tq, IDX_HEADS), lambda b, i: (b, i, 0)),
                  pl.BlockSpec((1, W_MIX, t_pad), lambda b, i: (b, 0, 0)),
                  pl.BlockSpec((1, t_pad, W_MIX), lambda b, i: (b, 0, 0)),
                  pl.BlockSpec((1, IDX_DIM, t_pad), lambda b, i: (b, 0, 0))],
        out_specs=pl.BlockSpec((1, tq, W_MIX), lambda b, i: (b, i, 0)),
        out_shape=jax.ShapeDtypeStruct((B, t_pad, W_MIX), jnp.float32),
        scratch_shapes=[pltpu.VMEM((tq, t_pad), jnp.int32)],
        compiler_params=pltpu.CompilerParams(dimension_semantics=("arbitrary", "arbitrary"),
                                             vmem_limit_bytes=VMEM_LIMIT_BYTES),
        name="dsa_prompt",
    )(q2, qi2, wi2, kT, v2, kiT)
    return out[:, :T].reshape(B, T, ATT_HEADS, ATT_HEAD_DIM)


def _layer_norm(x, g, b, eps=1e-5):
    xf = x.astype(jnp.float32)
    mu = jnp.mean(xf, -1, keepdims=True)
    var = jnp.mean(jnp.square(xf - mu), -1, keepdims=True)
    return ((xf - mu) * lax.rsqrt(var + eps) * g + b).astype(x.dtype)


def _rms_norm(x, g, eps=1e-6):
    xf = x.astype(jnp.float32)
    return (xf * lax.rsqrt(jnp.mean(xf * xf, -1, keepdims=True) + eps) * g).astype(x.dtype)


def _l2norm(x, eps=1e-6):
    xf = x.astype(jnp.float32)
    return xf * lax.rsqrt(jnp.sum(xf * xf, -1, keepdims=True) + eps)


def _split_cols(h):
    sizes = [W_MIX,
             3 * W_MIX, GDN_HEADS, GDN_HEADS, W_MIX,
             W_MIX, W_MIX, W_MIX, IDX_HEADS * IDX_DIM, IDX_DIM, IDX_HEADS,
             3 * W_MIX, MLSTM_HEADS, MLSTM_HEADS, W_MIX,
             N_BRANCH * D_MODEL]
    return jnp.split(h, np.cumsum(sizes)[:-1].tolist(), axis=-1)


def _take_rows(rows, idx):
    return jax.vmap(lambda r, i: r[i])(rows, idx)


def _short_conv(u, buf, w):
    T = u.shape[1]
    ext = jnp.concatenate([buf.astype(jnp.float32), u.astype(jnp.float32)], axis=1)
    y = sum(ext[:, j:j + T] * w[j] for j in range(CONV_W))
    return jax.nn.silu(y), ext[:, ext.shape[1] - (CONV_W - 1):]


def _run_chunks(chunk_fn, state, xs, is_prompt, chunk):
    if not is_prompt:
        state, out = chunk_fn(state, xs)
        return out, state
    state, out_meta = chunk_fn(state, tuple(t[:, :, :N_META] for t in xs))
    real = tuple(t[:, :, N_META:] for t in xs)
    n = real[0].shape[2] // chunk

    def to_chunks(t):
        return jnp.moveaxis(t.reshape(t.shape[:2] + (n, chunk) + t.shape[3:]), 2, 0)

    state, out = lax.scan(chunk_fn, state, tuple(to_chunks(t) for t in real))
    out = jnp.moveaxis(out, 0, 2)
    out = out.reshape(out.shape[:2] + (n * chunk,) + out.shape[4:])
    return jnp.concatenate([out_meta, out], axis=2), state


def _rwkv7(u, S0, shift0, mu, w_rkv, w0, w1, w2, a0, a1, a2, g1, g2, k_k, k_a, r_k, lnx_g, lnx_b):
    B, T, _ = u.shape
    uf = u.astype(jnp.float32)
    prev = jnp.concatenate([shift0[:, None].astype(jnp.float32), uf[:, :-1]], axis=1)
    xx = prev - uf
    xr, xw, xk, xv, xa, xg = (uf + xx * mu[j] for j in range(6))
    r = xr @ w_rkv[0]
    k = xk @ w_rkv[1]
    v = xv @ w_rkv[2]
    w = -jax.nn.softplus(-(w0 + jnp.tanh(xw @ w1) @ w2)) - 0.5
    a = jax.nn.sigmoid(a0 + (xa @ a1) @ a2)
    g = jax.nn.sigmoid(xg @ g1) @ g2
    hd = lambda t: t.reshape(B, T, RWKV_HEADS, RWKV_HEAD_DIM)
    kk = _l2norm(hd(k * k_k))
    k = hd(k * (1.0 + (a - 1.0) * k_a))
    r, v, a = hd(r), hd(v), hd(a)
    decay = jnp.exp(-jnp.exp(hd(w)))

    def step(S, inp):
        r_t, d_t, k_t, v_t, kk_t, a_t = inp
        sa = jnp.einsum('bhvk,bhk->bhv', S, kk_t)
        S = (S * d_t[:, :, None, :] - sa[..., None] * (kk_t * a_t)[:, :, None, :]
             + v_t[..., None] * k_t[:, :, None, :])
        return S, jnp.einsum('bhvk,bhk->bhv', S, r_t)

    seq = tuple(jnp.swapaxes(t, 0, 1) for t in (r, decay, k, v, kk, a))
    S, y = lax.scan(step, S0.astype(jnp.float32), seq)
    y = jnp.swapaxes(y, 0, 1)
    mean = jnp.mean(y, -1, keepdims=True)
    var = jnp.mean(jnp.square(y - mean), -1, keepdims=True)
    y = ((y - mean) * lax.rsqrt(var + RWKV_GN_EPS)).reshape(B, T, W_MIX) * lnx_g + lnx_b
    bonus = jnp.sum(r * k * r_k, -1, keepdims=True) * v
    y = (y + bonus.reshape(B, T, W_MIX)) * g
    return y.astype(u.dtype), S, u[:, -1]


def _gdn_chunk(S, xs):
    q, k, v, g, beta = xs
    L = q.shape[2]
    tri = jnp.tril(jnp.ones((L, L), bool))
    stri = jnp.tril(jnp.ones((L, L), bool), -1)
    gc = jnp.cumsum(g, axis=-1)
    dec = jnp.exp(jnp.where(tri, gc[..., :, None] - gc[..., None, :], -jnp.inf))
    kb = k * beta[..., None]
    A = jnp.where(stri, jnp.einsum('bhid,bhjd->bhij', kb, k) * dec, 0.0)
    rhs = jnp.concatenate([v * beta[..., None], kb * jnp.exp(gc)[..., None]], axis=-1)
    sol = lax.linalg.triangular_solve(A + jnp.eye(L, dtype=A.dtype), rhs, left_side=True,
                                      lower=True, unit_diagonal=True)
    u_, w_ = sol[..., :GDN_HEAD_DIM], sol[..., GDN_HEAD_DIM:]
    v_new = u_ - jnp.einsum('bhld,bhde->bhle', w_, S)
    attn = jnp.where(tri, jnp.einsum('bhid,bhjd->bhij', q, k) * dec, 0.0)
    o = (jnp.einsum('bhld,bhde->bhle', q * jnp.exp(gc)[..., None], S)
         + jnp.einsum('bhij,bhje->bhie', attn, v_new))
    g_last = gc[..., -1]
    S = (S * jnp.exp(g_last)[..., None, None]
         + jnp.einsum('bhld,bhle->bhde', k * jnp.exp(g_last[..., None] - gc)[..., None], v_new))
    return S, o


def _gdn(qkv, a_pre, b_pre, z, S0, buf0, conv_w, A_log, dt_bias, norm_g, is_prompt):
    B, T, _ = qkv.shape
    y, new_buf = _short_conv(qkv, buf0, conv_w)
    q, k, v = jnp.split(y, 3, axis=-1)
    hd = lambda t: t.reshape(B, T, GDN_HEADS, GDN_HEAD_DIM).transpose(0, 2, 1, 3)
    q = _l2norm(hd(q)) * GDN_HEAD_DIM ** -0.5
    k = _l2norm(hd(k))
    v = hd(v)
    g = (-jnp.exp(A_log.astype(jnp.float32)) * jax.nn.softplus(a_pre.astype(jnp.float32) + dt_bias)).transpose(0, 2, 1)
    beta = jax.nn.sigmoid(b_pre.astype(jnp.float32)).transpose(0, 2, 1)
    o, S = _run_chunks(_gdn_chunk, S0.astype(jnp.float32), (q, k, v, g, beta), is_prompt, GDN_CHUNK)
    o = _rms_norm(o.transpose(0, 2, 1, 3), norm_g)
    o = o * jax.nn.silu(z.astype(jnp.float32).reshape(B, T, GDN_HEADS, GDN_HEAD_DIM))
    return o.reshape(B, T, W_MIX).astype(qkv.dtype), S, new_buf


def _alibi_slopes():
    return jnp.asarray(2.0 ** (-8.0 * np.arange(1, ATT_HEADS + 1) / ATT_HEADS), jnp.float32)


def _dsa_inputs(c_q, c_k, c_v, c_qi, c_ki, c_wi, ln_g, ln_b):
    B, T, _ = c_q.shape
    hd = lambda t: t.reshape(B, T, ATT_HEADS, ATT_HEAD_DIM)
    qi = c_qi.reshape(B, T, IDX_HEADS, IDX_DIM)
    ki = _layer_norm(c_ki, ln_g, ln_b)
    wi = c_wi * IDX_HEADS ** -0.5
    return hd(c_q), hd(c_k), hd(c_v), qi, ki, wi


def _dsa_block(q, qi, wi, q_pos, ki, gather, k_top):
    S = ki.shape[1]
    s_pos = jnp.arange(S, dtype=jnp.int32)
    sc = jnp.einsum('bqhd,bsd->bqhs', qi.astype(jnp.float32), ki.astype(jnp.float32)) * IDX_DIM ** -0.5
    isc = jnp.einsum('bqhs,bqh->bqs', jax.nn.relu(sc), wi.astype(jnp.float32))
    isc = jnp.where((s_pos[None, :] <= q_pos[:, None])[None], isc, -jnp.inf)
    _, sel = lax.top_k(isc, k_top)
    valid = sel <= q_pos[None, :, None]
    k_sel, v_sel = gather(sel)
    logits = jnp.einsum('bqhd,bqkhd->bqhk', q.astype(jnp.float32), k_sel.astype(jnp.float32)) * ATT_HEAD_DIM ** -0.5
    dist = (q_pos[None, :, None] - sel).astype(jnp.float32)
    logits = logits - _alibi_slopes()[None, None, :, None] * dist[:, :, None, :]
    logits = jnp.where(valid[:, :, None, :], logits, -jnp.inf)
    p = jax.nn.softmax(logits, axis=-1)
    return jnp.einsum('bqhk,bqkhd->bqhd', p, v_sel.astype(jnp.float32)).astype(q.dtype)


def _dsa_prompt(q, k, v, qi, ki, wi):
    B, T = q.shape[:2]
    k_top = min(TOPK_MAX, T // 4)
    n_blk = -(-T // Q_BLOCK)
    t_pad = n_blk * Q_BLOCK

    def blocks(t):
        t = jnp.pad(t, [(0, 0), (0, t_pad - T)] + [(0, 0)] * (t.ndim - 2))
        return jnp.swapaxes(t.reshape((B, n_blk, Q_BLOCK) + t.shape[2:]), 0, 1)

    def gather(sel):
        return _take_rows(k, sel), _take_rows(v, sel)

    def body(args):
        q_b, qi_b, wi_b, pos_b = args
        return _dsa_block(q_b, qi_b, wi_b, pos_b, ki, gather, k_top)

    pos = jnp.arange(t_pad, dtype=jnp.int32).reshape(n_blk, Q_BLOCK)
    out = lax.map(body, (blocks(q), blocks(qi), blocks(wi), pos))
    return jnp.swapaxes(out, 0, 1).reshape(B, t_pad, ATT_HEADS, ATT_HEAD_DIM)[:, :T]


def _dsa_sample(q, k, v, qi, ki, wi, k_pool, v_pool, i_pool, page_table):
    nb, Q = q.shape[:2]
    past_len = page_table.shape[1] * PAGE_SIZE
    ki_past = i_pool[page_table].reshape(nb, past_len, IDX_DIM)
    ki_all = jnp.concatenate([ki_past.astype(jnp.float32), ki.astype(jnp.float32)], axis=1)
    k_top = min(TOPK_MAX, (past_len + Q) // 4)

    def gather(sel):
        in_past = (sel < past_len)[..., None, None]
        sp = jnp.minimum(sel, past_len - 1)
        phys = page_table[jnp.arange(nb)[:, None, None], sp // PAGE_SIZE]
        off = sp % PAGE_SIZE
        sn = jnp.clip(sel - past_len, 0, Q - 1)
        return (jnp.where(in_past, k_pool[phys, off], _take_rows(k, sn)),
                jnp.where(in_past, v_pool[phys, off], _take_rows(v, sn)))

    q_pos = past_len + jnp.arange(Q, dtype=jnp.int32)
    return _dsa_block(q, qi, wi, q_pos, ki_all, gather, k_top)


def _mlstm_chunk(state, xs):
    C, n, m = state
    q, k, v, li, lf = xs
    L = q.shape[2]
    tri = jnp.tril(jnp.ones((L, L), bool))
    b = jnp.cumsum(lf, axis=-1)
    a = li - b
    m_t = b + jnp.maximum(m[..., None], lax.cummax(a, axis=2))
    dmat = jnp.exp(jnp.where(tri, b[..., :, None] + a[..., None, :] - m_t[..., :, None], -jnp.inf))
    inter = jnp.exp(b + m[..., None] - m_t)
    s = jnp.einsum('bhtd,bhsd->bhts', q, k) * dmat
    num = inter[..., None] * jnp.einsum('bhtd,bhde->bhte', q, C) + jnp.einsum('bhts,bhse->bhte', s, v)
    den = inter * jnp.einsum('bhtd,bhd->bht', q, n) + jnp.sum(s, axis=-1)
    h = num / jnp.maximum(jnp.abs(den), jnp.exp(-m_t))[..., None]
    m_new = m_t[..., -1]
    wend = jnp.exp(b[..., -1:] + a - m_new[..., None])
    carry = jnp.exp(b[..., -1] + m - m_new)
    C = carry[..., None, None] * C + jnp.einsum('bhs,bhsd,bhse->bhde', wend, k, v)
    n = carry[..., None] * n + jnp.einsum('bhs,bhsd->bhd', wend, k)
    return (C, n, m_new), h


def _mlstm(qkv, i_pre, f_pre, o_pre, C0, n0, m0, b_i, b_f, norm_g, is_prompt):
    B, T, _ = qkv.shape
    q, k, v = jnp.split(qkv.astype(jnp.float32), 3, axis=-1)
    hd = lambda t: t.reshape(B, T, MLSTM_HEADS, MLSTM_HEAD_DIM).transpose(0, 2, 1, 3)
    q, k, v = hd(q), hd(k) * MLSTM_HEAD_DIM ** -0.5, hd(v)
    li = (i_pre.astype(jnp.float32) + b_i).transpose(0, 2, 1)
    lf = jax.nn.log_sigmoid(f_pre.astype(jnp.float32) + b_f).transpose(0, 2, 1)
    st0 = (C0.astype(jnp.float32), n0.astype(jnp.float32), m0.astype(jnp.float32))
    h, (C, n, m) = _run_chunks(_mlstm_chunk, st0, (q, k, v, li, lf), is_prompt, MLSTM_CHUNK)
    h = _rms_norm(h.transpose(0, 2, 1, 3), norm_g).reshape(B, T, W_MIX)
    h = jax.nn.sigmoid(o_pre.astype(jnp.float32)) * h
    return h.astype(qkv.dtype), C, n, m


def _expert_dispatch(xt, eid, gate, w_g, w_u, w_d):
    N = xt.shape[0]
    A = N * TOP_K_INNER
    e_flat = eid.reshape(-1)
    tok = jnp.arange(A, dtype=jnp.int32) // TOP_K_INNER
    order = jnp.argsort(e_flat)
    e_sorted = e_flat[order]
    counts = jnp.bincount(e_flat, length=N_EXPERTS)
    padded = (counts + MOE_BLOCK - 1) // MOE_BLOCK * MOE_BLOCK
    pad_end = jnp.cumsum(padded)
    pad_start = pad_end - padded
    start = jnp.cumsum(counts) - counts
    dest = pad_start[e_sorted] + jnp.arange(A, dtype=jnp.int32) - start[e_sorted]
    n_blocks = -(-A // MOE_BLOCK) + N_EXPERTS
    slot_tok = jnp.full((n_blocks * MOE_BLOCK,), N, jnp.int32).at[dest].set(tok[order])
    blk_exp = jnp.minimum(jnp.searchsorted(pad_end, jnp.arange(n_blocks) * MOE_BLOCK, side='right'), N_EXPERTS - 1)
    x_pad = jnp.concatenate([xt, jnp.zeros((1, xt.shape[1]), xt.dtype)], axis=0)
    xb = x_pad[slot_tok].reshape(n_blocks, MOE_BLOCK, xt.shape[1])

    def expert(args):
        xe, e = args
        return (jax.nn.silu(xe @ w_g[e]) * (xe @ w_u[e])) @ w_d[e]

    yb = lax.map(expert, (xb, blk_exp)).reshape(n_blocks * MOE_BLOCK, -1)
    y_sorted = yb[dest]
    y_assign = jnp.zeros_like(y_sorted).at[order].set(y_sorted)
    return jnp.einsum('nkd,nk->nd', y_assign.reshape(N, TOP_K_INNER, -1), gate.astype(y_assign.dtype))


def _moe(x, w_group, w_expert, w_g, w_u, w_d):
    shp = x.shape
    xt = x.reshape(-1, D_MODEL)
    N = xt.shape[0]
    xf = xt.astype(jnp.float32)
    pg = jax.nn.softmax(xf @ w_group.astype(jnp.float32), axis=-1)
    p_top, g_sel = lax.top_k(pg, 1)
    le = (xf @ w_expert.astype(jnp.float32)).reshape(N, N_GROUPS, EXPERTS_PER_GROUP)
    le_g = le[jnp.arange(N), g_sel[:, 0]]
    v2, j2 = lax.top_k(le_g, TOP_K_INNER)
    gate = jax.nn.softmax(v2, axis=-1) * p_top
    eid = g_sel * EXPERTS_PER_GROUP + j2
    return _expert_dispatch(xt, eid, gate, w_g, w_u, w_d).reshape(shp).astype(x.dtype)


def _layer(x, W, l, st, dsa_fn, is_prompt, alpha):
    B, T, _ = x.shape
    h = _mm(x.reshape(B * T, D_MODEL), W['w_in'][l]).reshape(B, T, -1)
    (a_u, b_qkv, b_a, b_b, b_z, c_q, c_k, c_v, c_qi, c_ki, c_wi,
     d_qkv, d_i, d_f, d_o, gates) = _split_cols(h)
    rwkv_S0, rwkv_shift0, gdn_S0, gdn_conv0, mC0, mn0, mm0 = st
    y_a, rwkv_S, rwkv_shift = _rwkv7(
        a_u, rwkv_S0, rwkv_shift0, W['rwkv_mu'][l], W['rwkv_w_rkv'][l], W['rwkv_w0'][l], W['rwkv_w1'][l],
        W['rwkv_w2'][l], W['rwkv_a0'][l], W['rwkv_a1'][l], W['rwkv_a2'][l], W['rwkv_g1'][l], W['rwkv_g2'][l],
        W['rwkv_k_k'][l], W['rwkv_k_a'][l], W['rwkv_r_k'][l], W['rwkv_lnx_g'][l], W['rwkv_lnx_b'][l])
    y_b, gdn_S, gdn_conv = _gdn(b_qkv, b_a, b_b, b_z, gdn_S0, gdn_conv0, W['gdn_conv_w'][l], W['gdn_A_log'][l],
                                W['gdn_dt_bias'][l], W['gdn_norm_g'][l], is_prompt)
    q, k, v, qi, ki, wi = _dsa_inputs(c_q, c_k, c_v, c_qi, c_ki, c_wi, W['idx_ln_g'][l], W['idx_ln_b'][l])
    y_c = dsa_fn(q, k, v, qi, ki, wi).reshape(B, T, W_MIX)
    y_d, mC, mn, mm = _mlstm(d_qkv, d_i, d_f, d_o, mC0, mn0, mm0, W['mlstm_b_i'][l], W['mlstm_b_f'][l],
                             W['mlstm_norm_g'][l], is_prompt)
    merged = 0.0
    for i, y_i in enumerate((y_a, y_b, y_c, y_d)):
        gate_i = jax.nn.sigmoid(gates[..., i * D_MODEL:(i + 1) * D_MODEL].astype(jnp.float32))
        merged = merged + gate_i * (y_i @ W['w_branch'][l, i])
    mix = merged.astype(x.dtype) @ W['w_out'][l]
    x = _layer_norm(alpha * x + mix, W['ln1_g'][l], W['ln1_b'][l])
    ffn = _moe(x, W['moe_w_group'][l], W['moe_w_expert'][l], W['moe_w_gate'][l], W['moe_w_up'][l], W['moe_w_down'][l])
    x = _layer_norm(alpha * x + ffn, W['ln2_g'][l], W['ln2_b'][l])
    return x, (k, v, ki, rwkv_S, rwkv_shift, gdn_S, gdn_conv, mC, mn, mm)


def kernel(x_prompt, x_sample, cache_k, cache_v, cache_idx_k, page_table, state_rwkv_S, state_rwkv_shift,
           state_gdn_S, state_gdn_conv, state_mlstm_C, state_mlstm_n, state_mlstm_m, meta_tokens, ln_in_g,
           ln_in_b, w_in, rwkv_mu, rwkv_w_rkv, rwkv_w0, rwkv_w1, rwkv_w2, rwkv_a0, rwkv_a1, rwkv_a2, rwkv_g1,
           rwkv_g2, rwkv_k_k, rwkv_k_a, rwkv_r_k, rwkv_lnx_g, rwkv_lnx_b, gdn_conv_w, gdn_A_log, gdn_dt_bias,
           gdn_norm_g, idx_ln_g, idx_ln_b, mlstm_b_i, mlstm_b_f, mlstm_norm_g, w_branch, w_out, ln1_g, ln1_b,
           ln2_g, ln2_b, moe_w_group, moe_w_expert, moe_w_gate, moe_w_up, moe_w_down):
    W = dict(w_in=w_in, rwkv_mu=rwkv_mu, rwkv_w_rkv=rwkv_w_rkv, rwkv_w0=rwkv_w0, rwkv_w1=rwkv_w1,
             rwkv_w2=rwkv_w2, rwkv_a0=rwkv_a0, rwkv_a1=rwkv_a1, rwkv_a2=rwkv_a2, rwkv_g1=rwkv_g1,
             rwkv_g2=rwkv_g2, rwkv_k_k=rwkv_k_k, rwkv_k_a=rwkv_k_a, rwkv_r_k=rwkv_r_k, rwkv_lnx_g=rwkv_lnx_g,
             rwkv_lnx_b=rwkv_lnx_b, gdn_conv_w=gdn_conv_w, gdn_A_log=gdn_A_log, gdn_dt_bias=gdn_dt_bias,
             gdn_norm_g=gdn_norm_g, idx_ln_g=idx_ln_g, idx_ln_b=idx_ln_b, mlstm_b_i=mlstm_b_i,
             mlstm_b_f=mlstm_b_f, mlstm_norm_g=mlstm_norm_g, w_branch=w_branch, w_out=w_out, ln1_g=ln1_g,
             ln1_b=ln1_b, ln2_g=ln2_g, ln2_b=ln2_b, moe_w_group=moe_w_group, moe_w_expert=moe_w_expert,
             moe_w_gate=moe_w_gate, moe_w_up=moe_w_up, moe_w_down=moe_w_down)
    f32 = jnp.float32
    depth = w_in.shape[0]
    alpha = (2 * depth) ** 0.25
    B = x_prompt.shape[0]
    meta = jnp.broadcast_to(meta_tokens.astype(x_prompt.dtype)[None], (B, N_META, D_MODEL))
    hp = _layer_norm(jnp.concatenate([meta, x_prompt], axis=1), ln_in_g, ln_in_b)
    hs = _layer_norm(x_sample, ln_in_g, ln_in_b)
    zero_state = (jnp.zeros((B, RWKV_HEADS, RWKV_HEAD_DIM, RWKV_HEAD_DIM), f32),
                  jnp.zeros((B, W_MIX), f32),
                  jnp.zeros((B, GDN_HEADS, GDN_HEAD_DIM, GDN_HEAD_DIM), f32),
                  jnp.zeros((B, CONV_W - 1, 3 * W_MIX), f32),
                  jnp.zeros((B, MLSTM_HEADS, MLSTM_HEAD_DIM, MLSTM_HEAD_DIM), f32),
                  jnp.zeros((B, MLSTM_HEADS, MLSTM_HEAD_DIM), f32),
                  jnp.zeros((B, MLSTM_HEADS), f32))
    new_p, new_s = [], []
    for l in range(depth):
        hp, sp = _layer(hp, W, l, zero_state, _dsa_prompt_pallas, True, alpha)
        dsa_s = functools.partial(_dsa_sample, k_pool=cache_k[l], v_pool=cache_v[l], i_pool=cache_idx_k[l],
                                  page_table=page_table)
        st_s = (state_rwkv_S[l], state_rwkv_shift[l], state_gdn_S[l], state_gdn_conv[l],
                state_mlstm_C[l], state_mlstm_n[l], state_mlstm_m[l])
        hs, ss = _layer(hs, W, l, st_s, dsa_s, False, alpha)
        new_p.append(sp)
        new_s.append(ss)

    def stack(rows, j):
        return jnp.stack([r[j] for r in rows])

    k_p, v_p, ik_p, rS_p, rsh_p, gS_p, gc_p, mC_p, mn_p, mm_p = (stack(new_p, j) for j in range(10))
    k_s, v_s, ik_s, rS_s, rsh_s, gS_s, gc_s, mC_s, mn_s, mm_s = (stack(new_s, j) for j in range(10))
    y_prompt = hp[:, N_META:]
    y_sample = hs
    return (y_prompt, y_sample, k_p, v_p, ik_p, k_s, v_s, ik_s, rS_p, rS_s, rsh_p, rsh_s,
            gS_p, gS_s, gc_p, gc_s, mC_p, mC_s, mn_p, mn_s, mm_p, mm_s)
```

```python
import math, functools
import jax, jax.numpy as jnp
from jax import lax
import numpy as np
from jax.experimental import pallas as pl
from jax.experimental.pallas import tpu as pltpu

D_MODEL = 2048
N_META = 16
N_BRANCH = 4
W_MIX = D_MODEL // 4
RWKV_HEAD_DIM = 64
RWKV_HEADS = W_MIX // RWKV_HEAD_DIM
RWKV_GN_EPS = 64e-5
GDN_HEADS = 4
GDN_HEAD_DIM = W_MIX // GDN_HEADS
CONV_W = 4
GDN_CHUNK = 64
ATT_HEADS = 4
ATT_HEAD_DIM = W_MIX // ATT_HEADS
IDX_HEADS = 8
IDX_DIM = 64
TOPK_MAX = 256
Q_BLOCK = 64
MLSTM_HEADS = 4
MLSTM_HEAD_DIM = W_MIX // MLSTM_HEADS
MLSTM_CHUNK = 64
N_GROUPS = 4
EXPERTS_PER_GROUP = 8
N_EXPERTS = N_GROUPS * EXPERTS_PER_GROUP
TOP_K_INNER = 2
D_EXPERT = D_MODEL // 4
MOE_BLOCK = 128
PAGE_SIZE = 128

VMEM_LIMIT_BYTES = 56 * 1024 * 1024


def _mm_body(x_ref, w_ref, o_ref):
    o_ref[...] = jnp.dot(x_ref[...].astype(jnp.bfloat16), w_ref[...].astype(jnp.bfloat16),
                         preferred_element_type=jnp.float32)


def _mm(x, w, tm=512, tn=1024):
    M, K = x.shape
    N = w.shape[1]
    tm = min(tm, M)
    tn = min(tn, N)
    return pl.pallas_call(
        _mm_body,
        grid=(pl.cdiv(N, tn), pl.cdiv(M, tm)),
        in_specs=[pl.BlockSpec((tm, K), lambda j, i: (i, 0)),
                  pl.BlockSpec((K, tn), lambda j, i: (0, j))],
        out_specs=pl.BlockSpec((tm, tn), lambda j, i: (i, j)),
        out_shape=jax.ShapeDtypeStruct((M, N), jnp.float32),
        compiler_params=pltpu.CompilerParams(dimension_semantics=("arbitrary", "arbitrary"),
                                             vmem_limit_bytes=VMEM_LIMIT_BYTES),
        name="proj_mm",
    )(x, w)


LANES = 128
INT_MIN = -2 ** 31
DSA_TQ = 128


def _round_bf16(x):
    return x.astype(jnp.bfloat16).astype(jnp.float32)


def _dsa_prompt_body(q_ref, qi_ref, wi_ref, kT_ref, v_ref, kiT_ref, o_ref, key_ref, *, k_top):
    tq, t_pad = key_ref.shape
    i = pl.program_id(1)
    row = lax.broadcasted_iota(jnp.int32, (tq, t_pad), 0) + i * tq
    col = lax.broadcasted_iota(jnp.int32, (tq, t_pad), 1)
    valid = col <= row

    wi = _round_bf16(wi_ref[0])
    isc = jnp.zeros((tq, t_pad), jnp.float32)
    for h in range(IDX_HEADS):
        sc = jnp.dot(qi_ref[0, h].astype(jnp.bfloat16), kiT_ref[0],
                     preferred_element_type=jnp.float32) * IDX_DIM ** -0.5
        isc = isc + _round_bf16(jnp.maximum(sc, 0.0)) * wi[:, h:h + 1]

    bits = lax.bitcast_convert_type(isc, jnp.int32)
    key = bits ^ ((bits >> 31) & 0x7FFFFFFF)
    key_ref[...] = jnp.where(valid, key, INT_MIN)

    def count_ge(c):
        return jnp.sum(jnp.where(key_ref[...] >= c, 1.0, 0.0), axis=1, keepdims=True)

    kf = float(k_top)
    thr = jnp.where(count_ge(jnp.zeros((tq, 1), jnp.int32)) >= kf, 0, INT_MIN).astype(jnp.int32)

    def bit_step(j, thr):
        cand = thr | jnp.left_shift(jnp.int32(1), 30 - j)
        return jnp.where(count_ge(cand) >= kf, cand, thr)

    thr = lax.fori_loop(0, 31, bit_step, thr)

    key = key_ref[...]
    gt = key > thr
    eq = key == thr
    need = kf - jnp.sum(jnp.where(gt, 1.0, 0.0), axis=1, keepdims=True)
    upper = (lax.broadcasted_iota(jnp.int32, (LANES, LANES), 0)
             < lax.broadcasted_iota(jnp.int32, (LANES, LANES), 1)).astype(jnp.bfloat16)
    eqf = jnp.where(eq, 1.0, 0.0)
    off = jnp.zeros((tq, 1), jnp.float32)
    pre = []
    for c in range(t_pad // LANES):
        e_c = eqf[:, c * LANES:(c + 1) * LANES]
        pre.append(jnp.dot(e_c.astype(jnp.bfloat16), upper, preferred_element_type=jnp.float32) + off)
        off = off + jnp.sum(e_c, axis=1, keepdims=True)
    prefix = jnp.concatenate(pre, axis=1)
    sel = valid & (gt | (eq & (prefix < need)))

    dist = (row - col).astype(jnp.float32)
    for h in range(ATT_HEADS):
        hs = slice(h * ATT_HEAD_DIM, (h + 1) * ATT_HEAD_DIM)
        slope = 2.0 ** (-8.0 * (h + 1) / ATT_HEADS)
        lg = jnp.dot(q_ref[0, :, hs].astype(jnp.bfloat16), kT_ref[0, hs, :],
                     preferred_element_type=jnp.float32) * ATT_HEAD_DIM ** -0.5 - slope * dist
        lg = jnp.where(sel, lg, -jnp.inf)
        e = jnp.exp(lg - jnp.max(lg, axis=1, keepdims=True))
        p = e * (1.0 / jnp.sum(e, axis=1, keepdims=True))
        o_ref[0, :, hs] = jnp.dot(p.astype(jnp.bfloat16), v_ref[0, :, hs], preferred_element_type=jnp.float32)


def _dsa_prompt_pallas(q, k, v, qi, ki, wi):
    B, T = q.shape[:2]
    k_top = min(TOPK_MAX, T // 4)
    tq = DSA_TQ
    t_pad = pl.cdiv(T, LANES) * LANES
    padt = lambda t: jnp.pad(t, [(0, 0), (0, t_pad - T)] + [(0, 0)] * (t.ndim - 2))
    q2 = padt(q.reshape(B, T, W_MIX))
    kT = jnp.swapaxes(padt(k.reshape(B, T, W_MIX)), 1, 2).astype(jnp.bfloat16)
    v2 = padt(v.reshape(B, T, W_MIX)).astype(jnp.bfloat16)
    qi2 = jnp.swapaxes(padt(qi), 1, 2)
    kiT = jnp.swapaxes(padt(ki), 1, 2).astype(jnp.bfloat16)
    wi2 = padt(wi)
    out = pl.pallas_call(
        functools.partial(_dsa_prompt_body, k_top=k_top),
        grid=(B, t_pad // tq),
        in_specs=[pl.BlockSpec((1, tq, W_MIX), lambda b, i: (b, i, 0)),
                  pl.BlockSpec((1, IDX_HEADS, tq, IDX_DIM), lambda b, i: (b, 0, i, 0)),
                  pl.BlockSpec((1, tq, IDX_HEADS), lambda b, i: (b, i, 0)),
                  pl.BlockSpec((1, W_MIX, t_pad), lambda b, i: (b, 0, 0)),
                  pl.BlockSpec((1, t_pad, W_MIX), lambda b, i: (b, 0, 0)),
                  pl.BlockSpec((1, IDX_DIM, t_pad), lambda b, i: (b, 0, 0))],
        out_specs=pl.BlockSpec((1, tq, W_MIX), lambda b, i: (b, i, 0)),
        out_shape=jax.ShapeDtypeStruct((B, t_pad, W_MIX), jnp.float32),
        scratch_shapes=[pltpu.VMEM((tq, t_pad), jnp.int32)],
        compiler_params=pltpu.CompilerParams(dimension_semantics=("arbitrary", "arbitrary"),
                                             vmem_limit_bytes=VMEM_LIMIT_BYTES),
        name="dsa_prompt",
    )(q2, qi2, wi2, kT, v2, kiT)
    return out[:, :T].reshape(B, T, ATT_HEADS, ATT_HEAD_DIM)


def _rwkv_scan_body(r_ref, d_ref, k_ref, v_ref, kk_ref, a_ref, s0_ref, y_ref, sT_ref, S_scr):
    H, tc, N = r_ref.shape[1:]
    c = pl.program_id(1)

    @pl.when(c == 0)
    def _():
        S_scr[...] = s0_ref[0]

    diag = (lax.broadcasted_iota(jnp.int32, (N, N), 0) == lax.broadcasted_iota(jnp.int32, (N, N), 1))

    def step(t, carry):
        for h in range(H):
            row = lambda ref: ref[0, h, pl.ds(t, 1), :]
            S = S_scr[h]
            kk = row(kk_ref)
            sa = jnp.sum(S * kk, axis=1, keepdims=True)
            v_col = jnp.sum(jnp.where(diag, row(v_ref), 0.0), axis=1, keepdims=True)
            S = S * row(d_ref) - sa * (kk * row(a_ref)) + v_col * row(k_ref)
            S_scr[h] = S
            y_col = jnp.sum(S * row(r_ref), axis=1, keepdims=True)
            y_ref[0, h, pl.ds(t, 1), :] = jnp.sum(jnp.where(diag, y_col, 0.0), axis=0, keepdims=True)
        return carry

    lax.fori_loop(0, tc, step, 0)

    @pl.when(c == pl.num_programs(1) - 1)
    def _():
        sT_ref[0] = S_scr[...]


def _rwkv_scan(r, d, k, v, kk, a, S0):
    B, T, H, N = r.shape
    tc = T
    for cand in range(min(T, 384), 7, -1):
        if T % cand == 0 and cand % 8 == 0:
            tc = cand
            break
    hm = lambda t: jnp.swapaxes(t, 1, 2)
    seq_spec = pl.BlockSpec((1, H, tc, N), lambda b, c: (b, 0, c, 0))
    st_spec = pl.BlockSpec((1, H, N, N), lambda b, c: (b, 0, 0, 0))
    y, S = pl.pallas_call(
        _rwkv_scan_body,
        grid=(B, T // tc),
        in_specs=[seq_spec] * 6 + [st_spec],
        out_specs=[seq_spec, st_spec],
        out_shape=[jax.ShapeDtypeStruct((B, H, T, N), jnp.float32),
                   jax.ShapeDtypeStruct((B, H, N, N), jnp.float32)],
        scratch_shapes=[pltpu.VMEM((H, N, N), jnp.float32)],
        compiler_params=pltpu.CompilerParams(dimension_semantics=("arbitrary", "arbitrary"),
                                             vmem_limit_bytes=VMEM_LIMIT_BYTES),
        name="rwkv_scan",
    )(hm(r), hm(d), hm(k), hm(v), hm(kk), hm(a), S0)
    return hm(y), S


def _bf(x):
    return x.astype(jnp.bfloat16)


def _dot(a, b):
    return jnp.dot(_bf(a), _bf(b), preferred_element_type=jnp.float32)


def _dot_nt(a, b):
    return lax.dot_general(_bf(a), _bf(b), (((1,), (1,)), ((), ())), preferred_element_type=jnp.float32)


def _dot3(a, b):
    a_hi, b_hi = _bf(a), _bf(b)
    a_lo, b_lo = _bf(a - a_hi.astype(jnp.float32)), _bf(b - b_hi.astype(jnp.float32))
    f = lambda x, y: jnp.dot(x, y, preferred_element_type=jnp.float32)
    return f(a_hi, b_hi) + (f(a_hi, b_lo) + f(a_lo, b_hi))


def _gdn_chunk_body(q_ref, k_ref, v_ref, g_ref, beta_ref, grow_ref, s0_ref, o_ref, sT_ref, S_scr):
    L = q_ref.shape[1]
    H, D = S_scr.shape[:2]
    c = pl.program_id(1)

    @pl.when(c == 0)
    def _():
        S_scr[...] = s0_ref[0]

    ii = lax.broadcasted_iota(jnp.int32, (L, L), 0)
    jj = lax.broadcasted_iota(jnp.int32, (L, L), 1)
    tri, stri = jj <= ii, jj < ii
    eye = jnp.where(ii == jj, 1.0, 0.0)
    for h in range(H):
        hs = slice(h * D, (h + 1) * D)
        q, k, v = q_ref[0, :, hs], k_ref[0, :, hs], v_ref[0, :, hs]
        g_col, b_col = g_ref[0, :, h:h + 1], beta_ref[0, :, h:h + 1]
        g_row = grow_ref[0, 0, h:h + 1, :]
        gc_col = jnp.sum(jnp.where(tri, g_row, 0.0), axis=1, keepdims=True)
        gc_row = jnp.sum(jnp.where(ii <= jj, g_col, 0.0), axis=0, keepdims=True)
        dec = jnp.exp(jnp.where(tri, gc_col - gc_row, -jnp.inf))
        kb = k * b_col
        A = jnp.where(stri, _dot_nt(kb, k) * dec, 0.0)
        egc = jnp.exp(gc_col)
        rhs = jnp.concatenate([v * b_col, kb * egc], axis=1)
        inv = eye - A
        P = _dot3(A, A)
        n_lvl = int(math.log2(L)) - 1
        for lvl in range(n_lvl):
            inv = inv + _dot3(inv, P)
            if lvl + 1 < n_lvl:
                P = _dot3(P, P)
        sol = _dot3(inv, rhs)
        S = S_scr[h]
        v_new = sol[:, :D] - _dot(sol[:, D:], S)
        attn = jnp.where(tri, _dot_nt(q, k) * dec, 0.0)
        o_ref[0, :, hs] = _dot(q * egc, S) + _dot(attn, v_new)
        g_last = gc_col[L - 1:L, :]
        k_sc = (k * jnp.exp(g_last - gc_col)).T
        S_scr[h] = S * jnp.exp(g_last) + _dot(k_sc, v_new)

    @pl.when(c == pl.num_programs(1) - 1)
    def _():
        sT_ref[0] = S_scr[...]


def _gdn_chunks(q, k, v, g, beta, S0, L):
    B, T, W = q.shape
    H, D = S0.shape[1:3]
    n = T // L
    g_row = jnp.swapaxes(g.reshape(B, n, L, H), 2, 3)
    seq = pl.BlockSpec((1, L, W), lambda b, c: (b, c, 0))
    gate = pl.BlockSpec((1, L, H), lambda b, c: (b, c, 0))
    st = pl.BlockSpec((1, H, D, D), lambda b, c: (b, 0, 0, 0))
    return pl.pallas_call(
        _gdn_chunk_body,
        grid=(B, n),
        in_specs=[seq, seq, seq, gate, gate, pl.BlockSpec((1, 1, H, L), lambda b, c: (b, c, 0, 0)), st],
        out_specs=[seq, st],
        out_shape=[jax.ShapeDtypeStruct((B, T, W), jnp.float32),
                   jax.ShapeDtypeStruct((B, H, D, D), jnp.float32)],
        scratch_shapes=[pltpu.VMEM((H, D, D), jnp.float32)],
        compiler_params=pltpu.CompilerParams(dimension_semantics=("arbitrary", "arbitrary"),
                                             vmem_limit_bytes=VMEM_LIMIT_BYTES),
        name="gdn_chunks",
    )(q, k, v, g, beta, g_row, S0)


def _layer_norm(x, g, b, eps=1e-5):
    xf = x.astype(jnp.float32)
    mu = jnp.mean(xf, -1, keepdims=True)
    var = jnp.mean(jnp.square(xf - mu), -1, keepdims=True)
    return ((xf - mu) * lax.rsqrt(var + eps) * g + b).astype(x.dtype)


def _rms_norm(x, g, eps=1e-6):
    xf = x.astype(jnp.float32)
    return (xf * lax.rsqrt(jnp.mean(xf * xf, -1, keepdims=True) + eps) * g).astype(x.dtype)


def _l2norm(x, eps=1e-6):
    xf = x.astype(jnp.float32)
    return xf * lax.rsqrt(jnp.sum(xf * xf, -1, keepdims=True) + eps)


def _split_cols(h):
    sizes = [W_MIX,
             3 * W_MIX, GDN_HEADS, GDN_HEADS, W_MIX,
             W_MIX, W_MIX, W_MIX, IDX_HEADS * IDX_DIM, IDX_DIM, IDX_HEADS,
             3 * W_MIX, MLSTM_HEADS, MLSTM_HEADS, W_MIX,
             N_BRANCH * D_MODEL]
    return jnp.split(h, np.cumsum(sizes)[:-1].tolist(), axis=-1)


def _take_rows(rows, idx):
    return jax.vmap(lambda r, i: r[i])(rows, idx)


def _short_conv(u, buf, w):
    T = u.shape[1]
    ext = jnp.concatenate([buf.astype(jnp.float32), u.astype(jnp.float32)], axis=1)
    y = sum(ext[:, j:j + T] * w[j] for j in range(CONV_W))
    return jax.nn.silu(y), ext[:, ext.shape[1] - (CONV_W - 1):]


def _run_chunks(chunk_fn, state, xs, is_prompt, chunk):
    if not is_prompt:
        state, out = chunk_fn(state, xs)
        return out, state
    state, out_meta = chunk_fn(state, tuple(t[:, :, :N_META] for t in xs))
    real = tuple(t[:, :, N_META:] for t in xs)
    n = real[0].shape[2] // chunk

    def to_chunks(t):
        return jnp.moveaxis(t.reshape(t.shape[:2] + (n, chunk) + t.shape[3:]), 2, 0)

    state, out = lax.scan(chunk_fn, state, tuple(to_chunks(t) for t in real))
    out = jnp.moveaxis(out, 0, 2)
    out = out.reshape(out.shape[:2] + (n * chunk,) + out.shape[4:])
    return jnp.concatenate([out_meta, out], axis=2), state


def _rwkv7(u, S0, shift0, mu, w_rkv, w0, w1, w2, a0, a1, a2, g1, g2, k_k, k_a, r_k, lnx_g, lnx_b):
    B, T, _ = u.shape
    uf = u.astype(jnp.float32)
    prev = jnp.concatenate([shift0[:, None].astype(jnp.float32), uf[:, :-1]], axis=1)
    xx = prev - uf
    xr, xw, xk, xv, xa, xg = (uf + xx * mu[j] for j in range(6))
    r = xr @ w_rkv[0]
    k = xk @ w_rkv[1]
    v = xv @ w_rkv[2]
    w = -jax.nn.softplus(-(w0 + jnp.tanh(xw @ w1) @ w2)) - 0.5
    a = jax.nn.sigmoid(a0 + (xa @ a1) @ a2)
    g = jax.nn.sigmoid(xg @ g1) @ g2
    hd = lambda t: t.reshape(B, T, RWKV_HEADS, RWKV_HEAD_DIM)
    kk = _l2norm(hd(k * k_k))
    k = hd(k * (1.0 + (a - 1.0) * k_a))
    r, v, a = hd(r), hd(v), hd(a)
    decay = jnp.exp(-jnp.exp(hd(w)))

    y, S = _rwkv_scan(r, decay, k, v, kk, a, S0.astype(jnp.float32))
    mean = jnp.mean(y, -1, keepdims=True)
    var = jnp.mean(jnp.square(y - mean), -1, keepdims=True)
    y = ((y - mean) * lax.rsqrt(var + RWKV_GN_EPS)).reshape(B, T, W_MIX) * lnx_g + lnx_b
    bonus = jnp.sum(r * k * r_k, -1, keepdims=True) * v
    y = (y + bonus.reshape(B, T, W_MIX)) * g
    return y.astype(u.dtype), S, u[:, -1]


def _gdn_chunk(S, xs):
    q, k, v, g, beta = xs
    L = q.shape[2]
    tri = jnp.tril(jnp.ones((L, L), bool))
    stri = jnp.tril(jnp.ones((L, L), bool), -1)
    gc = jnp.cumsum(g, axis=-1)
    dec = jnp.exp(jnp.where(tri, gc[..., :, None] - gc[..., None, :], -jnp.inf))
    kb = k * beta[..., None]
    A = jnp.where(stri, jnp.einsum('bhid,bhjd->bhij', kb, k) * dec, 0.0)
    rhs = jnp.concatenate([v * beta[..., None], kb * jnp.exp(gc)[..., None]], axis=-1)
    sol = lax.linalg.triangular_solve(A + jnp.eye(L, dtype=A.dtype), rhs, left_side=True,
                                      lower=True, unit_diagonal=True)
    u_, w_ = sol[..., :GDN_HEAD_DIM], sol[..., GDN_HEAD_DIM:]
    v_new = u_ - jnp.einsum('bhld,bhde->bhle', w_, S)
    attn = jnp.where(tri, jnp.einsum('bhid,bhjd->bhij', q, k) * dec, 0.0)
    o = (jnp.einsum('bhld,bhde->bhle', q * jnp.exp(gc)[..., None], S)
         + jnp.einsum('bhij,bhje->bhie', attn, v_new))
    g_last = gc[..., -1]
    S = (S * jnp.exp(g_last)[..., None, None]
         + jnp.einsum('bhld,bhle->bhde', k * jnp.exp(g_last[..., None] - gc)[..., None], v_new))
    return S, o


def _gdn(qkv, a_pre, b_pre, z, S0, buf0, conv_w, A_log, dt_bias, norm_g, is_prompt):
    B, T, _ = qkv.shape
    y, new_buf = _short_conv(qkv, buf0, conv_w)
    q, k, v = jnp.split(y, 3, axis=-1)
    hd4 = lambda t: t.reshape(B, T, GDN_HEADS, GDN_HEAD_DIM)
    q = _l2norm(hd4(q)) * GDN_HEAD_DIM ** -0.5
    k = _l2norm(hd4(k))
    g = -jnp.exp(A_log.astype(jnp.float32)) * jax.nn.softplus(a_pre.astype(jnp.float32) + dt_bias)
    beta = jax.nn.sigmoid(b_pre.astype(jnp.float32))
    if is_prompt:
        pad = (-N_META) % GDN_CHUNK
        fp = lambda t: jnp.pad(t.reshape(B, T, -1), [(0, 0), (pad, 0), (0, 0)])
        o, S = _gdn_chunks(fp(q), fp(k), fp(v), fp(g), fp(beta), S0.astype(jnp.float32), GDN_CHUNK)
        o = hd4(o[:, pad:])
    else:
        hd = lambda t: t.transpose(0, 2, 1, 3)
        S, o = _gdn_chunk(S0.astype(jnp.float32), (hd(q), hd(k), hd(hd4(v)), g.transpose(0, 2, 1),
                                                    beta.transpose(0, 2, 1)))
        o = o.transpose(0, 2, 1, 3)
    o = _rms_norm(o, norm_g)
    o = o * jax.nn.silu(z.astype(jnp.float32).reshape(B, T, GDN_HEADS, GDN_HEAD_DIM))
    return o.reshape(B, T, W_MIX).astype(qkv.dtype), S, new_buf


def _alibi_slopes():
    return jnp.asarray(2.0 ** (-8.0 * np.arange(1, ATT_HEADS + 1) / ATT_HEADS), jnp.float32)


def _dsa_inputs(c_q, c_k, c_v, c_qi, c_ki, c_wi, ln_g, ln_b):
    B, T, _ = c_q.shape
    hd = lambda t: t.reshape(B, T, ATT_HEADS, ATT_HEAD_DIM)
    qi = c_qi.reshape(B, T, IDX_HEADS, IDX_DIM)
    ki = _layer_norm(c_ki, ln_g, ln_b)
    wi = c_wi * IDX_HEADS ** -0.5
    return hd(c_q), hd(c_k), hd(c_v), qi, ki, wi


def _dsa_block(q, qi, wi, q_pos, ki, gather, k_top):
    S = ki.shape[1]
    s_pos = jnp.arange(S, dtype=jnp.int32)
    sc = jnp.einsum('bqhd,bsd->bqhs', qi.astype(jnp.float32), ki.astype(jnp.float32)) * IDX_DIM ** -0.5
    isc = jnp.einsum('bqhs,bqh->bqs', jax.nn.relu(sc), wi.astype(jnp.float32))
    isc = jnp.where((s_pos[None, :] <= q_pos[:, None])[None], isc, -jnp.inf)
    _, sel = lax.top_k(isc, k_top)
    valid = sel <= q_pos[None, :, None]
    k_sel, v_sel = gather(sel)
    logits = jnp.einsum('bqhd,bqkhd->bqhk', q.astype(jnp.float32), k_sel.astype(jnp.float32)) * ATT_HEAD_DIM ** -0.5
    dist = (q_pos[None, :, None] - sel).astype(jnp.float32)
    logits = logits - _alibi_slopes()[None, None, :, None] * dist[:, :, None, :]
    logits = jnp.where(valid[:, :, None, :], logits, -jnp.inf)
    p = jax.nn.softmax(logits, axis=-1)
    return jnp.einsum('bqhk,bqkhd->bqhd', p, v_sel.astype(jnp.float32)).astype(q.dtype)


def _dsa_prompt(q, k, v, qi, ki, wi):
    B, T = q.shape[:2]
    k_top = min(TOPK_MAX, T // 4)
    n_blk = -(-T // Q_BLOCK)
    t_pad = n_blk * Q_BLOCK

    def blocks(t):
        t = jnp.pad(t, [(0, 0), (0, t_pad - T)] + [(0, 0)] * (t.ndim - 2))
        return jnp.swapaxes(t.reshape((B, n_blk, Q_BLOCK) + t.shape[2:]), 0, 1)

    def gather(sel):
        return _take_rows(k, sel), _take_rows(v, sel)

    def body(args):
        q_b, qi_b, wi_b, pos_b = args
        return _dsa_block(q_b, qi_b, wi_b, pos_b, ki, gather, k_top)

    pos = jnp.arange(t_pad, dtype=jnp.int32).reshape(n_blk, Q_BLOCK)
    out = lax.map(body, (blocks(q), blocks(qi), blocks(wi), pos))
    return jnp.swapaxes(out, 0, 1).reshape(B, t_pad, ATT_HEADS, ATT_HEAD_DIM)[:, :T]


def _dsa_sample(q, k, v, qi, ki, wi, k_pool, v_pool, i_pool, page_table):
    nb, Q = q.shape[:2]
    past_len = page_table.shape[1] * PAGE_SIZE
    ki_past = i_pool[page_table].reshape(nb, past_len, IDX_DIM)
    ki_all = jnp.concatenate([ki_past.astype(jnp.float32), ki.astype(jnp.float32)], axis=1)
    k_top = min(TOPK_MAX, (past_len + Q) // 4)

    def gather(sel):
        in_past = (sel < past_len)[..., None, None]
        sp = jnp.minimum(sel, past_len - 1)
        phys = page_table[jnp.arange(nb)[:, None, None], sp // PAGE_SIZE]
        off = sp % PAGE_SIZE
        sn = jnp.clip(sel - past_len, 0, Q - 1)
        return (jnp.where(in_past, k_pool[phys, off], _take_rows(k, sn)),
                jnp.where(in_past, v_pool[phys, off], _take_rows(v, sn)))

    q_pos = past_len + jnp.arange(Q, dtype=jnp.int32)
    return _dsa_block(q, qi, wi, q_pos, ki_all, gather, k_top)


def _mlstm_chunk(state, xs):
    C, n, m = state
    q, k, v, li, lf = xs
    L = q.shape[2]
    tri = jnp.tril(jnp.ones((L, L), bool))
    b = jnp.cumsum(lf, axis=-1)
    a = li - b
    m_t = b + jnp.maximum(m[..., None], lax.cummax(a, axis=2))
    dmat = jnp.exp(jnp.where(tri, b[..., :, None] + a[..., None, :] - m_t[..., :, None], -jnp.inf))
    inter = jnp.exp(b + m[..., None] - m_t)
    s = jnp.einsum('bhtd,bhsd->bhts', q, k) * dmat
    num = inter[..., None] * jnp.einsum('bhtd,bhde->bhte', q, C) + jnp.einsum('bhts,bhse->bhte', s, v)
    den = inter * jnp.einsum('bhtd,bhd->bht', q, n) + jnp.sum(s, axis=-1)
    h = num / jnp.maximum(jnp.abs(den), jnp.exp(-m_t))[..., None]
    m_new = m_t[..., -1]
    wend = jnp.exp(b[..., -1:] + a - m_new[..., None])
    carry = jnp.exp(b[..., -1] + m - m_new)
    C = carry[..., None, None] * C + jnp.einsum('bhs,bhsd,bhse->bhde', wend, k, v)
    n = carry[..., None] * n + jnp.einsum('bhs,bhsd->bhd', wend, k)
    return (C, n, m_new), h


def _mlstm(qkv, i_pre, f_pre, o_pre, C0, n0, m0, b_i, b_f, norm_g, is_prompt):
    B, T, _ = qkv.shape
    q, k, v = jnp.split(qkv.astype(jnp.float32), 3, axis=-1)
    hd = lambda t: t.reshape(B, T, MLSTM_HEADS, MLSTM_HEAD_DIM).transpose(0, 2, 1, 3)
    q, k, v = hd(q), hd(k) * MLSTM_HEAD_DIM ** -0.5, hd(v)
    li = (i_pre.astype(jnp.float32) + b_i).transpose(0, 2, 1)
    lf = jax.nn.log_sigmoid(f_pre.astype(jnp.float32) + b_f).transpose(0, 2, 1)
    st0 = (C0.astype(jnp.float32), n0.astype(jnp.float32), m0.astype(jnp.float32))
    h, (C, n, m) = _run_chunks(_mlstm_chunk, st0, (q, k, v, li, lf), is_prompt, MLSTM_CHUNK)
    h = _rms_norm(h.transpose(0, 2, 1, 3), norm_g).reshape(B, T, W_MIX)
    h = jax.nn.sigmoid(o_pre.astype(jnp.float32)) * h
    return h.astype(qkv.dtype), C, n, m


def _expert_dispatch(xt, eid, gate, w_g, w_u, w_d):
    N = xt.shape[0]
    A = N * TOP_K_INNER
    e_flat = eid.reshape(-1)
    tok = jnp.arange(A, dtype=jnp.int32) // TOP_K_INNER
    order = jnp.argsort(e_flat)
    e_sorted = e_flat[order]
    counts = jnp.bincount(e_flat, length=N_EXPERTS)
    padded = (counts + MOE_BLOCK - 1) // MOE_BLOCK * MOE_BLOCK
    pad_end = jnp.cumsum(padded)
    pad_start = pad_end - padded
    start = jnp.cumsum(counts) - counts
    dest = pad_start[e_sorted] + jnp.arange(A, dtype=jnp.int32) - start[e_sorted]
    n_blocks = -(-A // MOE_BLOCK) + N_EXPERTS
    slot_tok = jnp.full((n_blocks * MOE_BLOCK,), N, jnp.int32).at[dest].set(tok[order])
    blk_exp = jnp.minimum(jnp.searchsorted(pad_end, jnp.arange(n_blocks) * MOE_BLOCK, side='right'), N_EXPERTS - 1)
    x_pad = jnp.concatenate([xt, jnp.zeros((1, xt.shape[1]), xt.dtype)], axis=0)
    xb = x_pad[slot_tok].reshape(n_blocks, MOE_BLOCK, xt.shape[1])

    def expert(args):
        xe, e = args
        return (jax.nn.silu(xe @ w_g[e]) * (xe @ w_u[e])) @ w_d[e]

    yb = lax.map(expert, (xb, blk_exp)).reshape(n_blocks * MOE_BLOCK, -1)
    y_sorted = yb[dest]
    y_assign = jnp.zeros_like(y_sorted).at[order].set(y_sorted)
    return jnp.einsum('nkd,nk->nd', y_assign.reshape(N, TOP_K_INNER, -1), gate.astype(y_assign.dtype))


def _moe(x, w_group, w_expert, w_g, w_u, w_d):
    shp = x.shape
    xt = x.reshape(-1, D_MODEL)
    N = xt.shape[0]
    xf = xt.astype(jnp.float32)
    pg = jax.nn.softmax(xf @ w_group.astype(jnp.float32), axis=-1)
    p_top, g_sel = lax.top_k(pg, 1)
    le = (xf @ w_expert.astype(jnp.float32)).reshape(N, N_GROUPS, EXPERTS_PER_GROUP)
    le_g = le[jnp.arange(N), g_sel[:, 0]]
    v2, j2 = lax.top_k(le_g, TOP_K_INNER)
    gate = jax.nn.softmax(v2, axis=-1) * p_top
    eid = g_sel * EXPERTS_PER_GROUP + j2
    return _expert_dispatch(xt, eid, gate, w_g, w_u, w_d).reshape(shp).astype(x.dtype)


def _layer(x, W, l, st, dsa_fn, is_prompt, alpha):
    B, T, _ = x.shape
    h = _mm(x.reshape(B * T, D_MODEL), W['w_in'][l]).reshape(B, T, -1)
    (a_u, b_qkv, b_a, b_b, b_z, c_q, c_k, c_v, c_qi, c_ki, c_wi,
     d_qkv, d_i, d_f, d_o, gates) = _split_cols(h)
    rwkv_S0, rwkv_shift0, gdn_S0, gdn_conv0, mC0, mn0, mm0 = st
    y_a, rwkv_S, rwkv_shift = _rwkv7(
        a_u, rwkv_S0, rwkv_shift0, W['rwkv_mu'][l], W['rwkv_w_rkv'][l], W['rwkv_w0'][l], W['rwkv_w1'][l],
        W['rwkv_w2'][l], W['rwkv_a0'][l], W['rwkv_a1'][l], W['rwkv_a2'][l], W['rwkv_g1'][l], W['rwkv_g2'][l],
        W['rwkv_k_k'][l], W['rwkv_k_a'][l], W['rwkv_r_k'][l], W['rwkv_lnx_g'][l], W['rwkv_lnx_b'][l])
    y_b, gdn_S, gdn_conv = _gdn(b_qkv, b_a, b_b, b_z, gdn_S0, gdn_conv0, W['gdn_conv_w'][l], W['gdn_A_log'][l],
                                W['gdn_dt_bias'][l], W['gdn_norm_g'][l], is_prompt)
    q, k, v, qi, ki, wi = _dsa_inputs(c_q, c_k, c_v, c_qi, c_ki, c_wi, W['idx_ln_g'][l], W['idx_ln_b'][l])
    y_c = dsa_fn(q, k, v, qi, ki, wi).reshape(B, T, W_MIX)
    y_d, mC, mn, mm = _mlstm(d_qkv, d_i, d_f, d_o, mC0, mn0, mm0, W['mlstm_b_i'][l], W['mlstm_b_f'][l],
                             W['mlstm_norm_g'][l], is_prompt)
    merged = 0.0
    for i, y_i in enumerate((y_a, y_b, y_c, y_d)):
        gate_i = jax.nn.sigmoid(gates[..., i * D_MODEL:(i + 1) * D_MODEL].astype(jnp.float32))
        merged = merged + gate_i * (y_i @ W['w_branch'][l, i])
    mix = merged.astype(x.dtype) @ W['w_out'][l]
    x = _layer_norm(alpha * x + mix, W['ln1_g'][l], W['ln1_b'][l])
    ffn = _moe(x, W['moe_w_group'][l], W['moe_w_expert'][l], W['moe_w_gate'][l], W['moe_w_up'][l], W['moe_w_down'][l])
    x = _layer_norm(alpha * x + ffn, W['ln2_g'][l], W['ln2_b'][l])
    return x, (k, v, ki, rwkv_S, rwkv_shift, gdn_S, gdn_conv, mC, mn, mm)


def kernel(x_prompt, x_sample, cache_k, cache_v, cache_idx_k, page_table, state_rwkv_S, state_rwkv_shift,
           state_gdn_S, state_gdn_conv, state_mlstm_C, state_mlstm_n, state_mlstm_m, meta_tokens, ln_in_g,
           ln_in_b, w_in, rwkv_mu, rwkv_w_rkv, rwkv_w0, rwkv_w1, rwkv_w2, rwkv_a0, rwkv_a1, rwkv_a2, rwkv_g1,
           rwkv_g2, rwkv_k_k, rwkv_k_a, rwkv_r_k, rwkv_lnx_g, rwkv_lnx_b, gdn_conv_w, gdn_A_log, gdn_dt_bias,
           gdn_norm_g, idx_ln_g, idx_ln_b, mlstm_b_i, mlstm_b_f, mlstm_norm_g, w_branch, w_out, ln1_g, ln1_b,
           ln2_g, ln2_b, moe_w_group, moe_w_expert, moe_w_gate, moe_w_up, moe_w_down):
    W = dict(w_in=w_in, rwkv_mu=rwkv_mu, rwkv_w_rkv=rwkv_w_rkv, rwkv_w0=rwkv_w0, rwkv_w1=rwkv_w1,
             rwkv_w2=rwkv_w2, rwkv_a0=rwkv_a0, rwkv_a1=rwkv_a1, rwkv_a2=rwkv_a2, rwkv_g1=rwkv_g1,
             rwkv_g2=rwkv_g2, rwkv_k_k=rwkv_k_k, rwkv_k_a=rwkv_k_a, rwkv_r_k=rwkv_r_k, rwkv_lnx_g=rwkv_lnx_g,
             rwkv_lnx_b=rwkv_lnx_b, gdn_conv_w=gdn_conv_w, gdn_A_log=gdn_A_log, gdn_dt_bias=gdn_dt_bias,
             gdn_norm_g=gdn_norm_g, idx_ln_g=idx_ln_g, idx_ln_b=idx_ln_b, mlstm_b_i=mlstm_b_i,
             mlstm_b_f=mlstm_b_f, mlstm_norm_g=mlstm_norm_g, w_branch=w_branch, w_out=w_out, ln1_g=ln1_g,
             ln1_b=ln1_b, ln2_g=ln2_g, ln2_b=ln2_b, moe_w_group=moe_w_group, moe_w_expert=moe_w_expert,
             moe_w_gate=moe_w_gate, moe_w_up=moe_w_up, moe_w_down=moe_w_down)
    f32 = jnp.float32
    depth = w_in.shape[0]
    alpha = (2 * depth) ** 0.25
    B = x_prompt.shape[0]
    meta = jnp.broadcast_to(meta_tokens.astype(x_prompt.dtype)[None], (B, N_META, D_MODEL))
    hp = _layer_norm(jnp.concatenate([meta, x_prompt], axis=1), ln_in_g, ln_in_b)
    hs = _layer_norm(x_sample, ln_in_g, ln_in_b)
    zero_state = (jnp.zeros((B, RWKV_HEADS, RWKV_HEAD_DIM, RWKV_HEAD_DIM), f32),
                  jnp.zeros((B, W_MIX), f32),
                  jnp.zeros((B, GDN_HEADS, GDN_HEAD_DIM, GDN_HEAD_DIM), f32),
                  jnp.zeros((B, CONV_W - 1, 3 * W_MIX), f32),
                  jnp.zeros((B, MLSTM_HEADS, MLSTM_HEAD_DIM, MLSTM_HEAD_DIM), f32),
                  jnp.zeros((B, MLSTM_HEADS, MLSTM_HEAD_DIM), f32),
                  jnp.zeros((B, MLSTM_HEADS), f32))
    new_p, new_s = [], []
    for l in range(depth):
        hp, sp = _layer(hp, W, l, zero_state, _dsa_prompt_pallas, True, alpha)
        dsa_s = functools.partial(_dsa_sample, k_pool=cache_k[l], v_pool=cache_v[l], i_pool=cache_idx_k[l],
                                  page_table=page_table)
        st_s = (state_rwkv_S[l], state_rwkv_shift[l], state_gdn_S[l], state_gdn_conv[l],
                state_mlstm_C[l], state_mlstm_n[l], state_mlstm_m[l])
        hs, ss = _layer(hs, W, l, st_s, dsa_s, False, alpha)
        new_p.append(sp)
        new_s.append(ss)

    def stack(rows, j):
        return jnp.stack([r[j] for r in rows])

    k_p, v_p, ik_p, rS_p, rsh_p, gS_p, gc_p, mC_p, mn_p, mm_p = (stack(new_p, j) for j in range(10))
    k_s, v_s, ik_s, rS_s, rsh_s, gS_s, gc_s, mC_s, mn_s, mm_s = (stack(new_s, j) for j in range(10))
    y_prompt = hp[:, N_META:]
    y_sample = hs
    return (y_prompt, y_sample, k_p, v_p, ik_p, k_s, v_s, ik_s, rS_p, rS_s, rsh_p, rsh_s,
            gS_p, gS_s, gc_p, gc_s, mC_p, mC_s, mn_p, mn_s, mm_p, mm_s)
```

```python
import math, functools
import jax, jax.numpy as jnp
from jax import lax
import numpy as np
from jax.experimental import pallas as pl
from jax.experimental.pallas import tpu as pltpu

D_MODEL = 2048
N_META = 16
N_BRANCH = 4
W_MIX = D_MODEL // 4
RWKV_HEAD_DIM = 64
RWKV_HEADS = W_MIX // RWKV_HEAD_DIM
RWKV_GN_EPS = 64e-5
GDN_HEADS = 4
GDN_HEAD_DIM = W_MIX // GDN_HEADS
CONV_W = 4
GDN_CHUNK = 64
ATT_HEADS = 4
ATT_HEAD_DIM = W_MIX // ATT_HEADS
IDX_HEADS = 8
IDX_DIM = 64
TOPK_MAX = 256
Q_BLOCK = 64
MLSTM_HEADS = 4
MLSTM_HEAD_DIM = W_MIX // MLSTM_HEADS
MLSTM_CHUNK = 64
N_GROUPS = 4
EXPERTS_PER_GROUP = 8
N_EXPERTS = N_GROUPS * EXPERTS_PER_GROUP
TOP_K_INNER = 2
D_EXPERT = D_MODEL // 4
MOE_BLOCK = 128
PAGE_SIZE = 128

VMEM_LIMIT_BYTES = 56 * 1024 * 1024


def _mm_body(x_ref, w_ref, o_ref):
    o_ref[...] = jnp.dot(x_ref[...].astype(jnp.bfloat16), w_ref[...].astype(jnp.bfloat16),
                         preferred_element_type=jnp.float32)


def _mm(x, w, tm=512, tn=1024):
    M, K = x.shape
    N = w.shape[1]
    tm = min(tm, M)
    tn = N if N <= 3 * W_MIX else tn
    return pl.pallas_call(
        _mm_body,
        grid=(pl.cdiv(N, tn), pl.cdiv(M, tm)),
        in_specs=[pl.BlockSpec((tm, K), lambda j, i: (i, 0)),
                  pl.BlockSpec((K, tn), lambda j, i: (0, j))],
        out_specs=pl.BlockSpec((tm, tn), lambda j, i: (i, j)),
        out_shape=jax.ShapeDtypeStruct((M, N), jnp.float32),
        compiler_params=pltpu.CompilerParams(dimension_semantics=("arbitrary", "arbitrary"),
                                             vmem_limit_bytes=VMEM_LIMIT_BYTES),
        name="proj_mm",
    )(x, w)


LANES = 128
INT_MIN = -2 ** 31
DSA_TQ = 128


def _round_bf16(x):
    return x.astype(jnp.bfloat16).astype(jnp.float32)


def _dsa_prompt_body(q_ref, qi_ref, wi_ref, kT_ref, v_ref, kiT_ref, o_ref, key_ref, *, k_top):
    tq, t_pad = key_ref.shape
    i = pl.program_id(1)
    row = lax.broadcasted_iota(jnp.int32, (tq, t_pad), 0) + i * tq
    col = lax.broadcasted_iota(jnp.int32, (tq, t_pad), 1)
    valid = col <= row

    wi = _round_bf16(wi_ref[0])
    isc = jnp.zeros((tq, t_pad), jnp.float32)
    for h in range(IDX_HEADS):
        sc = jnp.dot(qi_ref[0, h].astype(jnp.bfloat16), kiT_ref[0],
                     preferred_element_type=jnp.float32) * IDX_DIM ** -0.5
        isc = isc + _round_bf16(jnp.maximum(sc, 0.0)) * wi[:, h:h + 1]

    bits = lax.bitcast_convert_type(isc, jnp.int32)
    key = bits ^ ((bits >> 31) & 0x7FFFFFFF)
    key_ref[...] = jnp.where(valid, key, INT_MIN)

    def count_ge(c):
        return jnp.sum(jnp.where(key_ref[...] >= c, 1.0, 0.0), axis=1, keepdims=True)

    kf = float(k_top)
    thr = jnp.where(count_ge(jnp.zeros((tq, 1), jnp.int32)) >= kf, 0, INT_MIN).astype(jnp.int32)

    def bit_step(j, thr):
        cand = thr | jnp.left_shift(jnp.int32(1), 30 - j)
        return jnp.where(count_ge(cand) >= kf, cand, thr)

    thr = lax.fori_loop(0, 31, bit_step, thr)

    key = key_ref[...]
    gt = key > thr
    eq = key == thr
    need = kf - jnp.sum(jnp.where(gt, 1.0, 0.0), axis=1, keepdims=True)
    upper = (lax.broadcasted_iota(jnp.int32, (LANES, LANES), 0)
             < lax.broadcasted_iota(jnp.int32, (LANES, LANES), 1)).astype(jnp.bfloat16)
    eqf = jnp.where(eq, 1.0, 0.0)
    off = jnp.zeros((tq, 1), jnp.float32)
    pre = []
    for c in range(t_pad // LANES):
        e_c = eqf[:, c * LANES:(c + 1) * LANES]
        pre.append(jnp.dot(e_c.astype(jnp.bfloat16), upper, preferred_element_type=jnp.float32) + off)
        off = off + jnp.sum(e_c, axis=1, keepdims=True)
    prefix = jnp.concatenate(pre, axis=1)
    sel = valid & (gt | (eq & (prefix < need)))

    dist = (row - col).astype(jnp.float32)
    for h in range(ATT_HEADS):
        hs = slice(h * ATT_HEAD_DIM, (h + 1) * ATT_HEAD_DIM)
        slope = 2.0 ** (-8.0 * (h + 1) / ATT_HEADS)
        lg = jnp.dot(q_ref[0, :, hs].astype(jnp.bfloat16), kT_ref[0, hs, :],
                     preferred_element_type=jnp.float32) * ATT_HEAD_DIM ** -0.5 - slope * dist
        lg = jnp.where(sel, lg, -jnp.inf)
        e = jnp.exp(lg - jnp.max(lg, axis=1, keepdims=True))
        p = e * (1.0 / jnp.sum(e, axis=1, keepdims=True))
        o_ref[0, :, hs] = jnp.dot(p.astype(jnp.bfloat16), v_ref[0, :, hs], preferred_element_type=jnp.float32)


def _dsa_prompt_pallas(q, k, v, qi, ki, wi):
    B, T = q.shape[:2]
    k_top = min(TOPK_MAX, T // 4)
    tq = DSA_TQ
    t_pad = pl.cdiv(T, LANES) * LANES
    padt = lambda t: jnp.pad(t, [(0, 0), (0, t_pad - T)] + [(0, 0)] * (t.ndim - 2))
    q2 = padt(q.reshape(B, T, W_MIX))
    kT = jnp.swapaxes(padt(k.reshape(B, T, W_MIX)), 1, 2).astype(jnp.bfloat16)
    v2 = padt(v.reshape(B, T, W_MIX)).astype(jnp.bfloat16)
    qi2 = jnp.swapaxes(padt(qi), 1, 2)
    kiT = jnp.swapaxes(padt(ki), 1, 2).astype(jnp.bfloat16)
    wi2 = padt(wi)
    out = pl.pallas_call(
        functools.partial(_dsa_prompt_body, k_top=k_top),
        grid=(B, t_pad // tq),
        in_specs=[pl.BlockSpec((1, tq, W_MIX), lambda b, i: (b, i, 0)),
                  pl.BlockSpec((1, IDX_HEADS, tq, IDX_DIM), lambda b, i: (b, 0, i, 0)),
                  pl.BlockSpec((1, tq, IDX_HEADS), lambda b, i: (b, i, 0)),
                  pl.BlockSpec((1, W_MIX, t_pad), lambda b, i: (b, 0, 0)),
                  pl.BlockSpec((1, t_pad, W_MIX), lambda b, i: (b, 0, 0)),
                  pl.BlockSpec((1, IDX_DIM, t_pad), lambda b, i: (b, 0, 0))],
        out_specs=pl.BlockSpec((1, tq, W_MIX), lambda b, i: (b, i, 0)),
        out_shape=jax.ShapeDtypeStruct((B, t_pad, W_MIX), jnp.float32),
        scratch_shapes=[pltpu.VMEM((tq, t_pad), jnp.int32)],
        compiler_params=pltpu.CompilerParams(dimension_semantics=("arbitrary", "arbitrary"),
                                             vmem_limit_bytes=VMEM_LIMIT_BYTES),
        name="dsa_prompt",
    )(q2, qi2, wi2, kT, v2, kiT)
    return out[:, :T].reshape(B, T, ATT_HEADS, ATT_HEAD_DIM)


RWKV_GROUP = 8
RWKV_BB = 2


def _rwkv_scan_body(r_ref, d_ref, k_ref, v_ref, kk_ref, a_ref, s0_ref, y_ref, sT_ref, S_scr, Sb_scr):
    BB, tc = r_ref.shape[:2]
    P, N = S_scr.shape[1:3]
    G = RWKV_GROUP
    chains = [(b, p) for b in range(BB) for p in range(P)]
    c = pl.program_id(1)

    @pl.when(c == 0)
    def _():
        S_scr[...] = s0_ref[...]
        Sb_scr[...] = s0_ref[...].astype(jnp.bfloat16)

    lo_half = lax.broadcasted_iota(jnp.int32, (1, LANES), 1) < N
    sub = lax.broadcasted_iota(jnp.int32, (G, N), 0)
    sub2 = lax.broadcasted_iota(jnp.int32, (G, LANES), 0)
    r_mask = ((sub2 == 0) & lo_half) | ((sub2 == 1) & jnp.logical_not(lo_half))
    ii = lax.broadcasted_iota(jnp.int32, (N, LANES), 0)
    jj = lax.broadcasted_iota(jnp.int32, (N, LANES), 1)
    diagA, diagB = jj == ii, jj == ii + N

    def group(g, carry):
        t0 = pl.multiple_of(g * G, G)
        ld = lambda ref, ch: ref[ch[0], pl.ds(t0, G), ch[1] * LANES:(ch[1] + 1) * LANES]
        kk8 = [ld(kk_ref, ch) for ch in chains]
        beta8 = [kk8[i] * ld(a_ref, ch) for i, ch in enumerate(chains)]
        d8 = [ld(d_ref, ch) for ch in chains]
        k8 = [ld(k_ref, ch) for ch in chains]
        r8 = [ld(r_ref, ch) for ch in chains]
        v8 = [ld(v_ref, ch) for ch in chains]
        kkA = [jnp.where(lo_half, x, 0.0).astype(jnp.bfloat16) for x in kk8]
        kkB = [jnp.where(lo_half, 0.0, x).astype(jnp.bfloat16) for x in kk8]
        yA = [jnp.zeros((G, N), jnp.float32) for _ in chains]
        yB = [jnp.zeros((G, N), jnp.float32) for _ in chains]

        def y_step(t, i):
            b, p = chains[i]
            r2 = jnp.where(r_mask, r8[i][t:t + 1], 0.0).astype(jnp.bfloat16)
            y2 = _dot_nt(r2, Sb_scr[b, p])
            yA[i] = jnp.where(sub == t, jnp.broadcast_to(y2[0:1], (G, N)), yA[i])
            yB[i] = jnp.where(sub == t, jnp.broadcast_to(y2[1:2], (G, N)), yB[i])

        for t in range(G):
            row = lambda x: x[t:t + 1]
            sa_b = []
            for i, (b, p) in enumerate(chains):
                w1 = jnp.concatenate([jnp.broadcast_to(row(kkA[i]), (N, LANES)),
                                      jnp.broadcast_to(row(kkB[i]), (N, LANES))], axis=0)
                sa_b.append(_dot_nt(Sb_scr[b, p], w1))
            if t > 0:
                for i in range(len(chains)):
                    y_step(t - 1, i)
            for i, (b, p) in enumerate(chains):
                vA = jnp.sum(jnp.where(diagA, row(v8[i]), 0.0), axis=1, keepdims=True)
                vB = jnp.sum(jnp.where(diagB, row(v8[i]), 0.0), axis=1, keepdims=True)
                v_b = jnp.where(lo_half, vA, vB)
                S = S_scr[b, p] * row(d8[i]) - sa_b[i] * row(beta8[i]) + v_b * row(k8[i])
                S_scr[b, p] = S
                Sb_scr[b, p] = S.astype(jnp.bfloat16)
        for i, (b, p) in enumerate(chains):
            y_step(G - 1, i)
            y_ref[b, pl.ds(t0, G), p * LANES:(p + 1) * LANES] = jnp.concatenate([yA[i], yB[i]], axis=1)
        return carry

    lax.fori_loop(0, tc // G, group, 0)

    @pl.when(c == pl.num_programs(1) - 1)
    def _():
        sT_ref[...] = S_scr[...]


def _rwkv_scan(r, d, k, v, kk, a, S0):
    B, T, W = r.shape
    H, N = S0.shape[1:3]
    P = H // 2
    G = RWKV_GROUP
    bb = RWKV_BB if B % RWKV_BB == 0 else 1
    t_pad = pl.cdiv(T, G) * G
    tc = t_pad
    for cand in range(min(t_pad, 512), G - 1, -1):
        if t_pad % cand == 0 and cand % G == 0:
            tc = cand
            break
    pad = lambda x, c=0.0: jnp.pad(x, [(0, 0), (0, t_pad - T), (0, 0)], constant_values=c) if t_pad != T else x
    S2 = S0.reshape(B, P, 2, N, N).transpose(0, 1, 3, 2, 4).reshape(B, P, N, 2 * N)
    seq_spec = pl.BlockSpec((bb, tc, W), lambda b, c: (b, c, 0))
    st_spec = pl.BlockSpec((bb, P, N, 2 * N), lambda b, c: (b, 0, 0, 0))
    y, S = pl.pallas_call(
        _rwkv_scan_body,
        grid=(B // bb, t_pad // tc),
        in_specs=[seq_spec] * 6 + [st_spec],
        out_specs=[seq_spec, st_spec],
        out_shape=[jax.ShapeDtypeStruct((B, t_pad, W), jnp.float32),
                   jax.ShapeDtypeStruct((B, P, N, 2 * N), jnp.float32)],
        scratch_shapes=[pltpu.VMEM((bb, P, N, 2 * N), jnp.float32), pltpu.VMEM((bb, P, N, 2 * N), jnp.bfloat16)],
        compiler_params=pltpu.CompilerParams(dimension_semantics=("arbitrary", "arbitrary"),
                                             vmem_limit_bytes=VMEM_LIMIT_BYTES),
        name="rwkv_scan",
    )(pad(r), pad(d, 1.0), pad(k), pad(v), pad(kk), pad(a), S2)
    S = S.reshape(B, P, N, 2, N).transpose(0, 1, 3, 2, 4).reshape(B, H, N, N)
    return y[:, :T], S


def _bf(x):
    return x.astype(jnp.bfloat16)


def _dot(a, b):
    return jnp.dot(_bf(a), _bf(b), preferred_element_type=jnp.float32)


def _dot_nt(a, b):
    return lax.dot_general(_bf(a), _bf(b), (((1,), (1,)), ((), ())), preferred_element_type=jnp.float32)


def _dot3(a, b):
    a_hi, b_hi = _bf(a), _bf(b)
    a_lo, b_lo = _bf(a - a_hi.astype(jnp.float32)), _bf(b - b_hi.astype(jnp.float32))
    f = lambda x, y: jnp.dot(x, y, preferred_element_type=jnp.float32)
    return f(a_hi, b_hi) + (f(a_hi, b_lo) + f(a_lo, b_hi))


def _gdn_chunk_body(q_ref, k_ref, v_ref, g_ref, beta_ref, grow_ref, s0_ref, o_ref, sT_ref, S_scr):
    L = q_ref.shape[1]
    H, D = S_scr.shape[:2]
    c = pl.program_id(1)

    @pl.when(c == 0)
    def _():
        S_scr[...] = s0_ref[0]

    ii = lax.broadcasted_iota(jnp.int32, (L, L), 0)
    jj = lax.broadcasted_iota(jnp.int32, (L, L), 1)
    tri, stri = jj <= ii, jj < ii
    eye = jnp.where(ii == jj, 1.0, 0.0)
    for h in range(H):
        hs = slice(h * D, (h + 1) * D)
        q, k, v = q_ref[0, :, hs], k_ref[0, :, hs], v_ref[0, :, hs]
        g_col, b_col = g_ref[0, :, h:h + 1], beta_ref[0, :, h:h + 1]
        g_row = grow_ref[0, 0, h:h + 1, :]
        gc_col = jnp.sum(jnp.where(tri, g_row, 0.0), axis=1, keepdims=True)
        gc_row = jnp.sum(jnp.where(ii <= jj, g_col, 0.0), axis=0, keepdims=True)
        dec = jnp.exp(jnp.where(tri, gc_col - gc_row, -jnp.inf))
        kb = k * b_col
        A = jnp.where(stri, _dot_nt(kb, k) * dec, 0.0)
        egc = jnp.exp(gc_col)
        rhs = jnp.concatenate([v * b_col, kb * egc], axis=1)
        inv = eye - A
        P = _dot3(A, A)
        n_lvl = int(math.log2(L)) - 1
        for lvl in range(n_lvl):
            inv = inv + _dot3(inv, P)
            if lvl + 1 < n_lvl:
                P = _dot3(P, P)
        sol = _dot3(inv, rhs)
        S = S_scr[h]
        v_new = sol[:, :D] - _dot(sol[:, D:], S)
        attn = jnp.where(tri, _dot_nt(q, k) * dec, 0.0)
        o_ref[0, :, hs] = _dot(q * egc, S) + _dot(attn, v_new)
        g_last = gc_col[L - 1:L, :]
        k_sc = (k * jnp.exp(g_last - gc_col)).T
        S_scr[h] = S * jnp.exp(g_last) + _dot(k_sc, v_new)

    @pl.when(c == pl.num_programs(1) - 1)
    def _():
        sT_ref[0] = S_scr[...]


def _gdn_chunks(q, k, v, g, beta, S0, L):
    B, T, W = q.shape
    H, D = S0.shape[1:3]
    n = T // L
    g_row = jnp.swapaxes(g.reshape(B, n, L, H), 2, 3)
    seq = pl.BlockSpec((1, L, W), lambda b, c: (b, c, 0))
    gate = pl.BlockSpec((1, L, H), lambda b, c: (b, c, 0))
    st = pl.BlockSpec((1, H, D, D), lambda b, c: (b, 0, 0, 0))
    return pl.pallas_call(
        _gdn_chunk_body,
        grid=(B, n),
        in_specs=[seq, seq, seq, gate, gate, pl.BlockSpec((1, 1, H, L), lambda b, c: (b, c, 0, 0)), st],
        out_specs=[seq, st],
        out_shape=[jax.ShapeDtypeStruct((B, T, W), jnp.float32),
                   jax.ShapeDtypeStruct((B, H, D, D), jnp.float32)],
        scratch_shapes=[pltpu.VMEM((H, D, D), jnp.float32)],
        compiler_params=pltpu.CompilerParams(dimension_semantics=("arbitrary", "arbitrary"),
                                             vmem_limit_bytes=VMEM_LIMIT_BYTES),
        name="gdn_chunks",
    )(q, k, v, g, beta, g_row, S0)


def _layer_norm(x, g, b, eps=1e-5):
    xf = x.astype(jnp.float32)
    mu = jnp.mean(xf, -1, keepdims=True)
    var = jnp.mean(jnp.square(xf - mu), -1, keepdims=True)
    return ((xf - mu) * lax.rsqrt(var + eps) * g + b).astype(x.dtype)


def _rms_norm(x, g, eps=1e-6):
    xf = x.astype(jnp.float32)
    return (xf * lax.rsqrt(jnp.mean(xf * xf, -1, keepdims=True) + eps) * g).astype(x.dtype)


def _l2norm(x, eps=1e-6):
    xf = x.astype(jnp.float32)
    return xf * lax.rsqrt(jnp.sum(xf * xf, -1, keepdims=True) + eps)


def _proj_in(x, w_in):
    B, T, _ = x.shape
    sizes = [W_MIX,
             3 * W_MIX, GDN_HEADS, GDN_HEADS, W_MIX,
             W_MIX, W_MIX, W_MIX, IDX_HEADS * IDX_DIM, IDX_DIM, IDX_HEADS,
             3 * W_MIX, MLSTM_HEADS, MLSTM_HEADS, W_MIX,
             N_BRANCH * D_MODEL]
    offs = np.concatenate([[0], np.cumsum(sizes)]).tolist()
    xb = x.reshape(B * T, D_MODEL).astype(jnp.bfloat16)
    narrow = [i for i, s in enumerate(sizes) if s < LANES]
    outs = [None] * len(sizes)
    for i, s in enumerate(sizes):
        if i not in narrow:
            outs[i] = _mm(xb, w_in[:, offs[i]:offs[i + 1]]).reshape(B, T, s)
    small = _mm(xb, jnp.concatenate([w_in[:, offs[i]:offs[i + 1]] for i in narrow], axis=1))
    o = 0
    for i in narrow:
        outs[i] = small[:, o:o + sizes[i]].reshape(B, T, sizes[i])
        o += sizes[i]
    return outs


def _take_rows(rows, idx):
    return jax.vmap(lambda r, i: r[i])(rows, idx)


def _short_conv(u, buf, w):
    T = u.shape[1]
    ext = jnp.concatenate([buf.astype(jnp.float32), u.astype(jnp.float32)], axis=1)
    y = sum(ext[:, j:j + T] * w[j] for j in range(CONV_W))
    return jax.nn.silu(y), ext[:, ext.shape[1] - (CONV_W - 1):]


def _run_chunks(chunk_fn, state, xs, is_prompt, chunk):
    if not is_prompt:
        state, out = chunk_fn(state, xs)
        return out, state
    state, out_meta = chunk_fn(state, tuple(t[:, :, :N_META] for t in xs))
    real = tuple(t[:, :, N_META:] for t in xs)
    n = real[0].shape[2] // chunk

    def to_chunks(t):
        return jnp.moveaxis(t.reshape(t.shape[:2] + (n, chunk) + t.shape[3:]), 2, 0)

    state, out = lax.scan(chunk_fn, state, tuple(to_chunks(t) for t in real))
    out = jnp.moveaxis(out, 0, 2)
    out = out.reshape(out.shape[:2] + (n * chunk,) + out.shape[4:])
    return jnp.concatenate([out_meta, out], axis=2), state


def _rwkv7(u, S0, shift0, mu, w_rkv, w0, w1, w2, a0, a1, a2, g1, g2, k_k, k_a, r_k, lnx_g, lnx_b):
    B, T, _ = u.shape
    uf = u.astype(jnp.float32)
    prev = jnp.concatenate([shift0[:, None].astype(jnp.float32), uf[:, :-1]], axis=1)
    xx = prev - uf
    xr, xw, xk, xv, xa, xg = (uf + xx * mu[j] for j in range(6))
    r = xr @ w_rkv[0]
    k = xk @ w_rkv[1]
    v = xv @ w_rkv[2]
    w = -jax.nn.softplus(-(w0 + jnp.tanh(xw @ w1) @ w2)) - 0.5
    a = jax.nn.sigmoid(a0 + (xa @ a1) @ a2)
    g = jax.nn.sigmoid(xg @ g1) @ g2
    hd = lambda t: t.reshape(B, T, RWKV_HEADS, RWKV_HEAD_DIM)
    kk = _l2norm(hd(k * k_k))
    k = hd(k * (1.0 + (a - 1.0) * k_a))
    r, v, a = hd(r), hd(v), hd(a)
    decay = jnp.exp(-jnp.exp(hd(w)))

    flat = lambda t: t.reshape(B, T, W_MIX)
    y, S = _rwkv_scan(flat(r), flat(decay), flat(k), flat(v), flat(kk), flat(a), S0.astype(jnp.float32))
    y = hd(y)
    mean = jnp.mean(y, -1, keepdims=True)
    var = jnp.mean(jnp.square(y - mean), -1, keepdims=True)
    y = ((y - mean) * lax.rsqrt(var + RWKV_GN_EPS)).reshape(B, T, W_MIX) * lnx_g + lnx_b
    bonus = jnp.sum(r * k * r_k, -1, keepdims=True) * v
    y = (y + bonus.reshape(B, T, W_MIX)) * g
    return y.astype(u.dtype), S, u[:, -1]


def _gdn_chunk(S, xs):
    q, k, v, g, beta = xs
    L = q.shape[2]
    tri = jnp.tril(jnp.ones((L, L), bool))
    stri = jnp.tril(jnp.ones((L, L), bool), -1)
    gc = jnp.cumsum(g, axis=-1)
    dec = jnp.exp(jnp.where(tri, gc[..., :, None] - gc[..., None, :], -jnp.inf))
    kb = k * beta[..., None]
    A = jnp.where(stri, jnp.einsum('bhid,bhjd->bhij', kb, k) * dec, 0.0)
    rhs = jnp.concatenate([v * beta[..., None], kb * jnp.exp(gc)[..., None]], axis=-1)
    sol = lax.linalg.triangular_solve(A + jnp.eye(L, dtype=A.dtype), rhs, left_side=True,
                                      lower=True, unit_diagonal=True)
    u_, w_ = sol[..., :GDN_HEAD_DIM], sol[..., GDN_HEAD_DIM:]
    v_new = u_ - jnp.einsum('bhld,bhde->bhle', w_, S)
    attn = jnp.where(tri, jnp.einsum('bhid,bhjd->bhij', q, k) * dec, 0.0)
    o = (jnp.einsum('bhld,bhde->bhle', q * jnp.exp(gc)[..., None], S)
         + jnp.einsum('bhij,bhje->bhie', attn, v_new))
    g_last = gc[..., -1]
    S = (S * jnp.exp(g_last)[..., None, None]
         + jnp.einsum('bhld,bhle->bhde', k * jnp.exp(g_last[..., None] - gc)[..., None], v_new))
    return S, o


def _gdn(qkv, a_pre, b_pre, z, S0, buf0, conv_w, A_log, dt_bias, norm_g, is_prompt):
    B, T, _ = qkv.shape
    y, new_buf = _short_conv(qkv, buf0, conv_w)
    q, k, v = jnp.split(y, 3, axis=-1)
    hd4 = lambda t: t.reshape(B, T, GDN_HEADS, GDN_HEAD_DIM)
    q = _l2norm(hd4(q)) * GDN_HEAD_DIM ** -0.5
    k = _l2norm(hd4(k))
    g = -jnp.exp(A_log.astype(jnp.float32)) * jax.nn.softplus(a_pre.astype(jnp.float32) + dt_bias)
    beta = jax.nn.sigmoid(b_pre.astype(jnp.float32))
    if is_prompt:
        pad = (-N_META) % GDN_CHUNK
        fp = lambda t: jnp.pad(t.reshape(B, T, -1), [(0, 0), (pad, 0), (0, 0)])
        o, S = _gdn_chunks(fp(q), fp(k), fp(v), fp(g), fp(beta), S0.astype(jnp.float32), GDN_CHUNK)
        o = hd4(o[:, pad:])
    else:
        hd = lambda t: t.transpose(0, 2, 1, 3)
        S, o = _gdn_chunk(S0.astype(jnp.float32), (hd(q), hd(k), hd(hd4(v)), g.transpose(0, 2, 1),
                                                    beta.transpose(0, 2, 1)))
        o = o.transpose(0, 2, 1, 3)
    o = _rms_norm(o, norm_g)
    o = o * jax.nn.silu(z.astype(jnp.float32).reshape(B, T, GDN_HEADS, GDN_HEAD_DIM))
    return o.reshape(B, T, W_MIX).astype(qkv.dtype), S, new_buf


def _alibi_slopes():
    return jnp.asarray(2.0 ** (-8.0 * np.arange(1, ATT_HEADS + 1) / ATT_HEADS), jnp.float32)


def _dsa_inputs(c_q, c_k, c_v, c_qi, c_ki, c_wi, ln_g, ln_b):
    B, T, _ = c_q.shape
    hd = lambda t: t.reshape(B, T, ATT_HEADS, ATT_HEAD_DIM)
    qi = c_qi.reshape(B, T, IDX_HEADS, IDX_DIM)
    ki = _layer_norm(c_ki, ln_g, ln_b)
    wi = c_wi * IDX_HEADS ** -0.5
    return hd(c_q), hd(c_k), hd(c_v), qi, ki, wi


def _dsa_block(q, qi, wi, q_pos, ki, gather, k_top):
    S = ki.shape[1]
    s_pos = jnp.arange(S, dtype=jnp.int32)
    sc = jnp.einsum('bqhd,bsd->bqhs', qi.astype(jnp.float32), ki.astype(jnp.float32)) * IDX_DIM ** -0.5
    isc = jnp.einsum('bqhs,bqh->bqs', jax.nn.relu(sc), wi.astype(jnp.float32))
    isc = jnp.where((s_pos[None, :] <= q_pos[:, None])[None], isc, -jnp.inf)
    _, sel = lax.top_k(isc, k_top)
    valid = sel <= q_pos[None, :, None]
    k_sel, v_sel = gather(sel)
    logits = jnp.einsum('bqhd,bqkhd->bqhk', q.astype(jnp.float32), k_sel.astype(jnp.float32)) * ATT_HEAD_DIM ** -0.5
    dist = (q_pos[None, :, None] - sel).astype(jnp.float32)
    logits = logits - _alibi_slopes()[None, None, :, None] * dist[:, :, None, :]
    logits = jnp.where(valid[:, :, None, :], logits, -jnp.inf)
    p = jax.nn.softmax(logits, axis=-1)
    return jnp.einsum('bqhk,bqkhd->bqhd', p, v_sel.astype(jnp.float32)).astype(q.dtype)


def _dsa_prompt(q, k, v, qi, ki, wi):
    B, T = q.shape[:2]
    k_top = min(TOPK_MAX, T // 4)
    n_blk = -(-T // Q_BLOCK)
    t_pad = n_blk * Q_BLOCK

    def blocks(t):
        t = jnp.pad(t, [(0, 0), (0, t_pad - T)] + [(0, 0)] * (t.ndim - 2))
        return jnp.swapaxes(t.reshape((B, n_blk, Q_BLOCK) + t.shape[2:]), 0, 1)

    def gather(sel):
        return _take_rows(k, sel), _take_rows(v, sel)

    def body(args):
        q_b, qi_b, wi_b, pos_b = args
        return _dsa_block(q_b, qi_b, wi_b, pos_b, ki, gather, k_top)

    pos = jnp.arange(t_pad, dtype=jnp.int32).reshape(n_blk, Q_BLOCK)
    out = lax.map(body, (blocks(q), blocks(qi), blocks(wi), pos))
    return jnp.swapaxes(out, 0, 1).reshape(B, t_pad, ATT_HEADS, ATT_HEAD_DIM)[:, :T]


def _dsa_sample(q, k, v, qi, ki, wi, k_pool, v_pool, i_pool, page_table):
    nb, Q = q.shape[:2]
    past_len = page_table.shape[1] * PAGE_SIZE
    ki_past = i_pool[page_table].reshape(nb, past_len, IDX_DIM)
    ki_all = jnp.concatenate([ki_past.astype(jnp.float32), ki.astype(jnp.float32)], axis=1)
    k_top = min(TOPK_MAX, (past_len + Q) // 4)

    def gather(sel):
        in_past = (sel < past_len)[..., None, None]
        sp = jnp.minimum(sel, past_len - 1)
        phys = page_table[jnp.arange(nb)[:, None, None], sp // PAGE_SIZE]
        off = sp % PAGE_SIZE
        sn = jnp.clip(sel - past_len, 0, Q - 1)
        return (jnp.where(in_past, k_pool[phys, off], _take_rows(k, sn)),
                jnp.where(in_past, v_pool[phys, off], _take_rows(v, sn)))

    q_pos = past_len + jnp.arange(Q, dtype=jnp.int32)
    return _dsa_block(q, qi, wi, q_pos, ki_all, gather, k_top)


def _mlstm_chunk(state, xs):
    C, n, m = state
    q, k, v, li, lf = xs
    L = q.shape[2]
    tri = jnp.tril(jnp.ones((L, L), bool))
    b = jnp.cumsum(lf, axis=-1)
    a = li - b
    m_t = b + jnp.maximum(m[..., None], lax.cummax(a, axis=2))
    dmat = jnp.exp(jnp.where(tri, b[..., :, None] + a[..., None, :] - m_t[..., :, None], -jnp.inf))
    inter = jnp.exp(b + m[..., None] - m_t)
    s = jnp.einsum('bhtd,bhsd->bhts', q, k) * dmat
    num = inter[..., None] * jnp.einsum('bhtd,bhde->bhte', q, C) + jnp.einsum('bhts,bhse->bhte', s, v)
    den = inter * jnp.einsum('bhtd,bhd->bht', q, n) + jnp.sum(s, axis=-1)
    h = num / jnp.maximum(jnp.abs(den), jnp.exp(-m_t))[..., None]
    m_new = m_t[..., -1]
    wend = jnp.exp(b[..., -1:] + a - m_new[..., None])
    carry = jnp.exp(b[..., -1] + m - m_new)
    C = carry[..., None, None] * C + jnp.einsum('bhs,bhsd,bhse->bhde', wend, k, v)
    n = carry[..., None] * n + jnp.einsum('bhs,bhsd->bhd', wend, k)
    return (C, n, m_new), h


def _mlstm(qkv, i_pre, f_pre, o_pre, C0, n0, m0, b_i, b_f, norm_g, is_prompt):
    B, T, _ = qkv.shape
    q, k, v = jnp.split(qkv.astype(jnp.float32), 3, axis=-1)
    hd = lambda t: t.reshape(B, T, MLSTM_HEADS, MLSTM_HEAD_DIM).transpose(0, 2, 1, 3)
    q, k, v = hd(q), hd(k) * MLSTM_HEAD_DIM ** -0.5, hd(v)
    li = (i_pre.astype(jnp.float32) + b_i).transpose(0, 2, 1)
    lf = jax.nn.log_sigmoid(f_pre.astype(jnp.float32) + b_f).transpose(0, 2, 1)
    st0 = (C0.astype(jnp.float32), n0.astype(jnp.float32), m0.astype(jnp.float32))
    h, (C, n, m) = _run_chunks(_mlstm_chunk, st0, (q, k, v, li, lf), is_prompt, MLSTM_CHUNK)
    h = _rms_norm(h.transpose(0, 2, 1, 3), norm_g).reshape(B, T, W_MIX)
    h = jax.nn.sigmoid(o_pre.astype(jnp.float32)) * h
    return h.astype(qkv.dtype), C, n, m


def _moe_experts_body(blk_exp_ref, n_used_ref, x_ref, wg_ref, wu_ref, wd_ref, o_ref, wg_bf, wu_bf, wd_bf):
    i = pl.program_id(0)
    new_expert = (i == 0) | (blk_exp_ref[i] != blk_exp_ref[jnp.maximum(i - 1, 0)])

    @pl.when(new_expert)
    def _():
        wg_bf[...] = wg_ref[0].astype(jnp.bfloat16)
        wu_bf[...] = wu_ref[0].astype(jnp.bfloat16)
        wd_bf[...] = wd_ref[0].astype(jnp.bfloat16)

    @pl.when(i < n_used_ref[0])
    def _():
        xb = x_ref[...].astype(jnp.bfloat16)
        hg = jnp.dot(xb, wg_bf[...], preferred_element_type=jnp.float32)
        hu = jnp.dot(xb, wu_bf[...], preferred_element_type=jnp.float32)
        act = (hg * jax.nn.sigmoid(hg)) * hu
        o_ref[...] = jnp.dot(act.astype(jnp.bfloat16), wd_bf[...], preferred_element_type=jnp.float32)

    @pl.when(i >= n_used_ref[0])
    def _():
        o_ref[...] = jnp.zeros_like(o_ref)


def _moe_experts(xb, blk_exp, n_used, w_g, w_u, w_d):
    R, D = xb.shape
    E, _, F = w_g.shape
    n_blocks = R // MOE_BLOCK
    grid_spec = pltpu.PrefetchScalarGridSpec(
        num_scalar_prefetch=2,
        grid=(n_blocks,),
        in_specs=[pl.BlockSpec((MOE_BLOCK, D), lambda i, be, nu: (i, 0)),
                  pl.BlockSpec((1, D, F), lambda i, be, nu: (be[i], 0, 0)),
                  pl.BlockSpec((1, D, F), lambda i, be, nu: (be[i], 0, 0)),
                  pl.BlockSpec((1, F, D), lambda i, be, nu: (be[i], 0, 0))],
        out_specs=pl.BlockSpec((MOE_BLOCK, D), lambda i, be, nu: (i, 0)),
        scratch_shapes=[pltpu.VMEM((D, F), jnp.bfloat16), pltpu.VMEM((D, F), jnp.bfloat16),
                        pltpu.VMEM((F, D), jnp.bfloat16)])
    return pl.pallas_call(
        _moe_experts_body,
        grid_spec=grid_spec,
        out_shape=jax.ShapeDtypeStruct((R, D), jnp.float32),
        compiler_params=pltpu.CompilerParams(dimension_semantics=("arbitrary",),
                                             vmem_limit_bytes=VMEM_LIMIT_BYTES),
        name="moe_experts",
    )(blk_exp, n_used, xb, w_g, w_u, w_d)


def _expert_dispatch(xt, eid, gate, w_g, w_u, w_d):
    N = xt.shape[0]
    A = N * TOP_K_INNER
    e_flat = eid.reshape(-1)
    tok = jnp.arange(A, dtype=jnp.int32) // TOP_K_INNER
    order = jnp.argsort(e_flat)
    e_sorted = e_flat[order]
    counts = jnp.bincount(e_flat, length=N_EXPERTS)
    padded = (counts + MOE_BLOCK - 1) // MOE_BLOCK * MOE_BLOCK
    pad_end = jnp.cumsum(padded)
    pad_start = pad_end - padded
    start = jnp.cumsum(counts) - counts
    dest = pad_start[e_sorted] + jnp.arange(A, dtype=jnp.int32) - start[e_sorted]
    n_blocks = -(-A // MOE_BLOCK) + N_EXPERTS
    slot_tok = jnp.full((n_blocks * MOE_BLOCK,), N, jnp.int32).at[dest].set(tok[order])
    blk_exp = jnp.minimum(jnp.searchsorted(pad_end, jnp.arange(n_blocks) * MOE_BLOCK, side='right'), N_EXPERTS - 1)
    x_pad = jnp.concatenate([xt, jnp.zeros((1, xt.shape[1]), xt.dtype)], axis=0)
    n_used = (pad_end[-1:] // MOE_BLOCK).astype(jnp.int32)
    yb = _moe_experts(x_pad[slot_tok], blk_exp.astype(jnp.int32), n_used, w_g, w_u, w_d)
    slot = jnp.zeros((A,), jnp.int32).at[order].set(dest)
    y_assign = yb[slot]
    return jnp.einsum('nkd,nk->nd', y_assign.reshape(N, TOP_K_INNER, -1), gate.astype(y_assign.dtype))


def _moe(x, w_group, w_expert, w_g, w_u, w_d):
    shp = x.shape
    xt = x.reshape(-1, D_MODEL)
    N = xt.shape[0]
    xf = xt.astype(jnp.float32)
    pg = jax.nn.softmax(xf @ w_group.astype(jnp.float32), axis=-1)
    p_top, g_sel = lax.top_k(pg, 1)
    le = (xf @ w_expert.astype(jnp.float32)).reshape(N, N_GROUPS, EXPERTS_PER_GROUP)
    le_g = le[jnp.arange(N), g_sel[:, 0]]
    v2, j2 = lax.top_k(le_g, TOP_K_INNER)
    gate = jax.nn.softmax(v2, axis=-1) * p_top
    eid = g_sel * EXPERTS_PER_GROUP + j2
    return _expert_dispatch(xt, eid, gate, w_g, w_u, w_d).reshape(shp).astype(x.dtype)


def _layer(x, W, l, st, dsa_fn, is_prompt, alpha):
    B, T, _ = x.shape
    (a_u, b_qkv, b_a, b_b, b_z, c_q, c_k, c_v, c_qi, c_ki, c_wi,
     d_qkv, d_i, d_f, d_o, gates) = _proj_in(x, W['w_in'][l])
    rwkv_S0, rwkv_shift0, gdn_S0, gdn_conv0, mC0, mn0, mm0 = st
    y_a, rwkv_S, rwkv_shift = _rwkv7(
        a_u, rwkv_S0, rwkv_shift0, W['rwkv_mu'][l], W['rwkv_w_rkv'][l], W['rwkv_w0'][l], W['rwkv_w1'][l],
        W['rwkv_w2'][l], W['rwkv_a0'][l], W['rwkv_a1'][l], W['rwkv_a2'][l], W['rwkv_g1'][l], W['rwkv_g2'][l],
        W['rwkv_k_k'][l], W['rwkv_k_a'][l], W['rwkv_r_k'][l], W['rwkv_lnx_g'][l], W['rwkv_lnx_b'][l])
    y_b, gdn_S, gdn_conv = _gdn(b_qkv, b_a, b_b, b_z, gdn_S0, gdn_conv0, W['gdn_conv_w'][l], W['gdn_A_log'][l],
                                W['gdn_dt_bias'][l], W['gdn_norm_g'][l], is_prompt)
    q, k, v, qi, ki, wi = _dsa_inputs(c_q, c_k, c_v, c_qi, c_ki, c_wi, W['idx_ln_g'][l], W['idx_ln_b'][l])
    y_c = dsa_fn(q, k, v, qi, ki, wi).reshape(B, T, W_MIX)
    y_d, mC, mn, mm = _mlstm(d_qkv, d_i, d_f, d_o, mC0, mn0, mm0, W['mlstm_b_i'][l], W['mlstm_b_f'][l],
                             W['mlstm_norm_g'][l], is_prompt)
    merged = 0.0
    for i, y_i in enumerate((y_a, y_b, y_c, y_d)):
        gate_i = jax.nn.sigmoid(gates[..., i * D_MODEL:(i + 1) * D_MODEL].astype(jnp.float32))
        merged = merged + gate_i * (y_i @ W['w_branch'][l, i])
    mix = merged.astype(x.dtype) @ W['w_out'][l]
    x = _layer_norm(alpha * x + mix, W['ln1_g'][l], W['ln1_b'][l])
    ffn = _moe(x, W['moe_w_group'][l], W['moe_w_expert'][l], W['moe_w_gate'][l], W['moe_w_up'][l], W['moe_w_down'][l])
    x = _layer_norm(alpha * x + ffn, W['ln2_g'][l], W['ln2_b'][l])
    return x, (k, v, ki, rwkv_S, rwkv_shift, gdn_S, gdn_conv, mC, mn, mm)


def kernel(x_prompt, x_sample, cache_k, cache_v, cache_idx_k, page_table, state_rwkv_S, state_rwkv_shift,
           state_gdn_S, state_gdn_conv, state_mlstm_C, state_mlstm_n, state_mlstm_m, meta_tokens, ln_in_g,
           ln_in_b, w_in, rwkv_mu, rwkv_w_rkv, rwkv_w0, rwkv_w1, rwkv_w2, rwkv_a0, rwkv_a1, rwkv_a2, rwkv_g1,
           rwkv_g2, rwkv_k_k, rwkv_k_a, rwkv_r_k, rwkv_lnx_g, rwkv_lnx_b, gdn_conv_w, gdn_A_log, gdn_dt_bias,
           gdn_norm_g, idx_ln_g, idx_ln_b, mlstm_b_i, mlstm_b_f, mlstm_norm_g, w_branch, w_out, ln1_g, ln1_b,
           ln2_g, ln2_b, moe_w_group, moe_w_expert, moe_w_gate, moe_w_up, moe_w_down):
    W = dict(w_in=w_in, rwkv_mu=rwkv_mu, rwkv_w_rkv=rwkv_w_rkv, rwkv_w0=rwkv_w0, rwkv_w1=rwkv_w1,
             rwkv_w2=rwkv_w2, rwkv_a0=rwkv_a0, rwkv_a1=rwkv_a1, rwkv_a2=rwkv_a2, rwkv_g1=rwkv_g1,
             rwkv_g2=rwkv_g2, rwkv_k_k=rwkv_k_k, rwkv_k_a=rwkv_k_a, rwkv_r_k=rwkv_r_k, rwkv_lnx_g=rwkv_lnx_g,
             rwkv_lnx_b=rwkv_lnx_b, gdn_conv_w=gdn_conv_w, gdn_A_log=gdn_A_log, gdn_dt_bias=gdn_dt_bias,
             gdn_norm_g=gdn_norm_g, idx_ln_g=idx_ln_g, idx_ln_b=idx_ln_b, mlstm_b_i=mlstm_b_i,
             mlstm_b_f=mlstm_b_f, mlstm_norm_g=mlstm_norm_g, w_branch=w_branch, w_out=w_out, ln1_g=ln1_g,
             ln1_b=ln1_b, ln2_g=ln2_g, ln2_b=ln2_b, moe_w_group=moe_w_group, moe_w_expert=moe_w_expert,
             moe_w_gate=moe_w_gate, moe_w_up=moe_w_up, moe_w_down=moe_w_down)
    f32 = jnp.float32
    depth = w_in.shape[0]
    alpha = (2 * depth) ** 0.25
    B = x_prompt.shape[0]
    meta = jnp.broadcast_to(meta_tokens.astype(x_prompt.dtype)[None], (B, N_META, D_MODEL))
    hp = _layer_norm(jnp.concatenate([meta, x_prompt], axis=1), ln_in_g, ln_in_b)
    hs = _layer_norm(x_sample, ln_in_g, ln_in_b)
    zero_state = (jnp.zeros((B, RWKV_HEADS, RWKV_HEAD_DIM, RWKV_HEAD_DIM), f32),
                  jnp.zeros((B, W_MIX), f32),
                  jnp.zeros((B, GDN_HEADS, GDN_HEAD_DIM, GDN_HEAD_DIM), f32),
                  jnp.zeros((B, CONV_W - 1, 3 * W_MIX), f32),
                  jnp.zeros((B, MLSTM_HEADS, MLSTM_HEAD_DIM, MLSTM_HEAD_DIM), f32),
                  jnp.zeros((B, MLSTM_HEADS, MLSTM_HEAD_DIM), f32),
                  jnp.zeros((B, MLSTM_HEADS), f32))
    new_p, new_s = [], []
    for l in range(depth):
        hp, sp = _layer(hp, W, l, zero_state, _dsa_prompt_pallas, True, alpha)
        dsa_s = functools.partial(_dsa_sample, k_pool=cache_k[l], v_pool=cache_v[l], i_pool=cache_idx_k[l],
                                  page_table=page_table)
        st_s = (state_rwkv_S[l], state_rwkv_shift[l], state_gdn_S[l], state_gdn_conv[l],
                state_mlstm_C[l], state_mlstm_n[l], state_mlstm_m[l])
        hs, ss = _layer(hs, W, l, st_s, dsa_s, False, alpha)
        new_p.append(sp)
        new_s.append(ss)

    def stack(rows, j):
        return jnp.stack([r[j] for r in rows])

    k_p, v_p, ik_p, rS_p, rsh_p, gS_p, gc_p, mC_p, mn_p, mm_p = (stack(new_p, j) for j in range(10))
    k_s, v_s, ik_s, rS_s, rsh_s, gS_s, gc_s, mC_s, mn_s, mm_s = (stack(new_s, j) for j in range(10))
    y_prompt = hp[:, N_META:]
    y_sample = hs
    return (y_prompt, y_sample, k_p, v_p, ik_p, k_s, v_s, ik_s, rS_p, rS_s, rsh_p, rsh_s,
            gS_p, gS_s, gc_p, gc_s, mC_p, mC_s, mn_p, mn_s, mm_p, mm_s)
```

```python
import math, functools
import jax, jax.numpy as jnp
from jax import lax
import numpy as np
from jax.experimental import pallas as pl
from jax.experimental.pallas import tpu as pltpu

D_MODEL = 2048
N_META = 16
N_BRANCH = 4
W_MIX = D_MODEL // 4
RWKV_HEAD_DIM = 64
RWKV_HEADS = W_MIX // RWKV_HEAD_DIM
RWKV_GN_EPS = 64e-5
GDN_HEADS = 4
GDN_HEAD_DIM = W_MIX // GDN_HEADS
CONV_W = 4
GDN_CHUNK = 64
ATT_HEADS = 4
ATT_HEAD_DIM = W_MIX // ATT_HEADS
IDX_HEADS = 8
IDX_DIM = 64
TOPK_MAX = 256
Q_BLOCK = 64
MLSTM_HEADS = 4
MLSTM_HEAD_DIM = W_MIX // MLSTM_HEADS
MLSTM_CHUNK = 64
N_GROUPS = 4
EXPERTS_PER_GROUP = 8
N_EXPERTS = N_GROUPS * EXPERTS_PER_GROUP
TOP_K_INNER = 2
D_EXPERT = D_MODEL // 4
MOE_BLOCK = 128
PAGE_SIZE = 128

VMEM_LIMIT_BYTES = 56 * 1024 * 1024


def _mm_body(x_ref, w_ref, o_ref):
    o_ref[...] = jnp.dot(x_ref[...].astype(jnp.bfloat16), w_ref[...].astype(jnp.bfloat16),
                         preferred_element_type=jnp.float32)


def _mm(x, w, tm=512, tn=1024):
    M, K = x.shape
    N = w.shape[1]
    tm = min(tm, M)
    tn = N if N <= 3 * W_MIX else tn
    return pl.pallas_call(
        _mm_body,
        grid=(pl.cdiv(N, tn), pl.cdiv(M, tm)),
        in_specs=[pl.BlockSpec((tm, K), lambda j, i: (i, 0)),
                  pl.BlockSpec((K, tn), lambda j, i: (0, j))],
        out_specs=pl.BlockSpec((tm, tn), lambda j, i: (i, j)),
        out_shape=jax.ShapeDtypeStruct((M, N), jnp.float32),
        compiler_params=pltpu.CompilerParams(dimension_semantics=("arbitrary", "arbitrary"),
                                             vmem_limit_bytes=VMEM_LIMIT_BYTES),
        name="proj_mm",
    )(x, w)


LANES = 128
INT_MIN = -2 ** 31
DSA_TQ = 128


def _round_bf16(x):
    return x.astype(jnp.bfloat16).astype(jnp.float32)


def _topk_mask(isc, valid, key_ref, k_top):
    rows, t_pad = key_ref.shape
    bits = lax.bitcast_convert_type(isc, jnp.int32)
    key = bits ^ ((bits >> 31) & 0x7FFFFFFF)
    key_ref[...] = jnp.where(valid, key, INT_MIN)

    def count_ge(c):
        return jnp.sum(jnp.where(key_ref[...] >= c, 1.0, 0.0), axis=1, keepdims=True)

    kf = float(k_top)
    thr = jnp.where(count_ge(jnp.zeros((rows, 1), jnp.int32)) >= kf, 0, INT_MIN).astype(jnp.int32)

    def bit_step(j, thr):
        cand = thr | jnp.left_shift(jnp.int32(1), 30 - j)
        return jnp.where(count_ge(cand) >= kf, cand, thr)

    thr = lax.fori_loop(0, 31, bit_step, thr)

    key = key_ref[...]
    gt = key > thr
    eq = key == thr
    need = kf - jnp.sum(jnp.where(gt, 1.0, 0.0), axis=1, keepdims=True)
    upper = (lax.broadcasted_iota(jnp.int32, (LANES, LANES), 0)
             < lax.broadcasted_iota(jnp.int32, (LANES, LANES), 1)).astype(jnp.bfloat16)
    eqf = jnp.where(eq, 1.0, 0.0)
    off = jnp.zeros((rows, 1), jnp.float32)
    pre = []
    for c in range(t_pad // LANES):
        e_c = eqf[:, c * LANES:(c + 1) * LANES]
        pre.append(jnp.dot(e_c.astype(jnp.bfloat16), upper, preferred_element_type=jnp.float32) + off)
        off = off + jnp.sum(e_c, axis=1, keepdims=True)
    prefix = jnp.concatenate(pre, axis=1)
    return valid & (gt | (eq & (prefix < need)))


def _dsa_prompt_body(q_ref, qi_ref, wi_ref, kT_ref, v_ref, kiT_ref, o_ref, key_ref, *, k_top):
    tq, t_pad = key_ref.shape
    i = pl.program_id(1)
    row = lax.broadcasted_iota(jnp.int32, (tq, t_pad), 0) + i * tq
    col = lax.broadcasted_iota(jnp.int32, (tq, t_pad), 1)
    valid = col <= row

    wi = _round_bf16(wi_ref[0])
    isc = jnp.zeros((tq, t_pad), jnp.float32)
    for h in range(IDX_HEADS):
        sc = jnp.dot(qi_ref[0, h].astype(jnp.bfloat16), kiT_ref[0],
                     preferred_element_type=jnp.float32) * IDX_DIM ** -0.5
        isc = isc + _round_bf16(jnp.maximum(sc, 0.0)) * wi[:, h:h + 1]

    sel = _topk_mask(isc, valid, key_ref, k_top)

    dist = (row - col).astype(jnp.float32)
    for h in range(ATT_HEADS):
        hs = slice(h * ATT_HEAD_DIM, (h + 1) * ATT_HEAD_DIM)
        slope = 2.0 ** (-8.0 * (h + 1) / ATT_HEADS)
        lg = jnp.dot(q_ref[0, :, hs].astype(jnp.bfloat16), kT_ref[0, hs, :],
                     preferred_element_type=jnp.float32) * ATT_HEAD_DIM ** -0.5 - slope * dist
        lg = jnp.where(sel, lg, -jnp.inf)
        e = jnp.exp(lg - jnp.max(lg, axis=1, keepdims=True))
        p = e * (1.0 / jnp.sum(e, axis=1, keepdims=True))
        o_ref[0, :, hs] = jnp.dot(p.astype(jnp.bfloat16), v_ref[0, :, hs], preferred_element_type=jnp.float32)


def _dsa_prompt_pallas(q, k, v, qi, ki, wi):
    B, T = q.shape[:2]
    k_top = min(TOPK_MAX, T // 4)
    tq = DSA_TQ
    t_pad = pl.cdiv(T, LANES) * LANES
    padt = lambda t: jnp.pad(t, [(0, 0), (0, t_pad - T)] + [(0, 0)] * (t.ndim - 2))
    q2 = padt(q.reshape(B, T, W_MIX))
    kT = jnp.swapaxes(padt(k.reshape(B, T, W_MIX)), 1, 2).astype(jnp.bfloat16)
    v2 = padt(v.reshape(B, T, W_MIX)).astype(jnp.bfloat16)
    qi2 = jnp.swapaxes(padt(qi), 1, 2)
    kiT = jnp.swapaxes(padt(ki), 1, 2).astype(jnp.bfloat16)
    wi2 = padt(wi)
    out = pl.pallas_call(
        functools.partial(_dsa_prompt_body, k_top=k_top),
        grid=(B, t_pad // tq),
        in_specs=[pl.BlockSpec((1, tq, W_MIX), lambda b, i: (b, i, 0)),
                  pl.BlockSpec((1, IDX_HEADS, tq, IDX_DIM), lambda b, i: (b, 0, i, 0)),
                  pl.BlockSpec((1, tq, IDX_HEADS), lambda b, i: (b, i, 0)),
                  pl.BlockSpec((1, W_MIX, t_pad), lambda b, i: (b, 0, 0)),
                  pl.BlockSpec((1, t_pad, W_MIX), lambda b, i: (b, 0, 0)),
                  pl.BlockSpec((1, IDX_DIM, t_pad), lambda b, i: (b, 0, 0))],
        out_specs=pl.BlockSpec((1, tq, W_MIX), lambda b, i: (b, i, 0)),
        out_shape=jax.ShapeDtypeStruct((B, t_pad, W_MIX), jnp.float32),
        scratch_shapes=[pltpu.VMEM((tq, t_pad), jnp.int32)],
        compiler_params=pltpu.CompilerParams(dimension_semantics=("arbitrary", "arbitrary"),
                                             vmem_limit_bytes=VMEM_LIMIT_BYTES),
        name="dsa_prompt",
    )(q2, qi2, wi2, kT, v2, kiT)
    return out[:, :T].reshape(B, T, ATT_HEADS, ATT_HEAD_DIM)


SUBLANES = 8


def _dsa_sample_body(pt_ref, q_ref, kn_ref, vn_ref, qi_ref, kin_ref, wi_ref, *rest, n_pages, k_top):
    ki_pages, k_pages, v_pages = rest[:n_pages], rest[n_pages:2 * n_pages], rest[2 * n_pages:3 * n_pages]
    o_ref, key_ref = rest[3 * n_pages], rest[3 * n_pages + 1]
    R = SUBLANES
    t_pad = key_ref.shape[1]
    past_len = n_pages * PAGE_SIZE
    first_row = lax.broadcasted_iota(jnp.int32, (PAGE_SIZE, 1), 0) == 0
    new_page = lambda row: jnp.where(first_row, row, 0.0).astype(jnp.bfloat16)

    qi = qi_ref[0].astype(jnp.bfloat16)
    wi = _round_bf16(wi_ref[0])

    def idx_tile(ki_page):
        sc = _dot_nt(qi, ki_page) * IDX_DIM ** -0.5
        return jnp.sum(_round_bf16(jnp.maximum(sc, 0.0)) * wi, axis=0, keepdims=True)

    isc = jnp.concatenate([idx_tile(ki_pages[j][0, 0]) for j in range(n_pages)]
                          + [idx_tile(new_page(kin_ref[0]))], axis=1)
    col = lax.broadcasted_iota(jnp.int32, (R, t_pad), 1)
    valid = col <= past_len
    sel = _topk_mask(jnp.broadcast_to(isc, (R, t_pad)), valid, key_ref, k_top)

    head = lax.broadcasted_iota(jnp.int32, (R, W_MIX), 0)
    own = lax.broadcasted_iota(jnp.int32, (R, W_MIX), 1) // ATT_HEAD_DIM == head
    q4 = jnp.where(own, q_ref[0], 0.0).astype(jnp.bfloat16)
    hcol = lax.broadcasted_iota(jnp.int32, (R, 1), 0)
    slope = sum(jnp.where(hcol == h, 2.0 ** (-8.0 * (h + 1) / ATT_HEADS), 0.0) for h in range(ATT_HEADS))
    k_all = [k_pages[j][0, 0] for j in range(n_pages)] + [new_page(kn_ref[0])]
    lg = jnp.concatenate([_dot_nt(q4, kp) for kp in k_all], axis=1) * ATT_HEAD_DIM ** -0.5
    lg = lg - slope * (past_len - col).astype(jnp.float32)
    lg = jnp.where(sel, lg, -jnp.inf)
    e = jnp.exp(lg - jnp.max(lg, axis=1, keepdims=True))
    p = (e * (1.0 / jnp.sum(e, axis=1, keepdims=True))).astype(jnp.bfloat16)
    v_all = [v_pages[j][0, 0] for j in range(n_pages)] + [new_page(vn_ref[0])]
    acc = jnp.zeros((R, W_MIX), jnp.float32)
    for j, vp in enumerate(v_all):
        acc = acc + _dot(p[:, j * PAGE_SIZE:(j + 1) * PAGE_SIZE], vp)
    o_ref[0] = jnp.sum(jnp.where(own, acc, 0.0), axis=0, keepdims=True)


def _dsa_sample_pallas(q, k, v, qi, ki, wi, cache_k, cache_v, cache_idx_k, page_table, l):
    B, Q = q.shape[:2]
    assert Q == 1 and PAGE_SIZE == LANES
    n_pages = page_table.shape[1]
    n_pool = cache_k.shape[1]
    k_top = min(TOPK_MAX, (n_pages * PAGE_SIZE + Q) // 4)
    t_pad = (n_pages + 1) * PAGE_SIZE
    row_spec = lambda w: pl.BlockSpec((1, 1, w), lambda b, pt: (b, 0, 0))
    page_spec = lambda w, j: pl.BlockSpec((1, 1, PAGE_SIZE, w), lambda b, pt: (l, pt[b, j], 0, 0))
    grid_spec = pltpu.PrefetchScalarGridSpec(
        num_scalar_prefetch=1,
        grid=(B,),
        in_specs=[row_spec(W_MIX), row_spec(W_MIX), row_spec(W_MIX),
                  pl.BlockSpec((1, IDX_HEADS, IDX_DIM), lambda b, pt: (b, 0, 0)),
                  row_spec(IDX_DIM),
                  pl.BlockSpec((1, IDX_HEADS, 1), lambda b, pt: (b, 0, 0))]
                 + [page_spec(IDX_DIM, j) for j in range(n_pages)]
                 + [page_spec(W_MIX, j) for j in range(n_pages)] * 2,
        out_specs=row_spec(W_MIX),
        scratch_shapes=[pltpu.VMEM((SUBLANES, t_pad), jnp.int32)])
    pool = lambda c: c.reshape(c.shape[0], n_pool, PAGE_SIZE, -1)
    out = pl.pallas_call(
        functools.partial(_dsa_sample_body, n_pages=n_pages, k_top=k_top),
        grid_spec=grid_spec,
        out_shape=jax.ShapeDtypeStruct((B, 1, W_MIX), jnp.float32),
        compiler_params=pltpu.CompilerParams(dimension_semantics=("arbitrary",),
                                             vmem_limit_bytes=VMEM_LIMIT_BYTES),
        name="dsa_sample",
    )(page_table, q.reshape(B, 1, W_MIX), k.reshape(B, 1, W_MIX), v.reshape(B, 1, W_MIX),
      qi.reshape(B, IDX_HEADS, IDX_DIM), ki, wi.reshape(B, IDX_HEADS, 1),
      *([pool(cache_idx_k)] * n_pages), *([pool(cache_k)] * n_pages), *([pool(cache_v)] * n_pages))
    return out.reshape(B, 1, ATT_HEADS, ATT_HEAD_DIM)


RWKV_GROUP = 8
RWKV_BB = 2


def _rwkv_scan_body(r_ref, d_ref, k_ref, v_ref, kk_ref, a_ref, s0_ref, y_ref, sT_ref, S_scr, Sb_scr):
    BB, tc = r_ref.shape[:2]
    P, N = S_scr.shape[1:3]
    G = RWKV_GROUP
    chains = [(b, p) for b in range(BB) for p in range(P)]
    c = pl.program_id(1)

    @pl.when(c == 0)
    def _():
        S_scr[...] = s0_ref[...]
        Sb_scr[...] = s0_ref[...].astype(jnp.bfloat16)

    lo_half = lax.broadcasted_iota(jnp.int32, (1, LANES), 1) < N
    sub = lax.broadcasted_iota(jnp.int32, (G, N), 0)
    sub2 = lax.broadcasted_iota(jnp.int32, (G, LANES), 0)
    r_mask = ((sub2 == 0) & lo_half) | ((sub2 == 1) & jnp.logical_not(lo_half))
    ii = lax.broadcasted_iota(jnp.int32, (N, LANES), 0)
    jj = lax.broadcasted_iota(jnp.int32, (N, LANES), 1)
    diagA, diagB = jj == ii, jj == ii + N

    def group(g, carry):
        t0 = pl.multiple_of(g * G, G)
        ld = lambda ref, ch: ref[ch[0], pl.ds(t0, G), ch[1] * LANES:(ch[1] + 1) * LANES]
        kk8 = [ld(kk_ref, ch) for ch in chains]
        beta8 = [kk8[i] * ld(a_ref, ch) for i, ch in enumerate(chains)]
        d8 = [ld(d_ref, ch) for ch in chains]
        k8 = [ld(k_ref, ch) for ch in chains]
        r8 = [ld(r_ref, ch) for ch in chains]
        v8 = [ld(v_ref, ch) for ch in chains]
        kkA = [jnp.where(lo_half, x, 0.0).astype(jnp.bfloat16) for x in kk8]
        kkB = [jnp.where(lo_half, 0.0, x).astype(jnp.bfloat16) for x in kk8]
        yA = [jnp.zeros((G, N), jnp.float32) for _ in chains]
        yB = [jnp.zeros((G, N), jnp.float32) for _ in chains]

        def y_step(t, i):
            b, p = chains[i]
            r2 = jnp.where(r_mask, r8[i][t:t + 1], 0.0).astype(jnp.bfloat16)
            y2 = _dot_nt(r2, Sb_scr[b, p])
            yA[i] = jnp.where(sub == t, jnp.broadcast_to(y2[0:1], (G, N)), yA[i])
            yB[i] = jnp.where(sub == t, jnp.broadcast_to(y2[1:2], (G, N)), yB[i])

        for t in range(G):
            row = lambda x: x[t:t + 1]
            sa_b = []
            for i, (b, p) in enumerate(chains):
                w1 = jnp.concatenate([jnp.broadcast_to(row(kkA[i]), (N, LANES)),
                                      jnp.broadcast_to(row(kkB[i]), (N, LANES))], axis=0)
                sa_b.append(_dot_nt(Sb_scr[b, p], w1))
            if t > 0:
                for i in range(len(chains)):
                    y_step(t - 1, i)
            for i, (b, p) in enumerate(chains):
                vA = jnp.sum(jnp.where(diagA, row(v8[i]), 0.0), axis=1, keepdims=True)
                vB = jnp.sum(jnp.where(diagB, row(v8[i]), 0.0), axis=1, keepdims=True)
                v_b = jnp.where(lo_half, vA, vB)
                S = S_scr[b, p] * row(d8[i]) - sa_b[i] * row(beta8[i]) + v_b * row(k8[i])
                S_scr[b, p] = S
                Sb_scr[b, p] = S.astype(jnp.bfloat16)
        for i, (b, p) in enumerate(chains):
            y_step(G - 1, i)
            y_ref[b, pl.ds(t0, G), p * LANES:(p + 1) * LANES] = jnp.concatenate([yA[i], yB[i]], axis=1)
        return carry

    lax.fori_loop(0, tc // G, group, 0)

    @pl.when(c == pl.num_programs(1) - 1)
    def _():
        sT_ref[...] = S_scr[...]


def _rwkv_scan(r, d, k, v, kk, a, S0):
    B, T, W = r.shape
    H, N = S0.shape[1:3]
    P = H // 2
    G = RWKV_GROUP
    bb = RWKV_BB if B % RWKV_BB == 0 else 1
    t_pad = pl.cdiv(T, G) * G
    tc = t_pad
    for cand in range(min(t_pad, 512), G - 1, -1):
        if t_pad % cand == 0 and cand % G == 0:
            tc = cand
            break
    pad = lambda x, c=0.0: jnp.pad(x, [(0, 0), (0, t_pad - T), (0, 0)], constant_values=c) if t_pad != T else x
    S2 = S0.reshape(B, P, 2, N, N).transpose(0, 1, 3, 2, 4).reshape(B, P, N, 2 * N)
    seq_spec = pl.BlockSpec((bb, tc, W), lambda b, c: (b, c, 0))
    st_spec = pl.BlockSpec((bb, P, N, 2 * N), lambda b, c: (b, 0, 0, 0))
    y, S = pl.pallas_call(
        _rwkv_scan_body,
        grid=(B // bb, t_pad // tc),
        in_specs=[seq_spec] * 6 + [st_spec],
        out_specs=[seq_spec, st_spec],
        out_shape=[jax.ShapeDtypeStruct((B, t_pad, W), jnp.float32),
                   jax.ShapeDtypeStruct((B, P, N, 2 * N), jnp.float32)],
        scratch_shapes=[pltpu.VMEM((bb, P, N, 2 * N), jnp.float32), pltpu.VMEM((bb, P, N, 2 * N), jnp.bfloat16)],
        compiler_params=pltpu.CompilerParams(dimension_semantics=("arbitrary", "arbitrary"),
                                             vmem_limit_bytes=VMEM_LIMIT_BYTES),
        name="rwkv_scan",
    )(pad(r), pad(d, 1.0), pad(k), pad(v), pad(kk), pad(a), S2)
    S = S.reshape(B, P, N, 2, N).transpose(0, 1, 3, 2, 4).reshape(B, H, N, N)
    return y[:, :T], S


def _bf(x):
    return x.astype(jnp.bfloat16)


def _dot(a, b):
    return jnp.dot(_bf(a), _bf(b), preferred_element_type=jnp.float32)


def _dot_nt(a, b):
    return lax.dot_general(_bf(a), _bf(b), (((1,), (1,)), ((), ())), preferred_element_type=jnp.float32)


def _dot3(a, b):
    return jnp.dot(a, b, precision=lax.Precision.HIGHEST, preferred_element_type=jnp.float32)


def _gdn_chunk_body(q_ref, k_ref, v_ref, g_ref, beta_ref, grow_ref, s0_ref, o_ref, sT_ref, S_scr):
    L = q_ref.shape[1]
    H, D = S_scr.shape[:2]
    c = pl.program_id(1)

    @pl.when(c == 0)
    def _():
        S_scr[...] = s0_ref[0]

    ii = lax.broadcasted_iota(jnp.int32, (L, L), 0)
    jj = lax.broadcasted_iota(jnp.int32, (L, L), 1)
    tri, stri = jj <= ii, jj < ii
    eye = jnp.where(ii == jj, 1.0, 0.0)
    for h in range(H):
        hs = slice(h * D, (h + 1) * D)
        q, k, v = q_ref[0, :, hs], k_ref[0, :, hs], v_ref[0, :, hs]
        g_col, b_col = g_ref[0, :, h:h + 1], beta_ref[0, :, h:h + 1]
        g_row = grow_ref[0, 0, h:h + 1, :]
        gc_col = jnp.sum(jnp.where(tri, g_row, 0.0), axis=1, keepdims=True)
        gc_row = jnp.sum(jnp.where(ii <= jj, g_col, 0.0), axis=0, keepdims=True)
        dec = jnp.exp(jnp.where(tri, gc_col - gc_row, -jnp.inf))
        kb = k * b_col
        A = jnp.where(stri, _dot_nt(kb, k) * dec, 0.0)
        egc = jnp.exp(gc_col)
        rhs = jnp.concatenate([v * b_col, kb * egc], axis=1)
        inv = eye - A
        P = _dot3(A, A)
        n_lvl = int(math.log2(L)) - 1
        for lvl in range(n_lvl):
            inv = inv + _dot3(inv, P)
            if lvl + 1 < n_lvl:
                P = _dot3(P, P)
        sol = _dot3(inv, rhs)
        S = S_scr[h]
        v_new = sol[:, :D] - _dot(sol[:, D:], S)
        attn = jnp.where(tri, _dot_nt(q, k) * dec, 0.0)
        o_ref[0, :, hs] = _dot(q * egc, S) + _dot(attn, v_new)
        g_last = gc_col[L - 1:L, :]
        k_sc = (k * jnp.exp(g_last - gc_col)).T
        S_scr[h] = S * jnp.exp(g_last) + _dot(k_sc, v_new)

    @pl.when(c == pl.num_programs(1) - 1)
    def _():
        sT_ref[0] = S_scr[...]


def _gdn_chunks(q, k, v, g, beta, S0, L):
    B, T, W = q.shape
    H, D = S0.shape[1:3]
    n = T // L
    g_row = jnp.swapaxes(g.reshape(B, n, L, H), 2, 3)
    seq = pl.BlockSpec((1, L, W), lambda b, c: (b, c, 0))
    gate = pl.BlockSpec((1, L, H), lambda b, c: (b, c, 0))
    st = pl.BlockSpec((1, H, D, D), lambda b, c: (b, 0, 0, 0))
    return pl.pallas_call(
        _gdn_chunk_body,
        grid=(B, n),
        in_specs=[seq, seq, seq, gate, gate, pl.BlockSpec((1, 1, H, L), lambda b, c: (b, c, 0, 0)), st],
        out_specs=[seq, st],
        out_shape=[jax.ShapeDtypeStruct((B, T, W), jnp.float32),
                   jax.ShapeDtypeStruct((B, H, D, D), jnp.float32)],
        scratch_shapes=[pltpu.VMEM((H, D, D), jnp.float32)],
        compiler_params=pltpu.CompilerParams(dimension_semantics=("arbitrary", "arbitrary"),
                                             vmem_limit_bytes=VMEM_LIMIT_BYTES),
        name="gdn_chunks",
    )(q, k, v, g, beta, g_row, S0)


def _layer_norm(x, g, b, eps=1e-5):
    xf = x.astype(jnp.float32)
    mu = jnp.mean(xf, -1, keepdims=True)
    var = jnp.mean(jnp.square(xf - mu), -1, keepdims=True)
    return ((xf - mu) * lax.rsqrt(var + eps) * g + b).astype(x.dtype)


def _rms_norm(x, g, eps=1e-6):
    xf = x.astype(jnp.float32)
    return (xf * lax.rsqrt(jnp.mean(xf * xf, -1, keepdims=True) + eps) * g).astype(x.dtype)


def _l2norm(x, eps=1e-6):
    xf = x.astype(jnp.float32)
    return xf * lax.rsqrt(jnp.sum(xf * xf, -1, keepdims=True) + eps)


def _proj_in(x, w_in):
    B, T, _ = x.shape
    sizes = [W_MIX,
             3 * W_MIX, GDN_HEADS, GDN_HEADS, W_MIX,
             W_MIX, W_MIX, W_MIX, IDX_HEADS * IDX_DIM, IDX_DIM, IDX_HEADS,
             3 * W_MIX, MLSTM_HEADS, MLSTM_HEADS, W_MIX,
             N_BRANCH * D_MODEL]
    offs = np.concatenate([[0], np.cumsum(sizes)]).tolist()
    xb = x.reshape(B * T, D_MODEL).astype(jnp.bfloat16)
    narrow = [i for i, s in enumerate(sizes) if s < LANES]
    outs = [None] * len(sizes)
    for i, s in enumerate(sizes):
        if i not in narrow:
            outs[i] = _mm(xb, w_in[:, offs[i]:offs[i + 1]]).reshape(B, T, s)
    small = _mm(xb, jnp.concatenate([w_in[:, offs[i]:offs[i + 1]] for i in narrow], axis=1))
    o = 0
    for i in narrow:
        outs[i] = small[:, o:o + sizes[i]].reshape(B, T, sizes[i])
        o += sizes[i]
    return outs


def _take_rows(rows, idx):
    return jax.vmap(lambda r, i: r[i])(rows, idx)


def _short_conv(u, buf, w):
    T = u.shape[1]
    ext = jnp.concatenate([buf.astype(jnp.float32), u.astype(jnp.float32)], axis=1)
    y = sum(ext[:, j:j + T] * w[j] for j in range(CONV_W))
    return jax.nn.silu(y), ext[:, ext.shape[1] - (CONV_W - 1):]


def _run_chunks(chunk_fn, state, xs, is_prompt, chunk):
    if not is_prompt:
        state, out = chunk_fn(state, xs)
        return out, state
    state, out_meta = chunk_fn(state, tuple(t[:, :, :N_META] for t in xs))
    real = tuple(t[:, :, N_META:] for t in xs)
    n = real[0].shape[2] // chunk

    def to_chunks(t):
        return jnp.moveaxis(t.reshape(t.shape[:2] + (n, chunk) + t.shape[3:]), 2, 0)

    state, out = lax.scan(chunk_fn, state, tuple(to_chunks(t) for t in real))
    out = jnp.moveaxis(out, 0, 2)
    out = out.reshape(out.shape[:2] + (n * chunk,) + out.shape[4:])
    return jnp.concatenate([out_meta, out], axis=2), state


def _rwkv7(u, S0, shift0, mu, w_rkv, w0, w1, w2, a0, a1, a2, g1, g2, k_k, k_a, r_k, lnx_g, lnx_b):
    B, T, _ = u.shape
    uf = u.astype(jnp.float32)
    prev = jnp.concatenate([shift0[:, None].astype(jnp.float32), uf[:, :-1]], axis=1)
    xx = prev - uf
    xr, xw, xk, xv, xa, xg = (uf + xx * mu[j] for j in range(6))
    r = xr @ w_rkv[0]
    k = xk @ w_rkv[1]
    v = xv @ w_rkv[2]
    w = -jax.nn.softplus(-(w0 + jnp.tanh(xw @ w1) @ w2)) - 0.5
    a = jax.nn.sigmoid(a0 + (xa @ a1) @ a2)
    g = jax.nn.sigmoid(xg @ g1) @ g2
    hd = lambda t: t.reshape(B, T, RWKV_HEADS, RWKV_HEAD_DIM)
    kk = _l2norm(hd(k * k_k))
    k = hd(k * (1.0 + (a - 1.0) * k_a))
    r, v, a = hd(r), hd(v), hd(a)
    decay = jnp.exp(-jnp.exp(hd(w)))

    flat = lambda t: t.reshape(B, T, W_MIX)
    y, S = _rwkv_scan(flat(r), flat(decay), flat(k), flat(v), flat(kk), flat(a), S0.astype(jnp.float32))
    y = hd(y)
    mean = jnp.mean(y, -1, keepdims=True)
    var = jnp.mean(jnp.square(y - mean), -1, keepdims=True)
    y = ((y - mean) * lax.rsqrt(var + RWKV_GN_EPS)).reshape(B, T, W_MIX) * lnx_g + lnx_b
    bonus = jnp.sum(r * k * r_k, -1, keepdims=True) * v
    y = (y + bonus.reshape(B, T, W_MIX)) * g
    return y.astype(u.dtype), S, u[:, -1]


def _gdn_chunk(S, xs):
    q, k, v, g, beta = xs
    L = q.shape[2]
    tri = jnp.tril(jnp.ones((L, L), bool))
    stri = jnp.tril(jnp.ones((L, L), bool), -1)
    gc = jnp.cumsum(g, axis=-1)
    dec = jnp.exp(jnp.where(tri, gc[..., :, None] - gc[..., None, :], -jnp.inf))
    kb = k * beta[..., None]
    A = jnp.where(stri, jnp.einsum('bhid,bhjd->bhij', kb, k) * dec, 0.0)
    rhs = jnp.concatenate([v * beta[..., None], kb * jnp.exp(gc)[..., None]], axis=-1)
    sol = lax.linalg.triangular_solve(A + jnp.eye(L, dtype=A.dtype), rhs, left_side=True,
                                      lower=True, unit_diagonal=True)
    u_, w_ = sol[..., :GDN_HEAD_DIM], sol[..., GDN_HEAD_DIM:]
    v_new = u_ - jnp.einsum('bhld,bhde->bhle', w_, S)
    attn = jnp.where(tri, jnp.einsum('bhid,bhjd->bhij', q, k) * dec, 0.0)
    o = (jnp.einsum('bhld,bhde->bhle', q * jnp.exp(gc)[..., None], S)
         + jnp.einsum('bhij,bhje->bhie', attn, v_new))
    g_last = gc[..., -1]
    S = (S * jnp.exp(g_last)[..., None, None]
         + jnp.einsum('bhld,bhle->bhde', k * jnp.exp(g_last[..., None] - gc)[..., None], v_new))
    return S, o


def _gdn(qkv, a_pre, b_pre, z, S0, buf0, conv_w, A_log, dt_bias, norm_g, is_prompt):
    B, T, _ = qkv.shape
    y, new_buf = _short_conv(qkv, buf0, conv_w)
    q, k, v = jnp.split(y, 3, axis=-1)
    hd4 = lambda t: t.reshape(B, T, GDN_HEADS, GDN_HEAD_DIM)
    q = _l2norm(hd4(q)) * GDN_HEAD_DIM ** -0.5
    k = _l2norm(hd4(k))
    g = -jnp.exp(A_log.astype(jnp.float32)) * jax.nn.softplus(a_pre.astype(jnp.float32) + dt_bias)
    beta = jax.nn.sigmoid(b_pre.astype(jnp.float32))
    if is_prompt:
        pad = (-N_META) % GDN_CHUNK
        fp = lambda t: jnp.pad(t.reshape(B, T, -1), [(0, 0), (pad, 0), (0, 0)])
        o, S = _gdn_chunks(fp(q), fp(k), fp(v), fp(g), fp(beta), S0.astype(jnp.float32), GDN_CHUNK)
        o = hd4(o[:, pad:])
    else:
        hd = lambda t: t.transpose(0, 2, 1, 3)
        S, o = _gdn_chunk(S0.astype(jnp.float32), (hd(q), hd(k), hd(hd4(v)), g.transpose(0, 2, 1),
                                                    beta.transpose(0, 2, 1)))
        o = o.transpose(0, 2, 1, 3)
    o = _rms_norm(o, norm_g)
    o = o * jax.nn.silu(z.astype(jnp.float32).reshape(B, T, GDN_HEADS, GDN_HEAD_DIM))
    return o.reshape(B, T, W_MIX).astype(qkv.dtype), S, new_buf


def _alibi_slopes():
    return jnp.asarray(2.0 ** (-8.0 * np.arange(1, ATT_HEADS + 1) / ATT_HEADS), jnp.float32)


def _dsa_inputs(c_q, c_k, c_v, c_qi, c_ki, c_wi, ln_g, ln_b):
    B, T, _ = c_q.shape
    hd = lambda t: t.reshape(B, T, ATT_HEADS, ATT_HEAD_DIM)
    qi = c_qi.reshape(B, T, IDX_HEADS, IDX_DIM)
    ki = _layer_norm(c_ki, ln_g, ln_b)
    wi = c_wi * IDX_HEADS ** -0.5
    return hd(c_q), hd(c_k), hd(c_v), qi, ki, wi


def _dsa_block(q, qi, wi, q_pos, ki, gather, k_top):
    S = ki.shape[1]
    s_pos = jnp.arange(S, dtype=jnp.int32)
    sc = jnp.einsum('bqhd,bsd->bqhs', qi.astype(jnp.float32), ki.astype(jnp.float32)) * IDX_DIM ** -0.5
    isc = jnp.einsum('bqhs,bqh->bqs', jax.nn.relu(sc), wi.astype(jnp.float32))
    isc = jnp.where((s_pos[None, :] <= q_pos[:, None])[None], isc, -jnp.inf)
    _, sel = lax.top_k(isc, k_top)
    valid = sel <= q_pos[None, :, None]
    k_sel, v_sel = gather(sel)
    logits = jnp.einsum('bqhd,bqkhd->bqhk', q.astype(jnp.float32), k_sel.astype(jnp.float32)) * ATT_HEAD_DIM ** -0.5
    dist = (q_pos[None, :, None] - sel).astype(jnp.float32)
    logits = logits - _alibi_slopes()[None, None, :, None] * dist[:, :, None, :]
    logits = jnp.where(valid[:, :, None, :], logits, -jnp.inf)
    p = jax.nn.softmax(logits, axis=-1)
    return jnp.einsum('bqhk,bqkhd->bqhd', p, v_sel.astype(jnp.float32)).astype(q.dtype)


def _dsa_prompt(q, k, v, qi, ki, wi):
    B, T = q.shape[:2]
    k_top = min(TOPK_MAX, T // 4)
    n_blk = -(-T // Q_BLOCK)
    t_pad = n_blk * Q_BLOCK

    def blocks(t):
        t = jnp.pad(t, [(0, 0), (0, t_pad - T)] + [(0, 0)] * (t.ndim - 2))
        return jnp.swapaxes(t.reshape((B, n_blk, Q_BLOCK) + t.shape[2:]), 0, 1)

    def gather(sel):
        return _take_rows(k, sel), _take_rows(v, sel)

    def body(args):
        q_b, qi_b, wi_b, pos_b = args
        return _dsa_block(q_b, qi_b, wi_b, pos_b, ki, gather, k_top)

    pos = jnp.arange(t_pad, dtype=jnp.int32).reshape(n_blk, Q_BLOCK)
    out = lax.map(body, (blocks(q), blocks(qi), blocks(wi), pos))
    return jnp.swapaxes(out, 0, 1).reshape(B, t_pad, ATT_HEADS, ATT_HEAD_DIM)[:, :T]


def _dsa_sample(q, k, v, qi, ki, wi, k_pool, v_pool, i_pool, page_table):
    nb, Q = q.shape[:2]
    past_len = page_table.shape[1] * PAGE_SIZE
    ki_past = i_pool[page_table].reshape(nb, past_len, IDX_DIM)
    ki_all = jnp.concatenate([ki_past.astype(jnp.float32), ki.astype(jnp.float32)], axis=1)
    k_top = min(TOPK_MAX, (past_len + Q) // 4)

    def gather(sel):
        in_past = (sel < past_len)[..., None, None]
        sp = jnp.minimum(sel, past_len - 1)
        phys = page_table[jnp.arange(nb)[:, None, None], sp // PAGE_SIZE]
        off = sp % PAGE_SIZE
        sn = jnp.clip(sel - past_len, 0, Q - 1)
        return (jnp.where(in_past, k_pool[phys, off], _take_rows(k, sn)),
                jnp.where(in_past, v_pool[phys, off], _take_rows(v, sn)))

    q_pos = past_len + jnp.arange(Q, dtype=jnp.int32)
    return _dsa_block(q, qi, wi, q_pos, ki_all, gather, k_top)


def _mlstm_chunk(state, xs):
    C, n, m = state
    q, k, v, li, lf = xs
    L = q.shape[2]
    tri = jnp.tril(jnp.ones((L, L), bool))
    b = jnp.cumsum(lf, axis=-1)
    a = li - b
    m_t = b + jnp.maximum(m[..., None], lax.cummax(a, axis=2))
    dmat = jnp.exp(jnp.where(tri, b[..., :, None] + a[..., None, :] - m_t[..., :, None], -jnp.inf))
    inter = jnp.exp(b + m[..., None] - m_t)
    s = jnp.einsum('bhtd,bhsd->bhts', q, k) * dmat
    num = inter[..., None] * jnp.einsum('bhtd,bhde->bhte', q, C) + jnp.einsum('bhts,bhse->bhte', s, v)
    den = inter * jnp.einsum('bhtd,bhd->bht', q, n) + jnp.sum(s, axis=-1)
    h = num / jnp.maximum(jnp.abs(den), jnp.exp(-m_t))[..., None]
    m_new = m_t[..., -1]
    wend = jnp.exp(b[..., -1:] + a - m_new[..., None])
    carry = jnp.exp(b[..., -1] + m - m_new)
    C = carry[..., None, None] * C + jnp.einsum('bhs,bhsd,bhse->bhde', wend, k, v)
    n = carry[..., None] * n + jnp.einsum('bhs,bhsd->bhd', wend, k)
    return (C, n, m_new), h


def _mlstm(qkv, i_pre, f_pre, o_pre, C0, n0, m0, b_i, b_f, norm_g, is_prompt):
    B, T, _ = qkv.shape
    q, k, v = jnp.split(qkv.astype(jnp.float32), 3, axis=-1)
    hd = lambda t: t.reshape(B, T, MLSTM_HEADS, MLSTM_HEAD_DIM).transpose(0, 2, 1, 3)
    q, k, v = hd(q), hd(k) * MLSTM_HEAD_DIM ** -0.5, hd(v)
    li = (i_pre.astype(jnp.float32) + b_i).transpose(0, 2, 1)
    lf = jax.nn.log_sigmoid(f_pre.astype(jnp.float32) + b_f).transpose(0, 2, 1)
    st0 = (C0.astype(jnp.float32), n0.astype(jnp.float32), m0.astype(jnp.float32))
    h, (C, n, m) = _run_chunks(_mlstm_chunk, st0, (q, k, v, li, lf), is_prompt, MLSTM_CHUNK)
    h = _rms_norm(h.transpose(0, 2, 1, 3), norm_g).reshape(B, T, W_MIX)
    h = jax.nn.sigmoid(o_pre.astype(jnp.float32)) * h
    return h.astype(qkv.dtype), C, n, m


def _moe_experts_body(blk_exp_ref, n_used_ref, x_ref, wg_ref, wu_ref, wd_ref, o_ref, wg_bf, wu_bf, wd_bf):
    i = pl.program_id(0)
    new_expert = (i == 0) | (blk_exp_ref[i] != blk_exp_ref[jnp.maximum(i - 1, 0)])

    @pl.when(new_expert)
    def _():
        wg_bf[...] = wg_ref[0, 0].astype(jnp.bfloat16)
        wu_bf[...] = wu_ref[0, 0].astype(jnp.bfloat16)
        wd_bf[...] = wd_ref[0, 0].astype(jnp.bfloat16)

    @pl.when(i < n_used_ref[0])
    def _():
        xb = x_ref[...].astype(jnp.bfloat16)
        hg = jnp.dot(xb, wg_bf[...], preferred_element_type=jnp.float32)
        hu = jnp.dot(xb, wu_bf[...], preferred_element_type=jnp.float32)
        act = (hg * jax.nn.sigmoid(hg)) * hu
        o_ref[...] = jnp.dot(act.astype(jnp.bfloat16), wd_bf[...], preferred_element_type=jnp.float32)

    @pl.when(i >= n_used_ref[0])
    def _():
        o_ref[...] = jnp.zeros_like(o_ref)


def _moe_experts(xb, blk_exp, n_used, w_g, w_u, w_d, l):
    R, D = xb.shape
    F = w_g.shape[-1]
    n_blocks = R // MOE_BLOCK
    grid_spec = pltpu.PrefetchScalarGridSpec(
        num_scalar_prefetch=2,
        grid=(n_blocks,),
        in_specs=[pl.BlockSpec((MOE_BLOCK, D), lambda i, be, nu: (i, 0)),
                  pl.BlockSpec((1, 1, D, F), lambda i, be, nu: (l, be[i], 0, 0)),
                  pl.BlockSpec((1, 1, D, F), lambda i, be, nu: (l, be[i], 0, 0)),
                  pl.BlockSpec((1, 1, F, D), lambda i, be, nu: (l, be[i], 0, 0))],
        out_specs=pl.BlockSpec((MOE_BLOCK, D), lambda i, be, nu: (i, 0)),
        scratch_shapes=[pltpu.VMEM((D, F), jnp.bfloat16), pltpu.VMEM((D, F), jnp.bfloat16),
                        pltpu.VMEM((F, D), jnp.bfloat16)])
    return pl.pallas_call(
        _moe_experts_body,
        grid_spec=grid_spec,
        out_shape=jax.ShapeDtypeStruct((R, D), jnp.float32),
        compiler_params=pltpu.CompilerParams(dimension_semantics=("arbitrary",),
                                             vmem_limit_bytes=VMEM_LIMIT_BYTES),
        name="moe_experts",
    )(blk_exp, n_used, xb, w_g, w_u, w_d)


def _expert_dispatch(xt, eid, gate, w_g, w_u, w_d, l):
    N = xt.shape[0]
    A = N * TOP_K_INNER
    e_flat = eid.reshape(-1)
    tok = jnp.arange(A, dtype=jnp.int32) // TOP_K_INNER
    order = jnp.argsort(e_flat)
    e_sorted = e_flat[order]
    counts = jnp.bincount(e_flat, length=N_EXPERTS)
    padded = (counts + MOE_BLOCK - 1) // MOE_BLOCK * MOE_BLOCK
    pad_end = jnp.cumsum(padded)
    pad_start = pad_end - padded
    start = jnp.cumsum(counts) - counts
    dest = pad_start[e_sorted] + jnp.arange(A, dtype=jnp.int32) - start[e_sorted]
    n_blocks = -(-A // MOE_BLOCK) + N_EXPERTS
    slot_tok = jnp.full((n_blocks * MOE_BLOCK,), N, jnp.int32).at[dest].set(tok[order])
    blk_exp = jnp.minimum(jnp.searchsorted(pad_end, jnp.arange(n_blocks) * MOE_BLOCK, side='right'), N_EXPERTS - 1)
    x_pad = jnp.concatenate([xt, jnp.zeros((1, xt.shape[1]), xt.dtype)], axis=0)
    n_used = (pad_end[-1:] // MOE_BLOCK).astype(jnp.int32)
    yb = _moe_experts(x_pad[slot_tok], blk_exp.astype(jnp.int32), n_used, w_g, w_u, w_d, l)
    slot = jnp.zeros((A,), jnp.int32).at[order].set(dest).reshape(N, TOP_K_INNER)
    return sum(_round_bf16(yb[slot[:, j]]) * _round_bf16(gate[:, j:j + 1]) for j in range(TOP_K_INNER))


def _moe(x, w_group, w_expert, w_g, w_u, w_d, l):
    shp = x.shape
    xt = x.reshape(-1, D_MODEL)
    N = xt.shape[0]
    xf = xt.astype(jnp.float32)
    pg = jax.nn.softmax(xf @ w_group.astype(jnp.float32), axis=-1)
    p_top, g_sel = lax.top_k(pg, 1)
    le = (xf @ w_expert.astype(jnp.float32)).reshape(N, N_GROUPS, EXPERTS_PER_GROUP)
    le_g = le[jnp.arange(N), g_sel[:, 0]]
    v2, j2 = lax.top_k(le_g, TOP_K_INNER)
    gate = jax.nn.softmax(v2, axis=-1) * p_top
    eid = g_sel * EXPERTS_PER_GROUP + j2
    return _expert_dispatch(xt, eid, gate, w_g, w_u, w_d, l).reshape(shp).astype(x.dtype)


def _layer(x, W, l, st, dsa_fn, is_prompt, alpha):
    B, T, _ = x.shape
    (a_u, b_qkv, b_a, b_b, b_z, c_q, c_k, c_v, c_qi, c_ki, c_wi,
     d_qkv, d_i, d_f, d_o, gates) = _proj_in(x, W['w_in'][l])
    rwkv_S0, rwkv_shift0, gdn_S0, gdn_conv0, mC0, mn0, mm0 = st
    y_a, rwkv_S, rwkv_shift = _rwkv7(
        a_u, rwkv_S0, rwkv_shift0, W['rwkv_mu'][l], W['rwkv_w_rkv'][l], W['rwkv_w0'][l], W['rwkv_w1'][l],
        W['rwkv_w2'][l], W['rwkv_a0'][l], W['rwkv_a1'][l], W['rwkv_a2'][l], W['rwkv_g1'][l], W['rwkv_g2'][l],
        W['rwkv_k_k'][l], W['rwkv_k_a'][l], W['rwkv_r_k'][l], W['rwkv_lnx_g'][l], W['rwkv_lnx_b'][l])
    y_b, gdn_S, gdn_conv = _gdn(b_qkv, b_a, b_b, b_z, gdn_S0, gdn_conv0, W['gdn_conv_w'][l], W['gdn_A_log'][l],
                                W['gdn_dt_bias'][l], W['gdn_norm_g'][l], is_prompt)
    q, k, v, qi, ki, wi = _dsa_inputs(c_q, c_k, c_v, c_qi, c_ki, c_wi, W['idx_ln_g'][l], W['idx_ln_b'][l])
    y_c = dsa_fn(q, k, v, qi, ki, wi).reshape(B, T, W_MIX)
    y_d, mC, mn, mm = _mlstm(d_qkv, d_i, d_f, d_o, mC0, mn0, mm0, W['mlstm_b_i'][l], W['mlstm_b_f'][l],
                             W['mlstm_norm_g'][l], is_prompt)
    merged = 0.0
    for i, y_i in enumerate((y_a, y_b, y_c, y_d)):
        gate_i = jax.nn.sigmoid(gates[..., i * D_MODEL:(i + 1) * D_MODEL].astype(jnp.float32))
        merged = merged + gate_i * (y_i @ W['w_branch'][l, i])
    mix = merged.astype(x.dtype) @ W['w_out'][l]
    x = _layer_norm(alpha * x + mix, W['ln1_g'][l], W['ln1_b'][l])
    ffn = _moe(x, W['moe_w_group'][l], W['moe_w_expert'][l], W['moe_w_gate'], W['moe_w_up'], W['moe_w_down'], l)
    x = _layer_norm(alpha * x + ffn, W['ln2_g'][l], W['ln2_b'][l])
    return x, (k, v, ki, rwkv_S, rwkv_shift, gdn_S, gdn_conv, mC, mn, mm)


def kernel(x_prompt, x_sample, cache_k, cache_v, cache_idx_k, page_table, state_rwkv_S, state_rwkv_shift,
           state_gdn_S, state_gdn_conv, state_mlstm_C, state_mlstm_n, state_mlstm_m, meta_tokens, ln_in_g,
           ln_in_b, w_in, rwkv_mu, rwkv_w_rkv, rwkv_w0, rwkv_w1, rwkv_w2, rwkv_a0, rwkv_a1, rwkv_a2, rwkv_g1,
           rwkv_g2, rwkv_k_k, rwkv_k_a, rwkv_r_k, rwkv_lnx_g, rwkv_lnx_b, gdn_conv_w, gdn_A_log, gdn_dt_bias,
           gdn_norm_g, idx_ln_g, idx_ln_b, mlstm_b_i, mlstm_b_f, mlstm_norm_g, w_branch, w_out, ln1_g, ln1_b,
           ln2_g, ln2_b, moe_w_group, moe_w_expert, moe_w_gate, moe_w_up, moe_w_down):
    W = dict(w_in=w_in, rwkv_mu=rwkv_mu, rwkv_w_rkv=rwkv_w_rkv, rwkv_w0=rwkv_w0, rwkv_w1=rwkv_w1,
             rwkv_w2=rwkv_w2, rwkv_a0=rwkv_a0, rwkv_a1=rwkv_a1, rwkv_a2=rwkv_a2, rwkv_g1=rwkv_g1,
             rwkv_g2=rwkv_g2, rwkv_k_k=rwkv_k_k, rwkv_k_a=rwkv_k_a, rwkv_r_k=rwkv_r_k, rwkv_lnx_g=rwkv_lnx_g,
             rwkv_lnx_b=rwkv_lnx_b, gdn_conv_w=gdn_conv_w, gdn_A_log=gdn_A_log, gdn_dt_bias=gdn_dt_bias,
             gdn_norm_g=gdn_norm_g, idx_ln_g=idx_ln_g, idx_ln_b=idx_ln_b, mlstm_b_i=mlstm_b_i,
             mlstm_b_f=mlstm_b_f, mlstm_norm_g=mlstm_norm_g, w_branch=w_branch, w_out=w_out, ln1_g=ln1_g,
             ln1_b=ln1_b, ln2_g=ln2_g, ln2_b=ln2_b, moe_w_group=moe_w_group, moe_w_expert=moe_w_expert,
             moe_w_gate=moe_w_gate, moe_w_up=moe_w_up, moe_w_down=moe_w_down)
    f32 = jnp.float32
    depth = w_in.shape[0]
    alpha = (2 * depth) ** 0.25
    B = x_prompt.shape[0]
    meta = jnp.broadcast_to(meta_tokens.astype(x_prompt.dtype)[None], (B, N_META, D_MODEL))
    hp = _layer_norm(jnp.concatenate([meta, x_prompt], axis=1), ln_in_g, ln_in_b)
    hs = _layer_norm(x_sample, ln_in_g, ln_in_b)
    zero_state = (jnp.zeros((B, RWKV_HEADS, RWKV_HEAD_DIM, RWKV_HEAD_DIM), f32),
                  jnp.zeros((B, W_MIX), f32),
                  jnp.zeros((B, GDN_HEADS, GDN_HEAD_DIM, GDN_HEAD_DIM), f32),
                  jnp.zeros((B, CONV_W - 1, 3 * W_MIX), f32),
                  jnp.zeros((B, MLSTM_HEADS, MLSTM_HEAD_DIM, MLSTM_HEAD_DIM), f32),
                  jnp.zeros((B, MLSTM_HEADS, MLSTM_HEAD_DIM), f32),
                  jnp.zeros((B, MLSTM_HEADS), f32))
    new_p, new_s = [], []
    for l in range(depth):
        hp, sp = _layer(hp, W, l, zero_state, _dsa_prompt_pallas, True, alpha)
        dsa_s = functools.partial(_dsa_sample_pallas, cache_k=cache_k, cache_v=cache_v, cache_idx_k=cache_idx_k,
                                  page_table=page_table, l=l)
        st_s = (state_rwkv_S[l], state_rwkv_shift[l], state_gdn_S[l], state_gdn_conv[l],
                state_mlstm_C[l], state_mlstm_n[l], state_mlstm_m[l])
        hs, ss = _layer(hs, W, l, st_s, dsa_s, False, alpha)
        new_p.append(sp)
        new_s.append(ss)

    def stack(rows, j):
        return jnp.stack([r[j] for r in rows])

    k_p, v_p, ik_p, rS_p, rsh_p, gS_p, gc_p, mC_p, mn_p, mm_p = (stack(new_p, j) for j in range(10))
    k_s, v_s, ik_s, rS_s, rsh_s, gS_s, gc_s, mC_s, mn_s, mm_s = (stack(new_s, j) for j in range(10))
    y_prompt = hp[:, N_META:]
    y_sample = hs
    return (y_prompt, y_sample, k_p, v_p, ik_p, k_s, v_s, ik_s, rS_p, rS_s, rsh_p, rsh_s,
            gS_p, gS_s, gc_p, gc_s, mC_p, mC_s, mn_p, mn_s, mm_p, mm_s)
```

```python
import math, functools
import jax, jax.numpy as jnp
from jax import lax
import numpy as np
from jax.experimental import pallas as pl
from jax.experimental.pallas import tpu as pltpu

D_MODEL = 2048
N_META = 16
N_BRANCH = 4
W_MIX = D_MODEL // 4
RWKV_HEAD_DIM = 64
RWKV_HEADS = W_MIX // RWKV_HEAD_DIM
RWKV_GN_EPS = 64e-5
GDN_HEADS = 4
GDN_HEAD_DIM = W_MIX // GDN_HEADS
CONV_W = 4
GDN_CHUNK = 64
ATT_HEADS = 4
ATT_HEAD_DIM = W_MIX // ATT_HEADS
IDX_HEADS = 8
IDX_DIM = 64
TOPK_MAX = 256
Q_BLOCK = 64
MLSTM_HEADS = 4
MLSTM_HEAD_DIM = W_MIX // MLSTM_HEADS
MLSTM_CHUNK = 64
N_GROUPS = 4
EXPERTS_PER_GROUP = 8
N_EXPERTS = N_GROUPS * EXPERTS_PER_GROUP
TOP_K_INNER = 2
D_EXPERT = D_MODEL // 4
MOE_BLOCK = 128
PAGE_SIZE = 128

VMEM_LIMIT_BYTES = 56 * 1024 * 1024


def _mm_body(x_ref, w_ref, o_ref):
    o_ref[...] = jnp.dot(x_ref[...].astype(jnp.bfloat16), w_ref[...].astype(jnp.bfloat16),
                         preferred_element_type=jnp.float32)


def _mm(x, w, tm=512, tn=1024):
    M, K = x.shape
    N = w.shape[1]
    tm = min(tm, M)
    tn = N if N <= 3 * W_MIX else tn
    return pl.pallas_call(
        _mm_body,
        grid=(pl.cdiv(N, tn), pl.cdiv(M, tm)),
        in_specs=[pl.BlockSpec((tm, K), lambda j, i: (i, 0)),
                  pl.BlockSpec((K, tn), lambda j, i: (0, j))],
        out_specs=pl.BlockSpec((tm, tn), lambda j, i: (i, j)),
        out_shape=jax.ShapeDtypeStruct((M, N), jnp.float32),
        compiler_params=pltpu.CompilerParams(dimension_semantics=("arbitrary", "arbitrary"),
                                             vmem_limit_bytes=VMEM_LIMIT_BYTES),
        name="proj_mm",
    )(x, w)


LANES = 128
INT_MIN = -2 ** 31
DSA_TQ = 128
DSA_SEGMENTS = 4


def _round_bf16(x):
    return x.astype(jnp.bfloat16).astype(jnp.float32)


def _topk_mask(isc, valid, key_ref, k_top):
    rows, t_pad = key_ref.shape
    bits = lax.bitcast_convert_type(isc, jnp.int32)
    key = bits ^ ((bits >> 31) & 0x7FFFFFFF)
    key_ref[...] = jnp.where(valid, key, INT_MIN)

    def count_ge(c):
        return jnp.sum(jnp.where(key_ref[...] >= c, 1.0, 0.0), axis=1, keepdims=True)

    kf = float(k_top)
    thr = jnp.where(count_ge(jnp.zeros((rows, 1), jnp.int32)) >= kf, 0, INT_MIN).astype(jnp.int32)

    def bit_step(j, thr):
        cand = thr | jnp.left_shift(jnp.int32(1), 30 - j)
        return jnp.where(count_ge(cand) >= kf, cand, thr)

    thr = lax.fori_loop(0, 31, bit_step, thr)

    key = key_ref[...]
    gt = key > thr
    eq = key == thr
    need = kf - jnp.sum(jnp.where(gt, 1.0, 0.0), axis=1, keepdims=True)
    upper = (lax.broadcasted_iota(jnp.int32, (LANES, LANES), 0)
             < lax.broadcasted_iota(jnp.int32, (LANES, LANES), 1)).astype(jnp.bfloat16)
    eqf = jnp.where(eq, 1.0, 0.0)
    off = jnp.zeros((rows, 1), jnp.float32)
    pre = []
    for c in range(t_pad // LANES):
        e_c = eqf[:, c * LANES:(c + 1) * LANES]
        pre.append(jnp.dot(e_c.astype(jnp.bfloat16), upper, preferred_element_type=jnp.float32) + off)
        off = off + jnp.sum(e_c, axis=1, keepdims=True)
    prefix = jnp.concatenate(pre, axis=1)
    return valid & (gt | (eq & (prefix < need)))


def _dsa_prompt_body(q_ref, qi_ref, wi_ref, kT_ref, v_ref, kiT_ref, o_ref, key_ref, *, k_top, first_block):
    tq, t_pad = key_ref.shape
    i = pl.program_id(1) + first_block
    row = lax.broadcasted_iota(jnp.int32, (tq, t_pad), 0) + i * tq
    col = lax.broadcasted_iota(jnp.int32, (tq, t_pad), 1)
    valid = col <= row

    wi = _round_bf16(wi_ref[0])
    isc = jnp.zeros((tq, t_pad), jnp.float32)
    for h in range(IDX_HEADS):
        sc = jnp.dot(qi_ref[0, h].astype(jnp.bfloat16), kiT_ref[0],
                     preferred_element_type=jnp.float32) * IDX_DIM ** -0.5
        isc = isc + _round_bf16(jnp.maximum(sc, 0.0)) * wi[:, h:h + 1]

    sel = _topk_mask(isc, valid, key_ref, k_top)

    dist = (row - col).astype(jnp.float32)
    for h in range(ATT_HEADS):
        hs = slice(h * ATT_HEAD_DIM, (h + 1) * ATT_HEAD_DIM)
        slope = 2.0 ** (-8.0 * (h + 1) / ATT_HEADS)
        lg = jnp.dot(q_ref[0, :, hs].astype(jnp.bfloat16), kT_ref[0, hs, :],
                     preferred_element_type=jnp.float32) * ATT_HEAD_DIM ** -0.5 - slope * dist
        lg = jnp.where(sel, lg, -jnp.inf)
        e = jnp.exp(lg - jnp.max(lg, axis=1, keepdims=True))
        p = e * (1.0 / jnp.sum(e, axis=1, keepdims=True))
        o_ref[0, :, hs] = jnp.dot(p.astype(jnp.bfloat16), v_ref[0, :, hs], preferred_element_type=jnp.float32)


def _dsa_prompt_pallas(q, k, v, qi, ki, wi):
    B, T = q.shape[:2]
    k_top = min(TOPK_MAX, T // 4)
    tq = DSA_TQ
    t_pad = pl.cdiv(T, LANES) * LANES
    padt = lambda t: jnp.pad(t, [(0, 0), (0, t_pad - T)] + [(0, 0)] * (t.ndim - 2))
    q2 = padt(q.reshape(B, T, W_MIX))
    kT = jnp.swapaxes(padt(k.reshape(B, T, W_MIX)), 1, 2).astype(jnp.bfloat16)
    v2 = padt(v.reshape(B, T, W_MIX)).astype(jnp.bfloat16)
    qi2 = jnp.swapaxes(padt(qi), 1, 2)
    kiT = jnp.swapaxes(padt(ki), 1, 2).astype(jnp.bfloat16)
    wi2 = padt(wi)
    n_q = t_pad // tq
    bounds = sorted({-(-n_q * s // DSA_SEGMENTS) for s in range(DSA_SEGMENTS + 1)})
    outs = []
    for i0, i1 in zip(bounds[:-1], bounds[1:]):
        tk = i1 * tq
        outs.append(pl.pallas_call(
            functools.partial(_dsa_prompt_body, k_top=k_top, first_block=i0),
            grid=(B, i1 - i0),
            in_specs=[pl.BlockSpec((1, tq, W_MIX), lambda b, i, i0=i0: (b, i + i0, 0)),
                      pl.BlockSpec((1, IDX_HEADS, tq, IDX_DIM), lambda b, i, i0=i0: (b, 0, i + i0, 0)),
                      pl.BlockSpec((1, tq, IDX_HEADS), lambda b, i, i0=i0: (b, i + i0, 0)),
                      pl.BlockSpec((1, W_MIX, tk), lambda b, i: (b, 0, 0)),
                      pl.BlockSpec((1, tk, W_MIX), lambda b, i: (b, 0, 0)),
                      pl.BlockSpec((1, IDX_DIM, tk), lambda b, i: (b, 0, 0))],
            out_specs=pl.BlockSpec((1, tq, W_MIX), lambda b, i: (b, i, 0)),
            out_shape=jax.ShapeDtypeStruct((B, (i1 - i0) * tq, W_MIX), jnp.float32),
            scratch_shapes=[pltpu.VMEM((tq, tk), jnp.int32)],
            compiler_params=pltpu.CompilerParams(dimension_semantics=("arbitrary", "arbitrary"),
                                                 vmem_limit_bytes=VMEM_LIMIT_BYTES),
            name="dsa_prompt",
        )(q2, qi2, wi2, kT, v2, kiT))
    out = jnp.concatenate(outs, axis=1)
    return out[:, :T].reshape(B, T, ATT_HEADS, ATT_HEAD_DIM)


SUBLANES = 8


def _dsa_sample_body(pt_ref, q_ref, kn_ref, vn_ref, qi_ref, kin_ref, wi_ref, *rest, n_pages, k_top):
    ki_pages, k_pages, v_pages = rest[:n_pages], rest[n_pages:2 * n_pages], rest[2 * n_pages:3 * n_pages]
    o_ref, key_ref = rest[3 * n_pages], rest[3 * n_pages + 1]
    R = SUBLANES
    HP = ATT_HEADS * PAGE_SIZE
    t_pad = key_ref.shape[1]
    past_len = n_pages * PAGE_SIZE

    qi = qi_ref[0].astype(jnp.bfloat16)
    wi = _round_bf16(wi_ref[0])

    def idx_tile(ki_page):
        sc = _dot_nt(qi, ki_page) * IDX_DIM ** -0.5
        return jnp.sum(_round_bf16(jnp.maximum(sc, 0.0)) * wi, axis=0, keepdims=True)

    first_row = lax.broadcasted_iota(jnp.int32, (PAGE_SIZE, 1), 0) == 0
    kin_page = jnp.where(first_row, kin_ref[0], 0.0)
    isc = jnp.concatenate([idx_tile(ki_pages[j][0, 0]) for j in range(n_pages)] + [idx_tile(kin_page)], axis=1)
    col = lax.broadcasted_iota(jnp.int32, (R, t_pad), 1)
    sel = _topk_mask(jnp.broadcast_to(isc, (R, t_pad)), col <= past_len, key_ref, k_top)

    expand = (lax.broadcasted_iota(jnp.int32, (PAGE_SIZE, HP), 1) // ATT_HEADS
              == lax.broadcasted_iota(jnp.int32, (PAGE_SIZE, HP), 0)).astype(jnp.bfloat16)
    sel_rows = jnp.concatenate([jnp.where(sel[:, j * PAGE_SIZE:(j + 1) * PAGE_SIZE], 1.0, 0.0)
                                for j in range(n_pages + 1)], axis=0).astype(jnp.bfloat16)
    sel_hp = jnp.dot(sel_rows, expand, preferred_element_type=jnp.float32)

    lane = lax.broadcasted_iota(jnp.int32, (R, HP), 1)
    hrow = lax.broadcasted_iota(jnp.int32, (R, HP), 0) % ATT_HEADS
    own = lane % ATT_HEADS == hrow
    slope = sum(jnp.where(hrow == h, 2.0 ** (-8.0 * (h + 1) / ATT_HEADS), 0.0) for h in range(ATT_HEADS))
    q8 = q_ref[0].astype(jnp.bfloat16)
    zeros = jnp.zeros((HP - R, ATT_HEAD_DIM), jnp.float32)
    k_all = [k_pages[j][0, 0] for j in range(n_pages)] + [jnp.concatenate([kn_ref[0], zeros], axis=0)]
    v_all = [v_pages[j][0, 0] for j in range(n_pages)] + [jnp.concatenate([vn_ref[0], zeros], axis=0)]
    tiles = []
    for j, kp in enumerate(k_all):
        dist = (past_len - j * PAGE_SIZE - lane // ATT_HEADS).astype(jnp.float32)
        lg = _dot_nt(q8, kp) * ATT_HEAD_DIM ** -0.5 - slope * dist
        tiles.append(jnp.where(own & (sel_hp[j * R:(j + 1) * R] > 0.5), lg, -jnp.inf))
    lg = jnp.concatenate(tiles, axis=1)
    e = jnp.exp(lg - jnp.max(lg, axis=1, keepdims=True))
    p = (e * (1.0 / jnp.sum(e, axis=1, keepdims=True))).astype(jnp.bfloat16)
    acc = jnp.zeros((R, ATT_HEAD_DIM), jnp.float32)
    for j, vp in enumerate(v_all):
        acc = acc + _dot(p[:, j * HP:(j + 1) * HP], vp)
    o_ref[0] = acc[:ATT_HEADS]


def _dsa_sample_pallas(q, k, v, qi, ki, wi, cache_k, cache_v, cache_idx_k, page_table, l):
    B, Q = q.shape[:2]
    assert Q == 1 and PAGE_SIZE == LANES
    n_pages = page_table.shape[1]
    n_pool = cache_k.shape[1]
    k_top = min(TOPK_MAX, (n_pages * PAGE_SIZE + Q) // 4)
    t_pad = (n_pages + 1) * PAGE_SIZE
    HP = ATT_HEADS * PAGE_SIZE
    head_spec = pl.BlockSpec((1, SUBLANES, ATT_HEAD_DIM), lambda b, pt: (b, 0, 0))
    page_spec = lambda rows, w, j: pl.BlockSpec((1, 1, rows, w), lambda b, pt: (l, pt[b, j], 0, 0))
    grid_spec = pltpu.PrefetchScalarGridSpec(
        num_scalar_prefetch=1,
        grid=(B,),
        in_specs=[head_spec, head_spec, head_spec,
                  pl.BlockSpec((1, IDX_HEADS, IDX_DIM), lambda b, pt: (b, 0, 0)),
                  pl.BlockSpec((1, 1, IDX_DIM), lambda b, pt: (b, 0, 0)),
                  pl.BlockSpec((1, IDX_HEADS, 1), lambda b, pt: (b, 0, 0))]
                 + [page_spec(PAGE_SIZE, IDX_DIM, j) for j in range(n_pages)]
                 + [page_spec(HP, ATT_HEAD_DIM, j) for j in range(n_pages)] * 2,
        out_specs=pl.BlockSpec((1, ATT_HEADS, ATT_HEAD_DIM), lambda b, pt: (b, 0, 0)),
        scratch_shapes=[pltpu.VMEM((SUBLANES, t_pad), jnp.int32)])
    heads = lambda t: jnp.pad(t.reshape(B, ATT_HEADS, ATT_HEAD_DIM), [(0, 0), (0, SUBLANES - ATT_HEADS), (0, 0)])
    pool = lambda c: c.reshape(c.shape[0], n_pool, HP, ATT_HEAD_DIM)
    out = pl.pallas_call(
        functools.partial(_dsa_sample_body, n_pages=n_pages, k_top=k_top),
        grid_spec=grid_spec,
        out_shape=jax.ShapeDtypeStruct((B, ATT_HEADS, ATT_HEAD_DIM), jnp.float32),
        compiler_params=pltpu.CompilerParams(dimension_semantics=("arbitrary",),
                                             vmem_limit_bytes=VMEM_LIMIT_BYTES),
        name="dsa_sample",
    )(page_table, heads(q), heads(k), heads(v),
      qi.reshape(B, IDX_HEADS, IDX_DIM), ki, wi.reshape(B, IDX_HEADS, 1),
      *([cache_idx_k] * n_pages), *([pool(cache_k)] * n_pages), *([pool(cache_v)] * n_pages))
    return out.reshape(B, 1, ATT_HEADS, ATT_HEAD_DIM)


RWKV_GROUP = 8
RWKV_BB = 2


def _rwkv_scan_body(r_ref, d_ref, k_ref, v_ref, kk_ref, a_ref, s0_ref, y_ref, sT_ref, S_scr, Sb_scr):
    BB, tc = r_ref.shape[:2]
    P, N = S_scr.shape[1:3]
    G = RWKV_GROUP
    chains = [(b, p) for b in range(BB) for p in range(P)]
    c = pl.program_id(1)

    @pl.when(c == 0)
    def _():
        S_scr[...] = s0_ref[...]
        Sb_scr[...] = s0_ref[...].astype(jnp.bfloat16)

    lo_half = lax.broadcasted_iota(jnp.int32, (1, LANES), 1) < N
    sub = lax.broadcasted_iota(jnp.int32, (G, N), 0)
    sub2 = lax.broadcasted_iota(jnp.int32, (G, LANES), 0)
    r_mask = ((sub2 == 0) & lo_half) | ((sub2 == 1) & jnp.logical_not(lo_half))
    ii = lax.broadcasted_iota(jnp.int32, (N, LANES), 0)
    jj = lax.broadcasted_iota(jnp.int32, (N, LANES), 1)
    diagA, diagB = jj == ii, jj == ii + N

    def group(g, carry):
        t0 = pl.multiple_of(g * G, G)
        ld = lambda ref, ch: ref[ch[0], pl.ds(t0, G), ch[1] * LANES:(ch[1] + 1) * LANES]
        kk8 = [ld(kk_ref, ch) for ch in chains]
        beta8 = [kk8[i] * ld(a_ref, ch) for i, ch in enumerate(chains)]
        d8 = [ld(d_ref, ch) for ch in chains]
        k8 = [ld(k_ref, ch) for ch in chains]
        r8 = [ld(r_ref, ch) for ch in chains]
        v8 = [ld(v_ref, ch) for ch in chains]
        kkA = [jnp.where(lo_half, x, 0.0).astype(jnp.bfloat16) for x in kk8]
        kkB = [jnp.where(lo_half, 0.0, x).astype(jnp.bfloat16) for x in kk8]
        yA = [jnp.zeros((G, N), jnp.float32) for _ in chains]
        yB = [jnp.zeros((G, N), jnp.float32) for _ in chains]

        def y_step(t, i):
            b, p = chains[i]
            r2 = jnp.where(r_mask, r8[i][t:t + 1], 0.0).astype(jnp.bfloat16)
            y2 = _dot_nt(r2, Sb_scr[b, p])
            yA[i] = jnp.where(sub == t, jnp.broadcast_to(y2[0:1], (G, N)), yA[i])
            yB[i] = jnp.where(sub == t, jnp.broadcast_to(y2[1:2], (G, N)), yB[i])

        for t in range(G):
            row = lambda x: x[t:t + 1]
            sa_b = []
            for i, (b, p) in enumerate(chains):
                w1 = jnp.concatenate([jnp.broadcast_to(row(kkA[i]), (N, LANES)),
                                      jnp.broadcast_to(row(kkB[i]), (N, LANES))], axis=0)
                sa_b.append(_dot_nt(Sb_scr[b, p], w1))
            if t > 0:
                for i in range(len(chains)):
                    y_step(t - 1, i)
            for i, (b, p) in enumerate(chains):
                vA = jnp.sum(jnp.where(diagA, row(v8[i]), 0.0), axis=1, keepdims=True)
                vB = jnp.sum(jnp.where(diagB, row(v8[i]), 0.0), axis=1, keepdims=True)
                v_b = jnp.where(lo_half, vA, vB)
                S = S_scr[b, p] * row(d8[i]) - sa_b[i] * row(beta8[i]) + v_b * row(k8[i])
                S_scr[b, p] = S
                Sb_scr[b, p] = S.astype(jnp.bfloat16)
        for i, (b, p) in enumerate(chains):
            y_step(G - 1, i)
            y_ref[b, pl.ds(t0, G), p * LANES:(p + 1) * LANES] = jnp.concatenate([yA[i], yB[i]], axis=1)
        return carry

    lax.fori_loop(0, tc // G, group, 0)

    @pl.when(c == pl.num_programs(1) - 1)
    def _():
        sT_ref[...] = S_scr[...]


def _rwkv_scan(r, d, k, v, kk, a, S0):
    B, T, W = r.shape
    H, N = S0.shape[1:3]
    P = H // 2
    G = RWKV_GROUP
    bb = RWKV_BB if B % RWKV_BB == 0 else 1
    t_pad = pl.cdiv(T, G) * G
    tc = t_pad
    for cand in range(min(t_pad, 512), G - 1, -1):
        if t_pad % cand == 0 and cand % G == 0:
            tc = cand
            break
    pad = lambda x, c=0.0: jnp.pad(x, [(0, 0), (0, t_pad - T), (0, 0)], constant_values=c) if t_pad != T else x
    S2 = S0.reshape(B, P, 2, N, N).transpose(0, 1, 3, 2, 4).reshape(B, P, N, 2 * N)
    seq_spec = pl.BlockSpec((bb, tc, W), lambda b, c: (b, c, 0))
    st_spec = pl.BlockSpec((bb, P, N, 2 * N), lambda b, c: (b, 0, 0, 0))
    y, S = pl.pallas_call(
        _rwkv_scan_body,
        grid=(B // bb, t_pad // tc),
        in_specs=[seq_spec] * 6 + [st_spec],
        out_specs=[seq_spec, st_spec],
        out_shape=[jax.ShapeDtypeStruct((B, t_pad, W), jnp.float32),
                   jax.ShapeDtypeStruct((B, P, N, 2 * N), jnp.float32)],
        scratch_shapes=[pltpu.VMEM((bb, P, N, 2 * N), jnp.float32), pltpu.VMEM((bb, P, N, 2 * N), jnp.bfloat16)],
        compiler_params=pltpu.CompilerParams(dimension_semantics=("arbitrary", "arbitrary"),
                                             vmem_limit_bytes=VMEM_LIMIT_BYTES),
        name="rwkv_scan",
    )(pad(r), pad(d, 1.0), pad(k), pad(v), pad(kk), pad(a), S2)
    S = S.reshape(B, P, N, 2, N).transpose(0, 1, 3, 2, 4).reshape(B, H, N, N)
    return y[:, :T], S


def _bf(x):
    return x.astype(jnp.bfloat16)


def _dot(a, b):
    return jnp.dot(_bf(a), _bf(b), preferred_element_type=jnp.float32)


def _dot_nt(a, b):
    return lax.dot_general(_bf(a), _bf(b), (((1,), (1,)), ((), ())), preferred_element_type=jnp.float32)


def _dot_f32(a, b):
    return jnp.dot(a, b, precision=lax.Precision.HIGHEST, preferred_element_type=jnp.float32)


GDN_BB = 2


def _gdn_chunk_body(q_ref, k_ref, v_ref, g_ref, beta_ref, grow_ref, s0_ref, o_ref, sT_ref, S_scr):
    BB, L = q_ref.shape[:2]
    H, D = S_scr.shape[1:3]
    chains = [(b, h) for b in range(BB) for h in range(H)]
    c = pl.program_id(1)

    @pl.when(c == 0)
    def _():
        S_scr[...] = s0_ref[...]

    ii = lax.broadcasted_iota(jnp.int32, (L, L), 0)
    jj = lax.broadcasted_iota(jnp.int32, (L, L), 1)
    tri, stri = jj <= ii, jj < ii
    eye = jnp.where(ii == jj, 1.0, 0.0)
    every = lambda f: [f(i) for i in range(len(chains))]
    hs = lambda h: slice(h * D, (h + 1) * D)
    q = [q_ref[b, :, hs(h)] for b, h in chains]
    k = [k_ref[b, :, hs(h)] for b, h in chains]
    g_col = [g_ref[b, :, h:h + 1] for b, h in chains]
    b_col = [beta_ref[b, :, h:h + 1] for b, h in chains]
    g_row = [grow_ref[b, 0, h:h + 1, :] for b, h in chains]
    gc_col = every(lambda i: jnp.sum(jnp.where(tri, g_row[i], 0.0), axis=1, keepdims=True))
    gc_row = every(lambda i: jnp.sum(jnp.where(ii <= jj, g_col[i], 0.0), axis=0, keepdims=True))
    dec = every(lambda i: jnp.exp(jnp.where(tri, gc_col[i] - gc_row[i], -jnp.inf)))
    kb = every(lambda i: k[i] * b_col[i])
    egc = every(lambda i: jnp.exp(gc_col[i]))
    A = every(lambda i: jnp.where(stri, _dot_nt(kb[i], k[i]) * dec[i], 0.0))
    attn = every(lambda i: jnp.where(tri, _dot_nt(q[i], k[i]) * dec[i], 0.0))
    rhs = [jnp.concatenate([v_ref[b, :, hs(h)] * b_col[i], kb[i] * egc[i]], axis=1) for i, (b, h) in enumerate(chains)]
    inv = every(lambda i: eye - A[i])
    P = every(lambda i: _dot_f32(A[i], A[i]))
    n_lvl = int(math.log2(L)) - 1
    for lvl in range(n_lvl):
        inv = every(lambda i: inv[i] + _dot_f32(inv[i], P[i]))
        if lvl + 1 < n_lvl:
            P = every(lambda i: _dot_f32(P[i], P[i]))
    sol = every(lambda i: _dot_f32(inv[i], rhs[i]))
    S = [S_scr[b, h] for b, h in chains]
    v_new = every(lambda i: sol[i][:, :D] - _dot(sol[i][:, D:], S[i]))
    o_state = every(lambda i: _dot(q[i] * egc[i], S[i]))
    g_last = every(lambda i: gc_col[i][L - 1:L, :])
    k_sc = every(lambda i: (k[i] * jnp.exp(g_last[i] - gc_col[i])).T)
    for i, (b, h) in enumerate(chains):
        o_ref[b, :, hs(h)] = o_state[i] + _dot(attn[i], v_new[i])
    for i, (b, h) in enumerate(chains):
        S_scr[b, h] = S[i] * jnp.exp(g_last[i]) + _dot(k_sc[i], v_new[i])

    @pl.when(c == pl.num_programs(1) - 1)
    def _():
        sT_ref[...] = S_scr[...]


def _gdn_chunks(q, k, v, g, beta, S0, L):
    B, T, W = q.shape
    H, D = S0.shape[1:3]
    n = T // L
    bb = GDN_BB if B % GDN_BB == 0 else 1
    g_row = jnp.swapaxes(g.reshape(B, n, L, H), 2, 3)
    seq = pl.BlockSpec((bb, L, W), lambda b, c: (b, c, 0))
    gate = pl.BlockSpec((bb, L, H), lambda b, c: (b, c, 0))
    st = pl.BlockSpec((bb, H, D, D), lambda b, c: (b, 0, 0, 0))
    return pl.pallas_call(
        _gdn_chunk_body,
        grid=(B // bb, n),
        in_specs=[seq, seq, seq, gate, gate, pl.BlockSpec((bb, 1, H, L), lambda b, c: (b, c, 0, 0)), st],
        out_specs=[seq, st],
        out_shape=[jax.ShapeDtypeStruct((B, T, W), jnp.float32),
                   jax.ShapeDtypeStruct((B, H, D, D), jnp.float32)],
        scratch_shapes=[pltpu.VMEM((bb, H, D, D), jnp.float32)],
        compiler_params=pltpu.CompilerParams(dimension_semantics=("arbitrary", "arbitrary"),
                                             vmem_limit_bytes=VMEM_LIMIT_BYTES),
        name="gdn_chunks",
    )(q, k, v, g, beta, g_row, S0)


def _layer_norm(x, g, b, eps=1e-5):
    xf = x.astype(jnp.float32)
    mu = jnp.mean(xf, -1, keepdims=True)
    var = jnp.mean(jnp.square(xf - mu), -1, keepdims=True)
    return ((xf - mu) * lax.rsqrt(var + eps) * g + b).astype(x.dtype)


def _rms_norm(x, g, eps=1e-6):
    xf = x.astype(jnp.float32)
    return (xf * lax.rsqrt(jnp.mean(xf * xf, -1, keepdims=True) + eps) * g).astype(x.dtype)


def _l2norm(x, eps=1e-6):
    xf = x.astype(jnp.float32)
    return xf * lax.rsqrt(jnp.sum(xf * xf, -1, keepdims=True) + eps)


def _proj_in(x, w_in):
    B, T, _ = x.shape
    sizes = [W_MIX,
             3 * W_MIX, GDN_HEADS, GDN_HEADS, W_MIX,
             W_MIX, W_MIX, W_MIX, IDX_HEADS * IDX_DIM, IDX_DIM, IDX_HEADS,
             3 * W_MIX, MLSTM_HEADS, MLSTM_HEADS, W_MIX,
             N_BRANCH * D_MODEL]
    offs = np.concatenate([[0], np.cumsum(sizes)]).tolist()
    xb = x.reshape(B * T, D_MODEL).astype(jnp.bfloat16)
    narrow = [i for i, s in enumerate(sizes) if s < LANES]
    outs = [None] * len(sizes)
    for i, s in enumerate(sizes):
        if i not in narrow:
            outs[i] = _mm(xb, w_in[:, offs[i]:offs[i + 1]]).reshape(B, T, s)
    small = _mm(xb, jnp.concatenate([w_in[:, offs[i]:offs[i + 1]] for i in narrow], axis=1))
    o = 0
    for i in narrow:
        outs[i] = small[:, o:o + sizes[i]].reshape(B, T, sizes[i])
        o += sizes[i]
    return outs


def _take_rows(rows, idx):
    return jax.vmap(lambda r, i: r[i])(rows, idx)


def _short_conv(u, buf, w):
    T = u.shape[1]
    ext = jnp.concatenate([buf.astype(jnp.float32), u.astype(jnp.float32)], axis=1)
    y = sum(ext[:, j:j + T] * w[j] for j in range(CONV_W))
    return jax.nn.silu(y), ext[:, ext.shape[1] - (CONV_W - 1):]


def _run_chunks(chunk_fn, state, xs, is_prompt, chunk):
    if not is_prompt:
        state, out = chunk_fn(state, xs)
        return out, state
    state, out_meta = chunk_fn(state, tuple(t[:, :, :N_META] for t in xs))
    real = tuple(t[:, :, N_META:] for t in xs)
    n = real[0].shape[2] // chunk

    def to_chunks(t):
        return jnp.moveaxis(t.reshape(t.shape[:2] + (n, chunk) + t.shape[3:]), 2, 0)

    state, out = lax.scan(chunk_fn, state, tuple(to_chunks(t) for t in real))
    out = jnp.moveaxis(out, 0, 2)
    out = out.reshape(out.shape[:2] + (n * chunk,) + out.shape[4:])
    return jnp.concatenate([out_meta, out], axis=2), state


def _rwkv7(u, S0, shift0, mu, w_rkv, w0, w1, w2, a0, a1, a2, g1, g2, k_k, k_a, r_k, lnx_g, lnx_b):
    B, T, _ = u.shape
    uf = u.astype(jnp.float32)
    prev = jnp.concatenate([shift0[:, None].astype(jnp.float32), uf[:, :-1]], axis=1)
    xx = prev - uf
    xr, xw, xk, xv, xa, xg = (uf + xx * mu[j] for j in range(6))
    r = xr @ w_rkv[0]
    k = xk @ w_rkv[1]
    v = xv @ w_rkv[2]
    w = -jax.nn.softplus(-(w0 + jnp.tanh(xw @ w1) @ w2)) - 0.5
    a = jax.nn.sigmoid(a0 + (xa @ a1) @ a2)
    g = jax.nn.sigmoid(xg @ g1) @ g2
    hd = lambda t: t.reshape(B, T, RWKV_HEADS, RWKV_HEAD_DIM)
    kk = _l2norm(hd(k * k_k))
    k = hd(k * (1.0 + (a - 1.0) * k_a))
    r, v, a = hd(r), hd(v), hd(a)
    decay = jnp.exp(-jnp.exp(hd(w)))

    flat = lambda t: t.reshape(B, T, W_MIX)
    y, S = _rwkv_scan(flat(r), flat(decay), flat(k), flat(v), flat(kk), flat(a), S0.astype(jnp.float32))
    y = hd(y)
    mean = jnp.mean(y, -1, keepdims=True)
    var = jnp.mean(jnp.square(y - mean), -1, keepdims=True)
    y = ((y - mean) * lax.rsqrt(var + RWKV_GN_EPS)).reshape(B, T, W_MIX) * lnx_g + lnx_b
    bonus = jnp.sum(r * k * r_k, -1, keepdims=True) * v
    y = (y + bonus.reshape(B, T, W_MIX)) * g
    return y.astype(u.dtype), S, u[:, -1]


def _gdn_chunk(S, xs):
    q, k, v, g, beta = xs
    L = q.shape[2]
    tri = jnp.tril(jnp.ones((L, L), bool))
    stri = jnp.tril(jnp.ones((L, L), bool), -1)
    gc = jnp.cumsum(g, axis=-1)
    dec = jnp.exp(jnp.where(tri, gc[..., :, None] - gc[..., None, :], -jnp.inf))
    kb = k * beta[..., None]
    A = jnp.where(stri, jnp.einsum('bhid,bhjd->bhij', kb, k) * dec, 0.0)
    rhs = jnp.concatenate([v * beta[..., None], kb * jnp.exp(gc)[..., None]], axis=-1)
    sol = lax.linalg.triangular_solve(A + jnp.eye(L, dtype=A.dtype), rhs, left_side=True,
                                      lower=True, unit_diagonal=True)
    u_, w_ = sol[..., :GDN_HEAD_DIM], sol[..., GDN_HEAD_DIM:]
    v_new = u_ - jnp.einsum('bhld,bhde->bhle', w_, S)
    attn = jnp.where(tri, jnp.einsum('bhid,bhjd->bhij', q, k) * dec, 0.0)
    o = (jnp.einsum('bhld,bhde->bhle', q * jnp.exp(gc)[..., None], S)
         + jnp.einsum('bhij,bhje->bhie', attn, v_new))
    g_last = gc[..., -1]
    S = (S * jnp.exp(g_last)[..., None, None]
         + jnp.einsum('bhld,bhle->bhde', k * jnp.exp(g_last[..., None] - gc)[..., None], v_new))
    return S, o


def _gdn(qkv, a_pre, b_pre, z, S0, buf0, conv_w, A_log, dt_bias, norm_g, is_prompt):
    B, T, _ = qkv.shape
    y, new_buf = _short_conv(qkv, buf0, conv_w)
    q, k, v = jnp.split(y, 3, axis=-1)
    hd4 = lambda t: t.reshape(B, T, GDN_HEADS, GDN_HEAD_DIM)
    q = _l2norm(hd4(q)) * GDN_HEAD_DIM ** -0.5
    k = _l2norm(hd4(k))
    g = -jnp.exp(A_log.astype(jnp.float32)) * jax.nn.softplus(a_pre.astype(jnp.float32) + dt_bias)
    beta = jax.nn.sigmoid(b_pre.astype(jnp.float32))
    if is_prompt:
        pad = (-N_META) % GDN_CHUNK
        fp = lambda t: jnp.pad(t.reshape(B, T, -1), [(0, 0), (pad, 0), (0, 0)])
        o, S = _gdn_chunks(fp(q), fp(k), fp(v), fp(g), fp(beta), S0.astype(jnp.float32), GDN_CHUNK)
        o = hd4(o[:, pad:])
    else:
        hd = lambda t: t.transpose(0, 2, 1, 3)
        S, o = _gdn_chunk(S0.astype(jnp.float32), (hd(q), hd(k), hd(hd4(v)), g.transpose(0, 2, 1),
                                                    beta.transpose(0, 2, 1)))
        o = o.transpose(0, 2, 1, 3)
    o = _rms_norm(o, norm_g)
    o = o * jax.nn.silu(z.astype(jnp.float32).reshape(B, T, GDN_HEADS, GDN_HEAD_DIM))
    return o.reshape(B, T, W_MIX).astype(qkv.dtype), S, new_buf


def _alibi_slopes():
    return jnp.asarray(2.0 ** (-8.0 * np.arange(1, ATT_HEADS + 1) / ATT_HEADS), jnp.float32)


def _dsa_inputs(c_q, c_k, c_v, c_qi, c_ki, c_wi, ln_g, ln_b):
    B, T, _ = c_q.shape
    hd = lambda t: t.reshape(B, T, ATT_HEADS, ATT_HEAD_DIM)
    qi = c_qi.reshape(B, T, IDX_HEADS, IDX_DIM)
    ki = _layer_norm(c_ki, ln_g, ln_b)
    wi = c_wi * IDX_HEADS ** -0.5
    return hd(c_q), hd(c_k), hd(c_v), qi, ki, wi


def _dsa_block(q, qi, wi, q_pos, ki, gather, k_top):
    S = ki.shape[1]
    s_pos = jnp.arange(S, dtype=jnp.int32)
    sc = jnp.einsum('bqhd,bsd->bqhs', qi.astype(jnp.float32), ki.astype(jnp.float32)) * IDX_DIM ** -0.5
    isc = jnp.einsum('bqhs,bqh->bqs', jax.nn.relu(sc), wi.astype(jnp.float32))
    isc = jnp.where((s_pos[None, :] <= q_pos[:, None])[None], isc, -jnp.inf)
    _, sel = lax.top_k(isc, k_top)
    valid = sel <= q_pos[None, :, None]
    k_sel, v_sel = gather(sel)
    logits = jnp.einsum('bqhd,bqkhd->bqhk', q.astype(jnp.float32), k_sel.astype(jnp.float32)) * ATT_HEAD_DIM ** -0.5
    dist = (q_pos[None, :, None] - sel).astype(jnp.float32)
    logits = logits - _alibi_slopes()[None, None, :, None] * dist[:, :, None, :]
    logits = jnp.where(valid[:, :, None, :], logits, -jnp.inf)
    p = jax.nn.softmax(logits, axis=-1)
    return jnp.einsum('bqhk,bqkhd->bqhd', p, v_sel.astype(jnp.float32)).astype(q.dtype)


def _dsa_prompt(q, k, v, qi, ki, wi):
    B, T = q.shape[:2]
    k_top = min(TOPK_MAX, T // 4)
    n_blk = -(-T // Q_BLOCK)
    t_pad = n_blk * Q_BLOCK

    def blocks(t):
        t = jnp.pad(t, [(0, 0), (0, t_pad - T)] + [(0, 0)] * (t.ndim - 2))
        return jnp.swapaxes(t.reshape((B, n_blk, Q_BLOCK) + t.shape[2:]), 0, 1)

    def gather(sel):
        return _take_rows(k, sel), _take_rows(v, sel)

    def body(args):
        q_b, qi_b, wi_b, pos_b = args
        return _dsa_block(q_b, qi_b, wi_b, pos_b, ki, gather, k_top)

    pos = jnp.arange(t_pad, dtype=jnp.int32).reshape(n_blk, Q_BLOCK)
    out = lax.map(body, (blocks(q), blocks(qi), blocks(wi), pos))
    return jnp.swapaxes(out, 0, 1).reshape(B, t_pad, ATT_HEADS, ATT_HEAD_DIM)[:, :T]


def _dsa_sample(q, k, v, qi, ki, wi, k_pool, v_pool, i_pool, page_table):
    nb, Q = q.shape[:2]
    past_len = page_table.shape[1] * PAGE_SIZE
    ki_past = i_pool[page_table].reshape(nb, past_len, IDX_DIM)
    ki_all = jnp.concatenate([ki_past.astype(jnp.float32), ki.astype(jnp.float32)], axis=1)
    k_top = min(TOPK_MAX, (past_len + Q) // 4)

    def gather(sel):
        in_past = (sel < past_len)[..., None, None]
        sp = jnp.minimum(sel, past_len - 1)
        phys = page_table[jnp.arange(nb)[:, None, None], sp // PAGE_SIZE]
        off = sp % PAGE_SIZE
        sn = jnp.clip(sel - past_len, 0, Q - 1)
        return (jnp.where(in_past, k_pool[phys, off], _take_rows(k, sn)),
                jnp.where(in_past, v_pool[phys, off], _take_rows(v, sn)))

    q_pos = past_len + jnp.arange(Q, dtype=jnp.int32)
    return _dsa_block(q, qi, wi, q_pos, ki_all, gather, k_top)


def _mlstm_chunk(state, xs):
    C, n, m = state
    q, k, v, li, lf = xs
    L = q.shape[2]
    tri = jnp.tril(jnp.ones((L, L), bool))
    b = jnp.cumsum(lf, axis=-1)
    a = li - b
    m_t = b + jnp.maximum(m[..., None], lax.cummax(a, axis=2))
    dmat = jnp.exp(jnp.where(tri, b[..., :, None] + a[..., None, :] - m_t[..., :, None], -jnp.inf))
    inter = jnp.exp(b + m[..., None] - m_t)
    s = jnp.einsum('bhtd,bhsd->bhts', q, k) * dmat
    num = inter[..., None] * jnp.einsum('bhtd,bhde->bhte', q, C) + jnp.einsum('bhts,bhse->bhte', s, v)
    den = inter * jnp.einsum('bhtd,bhd->bht', q, n) + jnp.sum(s, axis=-1)
    h = num / jnp.maximum(jnp.abs(den), jnp.exp(-m_t))[..., None]
    m_new = m_t[..., -1]
    wend = jnp.exp(b[..., -1:] + a - m_new[..., None])
    carry = jnp.exp(b[..., -1] + m - m_new)
    C = carry[..., None, None] * C + jnp.einsum('bhs,bhsd,bhse->bhde', wend, k, v)
    n = carry[..., None] * n + jnp.einsum('bhs,bhsd->bhd', wend, k)
    return (C, n, m_new), h


def _mlstm(qkv, i_pre, f_pre, o_pre, C0, n0, m0, b_i, b_f, norm_g, is_prompt):
    B, T, _ = qkv.shape
    q, k, v = jnp.split(qkv.astype(jnp.float32), 3, axis=-1)
    hd = lambda t: t.reshape(B, T, MLSTM_HEADS, MLSTM_HEAD_DIM).transpose(0, 2, 1, 3)
    q, k, v = hd(q), hd(k) * MLSTM_HEAD_DIM ** -0.5, hd(v)
    li = (i_pre.astype(jnp.float32) + b_i).transpose(0, 2, 1)
    lf = jax.nn.log_sigmoid(f_pre.astype(jnp.float32) + b_f).transpose(0, 2, 1)
    st0 = (C0.astype(jnp.float32), n0.astype(jnp.float32), m0.astype(jnp.float32))
    h, (C, n, m) = _run_chunks(_mlstm_chunk, st0, (q, k, v, li, lf), is_prompt, MLSTM_CHUNK)
    h = _rms_norm(h.transpose(0, 2, 1, 3), norm_g).reshape(B, T, W_MIX)
    h = jax.nn.sigmoid(o_pre.astype(jnp.float32)) * h
    return h.astype(qkv.dtype), C, n, m


def _moe_experts_body(blk_exp_ref, n_used_ref, x_ref, wg_ref, wu_ref, wd_ref, o_ref, wg_bf, wu_bf, wd_bf):
    i = pl.program_id(0)
    new_expert = (i == 0) | (blk_exp_ref[i] != blk_exp_ref[jnp.maximum(i - 1, 0)])

    @pl.when(new_expert)
    def _():
        wg_bf[...] = wg_ref[0, 0].astype(jnp.bfloat16)
        wu_bf[...] = wu_ref[0, 0].astype(jnp.bfloat16)
        wd_bf[...] = wd_ref[0, 0].astype(jnp.bfloat16)

    @pl.when(i < n_used_ref[0])
    def _():
        xb = x_ref[...].astype(jnp.bfloat16)
        hg = jnp.dot(xb, wg_bf[...], preferred_element_type=jnp.float32)
        hu = jnp.dot(xb, wu_bf[...], preferred_element_type=jnp.float32)
        act = (hg * jax.nn.sigmoid(hg)) * hu
        o_ref[...] = jnp.dot(act.astype(jnp.bfloat16), wd_bf[...], preferred_element_type=jnp.float32)

    @pl.when(i >= n_used_ref[0])
    def _():
        o_ref[...] = jnp.zeros_like(o_ref)


def _moe_experts(xb, blk_exp, n_used, w_g, w_u, w_d, l):
    R, D = xb.shape
    F = w_g.shape[-1]
    n_blocks = R // MOE_BLOCK
    grid_spec = pltpu.PrefetchScalarGridSpec(
        num_scalar_prefetch=2,
        grid=(n_blocks,),
        in_specs=[pl.BlockSpec((MOE_BLOCK, D), lambda i, be, nu: (i, 0)),
                  pl.BlockSpec((1, 1, D, F), lambda i, be, nu: (l, be[i], 0, 0)),
                  pl.BlockSpec((1, 1, D, F), lambda i, be, nu: (l, be[i], 0, 0)),
                  pl.BlockSpec((1, 1, F, D), lambda i, be, nu: (l, be[i], 0, 0))],
        out_specs=pl.BlockSpec((MOE_BLOCK, D), lambda i, be, nu: (i, 0)),
        scratch_shapes=[pltpu.VMEM((D, F), jnp.bfloat16), pltpu.VMEM((D, F), jnp.bfloat16),
                        pltpu.VMEM((F, D), jnp.bfloat16)])
    return pl.pallas_call(
        _moe_experts_body,
        grid_spec=grid_spec,
        out_shape=jax.ShapeDtypeStruct((R, D), jnp.float32),
        compiler_params=pltpu.CompilerParams(dimension_semantics=("arbitrary",),
                                             vmem_limit_bytes=VMEM_LIMIT_BYTES),
        name="moe_experts",
    )(blk_exp, n_used, xb, w_g, w_u, w_d)


def _expert_dispatch(xt, eid, gate, w_g, w_u, w_d, l):
    N = xt.shape[0]
    A = N * TOP_K_INNER
    e_flat = eid.reshape(-1)
    tok = jnp.arange(A, dtype=jnp.int32) // TOP_K_INNER
    order = jnp.argsort(e_flat)
    e_sorted = e_flat[order]
    counts = jnp.bincount(e_flat, length=N_EXPERTS)
    padded = (counts + MOE_BLOCK - 1) // MOE_BLOCK * MOE_BLOCK
    pad_end = jnp.cumsum(padded)
    pad_start = pad_end - padded
    start = jnp.cumsum(counts) - counts
    dest = pad_start[e_sorted] + jnp.arange(A, dtype=jnp.int32) - start[e_sorted]
    n_blocks = -(-A // MOE_BLOCK) + N_EXPERTS
    slot_tok = jnp.full((n_blocks * MOE_BLOCK,), N, jnp.int32).at[dest].set(tok[order])
    blk_exp = jnp.minimum(jnp.searchsorted(pad_end, jnp.arange(n_blocks) * MOE_BLOCK, side='right'), N_EXPERTS - 1)
    x_pad = jnp.concatenate([xt, jnp.zeros((1, xt.shape[1]), xt.dtype)], axis=0)
    n_used = (pad_end[-1:] // MOE_BLOCK).astype(jnp.int32)
    yb = _moe_experts(x_pad[slot_tok], blk_exp.astype(jnp.int32), n_used, w_g, w_u, w_d, l)
    slot = jnp.zeros((A,), jnp.int32).at[order].set(dest).reshape(N, TOP_K_INNER)
    return sum(_round_bf16(yb[slot[:, j]]) * _round_bf16(gate[:, j:j + 1]) for j in range(TOP_K_INNER))


def _moe(x, w_group, w_expert, w_g, w_u, w_d, l):
    shp = x.shape
    xt = x.reshape(-1, D_MODEL)
    N = xt.shape[0]
    xf = xt.astype(jnp.float32)
    pg = jax.nn.softmax(xf @ w_group.astype(jnp.float32), axis=-1)
    p_top, g_sel = lax.top_k(pg, 1)
    le = (xf @ w_expert.astype(jnp.float32)).reshape(N, N_GROUPS, EXPERTS_PER_GROUP)
    le_g = le[jnp.arange(N), g_sel[:, 0]]
    v2, j2 = lax.top_k(le_g, TOP_K_INNER)
    gate = jax.nn.softmax(v2, axis=-1) * p_top
    eid = g_sel * EXPERTS_PER_GROUP + j2
    return _expert_dispatch(xt, eid, gate, w_g, w_u, w_d, l).reshape(shp).astype(x.dtype)


def _layer(x, W, l, st, dsa_fn, is_prompt, alpha):
    B, T, _ = x.shape
    (a_u, b_qkv, b_a, b_b, b_z, c_q, c_k, c_v, c_qi, c_ki, c_wi,
     d_qkv, d_i, d_f, d_o, gates) = _proj_in(x, W['w_in'][l])
    rwkv_S0, rwkv_shift0, gdn_S0, gdn_conv0, mC0, mn0, mm0 = st
    y_a, rwkv_S, rwkv_shift = _rwkv7(
        a_u, rwkv_S0, rwkv_shift0, W['rwkv_mu'][l], W['rwkv_w_rkv'][l], W['rwkv_w0'][l], W['rwkv_w1'][l],
        W['rwkv_w2'][l], W['rwkv_a0'][l], W['rwkv_a1'][l], W['rwkv_a2'][l], W['rwkv_g1'][l], W['rwkv_g2'][l],
        W['rwkv_k_k'][l], W['rwkv_k_a'][l], W['rwkv_r_k'][l], W['rwkv_lnx_g'][l], W['rwkv_lnx_b'][l])
    y_b, gdn_S, gdn_conv = _gdn(b_qkv, b_a, b_b, b_z, gdn_S0, gdn_conv0, W['gdn_conv_w'][l], W['gdn_A_log'][l],
                                W['gdn_dt_bias'][l], W['gdn_norm_g'][l], is_prompt)
    q, k, v, qi, ki, wi = _dsa_inputs(c_q, c_k, c_v, c_qi, c_ki, c_wi, W['idx_ln_g'][l], W['idx_ln_b'][l])
    y_c = dsa_fn(q, k, v, qi, ki, wi).reshape(B, T, W_MIX)
    y_d, mC, mn, mm = _mlstm(d_qkv, d_i, d_f, d_o, mC0, mn0, mm0, W['mlstm_b_i'][l], W['mlstm_b_f'][l],
                             W['mlstm_norm_g'][l], is_prompt)
    merged = 0.0
    for i, y_i in enumerate((y_a, y_b, y_c, y_d)):
        gate_i = jax.nn.sigmoid(gates[..., i * D_MODEL:(i + 1) * D_MODEL].astype(jnp.float32))
        merged = merged + gate_i * (y_i @ W['w_branch'][l, i])
    mix = merged.astype(x.dtype) @ W['w_out'][l]
    x = _layer_norm(alpha * x + mix, W['ln1_g'][l], W['ln1_b'][l])
    ffn = _moe(x, W['moe_w_group'][l], W['moe_w_expert'][l], W['moe_w_gate'], W['moe_w_up'], W['moe_w_down'], l)
    x = _layer_norm(alpha * x + ffn, W['ln2_g'][l], W['ln2_b'][l])
    return x, (k, v, ki, rwkv_S, rwkv_shift, gdn_S, gdn_conv, mC, mn, mm)


def kernel(x_prompt, x_sample, cache_k, cache_v, cache_idx_k, page_table, state_rwkv_S, state_rwkv_shift,
           state_gdn_S, state_gdn_conv, state_mlstm_C, state_mlstm_n, state_mlstm_m, meta_tokens, ln_in_g,
           ln_in_b, w_in, rwkv_mu, rwkv_w_rkv, rwkv_w0, rwkv_w1, rwkv_w2, rwkv_a0, rwkv_a1, rwkv_a2, rwkv_g1,
           rwkv_g2, rwkv_k_k, rwkv_k_a, rwkv_r_k, rwkv_lnx_g, rwkv_lnx_b, gdn_conv_w, gdn_A_log, gdn_dt_bias,
           gdn_norm_g, idx_ln_g, idx_ln_b, mlstm_b_i, mlstm_b_f, mlstm_norm_g, w_branch, w_out, ln1_g, ln1_b,
           ln2_g, ln2_b, moe_w_group, moe_w_expert, moe_w_gate, moe_w_up, moe_w_down):
    W = dict(w_in=w_in, rwkv_mu=rwkv_mu, rwkv_w_rkv=rwkv_w_rkv, rwkv_w0=rwkv_w0, rwkv_w1=rwkv_w1,
             rwkv_w2=rwkv_w2, rwkv_a0=rwkv_a0, rwkv_a1=rwkv_a1, rwkv_a2=rwkv_a2, rwkv_g1=rwkv_g1,
             rwkv_g2=rwkv_g2, rwkv_k_k=rwkv_k_k, rwkv_k_a=rwkv_k_a, rwkv_r_k=rwkv_r_k, rwkv_lnx_g=rwkv_lnx_g,
             rwkv_lnx_b=rwkv_lnx_b, gdn_conv_w=gdn_conv_w, gdn_A_log=gdn_A_log, gdn_dt_bias=gdn_dt_bias,
             gdn_norm_g=gdn_norm_g, idx_ln_g=idx_ln_g, idx_ln_b=idx_ln_b, mlstm_b_i=mlstm_b_i,
             mlstm_b_f=mlstm_b_f, mlstm_norm_g=mlstm_norm_g, w_branch=w_branch, w_out=w_out, ln1_g=ln1_g,
             ln1_b=ln1_b, ln2_g=ln2_g, ln2_b=ln2_b, moe_w_group=moe_w_group, moe_w_expert=moe_w_expert,
             moe_w_gate=moe_w_gate, moe_w_up=moe_w_up, moe_w_down=moe_w_down)
    f32 = jnp.float32
    depth = w_in.shape[0]
    alpha = (2 * depth) ** 0.25
    B = x_prompt.shape[0]
    meta = jnp.broadcast_to(meta_tokens.astype(x_prompt.dtype)[None], (B, N_META, D_MODEL))
    hp = _layer_norm(jnp.concatenate([meta, x_prompt], axis=1), ln_in_g, ln_in_b)
    hs = _layer_norm(x_sample, ln_in_g, ln_in_b)
    zero_state = (jnp.zeros((B, RWKV_HEADS, RWKV_HEAD_DIM, RWKV_HEAD_DIM), f32),
                  jnp.zeros((B, W_MIX), f32),
                  jnp.zeros((B, GDN_HEADS, GDN_HEAD_DIM, GDN_HEAD_DIM), f32),
                  jnp.zeros((B, CONV_W - 1, 3 * W_MIX), f32),
                  jnp.zeros((B, MLSTM_HEADS, MLSTM_HEAD_DIM, MLSTM_HEAD_DIM), f32),
                  jnp.zeros((B, MLSTM_HEADS, MLSTM_HEAD_DIM), f32),
                  jnp.zeros((B, MLSTM_HEADS), f32))
    new_p, new_s = [], []
    for l in range(depth):
        hp, sp = _layer(hp, W, l, zero_state, _dsa_prompt_pallas, True, alpha)
        dsa_s = functools.partial(_dsa_sample_pallas, cache_k=cache_k, cache_v=cache_v, cache_idx_k=cache_idx_k,
                                  page_table=page_table, l=l)
        st_s = (state_rwkv_S[l], state_rwkv_shift[l], state_gdn_S[l], state_gdn_conv[l],
                state_mlstm_C[l], state_mlstm_n[l], state_mlstm_m[l])
        hs, ss = _layer(hs, W, l, st_s, dsa_s, False, alpha)
        new_p.append(sp)
        new_s.append(ss)

    def stack(rows, j):
        return jnp.stack([r[j] for r in rows])

    k_p, v_p, ik_p, rS_p, rsh_p, gS_p, gc_p, mC_p, mn_p, mm_p = (stack(new_p, j) for j in range(10))
    k_s, v_s, ik_s, rS_s, rsh_s, gS_s, gc_s, mC_s, mn_s, mm_s = (stack(new_s, j) for j in range(10))
    y_prompt = hp[:, N_META:]
    y_sample = hs
    return (y_prompt, y_sample, k_p, v_p, ik_p, k_s, v_s, ik_s, rS_p, rS_s, rsh_p, rsh_s,
            gS_p, gS_s, gc_p, gc_s, mC_p, mC_s, mn_p, mn_s, mm_p, mm_s)
```

```python
import math, functools
import jax, jax.numpy as jnp
from jax import lax
import numpy as np
from jax.experimental import pallas as pl
from jax.experimental.pallas import tpu as pltpu

D_MODEL = 2048
N_META = 16
N_BRANCH = 4
W_MIX = D_MODEL // 4
RWKV_HEAD_DIM = 64
RWKV_HEADS = W_MIX // RWKV_HEAD_DIM
RWKV_GN_EPS = 64e-5
GDN_HEADS = 4
GDN_HEAD_DIM = W_MIX // GDN_HEADS
CONV_W = 4
GDN_CHUNK = 64
ATT_HEADS = 4
ATT_HEAD_DIM = W_MIX // ATT_HEADS
IDX_HEADS = 8
IDX_DIM = 64
TOPK_MAX = 256
Q_BLOCK = 64
MLSTM_HEADS = 4
MLSTM_HEAD_DIM = W_MIX // MLSTM_HEADS
MLSTM_CHUNK = 64
N_GROUPS = 4
EXPERTS_PER_GROUP = 8
N_EXPERTS = N_GROUPS * EXPERTS_PER_GROUP
TOP_K_INNER = 2
D_EXPERT = D_MODEL // 4
MOE_BLOCK = 128
PAGE_SIZE = 128

VMEM_LIMIT_BYTES = 56 * 1024 * 1024


def _mm_body(x_ref, w_ref, o_ref):
    o_ref[...] = jnp.dot(x_ref[...].astype(jnp.bfloat16), w_ref[...].astype(jnp.bfloat16),
                         preferred_element_type=jnp.float32)


def _mm(x, w, tm=1024, tn=1024):
    M, K = x.shape
    N = w.shape[1]
    tm = min(tm, M)
    tn = N if N <= 3 * W_MIX else tn
    return pl.pallas_call(
        _mm_body,
        grid=(pl.cdiv(N, tn), pl.cdiv(M, tm)),
        in_specs=[pl.BlockSpec((tm, K), lambda j, i: (i, 0)),
                  pl.BlockSpec((K, tn), lambda j, i: (0, j))],
        out_specs=pl.BlockSpec((tm, tn), lambda j, i: (i, j)),
        out_shape=jax.ShapeDtypeStruct((M, N), jnp.float32),
        compiler_params=pltpu.CompilerParams(dimension_semantics=("arbitrary", "arbitrary"),
                                             vmem_limit_bytes=VMEM_LIMIT_BYTES),
        name="proj_mm",
    )(x, w)


LANES = 128
INT_MIN = -2 ** 31
DSA_TQ = 128
DSA_SEGMENTS = 4


def _round_bf16(x):
    return x.astype(jnp.bfloat16).astype(jnp.float32)


def _topk_mask(isc, valid, key_ref, k_top):
    rows, t_pad = key_ref.shape
    bits = lax.bitcast_convert_type(isc, jnp.int32)
    key = bits ^ ((bits >> 31) & 0x7FFFFFFF)
    key_ref[...] = jnp.where(valid, key, INT_MIN)

    def count_ge(c):
        return jnp.sum(jnp.where(key_ref[...] >= c, 1.0, 0.0), axis=1, keepdims=True)

    kf = float(k_top)
    thr = jnp.where(count_ge(jnp.zeros((rows, 1), jnp.int32)) >= kf, 0, INT_MIN).astype(jnp.int32)

    def bit_step(j, thr):
        cand = thr | jnp.left_shift(jnp.int32(1), 30 - j)
        return jnp.where(count_ge(cand) >= kf, cand, thr)

    thr = lax.fori_loop(0, 31, bit_step, thr)

    key = key_ref[...]
    gt = key > thr
    eq = key == thr
    need = kf - jnp.sum(jnp.where(gt, 1.0, 0.0), axis=1, keepdims=True)
    upper = (lax.broadcasted_iota(jnp.int32, (LANES, LANES), 0)
             < lax.broadcasted_iota(jnp.int32, (LANES, LANES), 1)).astype(jnp.bfloat16)
    eqf = jnp.where(eq, 1.0, 0.0)
    off = jnp.zeros((rows, 1), jnp.float32)
    pre = []
    for c in range(t_pad // LANES):
        e_c = eqf[:, c * LANES:(c + 1) * LANES]
        pre.append(jnp.dot(e_c.astype(jnp.bfloat16), upper, preferred_element_type=jnp.float32) + off)
        off = off + jnp.sum(e_c, axis=1, keepdims=True)
    prefix = jnp.concatenate(pre, axis=1)
    return valid & (gt | (eq & (prefix < need)))


def _dsa_prompt_body(q_ref, qi_ref, wi_ref, kT_ref, v_ref, kiT_ref, o_ref, key_ref, *, k_top, first_block):
    tq, t_pad = key_ref.shape
    i = pl.program_id(1) + first_block
    row = lax.broadcasted_iota(jnp.int32, (tq, t_pad), 0) + i * tq
    col = lax.broadcasted_iota(jnp.int32, (tq, t_pad), 1)
    valid = col <= row

    wi = _round_bf16(wi_ref[0])
    isc = jnp.zeros((tq, t_pad), jnp.float32)
    for h in range(IDX_HEADS):
        sc = jnp.dot(qi_ref[0, h].astype(jnp.bfloat16), kiT_ref[0],
                     preferred_element_type=jnp.float32) * IDX_DIM ** -0.5
        isc = isc + _round_bf16(jnp.maximum(sc, 0.0)) * wi[:, h:h + 1]

    sel = _topk_mask(isc, valid, key_ref, k_top)

    dist = (row - col).astype(jnp.float32)
    for h in range(ATT_HEADS):
        hs = slice(h * ATT_HEAD_DIM, (h + 1) * ATT_HEAD_DIM)
        slope = 2.0 ** (-8.0 * (h + 1) / ATT_HEADS)
        lg = jnp.dot(q_ref[0, :, hs].astype(jnp.bfloat16), kT_ref[0, hs, :],
                     preferred_element_type=jnp.float32) * ATT_HEAD_DIM ** -0.5 - slope * dist
        lg = jnp.where(sel, lg, -jnp.inf)
        e = jnp.exp(lg - jnp.max(lg, axis=1, keepdims=True))
        p = e * (1.0 / jnp.sum(e, axis=1, keepdims=True))
        o_ref[0, :, hs] = jnp.dot(p.astype(jnp.bfloat16), v_ref[0, :, hs], preferred_element_type=jnp.float32)


def _dsa_prompt_pallas(q, k, v, qi, ki, wi):
    B, T = q.shape[:2]
    k_top = min(TOPK_MAX, T // 4)
    tq = DSA_TQ
    t_pad = pl.cdiv(T, LANES) * LANES
    padt = lambda t: jnp.pad(t, [(0, 0), (0, t_pad - T)] + [(0, 0)] * (t.ndim - 2))
    q2 = padt(q.reshape(B, T, W_MIX))
    kT = jnp.swapaxes(padt(k.reshape(B, T, W_MIX)), 1, 2).astype(jnp.bfloat16)
    v2 = padt(v.reshape(B, T, W_MIX)).astype(jnp.bfloat16)
    qi2 = jnp.swapaxes(padt(qi), 1, 2)
    kiT = jnp.swapaxes(padt(ki), 1, 2).astype(jnp.bfloat16)
    wi2 = padt(wi)
    n_q = t_pad // tq
    bounds = sorted({-(-n_q * s // DSA_SEGMENTS) for s in range(DSA_SEGMENTS + 1)})
    outs = []
    for i0, i1 in zip(bounds[:-1], bounds[1:]):
        tk = i1 * tq
        outs.append(pl.pallas_call(
            functools.partial(_dsa_prompt_body, k_top=k_top, first_block=i0),
            grid=(B, i1 - i0),
            in_specs=[pl.BlockSpec((1, tq, W_MIX), lambda b, i, i0=i0: (b, i + i0, 0)),
                      pl.BlockSpec((1, IDX_HEADS, tq, IDX_DIM), lambda b, i, i0=i0: (b, 0, i + i0, 0)),
                      pl.BlockSpec((1, tq, IDX_HEADS), lambda b, i, i0=i0: (b, i + i0, 0)),
                      pl.BlockSpec((1, W_MIX, tk), lambda b, i: (b, 0, 0)),
                      pl.BlockSpec((1, tk, W_MIX), lambda b, i: (b, 0, 0)),
                      pl.BlockSpec((1, IDX_DIM, tk), lambda b, i: (b, 0, 0))],
            out_specs=pl.BlockSpec((1, tq, W_MIX), lambda b, i: (b, i, 0)),
            out_shape=jax.ShapeDtypeStruct((B, (i1 - i0) * tq, W_MIX), jnp.float32),
            scratch_shapes=[pltpu.VMEM((tq, tk), jnp.int32)],
            compiler_params=pltpu.CompilerParams(dimension_semantics=("arbitrary", "arbitrary"),
                                                 vmem_limit_bytes=VMEM_LIMIT_BYTES),
            name="dsa_prompt",
        )(q2, qi2, wi2, kT, v2, kiT))
    out = jnp.concatenate(outs, axis=1)
    return out[:, :T].reshape(B, T, ATT_HEADS, ATT_HEAD_DIM)


SUBLANES = 8


def _dsa_sample_body(pt_ref, q_ref, kn_ref, vn_ref, qi_ref, kin_ref, wi_ref, *rest, n_pages, k_top):
    ki_pages, k_pages, v_pages = rest[:n_pages], rest[n_pages:2 * n_pages], rest[2 * n_pages:3 * n_pages]
    o_ref, key_ref = rest[3 * n_pages], rest[3 * n_pages + 1]
    R = SUBLANES
    HP = ATT_HEADS * PAGE_SIZE
    t_pad = key_ref.shape[1]
    past_len = n_pages * PAGE_SIZE

    qi = qi_ref[0].astype(jnp.bfloat16)
    wi = _round_bf16(wi_ref[0])

    def idx_tile(ki_page):
        sc = _dot_nt(qi, ki_page) * IDX_DIM ** -0.5
        return jnp.sum(_round_bf16(jnp.maximum(sc, 0.0)) * wi, axis=0, keepdims=True)

    first_row = lax.broadcasted_iota(jnp.int32, (PAGE_SIZE, 1), 0) == 0
    kin_page = jnp.where(first_row, kin_ref[0], 0.0)
    isc = jnp.concatenate([idx_tile(ki_pages[j][0, 0]) for j in range(n_pages)] + [idx_tile(kin_page)], axis=1)
    col = lax.broadcasted_iota(jnp.int32, (R, t_pad), 1)
    sel = _topk_mask(jnp.broadcast_to(isc, (R, t_pad)), col <= past_len, key_ref, k_top)

    expand = (lax.broadcasted_iota(jnp.int32, (PAGE_SIZE, HP), 1) // ATT_HEADS
              == lax.broadcasted_iota(jnp.int32, (PAGE_SIZE, HP), 0)).astype(jnp.bfloat16)
    sel_rows = jnp.concatenate([jnp.where(sel[:, j * PAGE_SIZE:(j + 1) * PAGE_SIZE], 1.0, 0.0)
                                for j in range(n_pages + 1)], axis=0).astype(jnp.bfloat16)
    sel_hp = jnp.dot(sel_rows, expand, preferred_element_type=jnp.float32)

    lane = lax.broadcasted_iota(jnp.int32, (R, HP), 1)
    hrow = lax.broadcasted_iota(jnp.int32, (R, HP), 0) % ATT_HEADS
    own = lane % ATT_HEADS == hrow
    slope = sum(jnp.where(hrow == h, 2.0 ** (-8.0 * (h + 1) / ATT_HEADS), 0.0) for h in range(ATT_HEADS))
    q8 = q_ref[0].astype(jnp.bfloat16)
    zeros = jnp.zeros((HP - R, ATT_HEAD_DIM), jnp.float32)
    k_all = [k_pages[j][0, 0] for j in range(n_pages)] + [jnp.concatenate([kn_ref[0], zeros], axis=0)]
    v_all = [v_pages[j][0, 0] for j in range(n_pages)] + [jnp.concatenate([vn_ref[0], zeros], axis=0)]
    tiles = []
    for j, kp in enumerate(k_all):
        dist = (past_len - j * PAGE_SIZE - lane // ATT_HEADS).astype(jnp.float32)
        lg = _dot_nt(q8, kp) * ATT_HEAD_DIM ** -0.5 - slope * dist
        tiles.append(jnp.where(own & (sel_hp[j * R:(j + 1) * R] > 0.5), lg, -jnp.inf))
    lg = jnp.concatenate(tiles, axis=1)
    e = jnp.exp(lg - jnp.max(lg, axis=1, keepdims=True))
    p = (e * (1.0 / jnp.sum(e, axis=1, keepdims=True))).astype(jnp.bfloat16)
    acc = jnp.zeros((R, ATT_HEAD_DIM), jnp.float32)
    for j, vp in enumerate(v_all):
        acc = acc + _dot(p[:, j * HP:(j + 1) * HP], vp)
    o_ref[0] = acc[:ATT_HEADS]


def _dsa_sample_pallas(q, k, v, qi, ki, wi, cache_k, cache_v, cache_idx_k, page_table, l):
    B, Q = q.shape[:2]
    assert Q == 1 and PAGE_SIZE == LANES
    n_pages = page_table.shape[1]
    n_pool = cache_k.shape[1]
    k_top = min(TOPK_MAX, (n_pages * PAGE_SIZE + Q) // 4)
    t_pad = (n_pages + 1) * PAGE_SIZE
    HP = ATT_HEADS * PAGE_SIZE
    head_spec = pl.BlockSpec((1, SUBLANES, ATT_HEAD_DIM), lambda b, pt: (b, 0, 0))
    page_spec = lambda rows, w, j: pl.BlockSpec((1, 1, rows, w), lambda b, pt: (l, pt[b, j], 0, 0))
    grid_spec = pltpu.PrefetchScalarGridSpec(
        num_scalar_prefetch=1,
        grid=(B,),
        in_specs=[head_spec, head_spec, head_spec,
                  pl.BlockSpec((1, IDX_HEADS, IDX_DIM), lambda b, pt: (b, 0, 0)),
                  pl.BlockSpec((1, 1, IDX_DIM), lambda b, pt: (b, 0, 0)),
                  pl.BlockSpec((1, IDX_HEADS, 1), lambda b, pt: (b, 0, 0))]
                 + [page_spec(PAGE_SIZE, IDX_DIM, j) for j in range(n_pages)]
                 + [page_spec(HP, ATT_HEAD_DIM, j) for j in range(n_pages)] * 2,
        out_specs=pl.BlockSpec((1, ATT_HEADS, ATT_HEAD_DIM), lambda b, pt: (b, 0, 0)),
        scratch_shapes=[pltpu.VMEM((SUBLANES, t_pad), jnp.int32)])
    heads = lambda t: jnp.pad(t.reshape(B, ATT_HEADS, ATT_HEAD_DIM), [(0, 0), (0, SUBLANES - ATT_HEADS), (0, 0)])
    pool = lambda c: c.reshape(c.shape[0], n_pool, HP, ATT_HEAD_DIM)
    out = pl.pallas_call(
        functools.partial(_dsa_sample_body, n_pages=n_pages, k_top=k_top),
        grid_spec=grid_spec,
        out_shape=jax.ShapeDtypeStruct((B, ATT_HEADS, ATT_HEAD_DIM), jnp.float32),
        compiler_params=pltpu.CompilerParams(dimension_semantics=("arbitrary",),
                                             vmem_limit_bytes=VMEM_LIMIT_BYTES),
        name="dsa_sample",
    )(page_table, heads(q), heads(k), heads(v),
      qi.reshape(B, IDX_HEADS, IDX_DIM), ki, wi.reshape(B, IDX_HEADS, 1),
      *([cache_idx_k] * n_pages), *([pool(cache_k)] * n_pages), *([pool(cache_v)] * n_pages))
    return out.reshape(B, 1, ATT_HEADS, ATT_HEAD_DIM)


RWKV_GROUP = 8
RWKV_BB = 2


def _rwkv_scan_body(r_ref, d_ref, k_ref, v_ref, kk_ref, a_ref, s0_ref, y_ref, sT_ref, S_scr, Sb_scr):
    BB, tc = r_ref.shape[:2]
    P, N = S_scr.shape[1:3]
    G = RWKV_GROUP
    chains = [(b, p) for b in range(BB) for p in range(P)]
    c = pl.program_id(1)

    @pl.when(c == 0)
    def _():
        S_scr[...] = s0_ref[...]
        Sb_scr[...] = s0_ref[...].astype(jnp.bfloat16)

    lo_half = lax.broadcasted_iota(jnp.int32, (1, LANES), 1) < N
    sub = lax.broadcasted_iota(jnp.int32, (G, N), 0)
    sub2 = lax.broadcasted_iota(jnp.int32, (G, LANES), 0)
    r_mask = ((sub2 == 0) & lo_half) | ((sub2 == 1) & jnp.logical_not(lo_half))
    ii = lax.broadcasted_iota(jnp.int32, (N, LANES), 0)
    jj = lax.broadcasted_iota(jnp.int32, (N, LANES), 1)
    diagA, diagB = jj == ii, jj == ii + N

    def group(g, carry):
        t0 = pl.multiple_of(g * G, G)
        ld = lambda ref, ch: ref[ch[0], pl.ds(t0, G), ch[1] * LANES:(ch[1] + 1) * LANES]
        kk8 = [ld(kk_ref, ch) for ch in chains]
        beta8 = [kk8[i] * ld(a_ref, ch) for i, ch in enumerate(chains)]
        d8 = [ld(d_ref, ch) for ch in chains]
        k8 = [ld(k_ref, ch) for ch in chains]
        r8 = [ld(r_ref, ch) for ch in chains]
        v8 = [ld(v_ref, ch) for ch in chains]
        kkA = [jnp.where(lo_half, x, 0.0).astype(jnp.bfloat16) for x in kk8]
        kkB = [jnp.where(lo_half, 0.0, x).astype(jnp.bfloat16) for x in kk8]
        yA = [jnp.zeros((G, N), jnp.float32) for _ in chains]
        yB = [jnp.zeros((G, N), jnp.float32) for _ in chains]

        def y_step(t, i):
            b, p = chains[i]
            r2 = jnp.where(r_mask, r8[i][t:t + 1], 0.0).astype(jnp.bfloat16)
            y2 = _dot_nt(r2, Sb_scr[b, p])
            yA[i] = jnp.where(sub == t, jnp.broadcast_to(y2[0:1], (G, N)), yA[i])
            yB[i] = jnp.where(sub == t, jnp.broadcast_to(y2[1:2], (G, N)), yB[i])

        for t in range(G):
            row = lambda x: x[t:t + 1]
            sa_b = []
            for i, (b, p) in enumerate(chains):
                w1 = jnp.concatenate([jnp.broadcast_to(row(kkA[i]), (N, LANES)),
                                      jnp.broadcast_to(row(kkB[i]), (N, LANES))], axis=0)
                sa_b.append(_dot_nt(Sb_scr[b, p], w1))
            if t > 0:
                for i in range(len(chains)):
                    y_step(t - 1, i)
            for i, (b, p) in enumerate(chains):
                vA = jnp.sum(jnp.where(diagA, row(v8[i]), 0.0), axis=1, keepdims=True)
                vB = jnp.sum(jnp.where(diagB, row(v8[i]), 0.0), axis=1, keepdims=True)
                v_b = jnp.where(lo_half, vA, vB)
                S = S_scr[b, p] * row(d8[i]) - sa_b[i] * row(beta8[i]) + v_b * row(k8[i])
                S_scr[b, p] = S
                Sb_scr[b, p] = S.astype(jnp.bfloat16)
        for i, (b, p) in enumerate(chains):
            y_step(G - 1, i)
            y_ref[b, pl.ds(t0, G), p * LANES:(p + 1) * LANES] = jnp.concatenate([yA[i], yB[i]], axis=1)
        return carry

    lax.fori_loop(0, tc // G, group, 0)

    @pl.when(c == pl.num_programs(1) - 1)
    def _():
        sT_ref[...] = S_scr[...]


def _rwkv_scan(r, d, k, v, kk, a, S0):
    B, T, W = r.shape
    H, N = S0.shape[1:3]
    P = H // 2
    G = RWKV_GROUP
    bb = RWKV_BB if B % RWKV_BB == 0 else 1
    t_pad = pl.cdiv(T, G) * G
    tc = t_pad
    for cand in range(min(t_pad, 512), G - 1, -1):
        if t_pad % cand == 0 and cand % G == 0:
            tc = cand
            break
    pad = lambda x, c=0.0: jnp.pad(x, [(0, 0), (0, t_pad - T), (0, 0)], constant_values=c) if t_pad != T else x
    S2 = S0.reshape(B, P, 2, N, N).transpose(0, 1, 3, 2, 4).reshape(B, P, N, 2 * N)
    seq_spec = pl.BlockSpec((bb, tc, W), lambda b, c: (b, c, 0))
    st_spec = pl.BlockSpec((bb, P, N, 2 * N), lambda b, c: (b, 0, 0, 0))
    y, S = pl.pallas_call(
        _rwkv_scan_body,
        grid=(B // bb, t_pad // tc),
        in_specs=[seq_spec] * 6 + [st_spec],
        out_specs=[seq_spec, st_spec],
        out_shape=[jax.ShapeDtypeStruct((B, t_pad, W), jnp.float32),
                   jax.ShapeDtypeStruct((B, P, N, 2 * N), jnp.float32)],
        scratch_shapes=[pltpu.VMEM((bb, P, N, 2 * N), jnp.float32), pltpu.VMEM((bb, P, N, 2 * N), jnp.bfloat16)],
        compiler_params=pltpu.CompilerParams(dimension_semantics=("arbitrary", "arbitrary"),
                                             vmem_limit_bytes=VMEM_LIMIT_BYTES),
        name="rwkv_scan",
    )(pad(r), pad(d, 1.0), pad(k), pad(v), pad(kk), pad(a), S2)
    S = S.reshape(B, P, N, 2, N).transpose(0, 1, 3, 2, 4).reshape(B, H, N, N)
    return y[:, :T], S


def _bf(x):
    return x.astype(jnp.bfloat16)


def _dot(a, b):
    return jnp.dot(_bf(a), _bf(b), preferred_element_type=jnp.float32)


def _dot_nt(a, b):
    return lax.dot_general(_bf(a), _bf(b), (((1,), (1,)), ((), ())), preferred_element_type=jnp.float32)


def _dot_f32(a, b):
    return jnp.dot(a, b, precision=lax.Precision.HIGHEST, preferred_element_type=jnp.float32)


GDN_BB = 2


def _gdn_chunk_body(q_ref, k_ref, v_ref, g_ref, beta_ref, grow_ref, s0_ref, o_ref, sT_ref, S_scr):
    BB, L = q_ref.shape[:2]
    H, D = S_scr.shape[1:3]
    chains = [(b, h) for b in range(BB) for h in range(H)]
    c = pl.program_id(1)

    @pl.when(c == 0)
    def _():
        S_scr[...] = s0_ref[...]

    ii = lax.broadcasted_iota(jnp.int32, (L, L), 0)
    jj = lax.broadcasted_iota(jnp.int32, (L, L), 1)
    tri, stri = jj <= ii, jj < ii
    eye = jnp.where(ii == jj, 1.0, 0.0)
    every = lambda f: [f(i) for i in range(len(chains))]
    hs = lambda h: slice(h * D, (h + 1) * D)
    q = [q_ref[b, :, hs(h)] for b, h in chains]
    k = [k_ref[b, :, hs(h)] for b, h in chains]
    g_col = [g_ref[b, :, h:h + 1] for b, h in chains]
    b_col = [beta_ref[b, :, h:h + 1] for b, h in chains]
    g_row = [grow_ref[b, 0, h:h + 1, :] for b, h in chains]
    gc_col = every(lambda i: jnp.sum(jnp.where(tri, g_row[i], 0.0), axis=1, keepdims=True))
    gc_row = every(lambda i: jnp.sum(jnp.where(ii <= jj, g_col[i], 0.0), axis=0, keepdims=True))
    dec = every(lambda i: jnp.exp(jnp.where(tri, gc_col[i] - gc_row[i], -jnp.inf)))
    kb = every(lambda i: k[i] * b_col[i])
    egc = every(lambda i: jnp.exp(gc_col[i]))
    A = every(lambda i: jnp.where(stri, _dot_nt(kb[i], k[i]) * dec[i], 0.0))
    attn = every(lambda i: jnp.where(tri, _dot_nt(q[i], k[i]) * dec[i], 0.0))
    rhs = [jnp.concatenate([v_ref[b, :, hs(h)] * b_col[i], kb[i] * egc[i]], axis=1) for i, (b, h) in enumerate(chains)]
    inv = every(lambda i: eye - A[i])
    P = every(lambda i: _dot_f32(A[i], A[i]))
    n_lvl = int(math.log2(L)) - 1
    for lvl in range(n_lvl):
        inv = every(lambda i: inv[i] + _dot_f32(inv[i], P[i]))
        if lvl + 1 < n_lvl:
            P = every(lambda i: _dot_f32(P[i], P[i]))
    sol = every(lambda i: _dot_f32(inv[i], rhs[i]))
    S = [S_scr[b, h] for b, h in chains]
    v_new = every(lambda i: sol[i][:, :D] - _dot(sol[i][:, D:], S[i]))
    o_state = every(lambda i: _dot(q[i] * egc[i], S[i]))
    g_last = every(lambda i: gc_col[i][L - 1:L, :])
    k_sc = every(lambda i: (k[i] * jnp.exp(g_last[i] - gc_col[i])).T)
    for i, (b, h) in enumerate(chains):
        o_ref[b, :, hs(h)] = o_state[i] + _dot(attn[i], v_new[i])
    for i, (b, h) in enumerate(chains):
        S_scr[b, h] = S[i] * jnp.exp(g_last[i]) + _dot(k_sc[i], v_new[i])

    @pl.when(c == pl.num_programs(1) - 1)
    def _():
        sT_ref[...] = S_scr[...]


def _gdn_chunks(q, k, v, g, beta, S0, L):
    B, T, W = q.shape
    H, D = S0.shape[1:3]
    n = T // L
    bb = GDN_BB if B % GDN_BB == 0 else 1
    g_row = jnp.swapaxes(g.reshape(B, n, L, H), 2, 3)
    seq = pl.BlockSpec((bb, L, W), lambda b, c: (b, c, 0))
    gate = pl.BlockSpec((bb, L, H), lambda b, c: (b, c, 0))
    st = pl.BlockSpec((bb, H, D, D), lambda b, c: (b, 0, 0, 0))
    return pl.pallas_call(
        _gdn_chunk_body,
        grid=(B // bb, n),
        in_specs=[seq, seq, seq, gate, gate, pl.BlockSpec((bb, 1, H, L), lambda b, c: (b, c, 0, 0)), st],
        out_specs=[seq, st],
        out_shape=[jax.ShapeDtypeStruct((B, T, W), jnp.float32),
                   jax.ShapeDtypeStruct((B, H, D, D), jnp.float32)],
        scratch_shapes=[pltpu.VMEM((bb, H, D, D), jnp.float32)],
        compiler_params=pltpu.CompilerParams(dimension_semantics=("arbitrary", "arbitrary"),
                                             vmem_limit_bytes=VMEM_LIMIT_BYTES),
        name="gdn_chunks",
    )(q, k, v, g, beta, g_row, S0)


def _layer_norm(x, g, b, eps=1e-5):
    xf = x.astype(jnp.float32)
    mu = jnp.mean(xf, -1, keepdims=True)
    var = jnp.mean(jnp.square(xf - mu), -1, keepdims=True)
    return ((xf - mu) * lax.rsqrt(var + eps) * g + b).astype(x.dtype)


def _rms_norm(x, g, eps=1e-6):
    xf = x.astype(jnp.float32)
    return (xf * lax.rsqrt(jnp.mean(xf * xf, -1, keepdims=True) + eps) * g).astype(x.dtype)


def _l2norm(x, eps=1e-6):
    xf = x.astype(jnp.float32)
    return xf * lax.rsqrt(jnp.sum(xf * xf, -1, keepdims=True) + eps)


def _proj_in(x, w_in):
    B, T, _ = x.shape
    sizes = [W_MIX,
             3 * W_MIX, GDN_HEADS, GDN_HEADS, W_MIX,
             W_MIX, W_MIX, W_MIX, IDX_HEADS * IDX_DIM, IDX_DIM, IDX_HEADS,
             3 * W_MIX, MLSTM_HEADS, MLSTM_HEADS, W_MIX,
             N_BRANCH * D_MODEL]
    offs = np.concatenate([[0], np.cumsum(sizes)]).tolist()
    xb = x.reshape(B * T, D_MODEL).astype(jnp.bfloat16)
    narrow = [i for i, s in enumerate(sizes) if s < LANES]
    outs = [None] * len(sizes)
    for i, s in enumerate(sizes):
        if i not in narrow:
            outs[i] = _mm(xb, w_in[:, offs[i]:offs[i + 1]]).reshape(B, T, s)
    small = _mm(xb, jnp.concatenate([w_in[:, offs[i]:offs[i + 1]] for i in narrow], axis=1))
    o = 0
    for i in narrow:
        outs[i] = small[:, o:o + sizes[i]].reshape(B, T, sizes[i])
        o += sizes[i]
    return outs


def _take_rows(rows, idx):
    return jax.vmap(lambda r, i: r[i])(rows, idx)


def _short_conv(u, buf, w):
    T = u.shape[1]
    ext = jnp.concatenate([buf.astype(jnp.float32), u.astype(jnp.float32)], axis=1)
    y = sum(ext[:, j:j + T] * w[j] for j in range(CONV_W))
    return jax.nn.silu(y), ext[:, ext.shape[1] - (CONV_W - 1):]


def _run_chunks(chunk_fn, state, xs, is_prompt, chunk):
    if not is_prompt:
        state, out = chunk_fn(state, xs)
        return out, state
    state, out_meta = chunk_fn(state, tuple(t[:, :, :N_META] for t in xs))
    real = tuple(t[:, :, N_META:] for t in xs)
    n = real[0].shape[2] // chunk

    def to_chunks(t):
        return jnp.moveaxis(t.reshape(t.shape[:2] + (n, chunk) + t.shape[3:]), 2, 0)

    state, out = lax.scan(chunk_fn, state, tuple(to_chunks(t) for t in real))
    out = jnp.moveaxis(out, 0, 2)
    out = out.reshape(out.shape[:2] + (n * chunk,) + out.shape[4:])
    return jnp.concatenate([out_meta, out], axis=2), state


def _rwkv7(u, S0, shift0, mu, w_rkv, w0, w1, w2, a0, a1, a2, g1, g2, k_k, k_a, r_k, lnx_g, lnx_b):
    B, T, _ = u.shape
    uf = u.astype(jnp.float32)
    prev = jnp.concatenate([shift0[:, None].astype(jnp.float32), uf[:, :-1]], axis=1)
    xx = prev - uf
    xr, xw, xk, xv, xa, xg = (uf + xx * mu[j] for j in range(6))
    r = xr @ w_rkv[0]
    k = xk @ w_rkv[1]
    v = xv @ w_rkv[2]
    w = -jax.nn.softplus(-(w0 + jnp.tanh(xw @ w1) @ w2)) - 0.5
    a = jax.nn.sigmoid(a0 + (xa @ a1) @ a2)
    g = jax.nn.sigmoid(xg @ g1) @ g2
    hd = lambda t: t.reshape(B, T, RWKV_HEADS, RWKV_HEAD_DIM)
    kk = _l2norm(hd(k * k_k))
    k = hd(k * (1.0 + (a - 1.0) * k_a))
    r, v, a = hd(r), hd(v), hd(a)
    decay = jnp.exp(-jnp.exp(hd(w)))

    flat = lambda t: t.reshape(B, T, W_MIX)
    y, S = _rwkv_scan(flat(r), flat(decay), flat(k), flat(v), flat(kk), flat(a), S0.astype(jnp.float32))
    y = hd(y)
    mean = jnp.mean(y, -1, keepdims=True)
    var = jnp.mean(jnp.square(y - mean), -1, keepdims=True)
    y = ((y - mean) * lax.rsqrt(var + RWKV_GN_EPS)).reshape(B, T, W_MIX) * lnx_g + lnx_b
    bonus = jnp.sum(r * k * r_k, -1, keepdims=True) * v
    y = (y + bonus.reshape(B, T, W_MIX)) * g
    return y.astype(u.dtype), S, u[:, -1]


def _gdn_chunk(S, xs):
    q, k, v, g, beta = xs
    L = q.shape[2]
    tri = jnp.tril(jnp.ones((L, L), bool))
    stri = jnp.tril(jnp.ones((L, L), bool), -1)
    gc = jnp.cumsum(g, axis=-1)
    dec = jnp.exp(jnp.where(tri, gc[..., :, None] - gc[..., None, :], -jnp.inf))
    kb = k * beta[..., None]
    A = jnp.where(stri, jnp.einsum('bhid,bhjd->bhij', kb, k) * dec, 0.0)
    rhs = jnp.concatenate([v * beta[..., None], kb * jnp.exp(gc)[..., None]], axis=-1)
    sol = lax.linalg.triangular_solve(A + jnp.eye(L, dtype=A.dtype), rhs, left_side=True,
                                      lower=True, unit_diagonal=True)
    u_, w_ = sol[..., :GDN_HEAD_DIM], sol[..., GDN_HEAD_DIM:]
    v_new = u_ - jnp.einsum('bhld,bhde->bhle', w_, S)
    attn = jnp.where(tri, jnp.einsum('bhid,bhjd->bhij', q, k) * dec, 0.0)
    o = (jnp.einsum('bhld,bhde->bhle', q * jnp.exp(gc)[..., None], S)
         + jnp.einsum('bhij,bhje->bhie', attn, v_new))
    g_last = gc[..., -1]
    S = (S * jnp.exp(g_last)[..., None, None]
         + jnp.einsum('bhld,bhle->bhde', k * jnp.exp(g_last[..., None] - gc)[..., None], v_new))
    return S, o


def _gdn(qkv, a_pre, b_pre, z, S0, buf0, conv_w, A_log, dt_bias, norm_g, is_prompt):
    B, T, _ = qkv.shape
    y, new_buf = _short_conv(qkv, buf0, conv_w)
    q, k, v = jnp.split(y, 3, axis=-1)
    hd4 = lambda t: t.reshape(B, T, GDN_HEADS, GDN_HEAD_DIM)
    q = _l2norm(hd4(q)) * GDN_HEAD_DIM ** -0.5
    k = _l2norm(hd4(k))
    g = -jnp.exp(A_log.astype(jnp.float32)) * jax.nn.softplus(a_pre.astype(jnp.float32) + dt_bias)
    beta = jax.nn.sigmoid(b_pre.astype(jnp.float32))
    if is_prompt:
        pad = (-N_META) % GDN_CHUNK
        fp = lambda t: jnp.pad(t.reshape(B, T, -1), [(0, 0), (pad, 0), (0, 0)])
        o, S = _gdn_chunks(fp(q), fp(k), fp(v), fp(g), fp(beta), S0.astype(jnp.float32), GDN_CHUNK)
        o = hd4(o[:, pad:])
    else:
        hd = lambda t: t.transpose(0, 2, 1, 3)
        S, o = _gdn_chunk(S0.astype(jnp.float32), (hd(q), hd(k), hd(hd4(v)), g.transpose(0, 2, 1),
                                                    beta.transpose(0, 2, 1)))
        o = o.transpose(0, 2, 1, 3)
    o = _rms_norm(o, norm_g)
    o = o * jax.nn.silu(z.astype(jnp.float32).reshape(B, T, GDN_HEADS, GDN_HEAD_DIM))
    return o.reshape(B, T, W_MIX).astype(qkv.dtype), S, new_buf


def _alibi_slopes():
    return jnp.asarray(2.0 ** (-8.0 * np.arange(1, ATT_HEADS + 1) / ATT_HEADS), jnp.float32)


def _dsa_inputs(c_q, c_k, c_v, c_qi, c_ki, c_wi, ln_g, ln_b):
    B, T, _ = c_q.shape
    hd = lambda t: t.reshape(B, T, ATT_HEADS, ATT_HEAD_DIM)
    qi = c_qi.reshape(B, T, IDX_HEADS, IDX_DIM)
    ki = _layer_norm(c_ki, ln_g, ln_b)
    wi = c_wi * IDX_HEADS ** -0.5
    return hd(c_q), hd(c_k), hd(c_v), qi, ki, wi


def _dsa_block(q, qi, wi, q_pos, ki, gather, k_top):
    S = ki.shape[1]
    s_pos = jnp.arange(S, dtype=jnp.int32)
    sc = jnp.einsum('bqhd,bsd->bqhs', qi.astype(jnp.float32), ki.astype(jnp.float32)) * IDX_DIM ** -0.5
    isc = jnp.einsum('bqhs,bqh->bqs', jax.nn.relu(sc), wi.astype(jnp.float32))
    isc = jnp.where((s_pos[None, :] <= q_pos[:, None])[None], isc, -jnp.inf)
    _, sel = lax.top_k(isc, k_top)
    valid = sel <= q_pos[None, :, None]
    k_sel, v_sel = gather(sel)
    logits = jnp.einsum('bqhd,bqkhd->bqhk', q.astype(jnp.float32), k_sel.astype(jnp.float32)) * ATT_HEAD_DIM ** -0.5
    dist = (q_pos[None, :, None] - sel).astype(jnp.float32)
    logits = logits - _alibi_slopes()[None, None, :, None] * dist[:, :, None, :]
    logits = jnp.where(valid[:, :, None, :], logits, -jnp.inf)
    p = jax.nn.softmax(logits, axis=-1)
    return jnp.einsum('bqhk,bqkhd->bqhd', p, v_sel.astype(jnp.float32)).astype(q.dtype)


def _dsa_prompt(q, k, v, qi, ki, wi):
    B, T = q.shape[:2]
    k_top = min(TOPK_MAX, T // 4)
    n_blk = -(-T // Q_BLOCK)
    t_pad = n_blk * Q_BLOCK

    def blocks(t):
        t = jnp.pad(t, [(0, 0), (0, t_pad - T)] + [(0, 0)] * (t.ndim - 2))
        return jnp.swapaxes(t.reshape((B, n_blk, Q_BLOCK) + t.shape[2:]), 0, 1)

    def gather(sel):
        return _take_rows(k, sel), _take_rows(v, sel)

    def body(args):
        q_b, qi_b, wi_b, pos_b = args
        return _dsa_block(q_b, qi_b, wi_b, pos_b, ki, gather, k_top)

    pos = jnp.arange(t_pad, dtype=jnp.int32).reshape(n_blk, Q_BLOCK)
    out = lax.map(body, (blocks(q), blocks(qi), blocks(wi), pos))
    return jnp.swapaxes(out, 0, 1).reshape(B, t_pad, ATT_HEADS, ATT_HEAD_DIM)[:, :T]


def _dsa_sample(q, k, v, qi, ki, wi, k_pool, v_pool, i_pool, page_table):
    nb, Q = q.shape[:2]
    past_len = page_table.shape[1] * PAGE_SIZE
    ki_past = i_pool[page_table].reshape(nb, past_len, IDX_DIM)
    ki_all = jnp.concatenate([ki_past.astype(jnp.float32), ki.astype(jnp.float32)], axis=1)
    k_top = min(TOPK_MAX, (past_len + Q) // 4)

    def gather(sel):
        in_past = (sel < past_len)[..., None, None]
        sp = jnp.minimum(sel, past_len - 1)
        phys = page_table[jnp.arange(nb)[:, None, None], sp // PAGE_SIZE]
        off = sp % PAGE_SIZE
        sn = jnp.clip(sel - past_len, 0, Q - 1)
        return (jnp.where(in_past, k_pool[phys, off], _take_rows(k, sn)),
                jnp.where(in_past, v_pool[phys, off], _take_rows(v, sn)))

    q_pos = past_len + jnp.arange(Q, dtype=jnp.int32)
    return _dsa_block(q, qi, wi, q_pos, ki_all, gather, k_top)


def _mlstm_chunk(state, xs):
    C, n, m = state
    q, k, v, li, lf = xs
    L = q.shape[2]
    tri = jnp.tril(jnp.ones((L, L), bool))
    b = jnp.cumsum(lf, axis=-1)
    a = li - b
    m_t = b + jnp.maximum(m[..., None], lax.cummax(a, axis=2))
    dmat = jnp.exp(jnp.where(tri, b[..., :, None] + a[..., None, :] - m_t[..., :, None], -jnp.inf))
    inter = jnp.exp(b + m[..., None] - m_t)
    s = jnp.einsum('bhtd,bhsd->bhts', q, k) * dmat
    num = inter[..., None] * jnp.einsum('bhtd,bhde->bhte', q, C) + jnp.einsum('bhts,bhse->bhte', s, v)
    den = inter * jnp.einsum('bhtd,bhd->bht', q, n) + jnp.sum(s, axis=-1)
    h = num / jnp.maximum(jnp.abs(den), jnp.exp(-m_t))[..., None]
    m_new = m_t[..., -1]
    wend = jnp.exp(b[..., -1:] + a - m_new[..., None])
    carry = jnp.exp(b[..., -1] + m - m_new)
    C = carry[..., None, None] * C + jnp.einsum('bhs,bhsd,bhse->bhde', wend, k, v)
    n = carry[..., None] * n + jnp.einsum('bhs,bhsd->bhd', wend, k)
    return (C, n, m_new), h


MLSTM_BB = 2


def _mlstm_chunk_body(qkv_ref, li_ref, lf_ref, lirow_ref, lfrow_ref, c0_ref, n0_ref, m0_ref,
                      h_ref, cT_ref, nT_ref, mT_ref, C_scr, n_scr, m_scr):
    BB, L = qkv_ref.shape[:2]
    H, D = C_scr.shape[1:3]
    W = H * D
    chains = [(b, h) for b in range(BB) for h in range(H)]
    c = pl.program_id(1)

    @pl.when(c == 0)
    def _():
        C_scr[...] = c0_ref[...]
        n_scr[...] = n0_ref[...]
        m_scr[...] = m0_ref[...]

    ii = lax.broadcasted_iota(jnp.int32, (L, L), 0)
    jj = lax.broadcasted_iota(jnp.int32, (L, L), 1)
    tri = jj <= ii
    every = lambda f: [f(i) for i in range(len(chains))]
    col_of = lambda ref: [ref[b, :, h:h + 1] for b, h in chains]
    row_of = lambda ref: [ref[b, 0, h:h + 1, :] for b, h in chains]
    q = [qkv_ref[b, :, h * D:(h + 1) * D] for b, h in chains]
    k = [qkv_ref[b, :, W + h * D:W + (h + 1) * D] * D ** -0.5 for b, h in chains]
    v = [qkv_ref[b, :, 2 * W + h * D:2 * W + (h + 1) * D] for b, h in chains]
    li_c, lf_c, li_r, lf_r = col_of(li_ref), col_of(lf_ref), row_of(lirow_ref), row_of(lfrow_ref)
    C = [C_scr[b, h] for b, h in chains]
    n = [n_scr[b, h] for b, h in chains]
    m = [m_scr[b, h][:, :1] for b, h in chains]
    b_c = every(lambda i: jnp.sum(jnp.where(tri, lf_r[i], 0.0), axis=1, keepdims=True))
    b_r = every(lambda i: jnp.sum(jnp.where(ii <= jj, lf_c[i], 0.0), axis=0, keepdims=True))
    a_c = every(lambda i: li_c[i] - b_c[i])
    a_r = every(lambda i: li_r[i] - b_r[i])
    cmax = every(lambda i: jnp.max(jnp.where(tri, a_r[i], -jnp.inf), axis=1, keepdims=True))
    m_t = every(lambda i: b_c[i] + jnp.maximum(m[i], cmax[i]))
    dmat = every(lambda i: jnp.exp(jnp.where(tri, b_c[i] + a_r[i] - m_t[i], -jnp.inf)))
    inter = every(lambda i: jnp.exp(b_c[i] + m[i] - m_t[i]))
    s = every(lambda i: _dot_nt(q[i], k[i]) * dmat[i])
    qC = every(lambda i: _dot(q[i], C[i]))
    sv = every(lambda i: _dot(s[i], v[i]))
    qn = every(lambda i: jnp.sum(_round_bf16(q[i]) * _round_bf16(n[i]), axis=1, keepdims=True))
    den = every(lambda i: inter[i] * qn[i] + jnp.sum(s[i], axis=1, keepdims=True))
    m_new = every(lambda i: m_t[i][L - 1:L, :])
    b_last = every(lambda i: b_c[i][L - 1:L, :])
    wend = every(lambda i: jnp.exp(b_last[i] + a_c[i] - m_new[i]))
    carry = every(lambda i: jnp.exp(b_last[i] + m[i] - m_new[i]))
    wk = every(lambda i: wend[i] * k[i])
    for i, (b, h) in enumerate(chains):
        num = inter[i] * qC[i] + sv[i]
        h_ref[b, :, h * D:(h + 1) * D] = num / jnp.maximum(jnp.abs(den[i]), jnp.exp(-m_t[i]))
    for i, (b, h) in enumerate(chains):
        C_scr[b, h] = carry[i] * C[i] + _dot(wk[i].T, v[i])
        n_scr[b, h] = carry[i] * n[i] + jnp.sum(_round_bf16(wend[i]) * _round_bf16(k[i]), axis=0, keepdims=True)
        m_scr[b, h] = jnp.broadcast_to(m_new[i], (1, D))

    @pl.when(c == pl.num_programs(1) - 1)
    def _():
        cT_ref[...] = C_scr[...]
        nT_ref[...] = n_scr[...]
        mT_ref[...] = m_scr[...]


def _mlstm_chunks(qkv, li, lf, C0, n0, m0, L):
    B, T, W3 = qkv.shape
    H, D = C0.shape[1:3]
    n_c = T // L
    bb = MLSTM_BB if B % MLSTM_BB == 0 else 1
    rows = lambda t: jnp.swapaxes(t.reshape(B, n_c, L, H), 2, 3)
    gate = pl.BlockSpec((bb, L, H), lambda b, c: (b, c, 0))
    gate_row = pl.BlockSpec((bb, 1, H, L), lambda b, c: (b, c, 0, 0))
    st_c = pl.BlockSpec((bb, H, D, D), lambda b, c: (b, 0, 0, 0))
    st_v = pl.BlockSpec((bb, H, 1, D), lambda b, c: (b, 0, 0, 0))
    vec = jax.ShapeDtypeStruct((B, H, 1, D), jnp.float32)
    h, C, n, m = pl.pallas_call(
        _mlstm_chunk_body,
        grid=(B // bb, n_c),
        in_specs=[pl.BlockSpec((bb, L, W3), lambda b, c: (b, c, 0)), gate, gate, gate_row, gate_row,
                  st_c, st_v, st_v],
        out_specs=[pl.BlockSpec((bb, L, H * D), lambda b, c: (b, c, 0)), st_c, st_v, st_v],
        out_shape=[jax.ShapeDtypeStruct((B, T, H * D), jnp.float32),
                   jax.ShapeDtypeStruct((B, H, D, D), jnp.float32), vec, vec],
        scratch_shapes=[pltpu.VMEM((bb, H, D, D), jnp.float32), pltpu.VMEM((bb, H, 1, D), jnp.float32),
                        pltpu.VMEM((bb, H, 1, D), jnp.float32)],
        compiler_params=pltpu.CompilerParams(dimension_semantics=("arbitrary", "arbitrary"),
                                             vmem_limit_bytes=VMEM_LIMIT_BYTES),
        name="mlstm_chunks",
    )(qkv, li, lf, rows(li), rows(lf), C0, n0[:, :, None, :], jnp.broadcast_to(m0[:, :, None, None], (B, H, 1, D)))
    return h, C, n[:, :, 0, :], m[:, :, 0, 0]


MLSTM_PAD_LOG_I = -1e30


def _mlstm(qkv, i_pre, f_pre, o_pre, C0, n0, m0, b_i, b_f, norm_g, is_prompt):
    B, T, _ = qkv.shape
    if is_prompt:
        pad = (-N_META) % MLSTM_CHUNK
        fp = lambda t, c=0.0: jnp.pad(t, [(0, 0), (pad, 0), (0, 0)], constant_values=c)
        li = fp(i_pre.astype(jnp.float32) + b_i, MLSTM_PAD_LOG_I)
        lf = fp(jax.nn.log_sigmoid(f_pre.astype(jnp.float32) + b_f))
        h, C, n, m = _mlstm_chunks(fp(qkv.astype(jnp.float32)), li, lf, C0.astype(jnp.float32),
                                   n0.astype(jnp.float32), m0.astype(jnp.float32), MLSTM_CHUNK)
        h = _rms_norm(h[:, pad:].reshape(B, T, MLSTM_HEADS, MLSTM_HEAD_DIM), norm_g).reshape(B, T, W_MIX)
        h = jax.nn.sigmoid(o_pre.astype(jnp.float32)) * h
        return h.astype(qkv.dtype), C, n, m
    q, k, v = jnp.split(qkv.astype(jnp.float32), 3, axis=-1)
    hd = lambda t: t.reshape(B, T, MLSTM_HEADS, MLSTM_HEAD_DIM).transpose(0, 2, 1, 3)
    q, k, v = hd(q), hd(k) * MLSTM_HEAD_DIM ** -0.5, hd(v)
    li = (i_pre.astype(jnp.float32) + b_i).transpose(0, 2, 1)
    lf = jax.nn.log_sigmoid(f_pre.astype(jnp.float32) + b_f).transpose(0, 2, 1)
    st0 = (C0.astype(jnp.float32), n0.astype(jnp.float32), m0.astype(jnp.float32))
    h, (C, n, m) = _run_chunks(_mlstm_chunk, st0, (q, k, v, li, lf), is_prompt, MLSTM_CHUNK)
    h = _rms_norm(h.transpose(0, 2, 1, 3), norm_g).reshape(B, T, W_MIX)
    h = jax.nn.sigmoid(o_pre.astype(jnp.float32)) * h
    return h.astype(qkv.dtype), C, n, m


def _moe_experts_body(blk_exp_ref, n_used_ref, x_ref, wg_ref, wu_ref, wd_ref, o_ref, wg_bf, wu_bf, wd_bf):
    i = pl.program_id(0)
    new_expert = (i == 0) | (blk_exp_ref[i] != blk_exp_ref[jnp.maximum(i - 1, 0)])

    @pl.when(new_expert)
    def _():
        wg_bf[...] = wg_ref[0, 0].astype(jnp.bfloat16)
        wu_bf[...] = wu_ref[0, 0].astype(jnp.bfloat16)
        wd_bf[...] = wd_ref[0, 0].astype(jnp.bfloat16)

    @pl.when(i < n_used_ref[0])
    def _():
        xb = x_ref[...].astype(jnp.bfloat16)
        hg = jnp.dot(xb, wg_bf[...], preferred_element_type=jnp.float32)
        hu = jnp.dot(xb, wu_bf[...], preferred_element_type=jnp.float32)
        act = (hg * jax.nn.sigmoid(hg)) * hu
        o_ref[...] = jnp.dot(act.astype(jnp.bfloat16), wd_bf[...],
                             preferred_element_type=jnp.float32).astype(o_ref.dtype)

    @pl.when(i >= n_used_ref[0])
    def _():
        o_ref[...] = jnp.zeros_like(o_ref)


def _moe_experts(xb, blk_exp, n_used, w_g, w_u, w_d, l):
    R, D = xb.shape
    F = w_g.shape[-1]
    n_blocks = R // MOE_BLOCK
    grid_spec = pltpu.PrefetchScalarGridSpec(
        num_scalar_prefetch=2,
        grid=(n_blocks,),
        in_specs=[pl.BlockSpec((MOE_BLOCK, D), lambda i, be, nu: (i, 0)),
                  pl.BlockSpec((1, 1, D, F), lambda i, be, nu: (l, be[i], 0, 0)),
                  pl.BlockSpec((1, 1, D, F), lambda i, be, nu: (l, be[i], 0, 0)),
                  pl.BlockSpec((1, 1, F, D), lambda i, be, nu: (l, be[i], 0, 0))],
        out_specs=pl.BlockSpec((MOE_BLOCK, D), lambda i, be, nu: (i, 0)),
        scratch_shapes=[pltpu.VMEM((D, F), jnp.bfloat16), pltpu.VMEM((D, F), jnp.bfloat16),
                        pltpu.VMEM((F, D), jnp.bfloat16)])
    return pl.pallas_call(
        _moe_experts_body,
        grid_spec=grid_spec,
        out_shape=jax.ShapeDtypeStruct((R, D), jnp.bfloat16),
        compiler_params=pltpu.CompilerParams(dimension_semantics=("arbitrary",),
                                             vmem_limit_bytes=VMEM_LIMIT_BYTES),
        name="moe_experts",
    )(blk_exp, n_used, xb, w_g, w_u, w_d)


def _expert_dispatch(xt, eid, gate, w_g, w_u, w_d, l):
    N = xt.shape[0]
    A = N * TOP_K_INNER
    e_flat = eid.reshape(-1)
    tok = jnp.arange(A, dtype=jnp.int32) // TOP_K_INNER
    order = jnp.argsort(e_flat)
    e_sorted = e_flat[order]
    counts = jnp.bincount(e_flat, length=N_EXPERTS)
    padded = (counts + MOE_BLOCK - 1) // MOE_BLOCK * MOE_BLOCK
    pad_end = jnp.cumsum(padded)
    pad_start = pad_end - padded
    start = jnp.cumsum(counts) - counts
    dest = pad_start[e_sorted] + jnp.arange(A, dtype=jnp.int32) - start[e_sorted]
    n_blocks = -(-A // MOE_BLOCK) + N_EXPERTS
    slot_tok = jnp.full((n_blocks * MOE_BLOCK,), N, jnp.int32).at[dest].set(tok[order])
    blk_exp = jnp.minimum(jnp.searchsorted(pad_end, jnp.arange(n_blocks) * MOE_BLOCK, side='right'), N_EXPERTS - 1)
    x_pad = jnp.concatenate([xt, jnp.zeros((1, xt.shape[1]), xt.dtype)], axis=0).astype(jnp.bfloat16)
    n_used = (pad_end[-1:] // MOE_BLOCK).astype(jnp.int32)
    yb = _moe_experts(x_pad[slot_tok], blk_exp.astype(jnp.int32), n_used, w_g, w_u, w_d, l)
    slot = jnp.zeros((A,), jnp.int32).at[order].set(dest).reshape(N, TOP_K_INNER)
    return sum(yb[slot[:, j]].astype(jnp.float32) * _round_bf16(gate[:, j:j + 1]) for j in range(TOP_K_INNER))


def _moe(x, w_group, w_expert, w_g, w_u, w_d, l):
    shp = x.shape
    xt = x.reshape(-1, D_MODEL)
    N = xt.shape[0]
    xf = xt.astype(jnp.float32)
    pg = jax.nn.softmax(xf @ w_group.astype(jnp.float32), axis=-1)
    p_top, g_sel = lax.top_k(pg, 1)
    le = (xf @ w_expert.astype(jnp.float32)).reshape(N, N_GROUPS, EXPERTS_PER_GROUP)
    le_g = le[jnp.arange(N), g_sel[:, 0]]
    v2, j2 = lax.top_k(le_g, TOP_K_INNER)
    gate = jax.nn.softmax(v2, axis=-1) * p_top
    eid = g_sel * EXPERTS_PER_GROUP + j2
    return _expert_dispatch(xt, eid, gate, w_g, w_u, w_d, l).reshape(shp).astype(x.dtype)


def _layer(x, W, l, st, dsa_fn, is_prompt, alpha):
    B, T, _ = x.shape
    (a_u, b_qkv, b_a, b_b, b_z, c_q, c_k, c_v, c_qi, c_ki, c_wi,
     d_qkv, d_i, d_f, d_o, gates) = _proj_in(x, W['w_in'][l])
    rwkv_S0, rwkv_shift0, gdn_S0, gdn_conv0, mC0, mn0, mm0 = st
    y_a, rwkv_S, rwkv_shift = _rwkv7(
        a_u, rwkv_S0, rwkv_shift0, W['rwkv_mu'][l], W['rwkv_w_rkv'][l], W['rwkv_w0'][l], W['rwkv_w1'][l],
        W['rwkv_w2'][l], W['rwkv_a0'][l], W['rwkv_a1'][l], W['rwkv_a2'][l], W['rwkv_g1'][l], W['rwkv_g2'][l],
        W['rwkv_k_k'][l], W['rwkv_k_a'][l], W['rwkv_r_k'][l], W['rwkv_lnx_g'][l], W['rwkv_lnx_b'][l])
    y_b, gdn_S, gdn_conv = _gdn(b_qkv, b_a, b_b, b_z, gdn_S0, gdn_conv0, W['gdn_conv_w'][l], W['gdn_A_log'][l],
                                W['gdn_dt_bias'][l], W['gdn_norm_g'][l], is_prompt)
    q, k, v, qi, ki, wi = _dsa_inputs(c_q, c_k, c_v, c_qi, c_ki, c_wi, W['idx_ln_g'][l], W['idx_ln_b'][l])
    y_c = dsa_fn(q, k, v, qi, ki, wi).reshape(B, T, W_MIX)
    y_d, mC, mn, mm = _mlstm(d_qkv, d_i, d_f, d_o, mC0, mn0, mm0, W['mlstm_b_i'][l], W['mlstm_b_f'][l],
                             W['mlstm_norm_g'][l], is_prompt)
    merged = 0.0
    for i, y_i in enumerate((y_a, y_b, y_c, y_d)):
        gate_i = jax.nn.sigmoid(gates[..., i * D_MODEL:(i + 1) * D_MODEL].astype(jnp.float32))
        merged = merged + gate_i * (y_i @ W['w_branch'][l, i])
    mix = merged.astype(x.dtype) @ W['w_out'][l]
    x = _layer_norm(alpha * x + mix, W['ln1_g'][l], W['ln1_b'][l])
    ffn = _moe(x, W['moe_w_group'][l], W['moe_w_expert'][l], W['moe_w_gate'], W['moe_w_up'], W['moe_w_down'], l)
    x = _layer_norm(alpha * x + ffn, W['ln2_g'][l], W['ln2_b'][l])
    return x, (k, v, ki, rwkv_S, rwkv_shift, gdn_S, gdn_conv, mC, mn, mm)


def kernel(x_prompt, x_sample, cache_k, cache_v, cache_idx_k, page_table, state_rwkv_S, state_rwkv_shift,
           state_gdn_S, state_gdn_conv, state_mlstm_C, state_mlstm_n, state_mlstm_m, meta_tokens, ln_in_g,
           ln_in_b, w_in, rwkv_mu, rwkv_w_rkv, rwkv_w0, rwkv_w1, rwkv_w2, rwkv_a0, rwkv_a1, rwkv_a2, rwkv_g1,
           rwkv_g2, rwkv_k_k, rwkv_k_a, rwkv_r_k, rwkv_lnx_g, rwkv_lnx_b, gdn_conv_w, gdn_A_log, gdn_dt_bias,
           gdn_norm_g, idx_ln_g, idx_ln_b, mlstm_b_i, mlstm_b_f, mlstm_norm_g, w_branch, w_out, ln1_g, ln1_b,
           ln2_g, ln2_b, moe_w_group, moe_w_expert, moe_w_gate, moe_w_up, moe_w_down):
    W = dict(w_in=w_in, rwkv_mu=rwkv_mu, rwkv_w_rkv=rwkv_w_rkv, rwkv_w0=rwkv_w0, rwkv_w1=rwkv_w1,
             rwkv_w2=rwkv_w2, rwkv_a0=rwkv_a0, rwkv_a1=rwkv_a1, rwkv_a2=rwkv_a2, rwkv_g1=rwkv_g1,
             rwkv_g2=rwkv_g2, rwkv_k_k=rwkv_k_k, rwkv_k_a=rwkv_k_a, rwkv_r_k=rwkv_r_k, rwkv_lnx_g=rwkv_lnx_g,
             rwkv_lnx_b=rwkv_lnx_b, gdn_conv_w=gdn_conv_w, gdn_A_log=gdn_A_log, gdn_dt_bias=gdn_dt_bias,
             gdn_norm_g=gdn_norm_g, idx_ln_g=idx_ln_g, idx_ln_b=idx_ln_b, mlstm_b_i=mlstm_b_i,
             mlstm_b_f=mlstm_b_f, mlstm_norm_g=mlstm_norm_g, w_branch=w_branch, w_out=w_out, ln1_g=ln1_g,
             ln1_b=ln1_b, ln2_g=ln2_g, ln2_b=ln2_b, moe_w_group=moe_w_group, moe_w_expert=moe_w_expert,
             moe_w_gate=moe_w_gate, moe_w_up=moe_w_up, moe_w_down=moe_w_down)
    f32 = jnp.float32
    depth = w_in.shape[0]
    alpha = (2 * depth) ** 0.25
    B = x_prompt.shape[0]
    meta = jnp.broadcast_to(meta_tokens.astype(x_prompt.dtype)[None], (B, N_META, D_MODEL))
    hp = _layer_norm(jnp.concatenate([meta, x_prompt], axis=1), ln_in_g, ln_in_b)
    hs = _layer_norm(x_sample, ln_in_g, ln_in_b)
    zero_state = (jnp.zeros((B, RWKV_HEADS, RWKV_HEAD_DIM, RWKV_HEAD_DIM), f32),
                  jnp.zeros((B, W_MIX), f32),
                  jnp.zeros((B, GDN_HEADS, GDN_HEAD_DIM, GDN_HEAD_DIM), f32),
                  jnp.zeros((B, CONV_W - 1, 3 * W_MIX), f32),
                  jnp.zeros((B, MLSTM_HEADS, MLSTM_HEAD_DIM, MLSTM_HEAD_DIM), f32),
                  jnp.zeros((B, MLSTM_HEADS, MLSTM_HEAD_DIM), f32),
                  jnp.zeros((B, MLSTM_HEADS), f32))
    new_p, new_s = [], []
    for l in range(depth):
        hp, sp = _layer(hp, W, l, zero_state, _dsa_prompt_pallas, True, alpha)
        dsa_s = functools.partial(_dsa_sample_pallas, cache_k=cache_k, cache_v=cache_v, cache_idx_k=cache_idx_k,
                                  page_table=page_table, l=l)
        st_s = (state_rwkv_S[l], state_rwkv_shift[l], state_gdn_S[l], state_gdn_conv[l],
                state_mlstm_C[l], state_mlstm_n[l], state_mlstm_m[l])
        hs, ss = _layer(hs, W, l, st_s, dsa_s, False, alpha)
        new_p.append(sp)
        new_s.append(ss)

    def stack(rows, j):
        return jnp.stack([r[j] for r in rows])

    k_p, v_p, ik_p, rS_p, rsh_p, gS_p, gc_p, mC_p, mn_p, mm_p = (stack(new_p, j) for j in range(10))
    k_s, v_s, ik_s, rS_s, rsh_s, gS_s, gc_s, mC_s, mn_s, mm_s = (stack(new_s, j) for j in range(10))
    y_prompt = hp[:, N_META:]
    y_sample = hs
    return (y_prompt, y_sample, k_p, v_p, ik_p, k_s, v_s, ik_s, rS_p, rS_s, rsh_p, rsh_s,
            gS_p, gS_s, gc_p, gc_s, mC_p, mC_s, mn_p, mn_s, mm_p, mm_s)
```

```python
import math, functools
import jax, jax.numpy as jnp
from jax import lax
import numpy as np
from jax.experimental import pallas as pl
from jax.experimental.pallas import tpu as pltpu

D_MODEL = 2048
N_META = 16
N_BRANCH = 4
W_MIX = D_MODEL // 4
RWKV_HEAD_DIM = 64
RWKV_HEADS = W_MIX // RWKV_HEAD_DIM
RWKV_GN_EPS = 64e-5
GDN_HEADS = 4
GDN_HEAD_DIM = W_MIX // GDN_HEADS
CONV_W = 4
GDN_CHUNK = 64
ATT_HEADS = 4
ATT_HEAD_DIM = W_MIX // ATT_HEADS
IDX_HEADS = 8
IDX_DIM = 64
TOPK_MAX = 256
Q_BLOCK = 64
MLSTM_HEADS = 4
MLSTM_HEAD_DIM = W_MIX // MLSTM_HEADS
MLSTM_CHUNK = 64
N_GROUPS = 4
EXPERTS_PER_GROUP = 8
N_EXPERTS = N_GROUPS * EXPERTS_PER_GROUP
TOP_K_INNER = 2
D_EXPERT = D_MODEL // 4
MOE_BLOCK = 128
PAGE_SIZE = 128

VMEM_LIMIT_BYTES = 56 * 1024 * 1024


def _mm_body(x_ref, w_ref, o_ref):
    o_ref[...] = jnp.dot(x_ref[...].astype(jnp.bfloat16), w_ref[...].astype(jnp.bfloat16),
                         preferred_element_type=jnp.float32)


def _mm(x, w, tm=1024, tn=1024):
    M, K = x.shape
    N = w.shape[1]
    tm = min(tm, M)
    tn = N if N <= 3 * W_MIX else tn
    return pl.pallas_call(
        _mm_body,
        grid=(pl.cdiv(N, tn), pl.cdiv(M, tm)),
        in_specs=[pl.BlockSpec((tm, K), lambda j, i: (i, 0)),
                  pl.BlockSpec((K, tn), lambda j, i: (0, j))],
        out_specs=pl.BlockSpec((tm, tn), lambda j, i: (i, j)),
        out_shape=jax.ShapeDtypeStruct((M, N), jnp.float32),
        compiler_params=pltpu.CompilerParams(dimension_semantics=("arbitrary", "arbitrary"),
                                             vmem_limit_bytes=VMEM_LIMIT_BYTES),
        name="proj_mm",
    )(x, w)


LANES = 128
INT_MIN = -2 ** 31
DSA_TQ = 128
DSA_SEGMENTS = 4


def _round_bf16(x):
    return x.astype(jnp.bfloat16).astype(jnp.float32)


def _topk_mask(isc, valid, key_ref, k_top):
    rows, t_pad = key_ref.shape
    bits = lax.bitcast_convert_type(isc, jnp.int32)
    key = bits ^ ((bits >> 31) & 0x7FFFFFFF)
    key_ref[...] = jnp.where(valid, key, INT_MIN)

    def count_ge(c):
        return jnp.sum(jnp.where(key_ref[...] >= c, 1.0, 0.0), axis=1, keepdims=True)

    kf = float(k_top)
    thr = jnp.where(count_ge(jnp.zeros((rows, 1), jnp.int32)) >= kf, 0, INT_MIN).astype(jnp.int32)

    def bit_step(j, thr):
        cand = thr | jnp.left_shift(jnp.int32(1), 30 - j)
        return jnp.where(count_ge(cand) >= kf, cand, thr)

    thr = lax.fori_loop(0, 31, bit_step, thr)

    key = key_ref[...]
    gt = key > thr
    eq = key == thr
    need = kf - jnp.sum(jnp.where(gt, 1.0, 0.0), axis=1, keepdims=True)
    upper = (lax.broadcasted_iota(jnp.int32, (LANES, LANES), 0)
             < lax.broadcasted_iota(jnp.int32, (LANES, LANES), 1)).astype(jnp.bfloat16)
    eqf = jnp.where(eq, 1.0, 0.0)
    off = jnp.zeros((rows, 1), jnp.float32)
    pre = []
    for c in range(t_pad // LANES):
        e_c = eqf[:, c * LANES:(c + 1) * LANES]
        pre.append(jnp.dot(e_c.astype(jnp.bfloat16), upper, preferred_element_type=jnp.float32) + off)
        off = off + jnp.sum(e_c, axis=1, keepdims=True)
    prefix = jnp.concatenate(pre, axis=1)
    return valid & (gt | (eq & (prefix < need)))


def _dsa_prompt_body(q_ref, qi_ref, wi_ref, kT_ref, v_ref, kiT_ref, o_ref, key_ref, *, k_top, first_block):
    tq, t_pad = key_ref.shape
    i = pl.program_id(1) + first_block
    row = lax.broadcasted_iota(jnp.int32, (tq, t_pad), 0) + i * tq
    col = lax.broadcasted_iota(jnp.int32, (tq, t_pad), 1)
    valid = col <= row

    wi = _round_bf16(wi_ref[0])
    isc = jnp.zeros((tq, t_pad), jnp.float32)
    for h in range(IDX_HEADS):
        sc = jnp.dot(qi_ref[0, h].astype(jnp.bfloat16), kiT_ref[0],
                     preferred_element_type=jnp.float32) * IDX_DIM ** -0.5
        isc = isc + _round_bf16(jnp.maximum(sc, 0.0)) * wi[:, h:h + 1]

    sel = _topk_mask(isc, valid, key_ref, k_top)

    dist = (row - col).astype(jnp.float32)
    for h in range(ATT_HEADS):
        hs = slice(h * ATT_HEAD_DIM, (h + 1) * ATT_HEAD_DIM)
        slope = 2.0 ** (-8.0 * (h + 1) / ATT_HEADS)
        lg = jnp.dot(q_ref[0, :, hs].astype(jnp.bfloat16), kT_ref[0, hs, :],
                     preferred_element_type=jnp.float32) * ATT_HEAD_DIM ** -0.5 - slope * dist
        lg = jnp.where(sel, lg, -jnp.inf)
        e = jnp.exp(lg - jnp.max(lg, axis=1, keepdims=True))
        p = e * (1.0 / jnp.sum(e, axis=1, keepdims=True))
        o_ref[0, :, hs] = jnp.dot(p.astype(jnp.bfloat16), v_ref[0, :, hs], preferred_element_type=jnp.float32)


def _dsa_prompt_pallas(q, k, v, qi, ki, wi):
    B, T = q.shape[:2]
    k_top = min(TOPK_MAX, T // 4)
    tq = DSA_TQ
    t_pad = pl.cdiv(T, LANES) * LANES
    padt = lambda t: jnp.pad(t, [(0, 0), (0, t_pad - T)] + [(0, 0)] * (t.ndim - 2))
    q2 = padt(q.reshape(B, T, W_MIX))
    kT = jnp.swapaxes(padt(k.reshape(B, T, W_MIX)), 1, 2).astype(jnp.bfloat16)
    v2 = padt(v.reshape(B, T, W_MIX)).astype(jnp.bfloat16)
    qi2 = jnp.swapaxes(padt(qi), 1, 2)
    kiT = jnp.swapaxes(padt(ki), 1, 2).astype(jnp.bfloat16)
    wi2 = padt(wi)
    n_q = t_pad // tq
    bounds = sorted({-(-n_q * s // DSA_SEGMENTS) for s in range(DSA_SEGMENTS + 1)})
    outs = []
    for i0, i1 in zip(bounds[:-1], bounds[1:]):
        tk = i1 * tq
        outs.append(pl.pallas_call(
            functools.partial(_dsa_prompt_body, k_top=k_top, first_block=i0),
            grid=(B, i1 - i0),
            in_specs=[pl.BlockSpec((1, tq, W_MIX), lambda b, i, i0=i0: (b, i + i0, 0)),
                      pl.BlockSpec((1, IDX_HEADS, tq, IDX_DIM), lambda b, i, i0=i0: (b, 0, i + i0, 0)),
                      pl.BlockSpec((1, tq, IDX_HEADS), lambda b, i, i0=i0: (b, i + i0, 0)),
                      pl.BlockSpec((1, W_MIX, tk), lambda b, i: (b, 0, 0)),
                      pl.BlockSpec((1, tk, W_MIX), lambda b, i: (b, 0, 0)),
                      pl.BlockSpec((1, IDX_DIM, tk), lambda b, i: (b, 0, 0))],
            out_specs=pl.BlockSpec((1, tq, W_MIX), lambda b, i: (b, i, 0)),
            out_shape=jax.ShapeDtypeStruct((B, (i1 - i0) * tq, W_MIX), jnp.float32),
            scratch_shapes=[pltpu.VMEM((tq, tk), jnp.int32)],
            compiler_params=pltpu.CompilerParams(dimension_semantics=("arbitrary", "arbitrary"),
                                                 vmem_limit_bytes=VMEM_LIMIT_BYTES),
            name="dsa_prompt",
        )(q2, qi2, wi2, kT, v2, kiT))
    out = jnp.concatenate(outs, axis=1)
    return out[:, :T].reshape(B, T, ATT_HEADS, ATT_HEAD_DIM)


SUBLANES = 8


def _dsa_sample_body(pt_ref, q_ref, kn_ref, vn_ref, qi_ref, kin_ref, wi_ref, *rest, n_pages, k_top):
    ki_pages, k_pages, v_pages = rest[:n_pages], rest[n_pages:2 * n_pages], rest[2 * n_pages:3 * n_pages]
    o_ref, key_ref = rest[3 * n_pages], rest[3 * n_pages + 1]
    R = SUBLANES
    HP = ATT_HEADS * PAGE_SIZE
    t_pad = key_ref.shape[1]
    past_len = n_pages * PAGE_SIZE

    qi = qi_ref[0].astype(jnp.bfloat16)
    wi = _round_bf16(wi_ref[0])

    def idx_tile(ki_page):
        sc = _dot_nt(qi, ki_page) * IDX_DIM ** -0.5
        return jnp.sum(_round_bf16(jnp.maximum(sc, 0.0)) * wi, axis=0, keepdims=True)

    first_row = lax.broadcasted_iota(jnp.int32, (PAGE_SIZE, 1), 0) == 0
    kin_page = jnp.where(first_row, kin_ref[0], 0.0)
    isc = jnp.concatenate([idx_tile(ki_pages[j][0, 0]) for j in range(n_pages)] + [idx_tile(kin_page)], axis=1)
    col = lax.broadcasted_iota(jnp.int32, (R, t_pad), 1)
    sel = _topk_mask(jnp.broadcast_to(isc, (R, t_pad)), col <= past_len, key_ref, k_top)

    expand = (lax.broadcasted_iota(jnp.int32, (PAGE_SIZE, HP), 1) // ATT_HEADS
              == lax.broadcasted_iota(jnp.int32, (PAGE_SIZE, HP), 0)).astype(jnp.bfloat16)
    sel_rows = jnp.concatenate([jnp.where(sel[:, j * PAGE_SIZE:(j + 1) * PAGE_SIZE], 1.0, 0.0)
                                for j in range(n_pages + 1)], axis=0).astype(jnp.bfloat16)
    sel_hp = jnp.dot(sel_rows, expand, preferred_element_type=jnp.float32)

    lane = lax.broadcasted_iota(jnp.int32, (R, HP), 1)
    hrow = lax.broadcasted_iota(jnp.int32, (R, HP), 0) % ATT_HEADS
    own = lane % ATT_HEADS == hrow
    slope = sum(jnp.where(hrow == h, 2.0 ** (-8.0 * (h + 1) / ATT_HEADS), 0.0) for h in range(ATT_HEADS))
    q8 = q_ref[0].astype(jnp.bfloat16)
    zeros = jnp.zeros((HP - R, ATT_HEAD_DIM), jnp.float32)
    k_all = [k_pages[j][0, 0] for j in range(n_pages)] + [jnp.concatenate([kn_ref[0], zeros], axis=0)]
    v_all = [v_pages[j][0, 0] for j in range(n_pages)] + [jnp.concatenate([vn_ref[0], zeros], axis=0)]
    tiles = []
    for j, kp in enumerate(k_all):
        dist = (past_len - j * PAGE_SIZE - lane // ATT_HEADS).astype(jnp.float32)
        lg = _dot_nt(q8, kp) * ATT_HEAD_DIM ** -0.5 - slope * dist
        tiles.append(jnp.where(own & (sel_hp[j * R:(j + 1) * R] > 0.5), lg, -jnp.inf))
    lg = jnp.concatenate(tiles, axis=1)
    e = jnp.exp(lg - jnp.max(lg, axis=1, keepdims=True))
    p = (e * (1.0 / jnp.sum(e, axis=1, keepdims=True))).astype(jnp.bfloat16)
    acc = jnp.zeros((R, ATT_HEAD_DIM), jnp.float32)
    for j, vp in enumerate(v_all):
        acc = acc + _dot(p[:, j * HP:(j + 1) * HP], vp)
    o_ref[0] = acc[:ATT_HEADS]


def _dsa_sample_pallas(q, k, v, qi, ki, wi, cache_k, cache_v, cache_idx_k, page_table, l):
    B, Q = q.shape[:2]
    assert Q == 1 and PAGE_SIZE == LANES
    n_pages = page_table.shape[1]
    n_pool = cache_k.shape[1]
    k_top = min(TOPK_MAX, (n_pages * PAGE_SIZE + Q) // 4)
    t_pad = (n_pages + 1) * PAGE_SIZE
    HP = ATT_HEADS * PAGE_SIZE
    head_spec = pl.BlockSpec((1, SUBLANES, ATT_HEAD_DIM), lambda b, pt: (b, 0, 0))
    page_spec = lambda rows, w, j: pl.BlockSpec((1, 1, rows, w), lambda b, pt: (l, pt[b, j], 0, 0))
    grid_spec = pltpu.PrefetchScalarGridSpec(
        num_scalar_prefetch=1,
        grid=(B,),
        in_specs=[head_spec, head_spec, head_spec,
                  pl.BlockSpec((1, IDX_HEADS, IDX_DIM), lambda b, pt: (b, 0, 0)),
                  pl.BlockSpec((1, 1, IDX_DIM), lambda b, pt: (b, 0, 0)),
                  pl.BlockSpec((1, IDX_HEADS, 1), lambda b, pt: (b, 0, 0))]
                 + [page_spec(PAGE_SIZE, IDX_DIM, j) for j in range(n_pages)]
                 + [page_spec(HP, ATT_HEAD_DIM, j) for j in range(n_pages)] * 2,
        out_specs=pl.BlockSpec((1, ATT_HEADS, ATT_HEAD_DIM), lambda b, pt: (b, 0, 0)),
        scratch_shapes=[pltpu.VMEM((SUBLANES, t_pad), jnp.int32)])
    heads = lambda t: jnp.pad(t.reshape(B, ATT_HEADS, ATT_HEAD_DIM), [(0, 0), (0, SUBLANES - ATT_HEADS), (0, 0)])
    pool = lambda c: c.reshape(c.shape[0], n_pool, HP, ATT_HEAD_DIM)
    out = pl.pallas_call(
        functools.partial(_dsa_sample_body, n_pages=n_pages, k_top=k_top),
        grid_spec=grid_spec,
        out_shape=jax.ShapeDtypeStruct((B, ATT_HEADS, ATT_HEAD_DIM), jnp.float32),
        compiler_params=pltpu.CompilerParams(dimension_semantics=("arbitrary",),
                                             vmem_limit_bytes=VMEM_LIMIT_BYTES),
        name="dsa_sample",
    )(page_table, heads(q), heads(k), heads(v),
      qi.reshape(B, IDX_HEADS, IDX_DIM), ki, wi.reshape(B, IDX_HEADS, 1),
      *([cache_idx_k] * n_pages), *([pool(cache_k)] * n_pages), *([pool(cache_v)] * n_pages))
    return out.reshape(B, 1, ATT_HEADS, ATT_HEAD_DIM)


RWKV_GROUP = 8
RWKV_BB = 2


def _rwkv_scan_body(r_ref, d_ref, k_ref, v_ref, kk_ref, a_ref, s0_ref, y_ref, sT_ref, S_scr, Sb_scr):
    BB, tc = r_ref.shape[:2]
    P, N = S_scr.shape[1:3]
    G = RWKV_GROUP
    chains = [(b, p) for b in range(BB) for p in range(P)]
    c = pl.program_id(1)

    @pl.when(c == 0)
    def _():
        S_scr[...] = s0_ref[...]
        Sb_scr[...] = s0_ref[...].astype(jnp.bfloat16)

    lo_half = lax.broadcasted_iota(jnp.int32, (1, LANES), 1) < N
    sub = lax.broadcasted_iota(jnp.int32, (G, N), 0)
    sub2 = lax.broadcasted_iota(jnp.int32, (G, LANES), 0)
    r_mask = ((sub2 == 0) & lo_half) | ((sub2 == 1) & jnp.logical_not(lo_half))
    ii = lax.broadcasted_iota(jnp.int32, (N, LANES), 0)
    jj = lax.broadcasted_iota(jnp.int32, (N, LANES), 1)
    diagA, diagB = jj == ii, jj == ii + N

    def group(g, carry):
        t0 = pl.multiple_of(g * G, G)
        ld = lambda ref, ch: ref[ch[0], pl.ds(t0, G), ch[1] * LANES:(ch[1] + 1) * LANES]
        kk8 = [ld(kk_ref, ch) for ch in chains]
        beta8 = [kk8[i] * ld(a_ref, ch) for i, ch in enumerate(chains)]
        d8 = [ld(d_ref, ch) for ch in chains]
        k8 = [ld(k_ref, ch) for ch in chains]
        r8 = [ld(r_ref, ch) for ch in chains]
        v8 = [ld(v_ref, ch) for ch in chains]
        kkA = [jnp.where(lo_half, x, 0.0).astype(jnp.bfloat16) for x in kk8]
        kkB = [jnp.where(lo_half, 0.0, x).astype(jnp.bfloat16) for x in kk8]
        yA = [jnp.zeros((G, N), jnp.float32) for _ in chains]
        yB = [jnp.zeros((G, N), jnp.float32) for _ in chains]

        def y_step(t, i):
            b, p = chains[i]
            r2 = jnp.where(r_mask, r8[i][t:t + 1], 0.0).astype(jnp.bfloat16)
            y2 = _dot_nt(r2, Sb_scr[b, p])
            yA[i] = jnp.where(sub == t, jnp.broadcast_to(y2[0:1], (G, N)), yA[i])
            yB[i] = jnp.where(sub == t, jnp.broadcast_to(y2[1:2], (G, N)), yB[i])

        for t in range(G):
            row = lambda x: x[t:t + 1]
            sa_b = []
            for i, (b, p) in enumerate(chains):
                w1 = jnp.concatenate([jnp.broadcast_to(row(kkA[i]), (N, LANES)),
                                      jnp.broadcast_to(row(kkB[i]), (N, LANES))], axis=0)
                sa_b.append(_dot_nt(Sb_scr[b, p], w1))
            if t > 0:
                for i in range(len(chains)):
                    y_step(t - 1, i)
            for i, (b, p) in enumerate(chains):
                vA = jnp.sum(jnp.where(diagA, row(v8[i]), 0.0), axis=1, keepdims=True)
                vB = jnp.sum(jnp.where(diagB, row(v8[i]), 0.0), axis=1, keepdims=True)
                v_b = jnp.where(lo_half, vA, vB)
                S = S_scr[b, p] * row(d8[i]) - sa_b[i] * row(beta8[i]) + v_b * row(k8[i])
                S_scr[b, p] = S
                Sb_scr[b, p] = S.astype(jnp.bfloat16)
        for i, (b, p) in enumerate(chains):
            y_step(G - 1, i)
            y_ref[b, pl.ds(t0, G), p * LANES:(p + 1) * LANES] = jnp.concatenate([yA[i], yB[i]], axis=1)
        return carry

    lax.fori_loop(0, tc // G, group, 0)

    @pl.when(c == pl.num_programs(1) - 1)
    def _():
        sT_ref[...] = S_scr[...]


def _rwkv_scan(r, d, k, v, kk, a, S0):
    B, T, W = r.shape
    H, N = S0.shape[1:3]
    P = H // 2
    G = RWKV_GROUP
    bb = RWKV_BB if B % RWKV_BB == 0 else 1
    t_pad = pl.cdiv(T, G) * G
    tc = t_pad
    for cand in range(min(t_pad, 512), G - 1, -1):
        if t_pad % cand == 0 and cand % G == 0:
            tc = cand
            break
    pad = lambda x, c=0.0: jnp.pad(x, [(0, 0), (0, t_pad - T), (0, 0)], constant_values=c) if t_pad != T else x
    S2 = S0.reshape(B, P, 2, N, N).transpose(0, 1, 3, 2, 4).reshape(B, P, N, 2 * N)
    seq_spec = pl.BlockSpec((bb, tc, W), lambda b, c: (b, c, 0))
    st_spec = pl.BlockSpec((bb, P, N, 2 * N), lambda b, c: (b, 0, 0, 0))
    y, S = pl.pallas_call(
        _rwkv_scan_body,
        grid=(B // bb, t_pad // tc),
        in_specs=[seq_spec] * 6 + [st_spec],
        out_specs=[seq_spec, st_spec],
        out_shape=[jax.ShapeDtypeStruct((B, t_pad, W), jnp.float32),
                   jax.ShapeDtypeStruct((B, P, N, 2 * N), jnp.float32)],
        scratch_shapes=[pltpu.VMEM((bb, P, N, 2 * N), jnp.float32), pltpu.VMEM((bb, P, N, 2 * N), jnp.bfloat16)],
        compiler_params=pltpu.CompilerParams(dimension_semantics=("arbitrary", "arbitrary"),
                                             vmem_limit_bytes=VMEM_LIMIT_BYTES),
        name="rwkv_scan",
    )(pad(r), pad(d, 1.0), pad(k), pad(v), pad(kk), pad(a), S2)
    S = S.reshape(B, P, N, 2, N).transpose(0, 1, 3, 2, 4).reshape(B, H, N, N)
    return y[:, :T], S


def _bf(x):
    return x.astype(jnp.bfloat16)


def _dot(a, b):
    return jnp.dot(_bf(a), _bf(b), preferred_element_type=jnp.float32)


def _dot_nt(a, b):
    return lax.dot_general(_bf(a), _bf(b), (((1,), (1,)), ((), ())), preferred_element_type=jnp.float32)


def _dot_f32(a, b):
    return jnp.dot(a, b, precision=lax.Precision.HIGHEST, preferred_element_type=jnp.float32)


GDN_BB = 2


def _gdn_chunk_body(q_ref, k_ref, v_ref, g_ref, beta_ref, grow_ref, s0_ref, o_ref, sT_ref, S_scr):
    BB, L = q_ref.shape[:2]
    H, D = S_scr.shape[1:3]
    chains = [(b, h) for b in range(BB) for h in range(H)]
    c = pl.program_id(1)

    @pl.when(c == 0)
    def _():
        S_scr[...] = s0_ref[...]

    ii = lax.broadcasted_iota(jnp.int32, (L, L), 0)
    jj = lax.broadcasted_iota(jnp.int32, (L, L), 1)
    tri, stri = jj <= ii, jj < ii
    eye = jnp.where(ii == jj, 1.0, 0.0)
    every = lambda f: [f(i) for i in range(len(chains))]
    hs = lambda h: slice(h * D, (h + 1) * D)
    q = [q_ref[b, :, hs(h)] for b, h in chains]
    k = [k_ref[b, :, hs(h)] for b, h in chains]
    g_col = [g_ref[b, :, h:h + 1] for b, h in chains]
    b_col = [beta_ref[b, :, h:h + 1] for b, h in chains]
    g_row = [grow_ref[b, 0, h:h + 1, :] for b, h in chains]
    gc_col = every(lambda i: jnp.sum(jnp.where(tri, g_row[i], 0.0), axis=1, keepdims=True))
    gc_row = every(lambda i: jnp.sum(jnp.where(ii <= jj, g_col[i], 0.0), axis=0, keepdims=True))
    dec = every(lambda i: jnp.exp(jnp.where(tri, gc_col[i] - gc_row[i], -jnp.inf)))
    kb = every(lambda i: k[i] * b_col[i])
    egc = every(lambda i: jnp.exp(gc_col[i]))
    A = every(lambda i: jnp.where(stri, _dot_nt(kb[i], k[i]) * dec[i], 0.0))
    attn = every(lambda i: jnp.where(tri, _dot_nt(q[i], k[i]) * dec[i], 0.0))
    rhs = [jnp.concatenate([v_ref[b, :, hs(h)] * b_col[i], kb[i] * egc[i]], axis=1) for i, (b, h) in enumerate(chains)]
    inv = every(lambda i: eye - A[i])
    P = every(lambda i: _dot_f32(A[i], A[i]))
    n_lvl = int(math.log2(L)) - 1
    for lvl in range(n_lvl):
        inv = every(lambda i: inv[i] + _dot_f32(inv[i], P[i]))
        if lvl + 1 < n_lvl:
            P = every(lambda i: _dot_f32(P[i], P[i]))
    sol = every(lambda i: _dot_f32(inv[i], rhs[i]))
    S = [S_scr[b, h] for b, h in chains]
    v_new = every(lambda i: sol[i][:, :D] - _dot(sol[i][:, D:], S[i]))
    o_state = every(lambda i: _dot(q[i] * egc[i], S[i]))
    g_last = every(lambda i: gc_col[i][L - 1:L, :])
    k_sc = every(lambda i: (k[i] * jnp.exp(g_last[i] - gc_col[i])).T)
    for i, (b, h) in enumerate(chains):
        o_ref[b, :, hs(h)] = o_state[i] + _dot(attn[i], v_new[i])
    for i, (b, h) in enumerate(chains):
        S_scr[b, h] = S[i] * jnp.exp(g_last[i]) + _dot(k_sc[i], v_new[i])

    @pl.when(c == pl.num_programs(1) - 1)
    def _():
        sT_ref[...] = S_scr[...]


def _gdn_chunks(q, k, v, g, beta, S0, L):
    B, T, W = q.shape
    H, D = S0.shape[1:3]
    n = T // L
    bb = GDN_BB if B % GDN_BB == 0 else 1
    g_row = jnp.swapaxes(g.reshape(B, n, L, H), 2, 3)
    seq = pl.BlockSpec((bb, L, W), lambda b, c: (b, c, 0))
    gate = pl.BlockSpec((bb, L, H), lambda b, c: (b, c, 0))
    st = pl.BlockSpec((bb, H, D, D), lambda b, c: (b, 0, 0, 0))
    return pl.pallas_call(
        _gdn_chunk_body,
        grid=(B // bb, n),
        in_specs=[seq, seq, seq, gate, gate, pl.BlockSpec((bb, 1, H, L), lambda b, c: (b, c, 0, 0)), st],
        out_specs=[seq, st],
        out_shape=[jax.ShapeDtypeStruct((B, T, W), jnp.float32),
                   jax.ShapeDtypeStruct((B, H, D, D), jnp.float32)],
        scratch_shapes=[pltpu.VMEM((bb, H, D, D), jnp.float32)],
        compiler_params=pltpu.CompilerParams(dimension_semantics=("arbitrary", "arbitrary"),
                                             vmem_limit_bytes=VMEM_LIMIT_BYTES),
        name="gdn_chunks",
    )(q, k, v, g, beta, g_row, S0)


def _layer_norm(x, g, b, eps=1e-5):
    xf = x.astype(jnp.float32)
    mu = jnp.mean(xf, -1, keepdims=True)
    var = jnp.mean(jnp.square(xf - mu), -1, keepdims=True)
    return ((xf - mu) * lax.rsqrt(var + eps) * g + b).astype(x.dtype)


def _rms_norm(x, g, eps=1e-6):
    xf = x.astype(jnp.float32)
    return (xf * lax.rsqrt(jnp.mean(xf * xf, -1, keepdims=True) + eps) * g).astype(x.dtype)


def _l2norm(x, eps=1e-6):
    xf = x.astype(jnp.float32)
    return xf * lax.rsqrt(jnp.sum(xf * xf, -1, keepdims=True) + eps)


def _proj_in(x, w_in):
    B, T, _ = x.shape
    sizes = [W_MIX,
             3 * W_MIX, GDN_HEADS, GDN_HEADS, W_MIX,
             W_MIX, W_MIX, W_MIX, IDX_HEADS * IDX_DIM, IDX_DIM, IDX_HEADS,
             3 * W_MIX, MLSTM_HEADS, MLSTM_HEADS, W_MIX,
             N_BRANCH * D_MODEL]
    offs = np.concatenate([[0], np.cumsum(sizes)]).tolist()
    xb = x.reshape(B * T, D_MODEL).astype(jnp.bfloat16)
    outs = [None] * len(sizes)

    def grouped(idx):
        y = _mm(xb, jnp.concatenate([w_in[:, offs[i]:offs[i + 1]] for i in idx], axis=1))
        o = 0
        for i in idx:
            outs[i] = y[:, o:o + sizes[i]].reshape(B, T, sizes[i])
            o += sizes[i]

    grouped([i for i, s in enumerate(sizes) if s < LANES])
    grouped([i for i, s in enumerate(sizes) if s == W_MIX])
    for i, s in enumerate(sizes):
        if s > W_MIX:
            outs[i] = _mm(xb, w_in[:, offs[i]:offs[i + 1]]).reshape(B, T, s)
    return outs


def _take_rows(rows, idx):
    return jax.vmap(lambda r, i: r[i])(rows, idx)


def _short_conv(u, buf, w):
    T = u.shape[1]
    ext = jnp.concatenate([buf.astype(jnp.float32), u.astype(jnp.float32)], axis=1)
    y = sum(ext[:, j:j + T] * w[j] for j in range(CONV_W))
    return jax.nn.silu(y), ext[:, ext.shape[1] - (CONV_W - 1):]


def _run_chunks(chunk_fn, state, xs, is_prompt, chunk):
    if not is_prompt:
        state, out = chunk_fn(state, xs)
        return out, state
    state, out_meta = chunk_fn(state, tuple(t[:, :, :N_META] for t in xs))
    real = tuple(t[:, :, N_META:] for t in xs)
    n = real[0].shape[2] // chunk

    def to_chunks(t):
        return jnp.moveaxis(t.reshape(t.shape[:2] + (n, chunk) + t.shape[3:]), 2, 0)

    state, out = lax.scan(chunk_fn, state, tuple(to_chunks(t) for t in real))
    out = jnp.moveaxis(out, 0, 2)
    out = out.reshape(out.shape[:2] + (n * chunk,) + out.shape[4:])
    return jnp.concatenate([out_meta, out], axis=2), state


def _rwkv7(u, S0, shift0, mu, w_rkv, w0, w1, w2, a0, a1, a2, g1, g2, k_k, k_a, r_k, lnx_g, lnx_b):
    B, T, _ = u.shape
    uf = u.astype(jnp.float32)
    prev = jnp.concatenate([shift0[:, None].astype(jnp.float32), uf[:, :-1]], axis=1)
    xx = prev - uf
    xr, xw, xk, xv, xa, xg = (uf + xx * mu[j] for j in range(6))
    r = xr @ w_rkv[0]
    k = xk @ w_rkv[1]
    v = xv @ w_rkv[2]
    w = -jax.nn.softplus(-(w0 + jnp.tanh(xw @ w1) @ w2)) - 0.5
    a = jax.nn.sigmoid(a0 + (xa @ a1) @ a2)
    g = jax.nn.sigmoid(xg @ g1) @ g2
    hd = lambda t: t.reshape(B, T, RWKV_HEADS, RWKV_HEAD_DIM)
    kk = _l2norm(hd(k * k_k))
    k = hd(k * (1.0 + (a - 1.0) * k_a))
    r, v, a = hd(r), hd(v), hd(a)
    decay = jnp.exp(-jnp.exp(hd(w)))

    flat = lambda t: t.reshape(B, T, W_MIX)
    y, S = _rwkv_scan(flat(r), flat(decay), flat(k), flat(v), flat(kk), flat(a), S0.astype(jnp.float32))
    y = hd(y)
    mean = jnp.mean(y, -1, keepdims=True)
    var = jnp.mean(jnp.square(y - mean), -1, keepdims=True)
    y = ((y - mean) * lax.rsqrt(var + RWKV_GN_EPS)).reshape(B, T, W_MIX) * lnx_g + lnx_b
    bonus = jnp.sum(r * k * r_k, -1, keepdims=True) * v
    y = (y + bonus.reshape(B, T, W_MIX)) * g
    return y.astype(u.dtype), S, u[:, -1]


def _gdn_chunk(S, xs):
    q, k, v, g, beta = xs
    L = q.shape[2]
    tri = jnp.tril(jnp.ones((L, L), bool))
    stri = jnp.tril(jnp.ones((L, L), bool), -1)
    gc = jnp.cumsum(g, axis=-1)
    dec = jnp.exp(jnp.where(tri, gc[..., :, None] - gc[..., None, :], -jnp.inf))
    kb = k * beta[..., None]
    A = jnp.where(stri, jnp.einsum('bhid,bhjd->bhij', kb, k) * dec, 0.0)
    rhs = jnp.concatenate([v * beta[..., None], kb * jnp.exp(gc)[..., None]], axis=-1)
    sol = lax.linalg.triangular_solve(A + jnp.eye(L, dtype=A.dtype), rhs, left_side=True,
                                      lower=True, unit_diagonal=True)
    u_, w_ = sol[..., :GDN_HEAD_DIM], sol[..., GDN_HEAD_DIM:]
    v_new = u_ - jnp.einsum('bhld,bhde->bhle', w_, S)
    attn = jnp.where(tri, jnp.einsum('bhid,bhjd->bhij', q, k) * dec, 0.0)
    o = (jnp.einsum('bhld,bhde->bhle', q * jnp.exp(gc)[..., None], S)
         + jnp.einsum('bhij,bhje->bhie', attn, v_new))
    g_last = gc[..., -1]
    S = (S * jnp.exp(g_last)[..., None, None]
         + jnp.einsum('bhld,bhle->bhde', k * jnp.exp(g_last[..., None] - gc)[..., None], v_new))
    return S, o


def _gdn(qkv, a_pre, b_pre, z, S0, buf0, conv_w, A_log, dt_bias, norm_g, is_prompt):
    B, T, _ = qkv.shape
    y, new_buf = _short_conv(qkv, buf0, conv_w)
    q, k, v = jnp.split(y, 3, axis=-1)
    hd4 = lambda t: t.reshape(B, T, GDN_HEADS, GDN_HEAD_DIM)
    q = _l2norm(hd4(q)) * GDN_HEAD_DIM ** -0.5
    k = _l2norm(hd4(k))
    g = -jnp.exp(A_log.astype(jnp.float32)) * jax.nn.softplus(a_pre.astype(jnp.float32) + dt_bias)
    beta = jax.nn.sigmoid(b_pre.astype(jnp.float32))
    if is_prompt:
        pad = (-N_META) % GDN_CHUNK
        fp = lambda t: jnp.pad(t.reshape(B, T, -1), [(0, 0), (pad, 0), (0, 0)])
        o, S = _gdn_chunks(fp(q), fp(k), fp(v), fp(g), fp(beta), S0.astype(jnp.float32), GDN_CHUNK)
        o = hd4(o[:, pad:])
    else:
        hd = lambda t: t.transpose(0, 2, 1, 3)
        S, o = _gdn_chunk(S0.astype(jnp.float32), (hd(q), hd(k), hd(hd4(v)), g.transpose(0, 2, 1),
                                                    beta.transpose(0, 2, 1)))
        o = o.transpose(0, 2, 1, 3)
    o = _rms_norm(o, norm_g)
    o = o * jax.nn.silu(z.astype(jnp.float32).reshape(B, T, GDN_HEADS, GDN_HEAD_DIM))
    return o.reshape(B, T, W_MIX).astype(qkv.dtype), S, new_buf


def _alibi_slopes():
    return jnp.asarray(2.0 ** (-8.0 * np.arange(1, ATT_HEADS + 1) / ATT_HEADS), jnp.float32)


def _dsa_inputs(c_q, c_k, c_v, c_qi, c_ki, c_wi, ln_g, ln_b):
    B, T, _ = c_q.shape
    hd = lambda t: t.reshape(B, T, ATT_HEADS, ATT_HEAD_DIM)
    qi = c_qi.reshape(B, T, IDX_HEADS, IDX_DIM)
    ki = _layer_norm(c_ki, ln_g, ln_b)
    wi = c_wi * IDX_HEADS ** -0.5
    return hd(c_q), hd(c_k), hd(c_v), qi, ki, wi


def _dsa_block(q, qi, wi, q_pos, ki, gather, k_top):
    S = ki.shape[1]
    s_pos = jnp.arange(S, dtype=jnp.int32)
    sc = jnp.einsum('bqhd,bsd->bqhs', qi.astype(jnp.float32), ki.astype(jnp.float32)) * IDX_DIM ** -0.5
    isc = jnp.einsum('bqhs,bqh->bqs', jax.nn.relu(sc), wi.astype(jnp.float32))
    isc = jnp.where((s_pos[None, :] <= q_pos[:, None])[None], isc, -jnp.inf)
    _, sel = lax.top_k(isc, k_top)
    valid = sel <= q_pos[None, :, None]
    k_sel, v_sel = gather(sel)
    logits = jnp.einsum('bqhd,bqkhd->bqhk', q.astype(jnp.float32), k_sel.astype(jnp.float32)) * ATT_HEAD_DIM ** -0.5
    dist = (q_pos[None, :, None] - sel).astype(jnp.float32)
    logits = logits - _alibi_slopes()[None, None, :, None] * dist[:, :, None, :]
    logits = jnp.where(valid[:, :, None, :], logits, -jnp.inf)
    p = jax.nn.softmax(logits, axis=-1)
    return jnp.einsum('bqhk,bqkhd->bqhd', p, v_sel.astype(jnp.float32)).astype(q.dtype)


def _dsa_prompt(q, k, v, qi, ki, wi):
    B, T = q.shape[:2]
    k_top = min(TOPK_MAX, T // 4)
    n_blk = -(-T // Q_BLOCK)
    t_pad = n_blk * Q_BLOCK

    def blocks(t):
        t = jnp.pad(t, [(0, 0), (0, t_pad - T)] + [(0, 0)] * (t.ndim - 2))
        return jnp.swapaxes(t.reshape((B, n_blk, Q_BLOCK) + t.shape[2:]), 0, 1)

    def gather(sel):
        return _take_rows(k, sel), _take_rows(v, sel)

    def body(args):
        q_b, qi_b, wi_b, pos_b = args
        return _dsa_block(q_b, qi_b, wi_b, pos_b, ki, gather, k_top)

    pos = jnp.arange(t_pad, dtype=jnp.int32).reshape(n_blk, Q_BLOCK)
    out = lax.map(body, (blocks(q), blocks(qi), blocks(wi), pos))
    return jnp.swapaxes(out, 0, 1).reshape(B, t_pad, ATT_HEADS, ATT_HEAD_DIM)[:, :T]


def _dsa_sample(q, k, v, qi, ki, wi, k_pool, v_pool, i_pool, page_table):
    nb, Q = q.shape[:2]
    past_len = page_table.shape[1] * PAGE_SIZE
    ki_past = i_pool[page_table].reshape(nb, past_len, IDX_DIM)
    ki_all = jnp.concatenate([ki_past.astype(jnp.float32), ki.astype(jnp.float32)], axis=1)
    k_top = min(TOPK_MAX, (past_len + Q) // 4)

    def gather(sel):
        in_past = (sel < past_len)[..., None, None]
        sp = jnp.minimum(sel, past_len - 1)
        phys = page_table[jnp.arange(nb)[:, None, None], sp // PAGE_SIZE]
        off = sp % PAGE_SIZE
        sn = jnp.clip(sel - past_len, 0, Q - 1)
        return (jnp.where(in_past, k_pool[phys, off], _take_rows(k, sn)),
                jnp.where(in_past, v_pool[phys, off], _take_rows(v, sn)))

    q_pos = past_len + jnp.arange(Q, dtype=jnp.int32)
    return _dsa_block(q, qi, wi, q_pos, ki_all, gather, k_top)


def _mlstm_chunk(state, xs):
    C, n, m = state
    q, k, v, li, lf = xs
    L = q.shape[2]
    tri = jnp.tril(jnp.ones((L, L), bool))
    b = jnp.cumsum(lf, axis=-1)
    a = li - b
    m_t = b + jnp.maximum(m[..., None], lax.cummax(a, axis=2))
    dmat = jnp.exp(jnp.where(tri, b[..., :, None] + a[..., None, :] - m_t[..., :, None], -jnp.inf))
    inter = jnp.exp(b + m[..., None] - m_t)
    s = jnp.einsum('bhtd,bhsd->bhts', q, k) * dmat
    num = inter[..., None] * jnp.einsum('bhtd,bhde->bhte', q, C) + jnp.einsum('bhts,bhse->bhte', s, v)
    den = inter * jnp.einsum('bhtd,bhd->bht', q, n) + jnp.sum(s, axis=-1)
    h = num / jnp.maximum(jnp.abs(den), jnp.exp(-m_t))[..., None]
    m_new = m_t[..., -1]
    wend = jnp.exp(b[..., -1:] + a - m_new[..., None])
    carry = jnp.exp(b[..., -1] + m - m_new)
    C = carry[..., None, None] * C + jnp.einsum('bhs,bhsd,bhse->bhde', wend, k, v)
    n = carry[..., None] * n + jnp.einsum('bhs,bhsd->bhd', wend, k)
    return (C, n, m_new), h


MLSTM_BB = 2


def _mlstm_chunk_body(qkv_ref, li_ref, lf_ref, lirow_ref, lfrow_ref, c0_ref, n0_ref, m0_ref,
                      h_ref, cT_ref, nT_ref, mT_ref, C_scr, n_scr, m_scr):
    BB, L = qkv_ref.shape[:2]
    H, D = C_scr.shape[1:3]
    W = H * D
    chains = [(b, h) for b in range(BB) for h in range(H)]
    c = pl.program_id(1)

    @pl.when(c == 0)
    def _():
        C_scr[...] = c0_ref[...]
        n_scr[...] = n0_ref[...]
        m_scr[...] = m0_ref[...]

    ii = lax.broadcasted_iota(jnp.int32, (L, L), 0)
    jj = lax.broadcasted_iota(jnp.int32, (L, L), 1)
    tri = jj <= ii
    every = lambda f: [f(i) for i in range(len(chains))]
    col_of = lambda ref: [ref[b, :, h:h + 1] for b, h in chains]
    row_of = lambda ref: [ref[b, 0, h:h + 1, :] for b, h in chains]
    q = [qkv_ref[b, :, h * D:(h + 1) * D] for b, h in chains]
    k = [qkv_ref[b, :, W + h * D:W + (h + 1) * D] * D ** -0.5 for b, h in chains]
    v = [qkv_ref[b, :, 2 * W + h * D:2 * W + (h + 1) * D] for b, h in chains]
    li_c, lf_c, li_r, lf_r = col_of(li_ref), col_of(lf_ref), row_of(lirow_ref), row_of(lfrow_ref)
    C = [C_scr[b, h] for b, h in chains]
    n = [n_scr[b, h] for b, h in chains]
    m = [m_scr[b, h][:, :1] for b, h in chains]
    b_c = every(lambda i: jnp.sum(jnp.where(tri, lf_r[i], 0.0), axis=1, keepdims=True))
    b_r = every(lambda i: jnp.sum(jnp.where(ii <= jj, lf_c[i], 0.0), axis=0, keepdims=True))
    a_c = every(lambda i: li_c[i] - b_c[i])
    a_r = every(lambda i: li_r[i] - b_r[i])
    cmax = every(lambda i: jnp.max(jnp.where(tri, a_r[i], -jnp.inf), axis=1, keepdims=True))
    m_t = every(lambda i: b_c[i] + jnp.maximum(m[i], cmax[i]))
    dmat = every(lambda i: jnp.exp(jnp.where(tri, b_c[i] + a_r[i] - m_t[i], -jnp.inf)))
    inter = every(lambda i: jnp.exp(b_c[i] + m[i] - m_t[i]))
    s = every(lambda i: _dot_nt(q[i], k[i]) * dmat[i])
    qC = every(lambda i: _dot(q[i], C[i]))
    sv = every(lambda i: _dot(s[i], v[i]))
    qn = every(lambda i: jnp.sum(_round_bf16(q[i]) * _round_bf16(n[i]), axis=1, keepdims=True))
    den = every(lambda i: inter[i] * qn[i] + jnp.sum(s[i], axis=1, keepdims=True))
    m_new = every(lambda i: m_t[i][L - 1:L, :])
    b_last = every(lambda i: b_c[i][L - 1:L, :])
    wend = every(lambda i: jnp.exp(b_last[i] + a_c[i] - m_new[i]))
    carry = every(lambda i: jnp.exp(b_last[i] + m[i] - m_new[i]))
    wk = every(lambda i: wend[i] * k[i])
    for i, (b, h) in enumerate(chains):
        num = inter[i] * qC[i] + sv[i]
        h_ref[b, :, h * D:(h + 1) * D] = num / jnp.maximum(jnp.abs(den[i]), jnp.exp(-m_t[i]))
    for i, (b, h) in enumerate(chains):
        C_scr[b, h] = carry[i] * C[i] + _dot(wk[i].T, v[i])
        n_scr[b, h] = carry[i] * n[i] + jnp.sum(_round_bf16(wend[i]) * _round_bf16(k[i]), axis=0, keepdims=True)
        m_scr[b, h] = jnp.broadcast_to(m_new[i], (1, D))

    @pl.when(c == pl.num_programs(1) - 1)
    def _():
        cT_ref[...] = C_scr[...]
        nT_ref[...] = n_scr[...]
        mT_ref[...] = m_scr[...]


def _mlstm_chunks(qkv, li, lf, C0, n0, m0, L):
    B, T, W3 = qkv.shape
    H, D = C0.shape[1:3]
    n_c = T // L
    bb = MLSTM_BB if B % MLSTM_BB == 0 else 1
    rows = lambda t: jnp.swapaxes(t.reshape(B, n_c, L, H), 2, 3)
    gate = pl.BlockSpec((bb, L, H), lambda b, c: (b, c, 0))
    gate_row = pl.BlockSpec((bb, 1, H, L), lambda b, c: (b, c, 0, 0))
    st_c = pl.BlockSpec((bb, H, D, D), lambda b, c: (b, 0, 0, 0))
    st_v = pl.BlockSpec((bb, H, 1, D), lambda b, c: (b, 0, 0, 0))
    vec = jax.ShapeDtypeStruct((B, H, 1, D), jnp.float32)
    h, C, n, m = pl.pallas_call(
        _mlstm_chunk_body,
        grid=(B // bb, n_c),
        in_specs=[pl.BlockSpec((bb, L, W3), lambda b, c: (b, c, 0)), gate, gate, gate_row, gate_row,
                  st_c, st_v, st_v],
        out_specs=[pl.BlockSpec((bb, L, H * D), lambda b, c: (b, c, 0)), st_c, st_v, st_v],
        out_shape=[jax.ShapeDtypeStruct((B, T, H * D), jnp.float32),
                   jax.ShapeDtypeStruct((B, H, D, D), jnp.float32), vec, vec],
        scratch_shapes=[pltpu.VMEM((bb, H, D, D), jnp.float32), pltpu.VMEM((bb, H, 1, D), jnp.float32),
                        pltpu.VMEM((bb, H, 1, D), jnp.float32)],
        compiler_params=pltpu.CompilerParams(dimension_semantics=("arbitrary", "arbitrary"),
                                             vmem_limit_bytes=VMEM_LIMIT_BYTES),
        name="mlstm_chunks",
    )(qkv, li, lf, rows(li), rows(lf), C0, n0[:, :, None, :], jnp.broadcast_to(m0[:, :, None, None], (B, H, 1, D)))
    return h, C, n[:, :, 0, :], m[:, :, 0, 0]


MLSTM_PAD_LOG_I = -1e30


def _mlstm(qkv, i_pre, f_pre, o_pre, C0, n0, m0, b_i, b_f, norm_g, is_prompt):
    B, T, _ = qkv.shape
    if is_prompt:
        pad = (-N_META) % MLSTM_CHUNK
        fp = lambda t, c=0.0: jnp.pad(t, [(0, 0), (pad, 0), (0, 0)], constant_values=c)
        li = fp(i_pre.astype(jnp.float32) + b_i, MLSTM_PAD_LOG_I)
        lf = fp(jax.nn.log_sigmoid(f_pre.astype(jnp.float32) + b_f))
        h, C, n, m = _mlstm_chunks(fp(qkv.astype(jnp.float32)), li, lf, C0.astype(jnp.float32),
                                   n0.astype(jnp.float32), m0.astype(jnp.float32), MLSTM_CHUNK)
        h = _rms_norm(h[:, pad:].reshape(B, T, MLSTM_HEADS, MLSTM_HEAD_DIM), norm_g).reshape(B, T, W_MIX)
        h = jax.nn.sigmoid(o_pre.astype(jnp.float32)) * h
        return h.astype(qkv.dtype), C, n, m
    q, k, v = jnp.split(qkv.astype(jnp.float32), 3, axis=-1)
    hd = lambda t: t.reshape(B, T, MLSTM_HEADS, MLSTM_HEAD_DIM).transpose(0, 2, 1, 3)
    q, k, v = hd(q), hd(k) * MLSTM_HEAD_DIM ** -0.5, hd(v)
    li = (i_pre.astype(jnp.float32) + b_i).transpose(0, 2, 1)
    lf = jax.nn.log_sigmoid(f_pre.astype(jnp.float32) + b_f).transpose(0, 2, 1)
    st0 = (C0.astype(jnp.float32), n0.astype(jnp.float32), m0.astype(jnp.float32))
    h, (C, n, m) = _run_chunks(_mlstm_chunk, st0, (q, k, v, li, lf), is_prompt, MLSTM_CHUNK)
    h = _rms_norm(h.transpose(0, 2, 1, 3), norm_g).reshape(B, T, W_MIX)
    h = jax.nn.sigmoid(o_pre.astype(jnp.float32)) * h
    return h.astype(qkv.dtype), C, n, m


def _moe_experts_body(blk_exp_ref, n_used_ref, x_ref, wg_ref, wu_ref, wd_ref, o_ref, wg_bf, wu_bf, wd_bf):
    i = pl.program_id(0)
    new_expert = (i == 0) | (blk_exp_ref[i] != blk_exp_ref[jnp.maximum(i - 1, 0)])

    @pl.when(new_expert)
    def _():
        wg_bf[...] = wg_ref[0, 0].astype(jnp.bfloat16)
        wu_bf[...] = wu_ref[0, 0].astype(jnp.bfloat16)
        wd_bf[...] = wd_ref[0, 0].astype(jnp.bfloat16)

    @pl.when(i < n_used_ref[0])
    def _():
        xb = x_ref[...].astype(jnp.bfloat16)
        hg = jnp.dot(xb, wg_bf[...], preferred_element_type=jnp.float32)
        hu = jnp.dot(xb, wu_bf[...], preferred_element_type=jnp.float32)
        act = (hg * jax.nn.sigmoid(hg)) * hu
        o_ref[...] = jnp.dot(act.astype(jnp.bfloat16), wd_bf[...],
                             preferred_element_type=jnp.float32).astype(o_ref.dtype)

    @pl.when(i >= n_used_ref[0])
    def _():
        o_ref[...] = jnp.zeros_like(o_ref)


def _moe_experts(xb, blk_exp, n_used, w_g, w_u, w_d, l):
    R, D = xb.shape
    F = w_g.shape[-1]
    n_blocks = R // MOE_BLOCK
    grid_spec = pltpu.PrefetchScalarGridSpec(
        num_scalar_prefetch=2,
        grid=(n_blocks,),
        in_specs=[pl.BlockSpec((MOE_BLOCK, D), lambda i, be, nu: (i, 0)),
                  pl.BlockSpec((1, 1, D, F), lambda i, be, nu: (l, be[i], 0, 0)),
                  pl.BlockSpec((1, 1, D, F), lambda i, be, nu: (l, be[i], 0, 0)),
                  pl.BlockSpec((1, 1, F, D), lambda i, be, nu: (l, be[i], 0, 0))],
        out_specs=pl.BlockSpec((MOE_BLOCK, D), lambda i, be, nu: (i, 0)),
        scratch_shapes=[pltpu.VMEM((D, F), jnp.bfloat16), pltpu.VMEM((D, F), jnp.bfloat16),
                        pltpu.VMEM((F, D), jnp.bfloat16)])
    return pl.pallas_call(
        _moe_experts_body,
        grid_spec=grid_spec,
        out_shape=jax.ShapeDtypeStruct((R, D), jnp.bfloat16),
        compiler_params=pltpu.CompilerParams(dimension_semantics=("arbitrary",),
                                             vmem_limit_bytes=VMEM_LIMIT_BYTES),
        name="moe_experts",
    )(blk_exp, n_used, xb, w_g, w_u, w_d)


def _expert_dispatch(xt, eid, gate, w_g, w_u, w_d, l):
    N = xt.shape[0]
    A = N * TOP_K_INNER
    e_flat = eid.reshape(-1)
    tok = jnp.arange(A, dtype=jnp.int32) // TOP_K_INNER
    onehot = (e_flat[:, None] == jnp.arange(N_EXPERTS, dtype=e_flat.dtype)[None, :]).astype(jnp.int32)
    running = jnp.cumsum(onehot, axis=0)
    counts = running[-1]
    padded = (counts + MOE_BLOCK - 1) // MOE_BLOCK * MOE_BLOCK
    pad_end = jnp.cumsum(padded)
    pad_start = pad_end - padded
    slot = jnp.sum(onehot * (pad_start[None, :] + running - 1), axis=1)
    n_blocks = -(-A // MOE_BLOCK) + N_EXPERTS
    slot_tok = jnp.full((n_blocks * MOE_BLOCK,), N, jnp.int32).at[slot].set(tok)
    blk_exp = jnp.minimum(jnp.searchsorted(pad_end, jnp.arange(n_blocks) * MOE_BLOCK, side='right'), N_EXPERTS - 1)
    x_pad = jnp.concatenate([xt, jnp.zeros((1, xt.shape[1]), xt.dtype)], axis=0).astype(jnp.bfloat16)
    n_used = (pad_end[-1:] // MOE_BLOCK).astype(jnp.int32)
    yb = _moe_experts(x_pad[slot_tok], blk_exp.astype(jnp.int32), n_used, w_g, w_u, w_d, l)
    slot = slot.reshape(N, TOP_K_INNER)
    return sum(yb[slot[:, j]].astype(jnp.float32) * _round_bf16(gate[:, j:j + 1]) for j in range(TOP_K_INNER))


def _moe(x, w_group, w_expert, w_g, w_u, w_d, l):
    shp = x.shape
    xt = x.reshape(-1, D_MODEL)
    N = xt.shape[0]
    xf = xt.astype(jnp.float32)
    pg = jax.nn.softmax(xf @ w_group.astype(jnp.float32), axis=-1)
    p_top, g_sel = lax.top_k(pg, 1)
    le = (xf @ w_expert.astype(jnp.float32)).reshape(N, N_GROUPS, EXPERTS_PER_GROUP)
    le_g = le[jnp.arange(N), g_sel[:, 0]]
    v2, j2 = lax.top_k(le_g, TOP_K_INNER)
    gate = jax.nn.softmax(v2, axis=-1) * p_top
    eid = g_sel * EXPERTS_PER_GROUP + j2
    return _expert_dispatch(xt, eid, gate, w_g, w_u, w_d, l).reshape(shp).astype(x.dtype)


def _layer(x, W, l, st, dsa_fn, is_prompt, alpha):
    B, T, _ = x.shape
    (a_u, b_qkv, b_a, b_b, b_z, c_q, c_k, c_v, c_qi, c_ki, c_wi,
     d_qkv, d_i, d_f, d_o, gates) = _proj_in(x, W['w_in'][l])
    rwkv_S0, rwkv_shift0, gdn_S0, gdn_conv0, mC0, mn0, mm0 = st
    y_a, rwkv_S, rwkv_shift = _rwkv7(
        a_u, rwkv_S0, rwkv_shift0, W['rwkv_mu'][l], W['rwkv_w_rkv'][l], W['rwkv_w0'][l], W['rwkv_w1'][l],
        W['rwkv_w2'][l], W['rwkv_a0'][l], W['rwkv_a1'][l], W['rwkv_a2'][l], W['rwkv_g1'][l], W['rwkv_g2'][l],
        W['rwkv_k_k'][l], W['rwkv_k_a'][l], W['rwkv_r_k'][l], W['rwkv_lnx_g'][l], W['rwkv_lnx_b'][l])
    y_b, gdn_S, gdn_conv = _gdn(b_qkv, b_a, b_b, b_z, gdn_S0, gdn_conv0, W['gdn_conv_w'][l], W['gdn_A_log'][l],
                                W['gdn_dt_bias'][l], W['gdn_norm_g'][l], is_prompt)
    q, k, v, qi, ki, wi = _dsa_inputs(c_q, c_k, c_v, c_qi, c_ki, c_wi, W['idx_ln_g'][l], W['idx_ln_b'][l])
    y_c = dsa_fn(q, k, v, qi, ki, wi).reshape(B, T, W_MIX)
    y_d, mC, mn, mm = _mlstm(d_qkv, d_i, d_f, d_o, mC0, mn0, mm0, W['mlstm_b_i'][l], W['mlstm_b_f'][l],
                             W['mlstm_norm_g'][l], is_prompt)
    merged = 0.0
    for i, y_i in enumerate((y_a, y_b, y_c, y_d)):
        gate_i = jax.nn.sigmoid(gates[..., i * D_MODEL:(i + 1) * D_MODEL].astype(jnp.float32))
        merged = merged + gate_i * (y_i @ W['w_branch'][l, i])
    mix = merged.astype(x.dtype) @ W['w_out'][l]
    x = _layer_norm(alpha * x + mix, W['ln1_g'][l], W['ln1_b'][l])
    ffn = _moe(x, W['moe_w_group'][l], W['moe_w_expert'][l], W['moe_w_gate'], W['moe_w_up'], W['moe_w_down'], l)
    x = _layer_norm(alpha * x + ffn, W['ln2_g'][l], W['ln2_b'][l])
    return x, (k, v, ki, rwkv_S, rwkv_shift, gdn_S, gdn_conv, mC, mn, mm)


def kernel(x_prompt, x_sample, cache_k, cache_v, cache_idx_k, page_table, state_rwkv_S, state_rwkv_shift,
           state_gdn_S, state_gdn_conv, state_mlstm_C, state_mlstm_n, state_mlstm_m, meta_tokens, ln_in_g,
           ln_in_b, w_in, rwkv_mu, rwkv_w_rkv, rwkv_w0, rwkv_w1, rwkv_w2, rwkv_a0, rwkv_a1, rwkv_a2, rwkv_g1,
           rwkv_g2, rwkv_k_k, rwkv_k_a, rwkv_r_k, rwkv_lnx_g, rwkv_lnx_b, gdn_conv_w, gdn_A_log, gdn_dt_bias,
           gdn_norm_g, idx_ln_g, idx_ln_b, mlstm_b_i, mlstm_b_f, mlstm_norm_g, w_branch, w_out, ln1_g, ln1_b,
           ln2_g, ln2_b, moe_w_group, moe_w_expert, moe_w_gate, moe_w_up, moe_w_down):
    W = dict(w_in=w_in, rwkv_mu=rwkv_mu, rwkv_w_rkv=rwkv_w_rkv, rwkv_w0=rwkv_w0, rwkv_w1=rwkv_w1,
             rwkv_w2=rwkv_w2, rwkv_a0=rwkv_a0, rwkv_a1=rwkv_a1, rwkv_a2=rwkv_a2, rwkv_g1=rwkv_g1,
             rwkv_g2=rwkv_g2, rwkv_k_k=rwkv_k_k, rwkv_k_a=rwkv_k_a, rwkv_r_k=rwkv_r_k, rwkv_lnx_g=rwkv_lnx_g,
             rwkv_lnx_b=rwkv_lnx_b, gdn_conv_w=gdn_conv_w, gdn_A_log=gdn_A_log, gdn_dt_bias=gdn_dt_bias,
             gdn_norm_g=gdn_norm_g, idx_ln_g=idx_ln_g, idx_ln_b=idx_ln_b, mlstm_b_i=mlstm_b_i,
             mlstm_b_f=mlstm_b_f, mlstm_norm_g=mlstm_norm_g, w_branch=w_branch, w_out=w_out, ln1_g=ln1_g,
             ln1_b=ln1_b, ln2_g=ln2_g, ln2_b=ln2_b, moe_w_group=moe_w_group, moe_w_expert=moe_w_expert,
             moe_w_gate=moe_w_gate, moe_w_up=moe_w_up, moe_w_down=moe_w_down)
    f32 = jnp.float32
    depth = w_in.shape[0]
    alpha = (2 * depth) ** 0.25
    B = x_prompt.shape[0]
    meta = jnp.broadcast_to(meta_tokens.astype(x_prompt.dtype)[None], (B, N_META, D_MODEL))
    hp = _layer_norm(jnp.concatenate([meta, x_prompt], axis=1), ln_in_g, ln_in_b)
    hs = _layer_norm(x_sample, ln_in_g, ln_in_b)
    zero_state = (jnp.zeros((B, RWKV_HEADS, RWKV_HEAD_DIM, RWKV_HEAD_DIM), f32),
                  jnp.zeros((B, W_MIX), f32),
                  jnp.zeros((B, GDN_HEADS, GDN_HEAD_DIM, GDN_HEAD_DIM), f32),
                  jnp.zeros((B, CONV_W - 1, 3 * W_MIX), f32),
                  jnp.zeros((B, MLSTM_HEADS, MLSTM_HEAD_DIM, MLSTM_HEAD_DIM), f32),
                  jnp.zeros((B, MLSTM_HEADS, MLSTM_HEAD_DIM), f32),
                  jnp.zeros((B, MLSTM_HEADS), f32))
    new_p, new_s = [], []
    for l in range(depth):
        hp, sp = _layer(hp, W, l, zero_state, _dsa_prompt_pallas, True, alpha)
        dsa_s = functools.partial(_dsa_sample_pallas, cache_k=cache_k, cache_v=cache_v, cache_idx_k=cache_idx_k,
                                  page_table=page_table, l=l)
        st_s = (state_rwkv_S[l], state_rwkv_shift[l], state_gdn_S[l], state_gdn_conv[l],
                state_mlstm_C[l], state_mlstm_n[l], state_mlstm_m[l])
        hs, ss = _layer(hs, W, l, st_s, dsa_s, False, alpha)
        new_p.append(sp)
        new_s.append(ss)

    def stack(rows, j):
        return jnp.stack([r[j] for r in rows])

    k_p, v_p, ik_p, rS_p, rsh_p, gS_p, gc_p, mC_p, mn_p, mm_p = (stack(new_p, j) for j in range(10))
    k_s, v_s, ik_s, rS_s, rsh_s, gS_s, gc_s, mC_s, mn_s, mm_s = (stack(new_s, j) for j in range(10))
    y_prompt = hp[:, N_META:]
    y_sample = hs
    return (y_prompt, y_sample, k_p, v_p, ik_p, k_s, v_s, ik_s, rS_p, rS_s, rsh_p, rsh_s,
            gS_p, gS_s, gc_p, gc_s, mC_p, mC_s, mn_p, mn_s, mm_p, mm_s)
```

```python
import math, functools
import jax, jax.numpy as jnp
from jax import lax
import numpy as np
from jax.experimental import pallas as pl
from jax.experimental.pallas import tpu as pltpu

D_MODEL = 2048
N_META = 16
N_BRANCH = 4
W_MIX = D_MODEL // 4
RWKV_HEAD_DIM = 64
RWKV_HEADS = W_MIX // RWKV_HEAD_DIM
RWKV_GN_EPS = 64e-5
GDN_HEADS = 4
GDN_HEAD_DIM = W_MIX // GDN_HEADS
CONV_W = 4
GDN_CHUNK = 64
ATT_HEADS = 4
ATT_HEAD_DIM = W_MIX // ATT_HEADS
IDX_HEADS = 8
IDX_DIM = 64
TOPK_MAX = 256
Q_BLOCK = 64
MLSTM_HEADS = 4
MLSTM_HEAD_DIM = W_MIX // MLSTM_HEADS
MLSTM_CHUNK = 64
N_GROUPS = 4
EXPERTS_PER_GROUP = 8
N_EXPERTS = N_GROUPS * EXPERTS_PER_GROUP
TOP_K_INNER = 2
D_EXPERT = D_MODEL // 4
MOE_BLOCK = 128
PAGE_SIZE = 128

VMEM_LIMIT_BYTES = 56 * 1024 * 1024


def _mm_body(x_ref, w_ref, o_ref):
    o_ref[...] = jnp.dot(x_ref[...].astype(jnp.bfloat16), w_ref[...].astype(jnp.bfloat16),
                         preferred_element_type=jnp.float32)


def _mm(x, w, tm=1024, tn=1024):
    M, K = x.shape
    N = w.shape[1]
    tm = min(tm, M)
    tn = N if N <= 3 * W_MIX else tn
    return pl.pallas_call(
        _mm_body,
        grid=(pl.cdiv(N, tn), pl.cdiv(M, tm)),
        in_specs=[pl.BlockSpec((tm, K), lambda j, i: (i, 0)),
                  pl.BlockSpec((K, tn), lambda j, i: (0, j))],
        out_specs=pl.BlockSpec((tm, tn), lambda j, i: (i, j)),
        out_shape=jax.ShapeDtypeStruct((M, N), jnp.float32),
        compiler_params=pltpu.CompilerParams(dimension_semantics=("arbitrary", "arbitrary"),
                                             vmem_limit_bytes=VMEM_LIMIT_BYTES),
        name="proj_mm",
    )(x, w)


def _mm_group_body(x_ref, w_ref, *o_refs, offs):
    if w_ref.shape[1] <= LANES:
        acc = jnp.dot(x_ref[...], w_ref[...], preferred_element_type=jnp.float32)
        for o_ref, (a, b) in zip(o_refs, offs):
            o_ref[...] = acc[:, a:b]
        return
    for o_ref, (a, b) in zip(o_refs, offs):
        o_ref[...] = jnp.dot(x_ref[...], w_ref[:, a:b], preferred_element_type=jnp.float32)


def _mm_group(x, w, sizes, tm=512):
    M, K = x.shape
    tm = min(tm, M)
    ends = np.cumsum(sizes).tolist()
    offs = tuple(zip([0] + ends[:-1], ends))
    return pl.pallas_call(
        functools.partial(_mm_group_body, offs=offs),
        grid=(pl.cdiv(M, tm),),
        in_specs=[pl.BlockSpec((tm, K), lambda i: (i, 0)),
                  pl.BlockSpec((K, ends[-1]), lambda i: (0, 0))],
        out_specs=[pl.BlockSpec((tm, s), lambda i: (i, 0)) for s in sizes],
        out_shape=[jax.ShapeDtypeStruct((M, s), jnp.float32) for s in sizes],
        compiler_params=pltpu.CompilerParams(dimension_semantics=("arbitrary",),
                                             vmem_limit_bytes=VMEM_LIMIT_BYTES),
        name="proj_mm_group",
    )(x, w)


LANES = 128
INT_MIN = -2 ** 31
DSA_TQ = 128
DSA_SEGMENTS = 4


def _round_bf16(x):
    return x.astype(jnp.bfloat16).astype(jnp.float32)


def _topk_mask(isc, valid, key_ref, k_top):
    rows, t_pad = key_ref.shape
    bits = lax.bitcast_convert_type(isc, jnp.int32)
    key = bits ^ ((bits >> 31) & 0x7FFFFFFF)
    key_ref[...] = jnp.where(valid, key, INT_MIN)

    def count_ge(c):
        return jnp.sum(jnp.where(key_ref[...] >= c, 1.0, 0.0), axis=1, keepdims=True)

    kf = float(k_top)
    thr = jnp.where(count_ge(jnp.zeros((rows, 1), jnp.int32)) >= kf, 0, INT_MIN).astype(jnp.int32)

    def bit_step(j, thr):
        cand = thr | jnp.left_shift(jnp.int32(1), 30 - j)
        return jnp.where(count_ge(cand) >= kf, cand, thr)

    thr = lax.fori_loop(0, 31, bit_step, thr)

    key = key_ref[...]
    gt = key > thr
    eq = key == thr
    need = kf - jnp.sum(jnp.where(gt, 1.0, 0.0), axis=1, keepdims=True)
    upper = (lax.broadcasted_iota(jnp.int32, (LANES, LANES), 0)
             < lax.broadcasted_iota(jnp.int32, (LANES, LANES), 1)).astype(jnp.bfloat16)
    eqf = jnp.where(eq, 1.0, 0.0)
    off = jnp.zeros((rows, 1), jnp.float32)
    pre = []
    for c in range(t_pad // LANES):
        e_c = eqf[:, c * LANES:(c + 1) * LANES]
        pre.append(jnp.dot(e_c.astype(jnp.bfloat16), upper, preferred_element_type=jnp.float32) + off)
        off = off + jnp.sum(e_c, axis=1, keepdims=True)
    prefix = jnp.concatenate(pre, axis=1)
    return valid & (gt | (eq & (prefix < need)))


def _dsa_prompt_body(q_ref, qi_ref, wi_ref, kT_ref, v_ref, kiT_ref, o_ref, key_ref, *, k_top, first_block):
    tq, t_pad = key_ref.shape
    i = pl.program_id(1) + first_block
    row = lax.broadcasted_iota(jnp.int32, (tq, t_pad), 0) + i * tq
    col = lax.broadcasted_iota(jnp.int32, (tq, t_pad), 1)
    valid = col <= row

    wi = _round_bf16(wi_ref[0])
    isc = jnp.zeros((tq, t_pad), jnp.float32)
    for h in range(IDX_HEADS):
        sc = jnp.dot(qi_ref[0, h].astype(jnp.bfloat16), kiT_ref[0],
                     preferred_element_type=jnp.float32) * IDX_DIM ** -0.5
        isc = isc + _round_bf16(jnp.maximum(sc, 0.0)) * wi[:, h:h + 1]

    sel = _topk_mask(isc, valid, key_ref, k_top)

    dist = (row - col).astype(jnp.float32)
    for h in range(ATT_HEADS):
        hs = slice(h * ATT_HEAD_DIM, (h + 1) * ATT_HEAD_DIM)
        slope = 2.0 ** (-8.0 * (h + 1) / ATT_HEADS)
        lg = jnp.dot(q_ref[0, :, hs].astype(jnp.bfloat16), kT_ref[0, hs, :],
                     preferred_element_type=jnp.float32) * ATT_HEAD_DIM ** -0.5 - slope * dist
        lg = jnp.where(sel, lg, -jnp.inf)
        e = jnp.exp(lg - jnp.max(lg, axis=1, keepdims=True))
        p = e * (1.0 / jnp.sum(e, axis=1, keepdims=True))
        o_ref[0, :, hs] = jnp.dot(p.astype(jnp.bfloat16), v_ref[0, :, hs], preferred_element_type=jnp.float32)


def _dsa_prompt_pallas(q, k, v, qi, ki, wi):
    B, T = q.shape[:2]
    k_top = min(TOPK_MAX, T // 4)
    tq = DSA_TQ
    t_pad = pl.cdiv(T, LANES) * LANES
    padt = lambda t: jnp.pad(t, [(0, 0), (0, t_pad - T)] + [(0, 0)] * (t.ndim - 2))
    q2 = padt(q.reshape(B, T, W_MIX))
    kT = jnp.swapaxes(padt(k.reshape(B, T, W_MIX)), 1, 2).astype(jnp.bfloat16)
    v2 = padt(v.reshape(B, T, W_MIX)).astype(jnp.bfloat16)
    qi2 = jnp.swapaxes(padt(qi), 1, 2)
    kiT = jnp.swapaxes(padt(ki), 1, 2).astype(jnp.bfloat16)
    wi2 = padt(wi)
    n_q = t_pad // tq
    bounds = sorted({-(-n_q * s // DSA_SEGMENTS) for s in range(DSA_SEGMENTS + 1)})
    outs = []
    for i0, i1 in zip(bounds[:-1], bounds[1:]):
        tk = i1 * tq
        outs.append(pl.pallas_call(
            functools.partial(_dsa_prompt_body, k_top=k_top, first_block=i0),
            grid=(B, i1 - i0),
            in_specs=[pl.BlockSpec((1, tq, W_MIX), lambda b, i, i0=i0: (b, i + i0, 0)),
                      pl.BlockSpec((1, IDX_HEADS, tq, IDX_DIM), lambda b, i, i0=i0: (b, 0, i + i0, 0)),
                      pl.BlockSpec((1, tq, IDX_HEADS), lambda b, i, i0=i0: (b, i + i0, 0)),
                      pl.BlockSpec((1, W_MIX, tk), lambda b, i: (b, 0, 0)),
                      pl.BlockSpec((1, tk, W_MIX), lambda b, i: (b, 0, 0)),
                      pl.BlockSpec((1, IDX_DIM, tk), lambda b, i: (b, 0, 0))],
            out_specs=pl.BlockSpec((1, tq, W_MIX), lambda b, i: (b, i, 0)),
            out_shape=jax.ShapeDtypeStruct((B, (i1 - i0) * tq, W_MIX), jnp.float32),
            scratch_shapes=[pltpu.VMEM((tq, tk), jnp.int32)],
            compiler_params=pltpu.CompilerParams(dimension_semantics=("arbitrary", "arbitrary"),
                                                 vmem_limit_bytes=VMEM_LIMIT_BYTES),
            name="dsa_prompt",
        )(q2, qi2, wi2, kT, v2, kiT))
    out = jnp.concatenate(outs, axis=1)
    return out[:, :T].reshape(B, T, ATT_HEADS, ATT_HEAD_DIM)


SUBLANES = 8


def _dsa_sample_body(pt_ref, q_ref, kn_ref, vn_ref, qi_ref, kin_ref, wi_ref, *rest, n_pages, k_top):
    ki_pages, k_pages, v_pages = rest[:n_pages], rest[n_pages:2 * n_pages], rest[2 * n_pages:3 * n_pages]
    o_ref, key_ref = rest[3 * n_pages], rest[3 * n_pages + 1]
    R = SUBLANES
    HP = ATT_HEADS * PAGE_SIZE
    t_pad = key_ref.shape[1]
    past_len = n_pages * PAGE_SIZE

    qi = qi_ref[0].astype(jnp.bfloat16)
    wi = _round_bf16(wi_ref[0])

    def idx_tile(ki_page):
        sc = _dot_nt(qi, ki_page) * IDX_DIM ** -0.5
        return jnp.sum(_round_bf16(jnp.maximum(sc, 0.0)) * wi, axis=0, keepdims=True)

    first_row = lax.broadcasted_iota(jnp.int32, (PAGE_SIZE, 1), 0) == 0
    kin_page = jnp.where(first_row, kin_ref[0], 0.0)
    isc = jnp.concatenate([idx_tile(ki_pages[j][0, 0]) for j in range(n_pages)] + [idx_tile(kin_page)], axis=1)
    col = lax.broadcasted_iota(jnp.int32, (R, t_pad), 1)
    sel = _topk_mask(jnp.broadcast_to(isc, (R, t_pad)), col <= past_len, key_ref, k_top)

    expand = (lax.broadcasted_iota(jnp.int32, (PAGE_SIZE, HP), 1) // ATT_HEADS
              == lax.broadcasted_iota(jnp.int32, (PAGE_SIZE, HP), 0)).astype(jnp.bfloat16)
    sel_rows = jnp.concatenate([jnp.where(sel[:, j * PAGE_SIZE:(j + 1) * PAGE_SIZE], 1.0, 0.0)
                                for j in range(n_pages + 1)], axis=0).astype(jnp.bfloat16)
    sel_hp = jnp.dot(sel_rows, expand, preferred_element_type=jnp.float32)

    lane = lax.broadcasted_iota(jnp.int32, (R, HP), 1)
    hrow = lax.broadcasted_iota(jnp.int32, (R, HP), 0) % ATT_HEADS
    own = lane % ATT_HEADS == hrow
    slope = sum(jnp.where(hrow == h, 2.0 ** (-8.0 * (h + 1) / ATT_HEADS), 0.0) for h in range(ATT_HEADS))
    q8 = q_ref[0].astype(jnp.bfloat16)
    zeros = jnp.zeros((HP - R, ATT_HEAD_DIM), jnp.float32)
    k_all = [k_pages[j][0, 0] for j in range(n_pages)] + [jnp.concatenate([kn_ref[0], zeros], axis=0)]
    v_all = [v_pages[j][0, 0] for j in range(n_pages)] + [jnp.concatenate([vn_ref[0], zeros], axis=0)]
    tiles = []
    for j, kp in enumerate(k_all):
        dist = (past_len - j * PAGE_SIZE - lane // ATT_HEADS).astype(jnp.float32)
        lg = _dot_nt(q8, kp) * ATT_HEAD_DIM ** -0.5 - slope * dist
        tiles.append(jnp.where(own & (sel_hp[j * R:(j + 1) * R] > 0.5), lg, -jnp.inf))
    lg = jnp.concatenate(tiles, axis=1)
    e = jnp.exp(lg - jnp.max(lg, axis=1, keepdims=True))
    p = (e * (1.0 / jnp.sum(e, axis=1, keepdims=True))).astype(jnp.bfloat16)
    acc = jnp.zeros((R, ATT_HEAD_DIM), jnp.float32)
    for j, vp in enumerate(v_all):
        acc = acc + _dot(p[:, j * HP:(j + 1) * HP], vp)
    o_ref[0] = acc[:ATT_HEADS]


def _dsa_sample_pallas(q, k, v, qi, ki, wi, cache_k, cache_v, cache_idx_k, page_table, l):
    B, Q = q.shape[:2]
    assert Q == 1 and PAGE_SIZE == LANES
    n_pages = page_table.shape[1]
    n_pool = cache_k.shape[1]
    k_top = min(TOPK_MAX, (n_pages * PAGE_SIZE + Q) // 4)
    t_pad = (n_pages + 1) * PAGE_SIZE
    HP = ATT_HEADS * PAGE_SIZE
    head_spec = pl.BlockSpec((1, SUBLANES, ATT_HEAD_DIM), lambda b, pt: (b, 0, 0))
    page_spec = lambda rows, w, j: pl.BlockSpec((1, 1, rows, w), lambda b, pt: (l, pt[b, j], 0, 0))
    grid_spec = pltpu.PrefetchScalarGridSpec(
        num_scalar_prefetch=1,
        grid=(B,),
        in_specs=[head_spec, head_spec, head_spec,
                  pl.BlockSpec((1, IDX_HEADS, IDX_DIM), lambda b, pt: (b, 0, 0)),
                  pl.BlockSpec((1, 1, IDX_DIM), lambda b, pt: (b, 0, 0)),
                  pl.BlockSpec((1, IDX_HEADS, 1), lambda b, pt: (b, 0, 0))]
                 + [page_spec(PAGE_SIZE, IDX_DIM, j) for j in range(n_pages)]
                 + [page_spec(HP, ATT_HEAD_DIM, j) for j in range(n_pages)] * 2,
        out_specs=pl.BlockSpec((1, ATT_HEADS, ATT_HEAD_DIM), lambda b, pt: (b, 0, 0)),
        scratch_shapes=[pltpu.VMEM((SUBLANES, t_pad), jnp.int32)])
    heads = lambda t: jnp.pad(t.reshape(B, ATT_HEADS, ATT_HEAD_DIM), [(0, 0), (0, SUBLANES - ATT_HEADS), (0, 0)])
    pool = lambda c: c.reshape(c.shape[0], n_pool, HP, ATT_HEAD_DIM)
    out = pl.pallas_call(
        functools.partial(_dsa_sample_body, n_pages=n_pages, k_top=k_top),
        grid_spec=grid_spec,
        out_shape=jax.ShapeDtypeStruct((B, ATT_HEADS, ATT_HEAD_DIM), jnp.float32),
        compiler_params=pltpu.CompilerParams(dimension_semantics=("arbitrary",),
                                             vmem_limit_bytes=VMEM_LIMIT_BYTES),
        name="dsa_sample",
    )(page_table, heads(q), heads(k), heads(v),
      qi.reshape(B, IDX_HEADS, IDX_DIM), ki, wi.reshape(B, IDX_HEADS, 1),
      *([cache_idx_k] * n_pages), *([pool(cache_k)] * n_pages), *([pool(cache_v)] * n_pages))
    return out.reshape(B, 1, ATT_HEADS, ATT_HEAD_DIM)


RWKV_GROUP = 8
RWKV_BB = 2


def _rwkv_scan_body(r_ref, d_ref, k_ref, v_ref, kk_ref, a_ref, s0_ref, y_ref, sT_ref, S_scr, Sb_scr):
    BB, tc = r_ref.shape[:2]
    P, N = S_scr.shape[1:3]
    G = RWKV_GROUP
    chains = [(b, p) for b in range(BB) for p in range(P)]
    c = pl.program_id(1)

    @pl.when(c == 0)
    def _():
        S_scr[...] = s0_ref[...]
        Sb_scr[...] = s0_ref[...].astype(jnp.bfloat16)

    lo_half = lax.broadcasted_iota(jnp.int32, (1, LANES), 1) < N
    sub = lax.broadcasted_iota(jnp.int32, (G, N), 0)
    sub2 = lax.broadcasted_iota(jnp.int32, (G, LANES), 0)
    r_mask = ((sub2 == 0) & lo_half) | ((sub2 == 1) & jnp.logical_not(lo_half))
    ii = lax.broadcasted_iota(jnp.int32, (N, LANES), 0)
    jj = lax.broadcasted_iota(jnp.int32, (N, LANES), 1)
    diagA, diagB = jj == ii, jj == ii + N

    def group(g, carry):
        t0 = pl.multiple_of(g * G, G)
        ld = lambda ref, ch: ref[ch[0], pl.ds(t0, G), ch[1] * LANES:(ch[1] + 1) * LANES]
        kk8 = [ld(kk_ref, ch) for ch in chains]
        beta8 = [kk8[i] * ld(a_ref, ch) for i, ch in enumerate(chains)]
        d8 = [ld(d_ref, ch) for ch in chains]
        k8 = [ld(k_ref, ch) for ch in chains]
        r8 = [ld(r_ref, ch) for ch in chains]
        v8 = [ld(v_ref, ch) for ch in chains]
        kkA = [jnp.where(lo_half, x, 0.0).astype(jnp.bfloat16) for x in kk8]
        kkB = [jnp.where(lo_half, 0.0, x).astype(jnp.bfloat16) for x in kk8]
        yA = [jnp.zeros((G, N), jnp.float32) for _ in chains]
        yB = [jnp.zeros((G, N), jnp.float32) for _ in chains]

        def y_step(t, i):
            b, p = chains[i]
            r2 = jnp.where(r_mask, r8[i][t:t + 1], 0.0).astype(jnp.bfloat16)
            y2 = _dot_nt(r2, Sb_scr[b, p])
            yA[i] = jnp.where(sub == t, jnp.broadcast_to(y2[0:1], (G, N)), yA[i])
            yB[i] = jnp.where(sub == t, jnp.broadcast_to(y2[1:2], (G, N)), yB[i])

        for t in range(G):
            row = lambda x: x[t:t + 1]
            sa_b = []
            for i, (b, p) in enumerate(chains):
                w1 = jnp.concatenate([jnp.broadcast_to(row(kkA[i]), (N, LANES)),
                                      jnp.broadcast_to(row(kkB[i]), (N, LANES))], axis=0)
                sa_b.append(_dot_nt(Sb_scr[b, p], w1))
            if t > 0:
                for i in range(len(chains)):
                    y_step(t - 1, i)
            for i, (b, p) in enumerate(chains):
                vA = jnp.sum(jnp.where(diagA, row(v8[i]), 0.0), axis=1, keepdims=True)
                vB = jnp.sum(jnp.where(diagB, row(v8[i]), 0.0), axis=1, keepdims=True)
                v_b = jnp.where(lo_half, vA, vB)
                S = S_scr[b, p] * row(d8[i]) - sa_b[i] * row(beta8[i]) + v_b * row(k8[i])
                S_scr[b, p] = S
                Sb_scr[b, p] = S.astype(jnp.bfloat16)
        for i, (b, p) in enumerate(chains):
            y_step(G - 1, i)
            y_ref[b, pl.ds(t0, G), p * LANES:(p + 1) * LANES] = jnp.concatenate([yA[i], yB[i]], axis=1)
        return carry

    lax.fori_loop(0, tc // G, group, 0)

    @pl.when(c == pl.num_programs(1) - 1)
    def _():
        sT_ref[...] = S_scr[...]


def _rwkv_scan(r, d, k, v, kk, a, S0):
    B, T, W = r.shape
    H, N = S0.shape[1:3]
    P = H // 2
    G = RWKV_GROUP
    bb = RWKV_BB if B % RWKV_BB == 0 else 1
    t_pad = pl.cdiv(T, G) * G
    tc = t_pad
    for cand in range(min(t_pad, 512), G - 1, -1):
        if t_pad % cand == 0 and cand % G == 0:
            tc = cand
            break
    pad = lambda x, c=0.0: jnp.pad(x, [(0, 0), (0, t_pad - T), (0, 0)], constant_values=c) if t_pad != T else x
    S2 = S0.reshape(B, P, 2, N, N).transpose(0, 1, 3, 2, 4).reshape(B, P, N, 2 * N)
    seq_spec = pl.BlockSpec((bb, tc, W), lambda b, c: (b, c, 0))
    st_spec = pl.BlockSpec((bb, P, N, 2 * N), lambda b, c: (b, 0, 0, 0))
    y, S = pl.pallas_call(
        _rwkv_scan_body,
        grid=(B // bb, t_pad // tc),
        in_specs=[seq_spec] * 6 + [st_spec],
        out_specs=[seq_spec, st_spec],
        out_shape=[jax.ShapeDtypeStruct((B, t_pad, W), jnp.float32),
                   jax.ShapeDtypeStruct((B, P, N, 2 * N), jnp.float32)],
        scratch_shapes=[pltpu.VMEM((bb, P, N, 2 * N), jnp.float32), pltpu.VMEM((bb, P, N, 2 * N), jnp.bfloat16)],
        compiler_params=pltpu.CompilerParams(dimension_semantics=("arbitrary", "arbitrary"),
                                             vmem_limit_bytes=VMEM_LIMIT_BYTES),
        name="rwkv_scan",
    )(pad(r), pad(d, 1.0), pad(k), pad(v), pad(kk), pad(a), S2)
    S = S.reshape(B, P, N, 2, N).transpose(0, 1, 3, 2, 4).reshape(B, H, N, N)
    return y[:, :T], S


def _bf(x):
    return x.astype(jnp.bfloat16)


def _dot(a, b):
    return jnp.dot(_bf(a), _bf(b), preferred_element_type=jnp.float32)


def _dot_nt(a, b):
    return lax.dot_general(_bf(a), _bf(b), (((1,), (1,)), ((), ())), preferred_element_type=jnp.float32)


def _dot_f32(a, b):
    return jnp.dot(a, b, precision=lax.Precision.HIGHEST, preferred_element_type=jnp.float32)


GDN_BB = 2


def _gdn_chunk_body(q_ref, k_ref, v_ref, g_ref, beta_ref, grow_ref, s0_ref, o_ref, sT_ref, S_scr):
    BB, L = q_ref.shape[:2]
    H, D = S_scr.shape[1:3]
    chains = [(b, h) for b in range(BB) for h in range(H)]
    c = pl.program_id(1)

    @pl.when(c == 0)
    def _():
        S_scr[...] = s0_ref[...]

    ii = lax.broadcasted_iota(jnp.int32, (L, L), 0)
    jj = lax.broadcasted_iota(jnp.int32, (L, L), 1)
    tri, stri = jj <= ii, jj < ii
    eye = jnp.where(ii == jj, 1.0, 0.0)
    every = lambda f: [f(i) for i in range(len(chains))]
    hs = lambda h: slice(h * D, (h + 1) * D)
    q = [q_ref[b, :, hs(h)] for b, h in chains]
    k = [k_ref[b, :, hs(h)] for b, h in chains]
    g_col = [g_ref[b, :, h:h + 1] for b, h in chains]
    b_col = [beta_ref[b, :, h:h + 1] for b, h in chains]
    g_row = [grow_ref[b, 0, h:h + 1, :] for b, h in chains]
    gc_col = every(lambda i: jnp.sum(jnp.where(tri, g_row[i], 0.0), axis=1, keepdims=True))
    gc_row = every(lambda i: jnp.sum(jnp.where(ii <= jj, g_col[i], 0.0), axis=0, keepdims=True))
    dec = every(lambda i: jnp.exp(jnp.where(tri, gc_col[i] - gc_row[i], -jnp.inf)))
    kb = every(lambda i: k[i] * b_col[i])
    egc = every(lambda i: jnp.exp(gc_col[i]))
    A = every(lambda i: jnp.where(stri, _dot_nt(kb[i], k[i]) * dec[i], 0.0))
    attn = every(lambda i: jnp.where(tri, _dot_nt(q[i], k[i]) * dec[i], 0.0))
    rhs = [jnp.concatenate([v_ref[b, :, hs(h)] * b_col[i], kb[i] * egc[i]], axis=1) for i, (b, h) in enumerate(chains)]
    inv = every(lambda i: eye - A[i])
    P = every(lambda i: _dot_f32(A[i], A[i]))
    n_lvl = int(math.log2(L)) - 1
    for lvl in range(n_lvl):
        inv = every(lambda i: inv[i] + _dot_f32(inv[i], P[i]))
        if lvl + 1 < n_lvl:
            P = every(lambda i: _dot_f32(P[i], P[i]))
    sol = every(lambda i: _dot_f32(inv[i], rhs[i]))
    S = [S_scr[b, h] for b, h in chains]
    v_new = every(lambda i: sol[i][:, :D] - _dot(sol[i][:, D:], S[i]))
    o_state = every(lambda i: _dot(q[i] * egc[i], S[i]))
    g_last = every(lambda i: gc_col[i][L - 1:L, :])
    k_sc = every(lambda i: (k[i] * jnp.exp(g_last[i] - gc_col[i])).T)
    for i, (b, h) in enumerate(chains):
        o_ref[b, :, hs(h)] = o_state[i] + _dot(attn[i], v_new[i])
    for i, (b, h) in enumerate(chains):
        S_scr[b, h] = S[i] * jnp.exp(g_last[i]) + _dot(k_sc[i], v_new[i])

    @pl.when(c == pl.num_programs(1) - 1)
    def _():
        sT_ref[...] = S_scr[...]


def _gdn_chunks(q, k, v, g, beta, S0, L):
    B, T, W = q.shape
    H, D = S0.shape[1:3]
    n = T // L
    bb = GDN_BB if B % GDN_BB == 0 else 1
    g_row = jnp.swapaxes(g.reshape(B, n, L, H), 2, 3)
    seq = pl.BlockSpec((bb, L, W), lambda b, c: (b, c, 0))
    gate = pl.BlockSpec((bb, L, H), lambda b, c: (b, c, 0))
    st = pl.BlockSpec((bb, H, D, D), lambda b, c: (b, 0, 0, 0))
    return pl.pallas_call(
        _gdn_chunk_body,
        grid=(B // bb, n),
        in_specs=[seq, seq, seq, gate, gate, pl.BlockSpec((bb, 1, H, L), lambda b, c: (b, c, 0, 0)), st],
        out_specs=[seq, st],
        out_shape=[jax.ShapeDtypeStruct((B, T, W), jnp.float32),
                   jax.ShapeDtypeStruct((B, H, D, D), jnp.float32)],
        scratch_shapes=[pltpu.VMEM((bb, H, D, D), jnp.float32)],
        compiler_params=pltpu.CompilerParams(dimension_semantics=("arbitrary", "arbitrary"),
                                             vmem_limit_bytes=VMEM_LIMIT_BYTES),
        name="gdn_chunks",
    )(q, k, v, g, beta, g_row, S0)


def _layer_norm(x, g, b, eps=1e-5):
    xf = x.astype(jnp.float32)
    mu = jnp.mean(xf, -1, keepdims=True)
    var = jnp.mean(jnp.square(xf - mu), -1, keepdims=True)
    return ((xf - mu) * lax.rsqrt(var + eps) * g + b).astype(x.dtype)


def _rms_norm(x, g, eps=1e-6):
    xf = x.astype(jnp.float32)
    return (xf * lax.rsqrt(jnp.mean(xf * xf, -1, keepdims=True) + eps) * g).astype(x.dtype)


def _l2norm(x, eps=1e-6):
    xf = x.astype(jnp.float32)
    return xf * lax.rsqrt(jnp.sum(xf * xf, -1, keepdims=True) + eps)


def _proj_in(x, w_in):
    B, T, _ = x.shape
    sizes = [W_MIX,
             3 * W_MIX, GDN_HEADS, GDN_HEADS, W_MIX,
             W_MIX, W_MIX, W_MIX, IDX_HEADS * IDX_DIM, IDX_DIM, IDX_HEADS,
             3 * W_MIX, MLSTM_HEADS, MLSTM_HEADS, W_MIX,
             N_BRANCH * D_MODEL]
    offs = np.concatenate([[0], np.cumsum(sizes)]).tolist()
    xb = x.reshape(B * T, D_MODEL).astype(jnp.bfloat16)
    outs = [None] * len(sizes)

    def grouped(idx):
        w = jnp.concatenate([w_in[:, offs[i]:offs[i + 1]] for i in idx], axis=1).astype(jnp.bfloat16)
        for i, y in zip(idx, _mm_group(xb, w, [sizes[i] for i in idx])):
            outs[i] = y.reshape(B, T, sizes[i])

    grouped([i for i, s in enumerate(sizes) if s < LANES])
    grouped([i for i, s in enumerate(sizes) if s == W_MIX])
    for i, s in enumerate(sizes):
        if s > W_MIX:
            outs[i] = _mm(xb, w_in[:, offs[i]:offs[i + 1]]).reshape(B, T, s)
    return outs


def _take_rows(rows, idx):
    return jax.vmap(lambda r, i: r[i])(rows, idx)


def _short_conv(u, buf, w):
    T = u.shape[1]
    ext = jnp.concatenate([buf.astype(jnp.float32), u.astype(jnp.float32)], axis=1)
    y = sum(ext[:, j:j + T] * w[j] for j in range(CONV_W))
    return jax.nn.silu(y), ext[:, ext.shape[1] - (CONV_W - 1):]


def _run_chunks(chunk_fn, state, xs, is_prompt, chunk):
    if not is_prompt:
        state, out = chunk_fn(state, xs)
        return out, state
    state, out_meta = chunk_fn(state, tuple(t[:, :, :N_META] for t in xs))
    real = tuple(t[:, :, N_META:] for t in xs)
    n = real[0].shape[2] // chunk

    def to_chunks(t):
        return jnp.moveaxis(t.reshape(t.shape[:2] + (n, chunk) + t.shape[3:]), 2, 0)

    state, out = lax.scan(chunk_fn, state, tuple(to_chunks(t) for t in real))
    out = jnp.moveaxis(out, 0, 2)
    out = out.reshape(out.shape[:2] + (n * chunk,) + out.shape[4:])
    return jnp.concatenate([out_meta, out], axis=2), state


def _rwkv7(u, S0, shift0, mu, w_rkv, w0, w1, w2, a0, a1, a2, g1, g2, k_k, k_a, r_k, lnx_g, lnx_b):
    B, T, _ = u.shape
    uf = u.astype(jnp.float32)
    prev = jnp.concatenate([shift0[:, None].astype(jnp.float32), uf[:, :-1]], axis=1)
    xx = prev - uf
    xr, xw, xk, xv, xa, xg = (uf + xx * mu[j] for j in range(6))
    r = xr @ w_rkv[0]
    k = xk @ w_rkv[1]
    v = xv @ w_rkv[2]
    w = -jax.nn.softplus(-(w0 + jnp.tanh(xw @ w1) @ w2)) - 0.5
    a = jax.nn.sigmoid(a0 + (xa @ a1) @ a2)
    g = jax.nn.sigmoid(xg @ g1) @ g2
    hd = lambda t: t.reshape(B, T, RWKV_HEADS, RWKV_HEAD_DIM)
    kk = _l2norm(hd(k * k_k))
    k = hd(k * (1.0 + (a - 1.0) * k_a))
    r, v, a = hd(r), hd(v), hd(a)
    decay = jnp.exp(-jnp.exp(hd(w)))

    flat = lambda t: t.reshape(B, T, W_MIX)
    y, S = _rwkv_scan(flat(r), flat(decay), flat(k), flat(v), flat(kk), flat(a), S0.astype(jnp.float32))
    y = hd(y)
    mean = jnp.mean(y, -1, keepdims=True)
    var = jnp.mean(jnp.square(y - mean), -1, keepdims=True)
    y = ((y - mean) * lax.rsqrt(var + RWKV_GN_EPS)).reshape(B, T, W_MIX) * lnx_g + lnx_b
    bonus = jnp.sum(r * k * r_k, -1, keepdims=True) * v
    y = (y + bonus.reshape(B, T, W_MIX)) * g
    return y.astype(u.dtype), S, u[:, -1]


def _gdn_chunk(S, xs):
    q, k, v, g, beta = xs
    L = q.shape[2]
    tri = jnp.tril(jnp.ones((L, L), bool))
    stri = jnp.tril(jnp.ones((L, L), bool), -1)
    gc = jnp.cumsum(g, axis=-1)
    dec = jnp.exp(jnp.where(tri, gc[..., :, None] - gc[..., None, :], -jnp.inf))
    kb = k * beta[..., None]
    A = jnp.where(stri, jnp.einsum('bhid,bhjd->bhij', kb, k) * dec, 0.0)
    rhs = jnp.concatenate([v * beta[..., None], kb * jnp.exp(gc)[..., None]], axis=-1)
    sol = lax.linalg.triangular_solve(A + jnp.eye(L, dtype=A.dtype), rhs, left_side=True,
                                      lower=True, unit_diagonal=True)
    u_, w_ = sol[..., :GDN_HEAD_DIM], sol[..., GDN_HEAD_DIM:]
    v_new = u_ - jnp.einsum('bhld,bhde->bhle', w_, S)
    attn = jnp.where(tri, jnp.einsum('bhid,bhjd->bhij', q, k) * dec, 0.0)
    o = (jnp.einsum('bhld,bhde->bhle', q * jnp.exp(gc)[..., None], S)
         + jnp.einsum('bhij,bhje->bhie', attn, v_new))
    g_last = gc[..., -1]
    S = (S * jnp.exp(g_last)[..., None, None]
         + jnp.einsum('bhld,bhle->bhde', k * jnp.exp(g_last[..., None] - gc)[..., None], v_new))
    return S, o


def _gdn(qkv, a_pre, b_pre, z, S0, buf0, conv_w, A_log, dt_bias, norm_g, is_prompt):
    B, T, _ = qkv.shape
    y, new_buf = _short_conv(qkv, buf0, conv_w)
    q, k, v = jnp.split(y, 3, axis=-1)
    hd4 = lambda t: t.reshape(B, T, GDN_HEADS, GDN_HEAD_DIM)
    q = _l2norm(hd4(q)) * GDN_HEAD_DIM ** -0.5
    k = _l2norm(hd4(k))
    g = -jnp.exp(A_log.astype(jnp.float32)) * jax.nn.softplus(a_pre.astype(jnp.float32) + dt_bias)
    beta = jax.nn.sigmoid(b_pre.astype(jnp.float32))
    if is_prompt:
        pad = (-N_META) % GDN_CHUNK
        fp = lambda t: jnp.pad(t.reshape(B, T, -1), [(0, 0), (pad, 0), (0, 0)])
        o, S = _gdn_chunks(fp(q), fp(k), fp(v), fp(g), fp(beta), S0.astype(jnp.float32), GDN_CHUNK)
        o = hd4(o[:, pad:])
    else:
        hd = lambda t: t.transpose(0, 2, 1, 3)
        S, o = _gdn_chunk(S0.astype(jnp.float32), (hd(q), hd(k), hd(hd4(v)), g.transpose(0, 2, 1),
                                                    beta.transpose(0, 2, 1)))
        o = o.transpose(0, 2, 1, 3)
    o = _rms_norm(o, norm_g)
    o = o * jax.nn.silu(z.astype(jnp.float32).reshape(B, T, GDN_HEADS, GDN_HEAD_DIM))
    return o.reshape(B, T, W_MIX).astype(qkv.dtype), S, new_buf


def _alibi_slopes():
    return jnp.asarray(2.0 ** (-8.0 * np.arange(1, ATT_HEADS + 1) / ATT_HEADS), jnp.float32)


def _dsa_inputs(c_q, c_k, c_v, c_qi, c_ki, c_wi, ln_g, ln_b):
    B, T, _ = c_q.shape
    hd = lambda t: t.reshape(B, T, ATT_HEADS, ATT_HEAD_DIM)
    qi = c_qi.reshape(B, T, IDX_HEADS, IDX_DIM)
    ki = _layer_norm(c_ki, ln_g, ln_b)
    wi = c_wi * IDX_HEADS ** -0.5
    return hd(c_q), hd(c_k), hd(c_v), qi, ki, wi


def _dsa_block(q, qi, wi, q_pos, ki, gather, k_top):
    S = ki.shape[1]
    s_pos = jnp.arange(S, dtype=jnp.int32)
    sc = jnp.einsum('bqhd,bsd->bqhs', qi.astype(jnp.float32), ki.astype(jnp.float32)) * IDX_DIM ** -0.5
    isc = jnp.einsum('bqhs,bqh->bqs', jax.nn.relu(sc), wi.astype(jnp.float32))
    isc = jnp.where((s_pos[None, :] <= q_pos[:, None])[None], isc, -jnp.inf)
    _, sel = lax.top_k(isc, k_top)
    valid = sel <= q_pos[None, :, None]
    k_sel, v_sel = gather(sel)
    logits = jnp.einsum('bqhd,bqkhd->bqhk', q.astype(jnp.float32), k_sel.astype(jnp.float32)) * ATT_HEAD_DIM ** -0.5
    dist = (q_pos[None, :, None] - sel).astype(jnp.float32)
    logits = logits - _alibi_slopes()[None, None, :, None] * dist[:, :, None, :]
    logits = jnp.where(valid[:, :, None, :], logits, -jnp.inf)
    p = jax.nn.softmax(logits, axis=-1)
    return jnp.einsum('bqhk,bqkhd->bqhd', p, v_sel.astype(jnp.float32)).astype(q.dtype)


def _dsa_prompt(q, k, v, qi, ki, wi):
    B, T = q.shape[:2]
    k_top = min(TOPK_MAX, T // 4)
    n_blk = -(-T // Q_BLOCK)
    t_pad = n_blk * Q_BLOCK

    def blocks(t):
        t = jnp.pad(t, [(0, 0), (0, t_pad - T)] + [(0, 0)] * (t.ndim - 2))
        return jnp.swapaxes(t.reshape((B, n_blk, Q_BLOCK) + t.shape[2:]), 0, 1)

    def gather(sel):
        return _take_rows(k, sel), _take_rows(v, sel)

    def body(args):
        q_b, qi_b, wi_b, pos_b = args
        return _dsa_block(q_b, qi_b, wi_b, pos_b, ki, gather, k_top)

    pos = jnp.arange(t_pad, dtype=jnp.int32).reshape(n_blk, Q_BLOCK)
    out = lax.map(body, (blocks(q), blocks(qi), blocks(wi), pos))
    return jnp.swapaxes(out, 0, 1).reshape(B, t_pad, ATT_HEADS, ATT_HEAD_DIM)[:, :T]


def _dsa_sample(q, k, v, qi, ki, wi, k_pool, v_pool, i_pool, page_table):
    nb, Q = q.shape[:2]
    past_len = page_table.shape[1] * PAGE_SIZE
    ki_past = i_pool[page_table].reshape(nb, past_len, IDX_DIM)
    ki_all = jnp.concatenate([ki_past.astype(jnp.float32), ki.astype(jnp.float32)], axis=1)
    k_top = min(TOPK_MAX, (past_len + Q) // 4)

    def gather(sel):
        in_past = (sel < past_len)[..., None, None]
        sp = jnp.minimum(sel, past_len - 1)
        phys = page_table[jnp.arange(nb)[:, None, None], sp // PAGE_SIZE]
        off = sp % PAGE_SIZE
        sn = jnp.clip(sel - past_len, 0, Q - 1)
        return (jnp.where(in_past, k_pool[phys, off], _take_rows(k, sn)),
                jnp.where(in_past, v_pool[phys, off], _take_rows(v, sn)))

    q_pos = past_len + jnp.arange(Q, dtype=jnp.int32)
    return _dsa_block(q, qi, wi, q_pos, ki_all, gather, k_top)


def _mlstm_chunk(state, xs):
    C, n, m = state
    q, k, v, li, lf = xs
    L = q.shape[2]
    tri = jnp.tril(jnp.ones((L, L), bool))
    b = jnp.cumsum(lf, axis=-1)
    a = li - b
    m_t = b + jnp.maximum(m[..., None], lax.cummax(a, axis=2))
    dmat = jnp.exp(jnp.where(tri, b[..., :, None] + a[..., None, :] - m_t[..., :, None], -jnp.inf))
    inter = jnp.exp(b + m[..., None] - m_t)
    s = jnp.einsum('bhtd,bhsd->bhts', q, k) * dmat
    num = inter[..., None] * jnp.einsum('bhtd,bhde->bhte', q, C) + jnp.einsum('bhts,bhse->bhte', s, v)
    den = inter * jnp.einsum('bhtd,bhd->bht', q, n) + jnp.sum(s, axis=-1)
    h = num / jnp.maximum(jnp.abs(den), jnp.exp(-m_t))[..., None]
    m_new = m_t[..., -1]
    wend = jnp.exp(b[..., -1:] + a - m_new[..., None])
    carry = jnp.exp(b[..., -1] + m - m_new)
    C = carry[..., None, None] * C + jnp.einsum('bhs,bhsd,bhse->bhde', wend, k, v)
    n = carry[..., None] * n + jnp.einsum('bhs,bhsd->bhd', wend, k)
    return (C, n, m_new), h


MLSTM_BB = 2


def _mlstm_chunk_body(qkv_ref, li_ref, lf_ref, lirow_ref, lfrow_ref, c0_ref, n0_ref, m0_ref,
                      h_ref, cT_ref, nT_ref, mT_ref, C_scr, n_scr, m_scr):
    BB, L = qkv_ref.shape[:2]
    H, D = C_scr.shape[1:3]
    W = H * D
    chains = [(b, h) for b in range(BB) for h in range(H)]
    c = pl.program_id(1)

    @pl.when(c == 0)
    def _():
        C_scr[...] = c0_ref[...]
        n_scr[...] = n0_ref[...]
        m_scr[...] = m0_ref[...]

    ii = lax.broadcasted_iota(jnp.int32, (L, L), 0)
    jj = lax.broadcasted_iota(jnp.int32, (L, L), 1)
    tri = jj <= ii
    every = lambda f: [f(i) for i in range(len(chains))]
    col_of = lambda ref: [ref[b, :, h:h + 1] for b, h in chains]
    row_of = lambda ref: [ref[b, 0, h:h + 1, :] for b, h in chains]
    q = [qkv_ref[b, :, h * D:(h + 1) * D] for b, h in chains]
    k = [qkv_ref[b, :, W + h * D:W + (h + 1) * D] * D ** -0.5 for b, h in chains]
    v = [qkv_ref[b, :, 2 * W + h * D:2 * W + (h + 1) * D] for b, h in chains]
    li_c, lf_c, li_r, lf_r = col_of(li_ref), col_of(lf_ref), row_of(lirow_ref), row_of(lfrow_ref)
    C = [C_scr[b, h] for b, h in chains]
    n = [n_scr[b, h] for b, h in chains]
    m = [m_scr[b, h][:, :1] for b, h in chains]
    b_c = every(lambda i: jnp.sum(jnp.where(tri, lf_r[i], 0.0), axis=1, keepdims=True))
    b_r = every(lambda i: jnp.sum(jnp.where(ii <= jj, lf_c[i], 0.0), axis=0, keepdims=True))
    a_c = every(lambda i: li_c[i] - b_c[i])
    a_r = every(lambda i: li_r[i] - b_r[i])
    cmax = every(lambda i: jnp.max(jnp.where(tri, a_r[i], -jnp.inf), axis=1, keepdims=True))
    m_t = every(lambda i: b_c[i] + jnp.maximum(m[i], cmax[i]))
    dmat = every(lambda i: jnp.exp(jnp.where(tri, b_c[i] + a_r[i] - m_t[i], -jnp.inf)))
    inter = every(lambda i: jnp.exp(b_c[i] + m[i] - m_t[i]))
    s = every(lambda i: _dot_nt(q[i], k[i]) * dmat[i])
    qC = every(lambda i: _dot(q[i], C[i]))
    sv = every(lambda i: _dot(s[i], v[i]))
    qn = every(lambda i: jnp.sum(_round_bf16(q[i]) * _round_bf16(n[i]), axis=1, keepdims=True))
    den = every(lambda i: inter[i] * qn[i] + jnp.sum(s[i], axis=1, keepdims=True))
    m_new = every(lambda i: m_t[i][L - 1:L, :])
    b_last = every(lambda i: b_c[i][L - 1:L, :])
    wend = every(lambda i: jnp.exp(b_last[i] + a_c[i] - m_new[i]))
    carry = every(lambda i: jnp.exp(b_last[i] + m[i] - m_new[i]))
    wk = every(lambda i: wend[i] * k[i])
    for i, (b, h) in enumerate(chains):
        num = inter[i] * qC[i] + sv[i]
        h_ref[b, :, h * D:(h + 1) * D] = num / jnp.maximum(jnp.abs(den[i]), jnp.exp(-m_t[i]))
    for i, (b, h) in enumerate(chains):
        C_scr[b, h] = carry[i] * C[i] + _dot(wk[i].T, v[i])
        n_scr[b, h] = carry[i] * n[i] + jnp.sum(_round_bf16(wend[i]) * _round_bf16(k[i]), axis=0, keepdims=True)
        m_scr[b, h] = jnp.broadcast_to(m_new[i], (1, D))

    @pl.when(c == pl.num_programs(1) - 1)
    def _():
        cT_ref[...] = C_scr[...]
        nT_ref[...] = n_scr[...]
        mT_ref[...] = m_scr[...]


def _mlstm_chunks(qkv, li, lf, C0, n0, m0, L):
    B, T, W3 = qkv.shape
    H, D = C0.shape[1:3]
    n_c = T // L
    bb = MLSTM_BB if B % MLSTM_BB == 0 else 1
    rows = lambda t: jnp.swapaxes(t.reshape(B, n_c, L, H), 2, 3)
    gate = pl.BlockSpec((bb, L, H), lambda b, c: (b, c, 0))
    gate_row = pl.BlockSpec((bb, 1, H, L), lambda b, c: (b, c, 0, 0))
    st_c = pl.BlockSpec((bb, H, D, D), lambda b, c: (b, 0, 0, 0))
    st_v = pl.BlockSpec((bb, H, 1, D), lambda b, c: (b, 0, 0, 0))
    vec = jax.ShapeDtypeStruct((B, H, 1, D), jnp.float32)
    h, C, n, m = pl.pallas_call(
        _mlstm_chunk_body,
        grid=(B // bb, n_c),
        in_specs=[pl.BlockSpec((bb, L, W3), lambda b, c: (b, c, 0)), gate, gate, gate_row, gate_row,
                  st_c, st_v, st_v],
        out_specs=[pl.BlockSpec((bb, L, H * D), lambda b, c: (b, c, 0)), st_c, st_v, st_v],
        out_shape=[jax.ShapeDtypeStruct((B, T, H * D), jnp.float32),
                   jax.ShapeDtypeStruct((B, H, D, D), jnp.float32), vec, vec],
        scratch_shapes=[pltpu.VMEM((bb, H, D, D), jnp.float32), pltpu.VMEM((bb, H, 1, D), jnp.float32),
                        pltpu.VMEM((bb, H, 1, D), jnp.float32)],
        compiler_params=pltpu.CompilerParams(dimension_semantics=("arbitrary", "arbitrary"),
                                             vmem_limit_bytes=VMEM_LIMIT_BYTES),
        name="mlstm_chunks",
    )(qkv, li, lf, rows(li), rows(lf), C0, n0[:, :, None, :], jnp.broadcast_to(m0[:, :, None, None], (B, H, 1, D)))
    return h, C, n[:, :, 0, :], m[:, :, 0, 0]


MLSTM_PAD_LOG_I = -1e30


def _mlstm(qkv, i_pre, f_pre, o_pre, C0, n0, m0, b_i, b_f, norm_g, is_prompt):
    B, T, _ = qkv.shape
    if is_prompt:
        pad = (-N_META) % MLSTM_CHUNK
        fp = lambda t, c=0.0: jnp.pad(t, [(0, 0), (pad, 0), (0, 0)], constant_values=c)
        li = fp(i_pre.astype(jnp.float32) + b_i, MLSTM_PAD_LOG_I)
        lf = fp(jax.nn.log_sigmoid(f_pre.astype(jnp.float32) + b_f))
        h, C, n, m = _mlstm_chunks(fp(qkv.astype(jnp.float32)), li, lf, C0.astype(jnp.float32),
                                   n0.astype(jnp.float32), m0.astype(jnp.float32), MLSTM_CHUNK)
        h = _rms_norm(h[:, pad:].reshape(B, T, MLSTM_HEADS, MLSTM_HEAD_DIM), norm_g).reshape(B, T, W_MIX)
        h = jax.nn.sigmoid(o_pre.astype(jnp.float32)) * h
        return h.astype(qkv.dtype), C, n, m
    q, k, v = jnp.split(qkv.astype(jnp.float32), 3, axis=-1)
    hd = lambda t: t.reshape(B, T, MLSTM_HEADS, MLSTM_HEAD_DIM).transpose(0, 2, 1, 3)
    q, k, v = hd(q), hd(k) * MLSTM_HEAD_DIM ** -0.5, hd(v)
    li = (i_pre.astype(jnp.float32) + b_i).transpose(0, 2, 1)
    lf = jax.nn.log_sigmoid(f_pre.astype(jnp.float32) + b_f).transpose(0, 2, 1)
    st0 = (C0.astype(jnp.float32), n0.astype(jnp.float32), m0.astype(jnp.float32))
    h, (C, n, m) = _run_chunks(_mlstm_chunk, st0, (q, k, v, li, lf), is_prompt, MLSTM_CHUNK)
    h = _rms_norm(h.transpose(0, 2, 1, 3), norm_g).reshape(B, T, W_MIX)
    h = jax.nn.sigmoid(o_pre.astype(jnp.float32)) * h
    return h.astype(qkv.dtype), C, n, m


def _moe_experts_body(blk_exp_ref, n_used_ref, x_ref, wg_ref, wu_ref, wd_ref, o_ref, wg_bf, wu_bf, wd_bf):
    i = pl.program_id(0)
    new_expert = (i == 0) | (blk_exp_ref[i] != blk_exp_ref[jnp.maximum(i - 1, 0)])

    @pl.when(new_expert)
    def _():
        wg_bf[...] = wg_ref[0, 0].astype(jnp.bfloat16)
        wu_bf[...] = wu_ref[0, 0].astype(jnp.bfloat16)
        wd_bf[...] = wd_ref[0, 0].astype(jnp.bfloat16)

    @pl.when(i < n_used_ref[0])
    def _():
        xb = x_ref[...].astype(jnp.bfloat16)
        hg = jnp.dot(xb, wg_bf[...], preferred_element_type=jnp.float32)
        hu = jnp.dot(xb, wu_bf[...], preferred_element_type=jnp.float32)
        act = (hg * jax.nn.sigmoid(hg)) * hu
        o_ref[...] = jnp.dot(act.astype(jnp.bfloat16), wd_bf[...],
                             preferred_element_type=jnp.float32).astype(o_ref.dtype)

    @pl.when(i >= n_used_ref[0])
    def _():
        o_ref[...] = jnp.zeros_like(o_ref)


def _moe_experts(xb, blk_exp, n_used, w_g, w_u, w_d, l):
    R, D = xb.shape
    F = w_g.shape[-1]
    n_blocks = R // MOE_BLOCK
    grid_spec = pltpu.PrefetchScalarGridSpec(
        num_scalar_prefetch=2,
        grid=(n_blocks,),
        in_specs=[pl.BlockSpec((MOE_BLOCK, D), lambda i, be, nu: (i, 0)),
                  pl.BlockSpec((1, 1, D, F), lambda i, be, nu: (l, be[i], 0, 0)),
                  pl.BlockSpec((1, 1, D, F), lambda i, be, nu: (l, be[i], 0, 0)),
                  pl.BlockSpec((1, 1, F, D), lambda i, be, nu: (l, be[i], 0, 0))],
        out_specs=pl.BlockSpec((MOE_BLOCK, D), lambda i, be, nu: (i, 0)),
        scratch_shapes=[pltpu.VMEM((D, F), jnp.bfloat16), pltpu.VMEM((D, F), jnp.bfloat16),
                        pltpu.VMEM((F, D), jnp.bfloat16)])
    return pl.pallas_call(
        _moe_experts_body,
        grid_spec=grid_spec,
        out_shape=jax.ShapeDtypeStruct((R, D), jnp.bfloat16),
        compiler_params=pltpu.CompilerParams(dimension_semantics=("arbitrary",),
                                             vmem_limit_bytes=VMEM_LIMIT_BYTES),
        name="moe_experts",
    )(blk_exp, n_used, xb, w_g, w_u, w_d)


def _expert_dispatch(xt, eid, gate, w_g, w_u, w_d, l):
    N = xt.shape[0]
    A = N * TOP_K_INNER
    e_flat = eid.reshape(-1)
    tok = jnp.arange(A, dtype=jnp.int32) // TOP_K_INNER
    onehot = (e_flat[:, None] == jnp.arange(N_EXPERTS, dtype=e_flat.dtype)[None, :]).astype(jnp.int32)
    running = jnp.cumsum(onehot, axis=0)
    counts = running[-1]
    padded = (counts + MOE_BLOCK - 1) // MOE_BLOCK * MOE_BLOCK
    pad_end = jnp.cumsum(padded)
    pad_start = pad_end - padded
    slot = jnp.sum(onehot * (pad_start[None, :] + running - 1), axis=1)
    n_blocks = -(-A // MOE_BLOCK) + N_EXPERTS
    slot_tok = jnp.full((n_blocks * MOE_BLOCK,), N, jnp.int32).at[slot].set(tok)
    blk_exp = jnp.minimum(jnp.searchsorted(pad_end, jnp.arange(n_blocks) * MOE_BLOCK, side='right'), N_EXPERTS - 1)
    x_pad = jnp.concatenate([xt, jnp.zeros((1, xt.shape[1]), xt.dtype)], axis=0).astype(jnp.bfloat16)
    n_used = (pad_end[-1:] // MOE_BLOCK).astype(jnp.int32)
    yb = _moe_experts(x_pad[slot_tok], blk_exp.astype(jnp.int32), n_used, w_g, w_u, w_d, l)
    slot = slot.reshape(N, TOP_K_INNER)
    return sum(yb[slot[:, j]].astype(jnp.float32) * _round_bf16(gate[:, j:j + 1]) for j in range(TOP_K_INNER))


def _moe(x, w_group, w_expert, w_g, w_u, w_d, l):
    shp = x.shape
    xt = x.reshape(-1, D_MODEL)
    N = xt.shape[0]
    xf = xt.astype(jnp.float32)
    pg = jax.nn.softmax(xf @ w_group.astype(jnp.float32), axis=-1)
    p_top, g_sel = lax.top_k(pg, 1)
    le = (xf @ w_expert.astype(jnp.float32)).reshape(N, N_GROUPS, EXPERTS_PER_GROUP)
    le_g = le[:, N_GROUPS - 1]
    for grp in range(N_GROUPS - 2, -1, -1):
        le_g = jnp.where(g_sel == grp, le[:, grp], le_g)
    v2, j2 = lax.top_k(le_g, TOP_K_INNER)
    gate = jax.nn.softmax(v2, axis=-1) * p_top
    eid = g_sel * EXPERTS_PER_GROUP + j2
    return _expert_dispatch(xt, eid, gate, w_g, w_u, w_d, l).reshape(shp).astype(x.dtype)


MERGE_TN = 512


def _branch_merge_body(*refs):
    y_refs, g_refs, w_refs, o_ref = refs[:N_BRANCH], refs[N_BRANCH:2 * N_BRANCH], refs[2 * N_BRANCH:3 * N_BRANCH], refs[-1]
    acc = 0.0
    for y_ref, g_ref, w_ref in zip(y_refs, g_refs, w_refs):
        acc = acc + jax.nn.sigmoid(g_ref[...]) * _dot(y_ref[...], w_ref[0, 0])
    o_ref[...] = acc.astype(o_ref.dtype)


def _out_ln_body(m_ref, w_ref, x_ref, g_ref, b_ref, o_ref, *, alpha, eps):
    z = alpha * x_ref[...] + jnp.dot(m_ref[...], w_ref[...], preferred_element_type=jnp.float32)
    mu = jnp.mean(z, -1, keepdims=True)
    var = jnp.mean(jnp.square(z - mu), -1, keepdims=True)
    o_ref[...] = (z - mu) * lax.rsqrt(var + eps) * g_ref[...] + b_ref[...]


def _merge_out_ln(ys, gates, x, w_branch, w_out, ln_g, ln_b, l, alpha, tm=1024):
    B, T, D = x.shape
    M = B * T
    tm = min(tm, M)
    tn = MERGE_TN
    nj = D // tn
    row = lambda t: t.reshape(M, t.shape[-1])
    merged = pl.pallas_call(
        _branch_merge_body,
        grid=(pl.cdiv(M, tm), nj),
        in_specs=[pl.BlockSpec((tm, W_MIX), lambda i, j: (i, 0))] * N_BRANCH
                 + [pl.BlockSpec((tm, tn), lambda i, j, b=b: (i, b * nj + j)) for b in range(N_BRANCH)]
                 + [pl.BlockSpec((1, 1, W_MIX, tn), lambda i, j, b=b: (l, b, 0, j)) for b in range(N_BRANCH)],
        out_specs=pl.BlockSpec((tm, tn), lambda i, j: (i, j)),
        out_shape=jax.ShapeDtypeStruct((M, D), jnp.bfloat16),
        compiler_params=pltpu.CompilerParams(dimension_semantics=("arbitrary", "arbitrary"),
                                             vmem_limit_bytes=VMEM_LIMIT_BYTES),
        name="branch_merge",
    )(*[row(y) for y in ys], *([row(gates)] * N_BRANCH), *([w_branch] * N_BRANCH))
    tm2 = min(tm // 2, M)
    out = pl.pallas_call(
        functools.partial(_out_ln_body, alpha=alpha, eps=1e-5),
        grid=(pl.cdiv(M, tm2),),
        in_specs=[pl.BlockSpec((tm2, D), lambda i: (i, 0)),
                  pl.BlockSpec((D, D), lambda i: (0, 0)),
                  pl.BlockSpec((tm2, D), lambda i: (i, 0)),
                  pl.BlockSpec((1, D), lambda i: (0, 0)),
                  pl.BlockSpec((1, D), lambda i: (0, 0))],
        out_specs=pl.BlockSpec((tm2, D), lambda i: (i, 0)),
        out_shape=jax.ShapeDtypeStruct((M, D), jnp.float32),
        compiler_params=pltpu.CompilerParams(dimension_semantics=("arbitrary",),
                                             vmem_limit_bytes=VMEM_LIMIT_BYTES),
        name="out_ln",
    )(merged, w_out.astype(jnp.bfloat16), row(x), ln_g.reshape(1, D), ln_b.reshape(1, D))
    return out.reshape(B, T, D)


def _layer(x, W, l, st, dsa_fn, is_prompt, alpha):
    B, T, _ = x.shape
    (a_u, b_qkv, b_a, b_b, b_z, c_q, c_k, c_v, c_qi, c_ki, c_wi,
     d_qkv, d_i, d_f, d_o, gates) = _proj_in(x, W['w_in'][l])
    rwkv_S0, rwkv_shift0, gdn_S0, gdn_conv0, mC0, mn0, mm0 = st
    y_a, rwkv_S, rwkv_shift = _rwkv7(
        a_u, rwkv_S0, rwkv_shift0, W['rwkv_mu'][l], W['rwkv_w_rkv'][l], W['rwkv_w0'][l], W['rwkv_w1'][l],
        W['rwkv_w2'][l], W['rwkv_a0'][l], W['rwkv_a1'][l], W['rwkv_a2'][l], W['rwkv_g1'][l], W['rwkv_g2'][l],
        W['rwkv_k_k'][l], W['rwkv_k_a'][l], W['rwkv_r_k'][l], W['rwkv_lnx_g'][l], W['rwkv_lnx_b'][l])
    y_b, gdn_S, gdn_conv = _gdn(b_qkv, b_a, b_b, b_z, gdn_S0, gdn_conv0, W['gdn_conv_w'][l], W['gdn_A_log'][l],
                                W['gdn_dt_bias'][l], W['gdn_norm_g'][l], is_prompt)
    q, k, v, qi, ki, wi = _dsa_inputs(c_q, c_k, c_v, c_qi, c_ki, c_wi, W['idx_ln_g'][l], W['idx_ln_b'][l])
    y_c = dsa_fn(q, k, v, qi, ki, wi).reshape(B, T, W_MIX)
    y_d, mC, mn, mm = _mlstm(d_qkv, d_i, d_f, d_o, mC0, mn0, mm0, W['mlstm_b_i'][l], W['mlstm_b_f'][l],
                             W['mlstm_norm_g'][l], is_prompt)
    x = _merge_out_ln((y_a, y_b, y_c, y_d), gates, x, W['w_branch'], W['w_out'][l], W['ln1_g'][l], W['ln1_b'][l],
                      l, alpha)
    ffn = _moe(x, W['moe_w_group'][l], W['moe_w_expert'][l], W['moe_w_gate'], W['moe_w_up'], W['moe_w_down'], l)
    x = _layer_norm(alpha * x + ffn, W['ln2_g'][l], W['ln2_b'][l])
    return x, (k, v, ki, rwkv_S, rwkv_shift, gdn_S, gdn_conv, mC, mn, mm)


def kernel(x_prompt, x_sample, cache_k, cache_v, cache_idx_k, page_table, state_rwkv_S, state_rwkv_shift,
           state_gdn_S, state_gdn_conv, state_mlstm_C, state_mlstm_n, state_mlstm_m, meta_tokens, ln_in_g,
           ln_in_b, w_in, rwkv_mu, rwkv_w_rkv, rwkv_w0, rwkv_w1, rwkv_w2, rwkv_a0, rwkv_a1, rwkv_a2, rwkv_g1,
           rwkv_g2, rwkv_k_k, rwkv_k_a, rwkv_r_k, rwkv_lnx_g, rwkv_lnx_b, gdn_conv_w, gdn_A_log, gdn_dt_bias,
           gdn_norm_g, idx_ln_g, idx_ln_b, mlstm_b_i, mlstm_b_f, mlstm_norm_g, w_branch, w_out, ln1_g, ln1_b,
           ln2_g, ln2_b, moe_w_group, moe_w_expert, moe_w_gate, moe_w_up, moe_w_down):
    W = dict(w_in=w_in, rwkv_mu=rwkv_mu, rwkv_w_rkv=rwkv_w_rkv, rwkv_w0=rwkv_w0, rwkv_w1=rwkv_w1,
             rwkv_w2=rwkv_w2, rwkv_a0=rwkv_a0, rwkv_a1=rwkv_a1, rwkv_a2=rwkv_a2, rwkv_g1=rwkv_g1,
             rwkv_g2=rwkv_g2, rwkv_k_k=rwkv_k_k, rwkv_k_a=rwkv_k_a, rwkv_r_k=rwkv_r_k, rwkv_lnx_g=rwkv_lnx_g,
             rwkv_lnx_b=rwkv_lnx_b, gdn_conv_w=gdn_conv_w, gdn_A_log=gdn_A_log, gdn_dt_bias=gdn_dt_bias,
             gdn_norm_g=gdn_norm_g, idx_ln_g=idx_ln_g, idx_ln_b=idx_ln_b, mlstm_b_i=mlstm_b_i,
             mlstm_b_f=mlstm_b_f, mlstm_norm_g=mlstm_norm_g, w_branch=w_branch, w_out=w_out, ln1_g=ln1_g,
             ln1_b=ln1_b, ln2_g=ln2_g, ln2_b=ln2_b, moe_w_group=moe_w_group, moe_w_expert=moe_w_expert,
             moe_w_gate=moe_w_gate, moe_w_up=moe_w_up, moe_w_down=moe_w_down)
    f32 = jnp.float32
    depth = w_in.shape[0]
    alpha = (2 * depth) ** 0.25
    B = x_prompt.shape[0]
    meta = jnp.broadcast_to(meta_tokens.astype(x_prompt.dtype)[None], (B, N_META, D_MODEL))
    hp = _layer_norm(jnp.concatenate([meta, x_prompt], axis=1), ln_in_g, ln_in_b)
    hs = _layer_norm(x_sample, ln_in_g, ln_in_b)
    zero_state = (jnp.zeros((B, RWKV_HEADS, RWKV_HEAD_DIM, RWKV_HEAD_DIM), f32),
                  jnp.zeros((B, W_MIX), f32),
                  jnp.zeros((B, GDN_HEADS, GDN_HEAD_DIM, GDN_HEAD_DIM), f32),
                  jnp.zeros((B, CONV_W - 1, 3 * W_MIX), f32),
                  jnp.zeros((B, MLSTM_HEADS, MLSTM_HEAD_DIM, MLSTM_HEAD_DIM), f32),
                  jnp.zeros((B, MLSTM_HEADS, MLSTM_HEAD_DIM), f32),
                  jnp.zeros((B, MLSTM_HEADS), f32))
    new_p, new_s = [], []
    for l in range(depth):
        hp, sp = _layer(hp, W, l, zero_state, _dsa_prompt_pallas, True, alpha)
        dsa_s = functools.partial(_dsa_sample_pallas, cache_k=cache_k, cache_v=cache_v, cache_idx_k=cache_idx_k,
                                  page_table=page_table, l=l)
        st_s = (state_rwkv_S[l], state_rwkv_shift[l], state_gdn_S[l], state_gdn_conv[l],
                state_mlstm_C[l], state_mlstm_n[l], state_mlstm_m[l])
        hs, ss = _layer(hs, W, l, st_s, dsa_s, False, alpha)
        new_p.append(sp)
        new_s.append(ss)

    def stack(rows, j):
        return jnp.stack([r[j] for r in rows])

    k_p, v_p, ik_p, rS_p, rsh_p, gS_p, gc_p, mC_p, mn_p, mm_p = (stack(new_p, j) for j in range(10))
    k_s, v_s, ik_s, rS_s, rsh_s, gS_s, gc_s, mC_s, mn_s, mm_s = (stack(new_s, j) for j in range(10))
    y_prompt = hp[:, N_META:]
    y_sample = hs
    return (y_prompt, y_sample, k_p, v_p, ik_p, k_s, v_s, ik_s, rS_p, rS_s, rsh_p, rsh_s,
            gS_p, gS_s, gc_p, gc_s, mC_p, mC_s, mn_p, mn_s, mm_p, mm_s)
```

```python
import math, functools
import jax, jax.numpy as jnp
from jax import lax
import numpy as np
from jax.experimental import pallas as pl
from jax.experimental.pallas import tpu as pltpu

D_MODEL = 2048
N_META = 16
N_BRANCH = 4
W_MIX = D_MODEL // 4
RWKV_HEAD_DIM = 64
RWKV_HEADS = W_MIX // RWKV_HEAD_DIM
RWKV_GN_EPS = 64e-5
GDN_HEADS = 4
GDN_HEAD_DIM = W_MIX // GDN_HEADS
CONV_W = 4
GDN_CHUNK = 64
ATT_HEADS = 4
ATT_HEAD_DIM = W_MIX // ATT_HEADS
IDX_HEADS = 8
IDX_DIM = 64
TOPK_MAX = 256
MLSTM_HEADS = 4
MLSTM_HEAD_DIM = W_MIX // MLSTM_HEADS
MLSTM_CHUNK = 64
N_GROUPS = 4
EXPERTS_PER_GROUP = 8
N_EXPERTS = N_GROUPS * EXPERTS_PER_GROUP
TOP_K_INNER = 2
D_EXPERT = D_MODEL // 4
MOE_BLOCK = 128
PAGE_SIZE = 128

VMEM_LIMIT_BYTES = 56 * 1024 * 1024


def _mm_body(x_ref, w_ref, o_ref):
    o_ref[...] = jnp.dot(x_ref[...].astype(jnp.bfloat16), w_ref[...].astype(jnp.bfloat16),
                         preferred_element_type=jnp.float32)


def _mm(x, w, tm=1024, tn=1024):
    M, K = x.shape
    N = w.shape[1]
    tm = min(tm, M)
    tn = N if N <= 3 * W_MIX else tn
    return pl.pallas_call(
        _mm_body,
        grid=(pl.cdiv(N, tn), pl.cdiv(M, tm)),
        in_specs=[pl.BlockSpec((tm, K), lambda j, i: (i, 0)),
                  pl.BlockSpec((K, tn), lambda j, i: (0, j))],
        out_specs=pl.BlockSpec((tm, tn), lambda j, i: (i, j)),
        out_shape=jax.ShapeDtypeStruct((M, N), jnp.float32),
        compiler_params=pltpu.CompilerParams(dimension_semantics=("arbitrary", "arbitrary"),
                                             vmem_limit_bytes=VMEM_LIMIT_BYTES),
        name="proj_mm",
    )(x, w)


def _mm_group_body(x_ref, w_ref, *o_refs, offs):
    if w_ref.shape[1] <= LANES:
        acc = jnp.dot(x_ref[...], w_ref[...], preferred_element_type=jnp.float32)
        for o_ref, (a, b) in zip(o_refs, offs):
            o_ref[...] = acc[:, a:b]
        return
    for o_ref, (a, b) in zip(o_refs, offs):
        o_ref[...] = jnp.dot(x_ref[...], w_ref[:, a:b], preferred_element_type=jnp.float32)


def _mm_group(x, w, sizes, tm=512):
    M, K = x.shape
    tm = min(tm, M)
    ends = np.cumsum(sizes).tolist()
    offs = tuple(zip([0] + ends[:-1], ends))
    return pl.pallas_call(
        functools.partial(_mm_group_body, offs=offs),
        grid=(pl.cdiv(M, tm),),
        in_specs=[pl.BlockSpec((tm, K), lambda i: (i, 0)),
                  pl.BlockSpec((K, ends[-1]), lambda i: (0, 0))],
        out_specs=[pl.BlockSpec((tm, s), lambda i: (i, 0)) for s in sizes],
        out_shape=[jax.ShapeDtypeStruct((M, s), jnp.float32) for s in sizes],
        compiler_params=pltpu.CompilerParams(dimension_semantics=("arbitrary",),
                                             vmem_limit_bytes=VMEM_LIMIT_BYTES),
        name="proj_mm_group",
    )(x, w)


LANES = 128
INT_MIN = -2 ** 31
DSA_TQ = 128
DSA_SEGMENTS = 6


def _round_bf16(x):
    return x.astype(jnp.bfloat16).astype(jnp.float32)


def _topk_mask(isc, valid, key_ref, k_top):
    rows, t_pad = key_ref.shape
    bits = lax.bitcast_convert_type(isc, jnp.int32)
    key = bits ^ ((bits >> 31) & 0x7FFFFFFF)
    key_ref[...] = jnp.where(valid, key, INT_MIN)

    def count_ge(c):
        return jnp.sum(jnp.where(key_ref[...] >= c, 1.0, 0.0), axis=1, keepdims=True)

    kf = float(k_top)
    thr = jnp.where(count_ge(jnp.zeros((rows, 1), jnp.int32)) >= kf, 0, INT_MIN).astype(jnp.int32)

    def bit_step(j, thr):
        cand = thr | jnp.left_shift(jnp.int32(1), 30 - j)
        return jnp.where(count_ge(cand) >= kf, cand, thr)

    thr = lax.fori_loop(0, 31, bit_step, thr)

    key = key_ref[...]
    gt = key > thr
    eq = key == thr
    need = kf - jnp.sum(jnp.where(gt, 1.0, 0.0), axis=1, keepdims=True)
    upper = (lax.broadcasted_iota(jnp.int32, (LANES, LANES), 0)
             < lax.broadcasted_iota(jnp.int32, (LANES, LANES), 1)).astype(jnp.bfloat16)
    eqf = jnp.where(eq, 1.0, 0.0)
    off = jnp.zeros((rows, 1), jnp.float32)
    pre = []
    for c in range(t_pad // LANES):
        e_c = eqf[:, c * LANES:(c + 1) * LANES]
        pre.append(jnp.dot(e_c.astype(jnp.bfloat16), upper, preferred_element_type=jnp.float32) + off)
        off = off + jnp.sum(e_c, axis=1, keepdims=True)
    prefix = jnp.concatenate(pre, axis=1)
    return valid & (gt | (eq & (prefix < need)))


def _dsa_prompt_body(q_ref, qi_ref, wi_ref, kT_ref, v_ref, kiT_ref, o_ref, key_ref, *, k_top, first_block):
    tq, t_pad = key_ref.shape
    i = pl.program_id(1) + first_block
    row = lax.broadcasted_iota(jnp.int32, (tq, t_pad), 0) + i * tq
    col = lax.broadcasted_iota(jnp.int32, (tq, t_pad), 1)
    valid = col <= row

    wi = _round_bf16(wi_ref[0])
    isc = jnp.zeros((tq, t_pad), jnp.float32)
    for h in range(IDX_HEADS):
        sc = jnp.dot(qi_ref[0, h].astype(jnp.bfloat16), kiT_ref[0],
                     preferred_element_type=jnp.float32) * IDX_DIM ** -0.5
        isc = isc + _round_bf16(jnp.maximum(sc, 0.0)) * wi[:, h:h + 1]

    sel = _topk_mask(isc, valid, key_ref, k_top)

    dist = (row - col).astype(jnp.float32)
    for h in range(ATT_HEADS):
        hs = slice(h * ATT_HEAD_DIM, (h + 1) * ATT_HEAD_DIM)
        slope = 2.0 ** (-8.0 * (h + 1) / ATT_HEADS)
        lg = jnp.dot(q_ref[0, :, hs].astype(jnp.bfloat16), kT_ref[0, hs, :],
                     preferred_element_type=jnp.float32) * ATT_HEAD_DIM ** -0.5 - slope * dist
        lg = jnp.where(sel, lg, -jnp.inf)
        e = jnp.exp(lg - jnp.max(lg, axis=1, keepdims=True))
        p = e * (1.0 / jnp.sum(e, axis=1, keepdims=True))
        o_ref[0, :, hs] = jnp.dot(p.astype(jnp.bfloat16), v_ref[0, :, hs], preferred_element_type=jnp.float32)


def _dsa_prompt_pallas(q, k, v, qi, ki, wi):
    B, T = q.shape[:2]
    k_top = min(TOPK_MAX, T // 4)
    tq = DSA_TQ
    t_pad = pl.cdiv(T, LANES) * LANES
    padt = lambda t: jnp.pad(t, [(0, 0), (0, t_pad - T)] + [(0, 0)] * (t.ndim - 2))
    q2 = padt(q.reshape(B, T, W_MIX))
    kT = jnp.swapaxes(padt(k.reshape(B, T, W_MIX)), 1, 2).astype(jnp.bfloat16)
    v2 = padt(v.reshape(B, T, W_MIX)).astype(jnp.bfloat16)
    qi2 = jnp.swapaxes(padt(qi), 1, 2)
    kiT = jnp.swapaxes(padt(ki), 1, 2).astype(jnp.bfloat16)
    wi2 = padt(wi)
    n_q = t_pad // tq
    bounds = sorted({-(-n_q * s // DSA_SEGMENTS) for s in range(DSA_SEGMENTS + 1)})
    outs = []
    for i0, i1 in zip(bounds[:-1], bounds[1:]):
        tk = i1 * tq
        outs.append(pl.pallas_call(
            functools.partial(_dsa_prompt_body, k_top=k_top, first_block=i0),
            grid=(B, i1 - i0),
            in_specs=[pl.BlockSpec((1, tq, W_MIX), lambda b, i, i0=i0: (b, i + i0, 0)),
                      pl.BlockSpec((1, IDX_HEADS, tq, IDX_DIM), lambda b, i, i0=i0: (b, 0, i + i0, 0)),
                      pl.BlockSpec((1, tq, IDX_HEADS), lambda b, i, i0=i0: (b, i + i0, 0)),
                      pl.BlockSpec((1, W_MIX, tk), lambda b, i: (b, 0, 0)),
                      pl.BlockSpec((1, tk, W_MIX), lambda b, i: (b, 0, 0)),
                      pl.BlockSpec((1, IDX_DIM, tk), lambda b, i: (b, 0, 0))],
            out_specs=pl.BlockSpec((1, tq, W_MIX), lambda b, i: (b, i, 0)),
            out_shape=jax.ShapeDtypeStruct((B, (i1 - i0) * tq, W_MIX), jnp.float32),
            scratch_shapes=[pltpu.VMEM((tq, tk), jnp.int32)],
            compiler_params=pltpu.CompilerParams(dimension_semantics=("arbitrary", "arbitrary"),
                                                 vmem_limit_bytes=VMEM_LIMIT_BYTES),
            name="dsa_prompt",
        )(q2, qi2, wi2, kT, v2, kiT))
    out = jnp.concatenate(outs, axis=1)
    return out[:, :T].reshape(B, T, ATT_HEADS, ATT_HEAD_DIM)


SUBLANES = 8


def _dsa_sample_body(pt_ref, q_ref, kn_ref, vn_ref, qi_ref, kin_ref, wi_ref, *rest, n_pages, k_top):
    ki_pages, k_pages, v_pages = rest[:n_pages], rest[n_pages:2 * n_pages], rest[2 * n_pages:3 * n_pages]
    o_ref, key_ref = rest[3 * n_pages], rest[3 * n_pages + 1]
    R = SUBLANES
    HP = ATT_HEADS * PAGE_SIZE
    t_pad = key_ref.shape[1]
    past_len = n_pages * PAGE_SIZE

    qi = qi_ref[0].astype(jnp.bfloat16)
    wi = _round_bf16(wi_ref[0])

    def idx_tile(ki_page):
        sc = _dot_nt(qi, ki_page) * IDX_DIM ** -0.5
        return jnp.sum(_round_bf16(jnp.maximum(sc, 0.0)) * wi, axis=0, keepdims=True)

    first_row = lax.broadcasted_iota(jnp.int32, (PAGE_SIZE, 1), 0) == 0
    kin_page = jnp.where(first_row, kin_ref[0], 0.0)
    isc = jnp.concatenate([idx_tile(ki_pages[j][0, 0]) for j in range(n_pages)] + [idx_tile(kin_page)], axis=1)
    col = lax.broadcasted_iota(jnp.int32, (R, t_pad), 1)
    sel = _topk_mask(jnp.broadcast_to(isc, (R, t_pad)), col <= past_len, key_ref, k_top)

    expand = (lax.broadcasted_iota(jnp.int32, (PAGE_SIZE, HP), 1) // ATT_HEADS
              == lax.broadcasted_iota(jnp.int32, (PAGE_SIZE, HP), 0)).astype(jnp.bfloat16)
    sel_rows = jnp.concatenate([jnp.where(sel[:, j * PAGE_SIZE:(j + 1) * PAGE_SIZE], 1.0, 0.0)
                                for j in range(n_pages + 1)], axis=0).astype(jnp.bfloat16)
    sel_hp = jnp.dot(sel_rows, expand, preferred_element_type=jnp.float32)

    lane = lax.broadcasted_iota(jnp.int32, (R, HP), 1)
    hrow = lax.broadcasted_iota(jnp.int32, (R, HP), 0) % ATT_HEADS
    own = lane % ATT_HEADS == hrow
    slope = sum(jnp.where(hrow == h, 2.0 ** (-8.0 * (h + 1) / ATT_HEADS), 0.0) for h in range(ATT_HEADS))
    q8 = q_ref[0].astype(jnp.bfloat16)
    zeros = jnp.zeros((HP - R, ATT_HEAD_DIM), jnp.float32)
    k_all = [k_pages[j][0, 0] for j in range(n_pages)] + [jnp.concatenate([kn_ref[0], zeros], axis=0)]
    v_all = [v_pages[j][0, 0] for j in range(n_pages)] + [jnp.concatenate([vn_ref[0], zeros], axis=0)]
    tiles = []
    for j, kp in enumerate(k_all):
        dist = (past_len - j * PAGE_SIZE - lane // ATT_HEADS).astype(jnp.float32)
        lg = _dot_nt(q8, kp) * ATT_HEAD_DIM ** -0.5 - slope * dist
        tiles.append(jnp.where(own & (sel_hp[j * R:(j + 1) * R] > 0.5), lg, -jnp.inf))
    lg = jnp.concatenate(tiles, axis=1)
    e = jnp.exp(lg - jnp.max(lg, axis=1, keepdims=True))
    p = (e * (1.0 / jnp.sum(e, axis=1, keepdims=True))).astype(jnp.bfloat16)
    acc = jnp.zeros((R, ATT_HEAD_DIM), jnp.float32)
    for j, vp in enumerate(v_all):
        acc = acc + _dot(p[:, j * HP:(j + 1) * HP], vp)
    o_ref[0] = acc[:ATT_HEADS]


def _dsa_sample_pallas(q, k, v, qi, ki, wi, cache_k, cache_v, cache_idx_k, page_table, l):
    B, Q = q.shape[:2]
    assert Q == 1 and PAGE_SIZE == LANES
    n_pages = page_table.shape[1]
    n_pool = cache_k.shape[1]
    k_top = min(TOPK_MAX, (n_pages * PAGE_SIZE + Q) // 4)
    t_pad = (n_pages + 1) * PAGE_SIZE
    HP = ATT_HEADS * PAGE_SIZE
    head_spec = pl.BlockSpec((1, SUBLANES, ATT_HEAD_DIM), lambda b, pt: (b, 0, 0))
    page_spec = lambda rows, w, j: pl.BlockSpec((1, 1, rows, w), lambda b, pt: (l, pt[b, j], 0, 0))
    grid_spec = pltpu.PrefetchScalarGridSpec(
        num_scalar_prefetch=1,
        grid=(B,),
        in_specs=[head_spec, head_spec, head_spec,
                  pl.BlockSpec((1, IDX_HEADS, IDX_DIM), lambda b, pt: (b, 0, 0)),
                  pl.BlockSpec((1, 1, IDX_DIM), lambda b, pt: (b, 0, 0)),
                  pl.BlockSpec((1, IDX_HEADS, 1), lambda b, pt: (b, 0, 0))]
                 + [page_spec(PAGE_SIZE, IDX_DIM, j) for j in range(n_pages)]
                 + [page_spec(HP, ATT_HEAD_DIM, j) for j in range(n_pages)] * 2,
        out_specs=pl.BlockSpec((1, ATT_HEADS, ATT_HEAD_DIM), lambda b, pt: (b, 0, 0)),
        scratch_shapes=[pltpu.VMEM((SUBLANES, t_pad), jnp.int32)])
    heads = lambda t: jnp.pad(t.reshape(B, ATT_HEADS, ATT_HEAD_DIM), [(0, 0), (0, SUBLANES - ATT_HEADS), (0, 0)])
    pool = lambda c: c.reshape(c.shape[0], n_pool, HP, ATT_HEAD_DIM)
    out = pl.pallas_call(
        functools.partial(_dsa_sample_body, n_pages=n_pages, k_top=k_top),
        grid_spec=grid_spec,
        out_shape=jax.ShapeDtypeStruct((B, ATT_HEADS, ATT_HEAD_DIM), jnp.float32),
        compiler_params=pltpu.CompilerParams(dimension_semantics=("arbitrary",),
                                             vmem_limit_bytes=VMEM_LIMIT_BYTES),
        name="dsa_sample",
    )(page_table, heads(q), heads(k), heads(v),
      qi.reshape(B, IDX_HEADS, IDX_DIM), ki, wi.reshape(B, IDX_HEADS, 1),
      *([cache_idx_k] * n_pages), *([pool(cache_k)] * n_pages), *([pool(cache_v)] * n_pages))
    return out.reshape(B, 1, ATT_HEADS, ATT_HEAD_DIM)


RWKV_GROUP = 8
RWKV_BB = 2


def _rwkv_scan_body(r_ref, d_ref, k_ref, v_ref, kk_ref, a_ref, s0_ref, y_ref, sT_ref, S_scr, Sb_scr):
    BB, tc = r_ref.shape[:2]
    P, N = S_scr.shape[1:3]
    G = RWKV_GROUP
    chains = [(b, p) for b in range(BB) for p in range(P)]
    c = pl.program_id(1)

    @pl.when(c == 0)
    def _():
        S_scr[...] = s0_ref[...]
        Sb_scr[...] = s0_ref[...].astype(jnp.bfloat16)

    lo_half = lax.broadcasted_iota(jnp.int32, (1, LANES), 1) < N
    sub = lax.broadcasted_iota(jnp.int32, (G, N), 0)
    sub2 = lax.broadcasted_iota(jnp.int32, (G, LANES), 0)
    r_mask = ((sub2 == 0) & lo_half) | ((sub2 == 1) & jnp.logical_not(lo_half))
    ii = lax.broadcasted_iota(jnp.int32, (N, LANES), 0)
    jj = lax.broadcasted_iota(jnp.int32, (N, LANES), 1)
    diagA, diagB = jj == ii, jj == ii + N

    def group(g, carry):
        t0 = pl.multiple_of(g * G, G)
        ld = lambda ref, ch: ref[ch[0], pl.ds(t0, G), ch[1] * LANES:(ch[1] + 1) * LANES]
        kk8 = [ld(kk_ref, ch) for ch in chains]
        beta8 = [kk8[i] * ld(a_ref, ch) for i, ch in enumerate(chains)]
        d8 = [ld(d_ref, ch) for ch in chains]
        k8 = [ld(k_ref, ch) for ch in chains]
        r8 = [ld(r_ref, ch) for ch in chains]
        v8 = [ld(v_ref, ch) for ch in chains]
        kkA = [jnp.where(lo_half, x, 0.0).astype(jnp.bfloat16) for x in kk8]
        kkB = [jnp.where(lo_half, 0.0, x).astype(jnp.bfloat16) for x in kk8]
        yA = [jnp.zeros((G, N), jnp.float32) for _ in chains]
        yB = [jnp.zeros((G, N), jnp.float32) for _ in chains]

        def y_step(t, i):
            b, p = chains[i]
            r2 = jnp.where(r_mask, r8[i][t:t + 1], 0.0).astype(jnp.bfloat16)
            y2 = _dot_nt(r2, Sb_scr[b, p])
            yA[i] = jnp.where(sub == t, jnp.broadcast_to(y2[0:1], (G, N)), yA[i])
            yB[i] = jnp.where(sub == t, jnp.broadcast_to(y2[1:2], (G, N)), yB[i])

        for t in range(G):
            row = lambda x: x[t:t + 1]
            sa_b = []
            for i, (b, p) in enumerate(chains):
                w1 = jnp.concatenate([jnp.broadcast_to(row(kkA[i]), (N, LANES)),
                                      jnp.broadcast_to(row(kkB[i]), (N, LANES))], axis=0)
                sa_b.append(_dot_nt(Sb_scr[b, p], w1))
            if t > 0:
                for i in range(len(chains)):
                    y_step(t - 1, i)
            for i, (b, p) in enumerate(chains):
                vA = jnp.sum(jnp.where(diagA, row(v8[i]), 0.0), axis=1, keepdims=True)
                vB = jnp.sum(jnp.where(diagB, row(v8[i]), 0.0), axis=1, keepdims=True)
                v_b = jnp.where(lo_half, vA, vB)
                S = S_scr[b, p] * row(d8[i]) - sa_b[i] * row(beta8[i]) + v_b * row(k8[i])
                S_scr[b, p] = S
                Sb_scr[b, p] = S.astype(jnp.bfloat16)
        for i, (b, p) in enumerate(chains):
            y_step(G - 1, i)
            y_ref[b, pl.ds(t0, G), p * LANES:(p + 1) * LANES] = jnp.concatenate([yA[i], yB[i]], axis=1)
        return carry

    lax.fori_loop(0, tc // G, group, 0)

    @pl.when(c == pl.num_programs(1) - 1)
    def _():
        sT_ref[...] = S_scr[...]


def _rwkv_scan(r, d, k, v, kk, a, S0):
    B, T, W = r.shape
    H, N = S0.shape[1:3]
    P = H // 2
    G = RWKV_GROUP
    bb = RWKV_BB if B % RWKV_BB == 0 else 1
    t_pad = pl.cdiv(T, G) * G
    tc = t_pad
    for cand in range(min(t_pad, 512), G - 1, -1):
        if t_pad % cand == 0 and cand % G == 0:
            tc = cand
            break
    pad = lambda x, c=0.0: jnp.pad(x, [(0, 0), (0, t_pad - T), (0, 0)], constant_values=c) if t_pad != T else x
    S2 = S0.reshape(B, P, 2, N, N).transpose(0, 1, 3, 2, 4).reshape(B, P, N, 2 * N)
    seq_spec = pl.BlockSpec((bb, tc, W), lambda b, c: (b, c, 0))
    st_spec = pl.BlockSpec((bb, P, N, 2 * N), lambda b, c: (b, 0, 0, 0))
    y, S = pl.pallas_call(
        _rwkv_scan_body,
        grid=(B // bb, t_pad // tc),
        in_specs=[seq_spec] * 6 + [st_spec],
        out_specs=[seq_spec, st_spec],
        out_shape=[jax.ShapeDtypeStruct((B, t_pad, W), jnp.float32),
                   jax.ShapeDtypeStruct((B, P, N, 2 * N), jnp.float32)],
        scratch_shapes=[pltpu.VMEM((bb, P, N, 2 * N), jnp.float32), pltpu.VMEM((bb, P, N, 2 * N), jnp.bfloat16)],
        compiler_params=pltpu.CompilerParams(dimension_semantics=("arbitrary", "arbitrary"),
                                             vmem_limit_bytes=VMEM_LIMIT_BYTES),
        name="rwkv_scan",
    )(pad(r), pad(d, 1.0), pad(k), pad(v), pad(kk), pad(a), S2)
    S = S.reshape(B, P, N, 2, N).transpose(0, 1, 3, 2, 4).reshape(B, H, N, N)
    return y[:, :T], S


def _bf(x):
    return x.astype(jnp.bfloat16)


def _dot(a, b):
    return jnp.dot(_bf(a), _bf(b), preferred_element_type=jnp.float32)


def _dot_nt(a, b):
    return lax.dot_general(_bf(a), _bf(b), (((1,), (1,)), ((), ())), preferred_element_type=jnp.float32)


def _dot_f32(a, b):
    return jnp.dot(a, b, precision=lax.Precision.HIGHEST, preferred_element_type=jnp.float32)


GDN_BB = 2


def _gdn_chunk_body(q_ref, k_ref, v_ref, g_ref, beta_ref, grow_ref, s0_ref, o_ref, sT_ref, S_scr):
    BB, L = q_ref.shape[:2]
    H, D = S_scr.shape[1:3]
    chains = [(b, h) for b in range(BB) for h in range(H)]
    c = pl.program_id(1)

    @pl.when(c == 0)
    def _():
        S_scr[...] = s0_ref[...]

    ii = lax.broadcasted_iota(jnp.int32, (L, L), 0)
    jj = lax.broadcasted_iota(jnp.int32, (L, L), 1)
    tri, stri = jj <= ii, jj < ii
    eye = jnp.where(ii == jj, 1.0, 0.0)
    every = lambda f: [f(i) for i in range(len(chains))]
    hs = lambda h: slice(h * D, (h + 1) * D)
    q = [q_ref[b, :, hs(h)] for b, h in chains]
    k = [k_ref[b, :, hs(h)] for b, h in chains]
    g_col = [g_ref[b, :, h:h + 1] for b, h in chains]
    b_col = [beta_ref[b, :, h:h + 1] for b, h in chains]
    g_row = [grow_ref[b, 0, h:h + 1, :] for b, h in chains]
    gc_col = every(lambda i: jnp.sum(jnp.where(tri, g_row[i], 0.0), axis=1, keepdims=True))
    gc_row = every(lambda i: jnp.sum(jnp.where(ii <= jj, g_col[i], 0.0), axis=0, keepdims=True))
    dec = every(lambda i: jnp.exp(jnp.where(tri, gc_col[i] - gc_row[i], -jnp.inf)))
    kb = every(lambda i: k[i] * b_col[i])
    egc = every(lambda i: jnp.exp(gc_col[i]))
    A = every(lambda i: jnp.where(stri, _dot_nt(kb[i], k[i]) * dec[i], 0.0))
    attn = every(lambda i: jnp.where(tri, _dot_nt(q[i], k[i]) * dec[i], 0.0))
    rhs = [jnp.concatenate([v_ref[b, :, hs(h)] * b_col[i], kb[i] * egc[i]], axis=1) for i, (b, h) in enumerate(chains)]
    inv = every(lambda i: eye - A[i])
    P = every(lambda i: _dot_f32(A[i], A[i]))
    n_lvl = int(math.log2(L)) - 1
    for lvl in range(n_lvl):
        inv = every(lambda i: inv[i] + _dot_f32(inv[i], P[i]))
        if lvl + 1 < n_lvl:
            P = every(lambda i: _dot_f32(P[i], P[i]))
    sol = every(lambda i: _dot_f32(inv[i], rhs[i]))
    S = [S_scr[b, h] for b, h in chains]
    v_new = every(lambda i: sol[i][:, :D] - _dot(sol[i][:, D:], S[i]))
    o_state = every(lambda i: _dot(q[i] * egc[i], S[i]))
    g_last = every(lambda i: gc_col[i][L - 1:L, :])
    k_sc = every(lambda i: (k[i] * jnp.exp(g_last[i] - gc_col[i])).T)
    for i, (b, h) in enumerate(chains):
        o_ref[b, :, hs(h)] = o_state[i] + _dot(attn[i], v_new[i])
    for i, (b, h) in enumerate(chains):
        S_scr[b, h] = S[i] * jnp.exp(g_last[i]) + _dot(k_sc[i], v_new[i])

    @pl.when(c == pl.num_programs(1) - 1)
    def _():
        sT_ref[...] = S_scr[...]


def _gdn_chunks(q, k, v, g, beta, S0, L):
    B, T, W = q.shape
    H, D = S0.shape[1:3]
    n = T // L
    bb = GDN_BB if B % GDN_BB == 0 else 1
    g_row = jnp.swapaxes(g.reshape(B, n, L, H), 2, 3)
    seq = pl.BlockSpec((bb, L, W), lambda b, c: (b, c, 0))
    gate = pl.BlockSpec((bb, L, H), lambda b, c: (b, c, 0))
    st = pl.BlockSpec((bb, H, D, D), lambda b, c: (b, 0, 0, 0))
    return pl.pallas_call(
        _gdn_chunk_body,
        grid=(B // bb, n),
        in_specs=[seq, seq, seq, gate, gate, pl.BlockSpec((bb, 1, H, L), lambda b, c: (b, c, 0, 0)), st],
        out_specs=[seq, st],
        out_shape=[jax.ShapeDtypeStruct((B, T, W), jnp.float32),
                   jax.ShapeDtypeStruct((B, H, D, D), jnp.float32)],
        scratch_shapes=[pltpu.VMEM((bb, H, D, D), jnp.float32)],
        compiler_params=pltpu.CompilerParams(dimension_semantics=("arbitrary", "arbitrary"),
                                             vmem_limit_bytes=VMEM_LIMIT_BYTES),
        name="gdn_chunks",
    )(q, k, v, g, beta, g_row, S0)


def _layer_norm(x, g, b, eps=1e-5):
    xf = x.astype(jnp.float32)
    mu = jnp.mean(xf, -1, keepdims=True)
    var = jnp.mean(jnp.square(xf - mu), -1, keepdims=True)
    return ((xf - mu) * lax.rsqrt(var + eps) * g + b).astype(x.dtype)


def _rms_norm(x, g, eps=1e-6):
    xf = x.astype(jnp.float32)
    return (xf * lax.rsqrt(jnp.mean(xf * xf, -1, keepdims=True) + eps) * g).astype(x.dtype)


def _l2norm(x, eps=1e-6):
    xf = x.astype(jnp.float32)
    return xf * lax.rsqrt(jnp.sum(xf * xf, -1, keepdims=True) + eps)


def _proj_in(x, w_in):
    B, T, _ = x.shape
    sizes = [W_MIX,
             3 * W_MIX, GDN_HEADS, GDN_HEADS, W_MIX,
             W_MIX, W_MIX, W_MIX, IDX_HEADS * IDX_DIM, IDX_DIM, IDX_HEADS,
             3 * W_MIX, MLSTM_HEADS, MLSTM_HEADS, W_MIX,
             N_BRANCH * D_MODEL]
    offs = np.concatenate([[0], np.cumsum(sizes)]).tolist()
    xb = x.reshape(B * T, D_MODEL).astype(jnp.bfloat16)
    outs = [None] * len(sizes)

    def grouped(idx):
        w = jnp.concatenate([w_in[:, offs[i]:offs[i + 1]] for i in idx], axis=1).astype(jnp.bfloat16)
        for i, y in zip(idx, _mm_group(xb, w, [sizes[i] for i in idx])):
            outs[i] = y.reshape(B, T, sizes[i])

    grouped([i for i, s in enumerate(sizes) if s < LANES])
    grouped([i for i, s in enumerate(sizes) if s == W_MIX])
    for i, s in enumerate(sizes):
        if s > W_MIX:
            outs[i] = _mm(xb, w_in[:, offs[i]:offs[i + 1]]).reshape(B, T, s)
    return outs


def _short_conv(u, buf, w):
    T = u.shape[1]
    ext = jnp.concatenate([buf.astype(jnp.float32), u.astype(jnp.float32)], axis=1)
    y = sum(ext[:, j:j + T] * w[j] for j in range(CONV_W))
    return jax.nn.silu(y), ext[:, ext.shape[1] - (CONV_W - 1):]


def _rwkv7(u, S0, shift0, mu, w_rkv, w0, w1, w2, a0, a1, a2, g1, g2, k_k, k_a, r_k, lnx_g, lnx_b):
    B, T, _ = u.shape
    uf = u.astype(jnp.float32)
    prev = jnp.concatenate([shift0[:, None].astype(jnp.float32), uf[:, :-1]], axis=1)
    xx = prev - uf
    xr, xw, xk, xv, xa, xg = (uf + xx * mu[j] for j in range(6))
    r = xr @ w_rkv[0]
    k = xk @ w_rkv[1]
    v = xv @ w_rkv[2]
    w = -jax.nn.softplus(-(w0 + jnp.tanh(xw @ w1) @ w2)) - 0.5
    a = jax.nn.sigmoid(a0 + (xa @ a1) @ a2)
    g = jax.nn.sigmoid(xg @ g1) @ g2
    hd = lambda t: t.reshape(B, T, RWKV_HEADS, RWKV_HEAD_DIM)
    kk = _l2norm(hd(k * k_k))
    k = hd(k * (1.0 + (a - 1.0) * k_a))
    r, v, a = hd(r), hd(v), hd(a)
    decay = jnp.exp(-jnp.exp(hd(w)))

    flat = lambda t: t.reshape(B, T, W_MIX)
    y, S = _rwkv_scan(flat(r), flat(decay), flat(k), flat(v), flat(kk), flat(a), S0.astype(jnp.float32))
    y = hd(y)
    mean = jnp.mean(y, -1, keepdims=True)
    var = jnp.mean(jnp.square(y - mean), -1, keepdims=True)
    y = ((y - mean) * lax.rsqrt(var + RWKV_GN_EPS)).reshape(B, T, W_MIX) * lnx_g + lnx_b
    bonus = jnp.sum(r * k * r_k, -1, keepdims=True) * v
    y = (y + bonus.reshape(B, T, W_MIX)) * g
    return y.astype(u.dtype), S, u[:, -1]


def _gdn_chunk(S, xs):
    q, k, v, g, beta = xs
    L = q.shape[2]
    tri = jnp.tril(jnp.ones((L, L), bool))
    stri = jnp.tril(jnp.ones((L, L), bool), -1)
    gc = jnp.cumsum(g, axis=-1)
    dec = jnp.exp(jnp.where(tri, gc[..., :, None] - gc[..., None, :], -jnp.inf))
    kb = k * beta[..., None]
    A = jnp.where(stri, jnp.einsum('bhid,bhjd->bhij', kb, k) * dec, 0.0)
    rhs = jnp.concatenate([v * beta[..., None], kb * jnp.exp(gc)[..., None]], axis=-1)
    sol = lax.linalg.triangular_solve(A + jnp.eye(L, dtype=A.dtype), rhs, left_side=True,
                                      lower=True, unit_diagonal=True)
    u_, w_ = sol[..., :GDN_HEAD_DIM], sol[..., GDN_HEAD_DIM:]
    v_new = u_ - jnp.einsum('bhld,bhde->bhle', w_, S)
    attn = jnp.where(tri, jnp.einsum('bhid,bhjd->bhij', q, k) * dec, 0.0)
    o = (jnp.einsum('bhld,bhde->bhle', q * jnp.exp(gc)[..., None], S)
         + jnp.einsum('bhij,bhje->bhie', attn, v_new))
    g_last = gc[..., -1]
    S = (S * jnp.exp(g_last)[..., None, None]
         + jnp.einsum('bhld,bhle->bhde', k * jnp.exp(g_last[..., None] - gc)[..., None], v_new))
    return S, o


def _gdn(qkv, a_pre, b_pre, z, S0, buf0, conv_w, A_log, dt_bias, norm_g, is_prompt):
    B, T, _ = qkv.shape
    y, new_buf = _short_conv(qkv, buf0, conv_w)
    q, k, v = jnp.split(y, 3, axis=-1)
    hd4 = lambda t: t.reshape(B, T, GDN_HEADS, GDN_HEAD_DIM)
    q = _l2norm(hd4(q)) * GDN_HEAD_DIM ** -0.5
    k = _l2norm(hd4(k))
    g = -jnp.exp(A_log.astype(jnp.float32)) * jax.nn.softplus(a_pre.astype(jnp.float32) + dt_bias)
    beta = jax.nn.sigmoid(b_pre.astype(jnp.float32))
    if is_prompt:
        pad = (-N_META) % GDN_CHUNK
        fp = lambda t: jnp.pad(t.reshape(B, T, -1), [(0, 0), (pad, 0), (0, 0)])
        o, S = _gdn_chunks(fp(q), fp(k), fp(v), fp(g), fp(beta), S0.astype(jnp.float32), GDN_CHUNK)
        o = hd4(o[:, pad:])
    else:
        hd = lambda t: t.transpose(0, 2, 1, 3)
        S, o = _gdn_chunk(S0.astype(jnp.float32), (hd(q), hd(k), hd(hd4(v)), g.transpose(0, 2, 1),
                                                    beta.transpose(0, 2, 1)))
        o = o.transpose(0, 2, 1, 3)
    o = _rms_norm(o, norm_g)
    o = o * jax.nn.silu(z.astype(jnp.float32).reshape(B, T, GDN_HEADS, GDN_HEAD_DIM))
    return o.reshape(B, T, W_MIX).astype(qkv.dtype), S, new_buf


def _dsa_inputs(c_q, c_k, c_v, c_qi, c_ki, c_wi, ln_g, ln_b):
    B, T, _ = c_q.shape
    hd = lambda t: t.reshape(B, T, ATT_HEADS, ATT_HEAD_DIM)
    qi = c_qi.reshape(B, T, IDX_HEADS, IDX_DIM)
    ki = _layer_norm(c_ki, ln_g, ln_b)
    wi = c_wi * IDX_HEADS ** -0.5
    return hd(c_q), hd(c_k), hd(c_v), qi, ki, wi


def _mlstm_chunk(state, xs):
    C, n, m = state
    q, k, v, li, lf = xs
    L = q.shape[2]
    tri = jnp.tril(jnp.ones((L, L), bool))
    b = jnp.cumsum(lf, axis=-1)
    a = li - b
    m_t = b + jnp.maximum(m[..., None], lax.cummax(a, axis=2))
    dmat = jnp.exp(jnp.where(tri, b[..., :, None] + a[..., None, :] - m_t[..., :, None], -jnp.inf))
    inter = jnp.exp(b + m[..., None] - m_t)
    s = jnp.einsum('bhtd,bhsd->bhts', q, k) * dmat
    num = inter[..., None] * jnp.einsum('bhtd,bhde->bhte', q, C) + jnp.einsum('bhts,bhse->bhte', s, v)
    den = inter * jnp.einsum('bhtd,bhd->bht', q, n) + jnp.sum(s, axis=-1)
    h = num / jnp.maximum(jnp.abs(den), jnp.exp(-m_t))[..., None]
    m_new = m_t[..., -1]
    wend = jnp.exp(b[..., -1:] + a - m_new[..., None])
    carry = jnp.exp(b[..., -1] + m - m_new)
    C = carry[..., None, None] * C + jnp.einsum('bhs,bhsd,bhse->bhde', wend, k, v)
    n = carry[..., None] * n + jnp.einsum('bhs,bhsd->bhd', wend, k)
    return (C, n, m_new), h


MLSTM_BB = 2


def _mlstm_chunk_body(qkv_ref, li_ref, lf_ref, lirow_ref, lfrow_ref, c0_ref, n0_ref, m0_ref,
                      h_ref, cT_ref, nT_ref, mT_ref, C_scr, n_scr, m_scr):
    BB, L = qkv_ref.shape[:2]
    H, D = C_scr.shape[1:3]
    W = H * D
    chains = [(b, h) for b in range(BB) for h in range(H)]
    c = pl.program_id(1)

    @pl.when(c == 0)
    def _():
        C_scr[...] = c0_ref[...]
        n_scr[...] = n0_ref[...]
        m_scr[...] = m0_ref[...]

    ii = lax.broadcasted_iota(jnp.int32, (L, L), 0)
    jj = lax.broadcasted_iota(jnp.int32, (L, L), 1)
    tri = jj <= ii
    every = lambda f: [f(i) for i in range(len(chains))]
    col_of = lambda ref: [ref[b, :, h:h + 1] for b, h in chains]
    row_of = lambda ref: [ref[b, 0, h:h + 1, :] for b, h in chains]
    q = [qkv_ref[b, :, h * D:(h + 1) * D] for b, h in chains]
    k = [qkv_ref[b, :, W + h * D:W + (h + 1) * D] * D ** -0.5 for b, h in chains]
    v = [qkv_ref[b, :, 2 * W + h * D:2 * W + (h + 1) * D] for b, h in chains]
    li_c, lf_c, li_r, lf_r = col_of(li_ref), col_of(lf_ref), row_of(lirow_ref), row_of(lfrow_ref)
    C = [C_scr[b, h] for b, h in chains]
    n = [n_scr[b, h] for b, h in chains]
    m = [m_scr[b, h][:, :1] for b, h in chains]
    b_c = every(lambda i: jnp.sum(jnp.where(tri, lf_r[i], 0.0), axis=1, keepdims=True))
    b_r = every(lambda i: jnp.sum(jnp.where(ii <= jj, lf_c[i], 0.0), axis=0, keepdims=True))
    a_c = every(lambda i: li_c[i] - b_c[i])
    a_r = every(lambda i: li_r[i] - b_r[i])
    cmax = every(lambda i: jnp.max(jnp.where(tri, a_r[i], -jnp.inf), axis=1, keepdims=True))
    m_t = every(lambda i: b_c[i] + jnp.maximum(m[i], cmax[i]))
    dmat = every(lambda i: jnp.exp(jnp.where(tri, b_c[i] + a_r[i] - m_t[i], -jnp.inf)))
    inter = every(lambda i: jnp.exp(b_c[i] + m[i] - m_t[i]))
    s = every(lambda i: _dot_nt(q[i], k[i]) * dmat[i])
    qC = every(lambda i: _dot(q[i], C[i]))
    sv = every(lambda i: _dot(s[i], v[i]))
    qn = every(lambda i: jnp.sum(_round_bf16(q[i]) * _round_bf16(n[i]), axis=1, keepdims=True))
    den = every(lambda i: inter[i] * qn[i] + jnp.sum(s[i], axis=1, keepdims=True))
    m_new = every(lambda i: m_t[i][L - 1:L, :])
    b_last = every(lambda i: b_c[i][L - 1:L, :])
    wend = every(lambda i: jnp.exp(b_last[i] + a_c[i] - m_new[i]))
    carry = every(lambda i: jnp.exp(b_last[i] + m[i] - m_new[i]))
    wk = every(lambda i: wend[i] * k[i])
    for i, (b, h) in enumerate(chains):
        num = inter[i] * qC[i] + sv[i]
        h_ref[b, :, h * D:(h + 1) * D] = num / jnp.maximum(jnp.abs(den[i]), jnp.exp(-m_t[i]))
    for i, (b, h) in enumerate(chains):
        C_scr[b, h] = carry[i] * C[i] + _dot(wk[i].T, v[i])
        n_scr[b, h] = carry[i] * n[i] + jnp.sum(_round_bf16(wend[i]) * _round_bf16(k[i]), axis=0, keepdims=True)
        m_scr[b, h] = jnp.broadcast_to(m_new[i], (1, D))

    @pl.when(c == pl.num_programs(1) - 1)
    def _():
        cT_ref[...] = C_scr[...]
        nT_ref[...] = n_scr[...]
        mT_ref[...] = m_scr[...]


def _mlstm_chunks(qkv, li, lf, C0, n0, m0, L):
    B, T, W3 = qkv.shape
    H, D = C0.shape[1:3]
    n_c = T // L
    bb = MLSTM_BB if B % MLSTM_BB == 0 else 1
    rows = lambda t: jnp.swapaxes(t.reshape(B, n_c, L, H), 2, 3)
    gate = pl.BlockSpec((bb, L, H), lambda b, c: (b, c, 0))
    gate_row = pl.BlockSpec((bb, 1, H, L), lambda b, c: (b, c, 0, 0))
    st_c = pl.BlockSpec((bb, H, D, D), lambda b, c: (b, 0, 0, 0))
    st_v = pl.BlockSpec((bb, H, 1, D), lambda b, c: (b, 0, 0, 0))
    vec = jax.ShapeDtypeStruct((B, H, 1, D), jnp.float32)
    h, C, n, m = pl.pallas_call(
        _mlstm_chunk_body,
        grid=(B // bb, n_c),
        in_specs=[pl.BlockSpec((bb, L, W3), lambda b, c: (b, c, 0)), gate, gate, gate_row, gate_row,
                  st_c, st_v, st_v],
        out_specs=[pl.BlockSpec((bb, L, H * D), lambda b, c: (b, c, 0)), st_c, st_v, st_v],
        out_shape=[jax.ShapeDtypeStruct((B, T, H * D), jnp.float32),
                   jax.ShapeDtypeStruct((B, H, D, D), jnp.float32), vec, vec],
        scratch_shapes=[pltpu.VMEM((bb, H, D, D), jnp.float32), pltpu.VMEM((bb, H, 1, D), jnp.float32),
                        pltpu.VMEM((bb, H, 1, D), jnp.float32)],
        compiler_params=pltpu.CompilerParams(dimension_semantics=("arbitrary", "arbitrary"),
                                             vmem_limit_bytes=VMEM_LIMIT_BYTES),
        name="mlstm_chunks",
    )(qkv, li, lf, rows(li), rows(lf), C0, n0[:, :, None, :], jnp.broadcast_to(m0[:, :, None, None], (B, H, 1, D)))
    return h, C, n[:, :, 0, :], m[:, :, 0, 0]


MLSTM_PAD_LOG_I = -1e30


def _mlstm(qkv, i_pre, f_pre, o_pre, C0, n0, m0, b_i, b_f, norm_g, is_prompt):
    B, T, _ = qkv.shape
    if is_prompt:
        pad = (-N_META) % MLSTM_CHUNK
        fp = lambda t, c=0.0: jnp.pad(t, [(0, 0), (pad, 0), (0, 0)], constant_values=c)
        li = fp(i_pre.astype(jnp.float32) + b_i, MLSTM_PAD_LOG_I)
        lf = fp(jax.nn.log_sigmoid(f_pre.astype(jnp.float32) + b_f))
        h, C, n, m = _mlstm_chunks(fp(qkv.astype(jnp.float32)), li, lf, C0.astype(jnp.float32),
                                   n0.astype(jnp.float32), m0.astype(jnp.float32), MLSTM_CHUNK)
        h = _rms_norm(h[:, pad:].reshape(B, T, MLSTM_HEADS, MLSTM_HEAD_DIM), norm_g).reshape(B, T, W_MIX)
        h = jax.nn.sigmoid(o_pre.astype(jnp.float32)) * h
        return h.astype(qkv.dtype), C, n, m
    q, k, v = jnp.split(qkv.astype(jnp.float32), 3, axis=-1)
    hd = lambda t: t.reshape(B, T, MLSTM_HEADS, MLSTM_HEAD_DIM).transpose(0, 2, 1, 3)
    q, k, v = hd(q), hd(k) * MLSTM_HEAD_DIM ** -0.5, hd(v)
    li = (i_pre.astype(jnp.float32) + b_i).transpose(0, 2, 1)
    lf = jax.nn.log_sigmoid(f_pre.astype(jnp.float32) + b_f).transpose(0, 2, 1)
    st0 = (C0.astype(jnp.float32), n0.astype(jnp.float32), m0.astype(jnp.float32))
    (C, n, m), h = _mlstm_chunk(st0, (q, k, v, li, lf))
    h = _rms_norm(h.transpose(0, 2, 1, 3), norm_g).reshape(B, T, W_MIX)
    h = jax.nn.sigmoid(o_pre.astype(jnp.float32)) * h
    return h.astype(qkv.dtype), C, n, m


def _moe_experts_body(blk_exp_ref, n_used_ref, x_ref, wg_ref, wu_ref, wd_ref, o_ref, wg_bf, wu_bf, wd_bf):
    i = pl.program_id(0)
    new_expert = (i == 0) | (blk_exp_ref[i] != blk_exp_ref[jnp.maximum(i - 1, 0)])

    @pl.when(new_expert)
    def _():
        wg_bf[...] = wg_ref[0, 0].astype(jnp.bfloat16)
        wu_bf[...] = wu_ref[0, 0].astype(jnp.bfloat16)
        wd_bf[...] = wd_ref[0, 0].astype(jnp.bfloat16)

    @pl.when(i < n_used_ref[0])
    def _():
        xb = x_ref[...].astype(jnp.bfloat16)
        hg = jnp.dot(xb, wg_bf[...], preferred_element_type=jnp.float32)
        hu = jnp.dot(xb, wu_bf[...], preferred_element_type=jnp.float32)
        act = (hg * jax.nn.sigmoid(hg)) * hu
        o_ref[...] = jnp.dot(act.astype(jnp.bfloat16), wd_bf[...],
                             preferred_element_type=jnp.float32).astype(o_ref.dtype)

    @pl.when(i >= n_used_ref[0])
    def _():
        o_ref[...] = jnp.zeros_like(o_ref)


def _moe_experts(xb, blk_exp, n_used, w_g, w_u, w_d, l):
    R, D = xb.shape
    F = w_g.shape[-1]
    n_blocks = R // MOE_BLOCK
    grid_spec = pltpu.PrefetchScalarGridSpec(
        num_scalar_prefetch=2,
        grid=(n_blocks,),
        in_specs=[pl.BlockSpec((MOE_BLOCK, D), lambda i, be, nu: (i, 0)),
                  pl.BlockSpec((1, 1, D, F), lambda i, be, nu: (l, be[i], 0, 0)),
                  pl.BlockSpec((1, 1, D, F), lambda i, be, nu: (l, be[i], 0, 0)),
                  pl.BlockSpec((1, 1, F, D), lambda i, be, nu: (l, be[i], 0, 0))],
        out_specs=pl.BlockSpec((MOE_BLOCK, D), lambda i, be, nu: (i, 0)),
        scratch_shapes=[pltpu.VMEM((D, F), jnp.bfloat16), pltpu.VMEM((D, F), jnp.bfloat16),
                        pltpu.VMEM((F, D), jnp.bfloat16)])
    return pl.pallas_call(
        _moe_experts_body,
        grid_spec=grid_spec,
        out_shape=jax.ShapeDtypeStruct((R, D), jnp.bfloat16),
        compiler_params=pltpu.CompilerParams(dimension_semantics=("arbitrary",),
                                             vmem_limit_bytes=VMEM_LIMIT_BYTES),
        name="moe_experts",
    )(blk_exp, n_used, xb, w_g, w_u, w_d)


def _expert_dispatch(xt, eid, gate, w_g, w_u, w_d, l):
    N = xt.shape[0]
    A = N * TOP_K_INNER
    e_flat = eid.reshape(-1)
    tok = jnp.arange(A, dtype=jnp.int32) // TOP_K_INNER
    onehot = (e_flat[:, None] == jnp.arange(N_EXPERTS, dtype=e_flat.dtype)[None, :]).astype(jnp.int32)
    running = jnp.cumsum(onehot, axis=0)
    counts = running[-1]
    padded = (counts + MOE_BLOCK - 1) // MOE_BLOCK * MOE_BLOCK
    pad_end = jnp.cumsum(padded)
    pad_start = pad_end - padded
    slot = jnp.sum(onehot * (pad_start[None, :] + running - 1), axis=1)
    n_blocks = -(-A // MOE_BLOCK) + N_EXPERTS
    slot_tok = jnp.full((n_blocks * MOE_BLOCK,), N, jnp.int32).at[slot].set(tok)
    blk_exp = jnp.minimum(jnp.searchsorted(pad_end, jnp.arange(n_blocks) * MOE_BLOCK, side='right'), N_EXPERTS - 1)
    x_pad = jnp.concatenate([xt, jnp.zeros((1, xt.shape[1]), xt.dtype)], axis=0).astype(jnp.bfloat16)
    n_used = (pad_end[-1:] // MOE_BLOCK).astype(jnp.int32)
    yb = _moe_experts(x_pad[slot_tok], blk_exp.astype(jnp.int32), n_used, w_g, w_u, w_d, l)
    slot = slot.reshape(N, TOP_K_INNER)
    return sum(yb[slot[:, j]].astype(jnp.float32) * _round_bf16(gate[:, j:j + 1]) for j in range(TOP_K_INNER))


def _moe(x, w_group, w_expert, w_g, w_u, w_d, l):
    shp = x.shape
    xt = x.reshape(-1, D_MODEL)
    N = xt.shape[0]
    xf = xt.astype(jnp.float32)
    pg = jax.nn.softmax(xf @ w_group.astype(jnp.float32), axis=-1)
    p_top, g_sel = lax.top_k(pg, 1)
    le = (xf @ w_expert.astype(jnp.float32)).reshape(N, N_GROUPS, EXPERTS_PER_GROUP)
    le_g = le[:, N_GROUPS - 1]
    for grp in range(N_GROUPS - 2, -1, -1):
        le_g = jnp.where(g_sel == grp, le[:, grp], le_g)
    v2, j2 = lax.top_k(le_g, TOP_K_INNER)
    gate = jax.nn.softmax(v2, axis=-1) * p_top
    eid = g_sel * EXPERTS_PER_GROUP + j2
    return _expert_dispatch(xt, eid, gate, w_g, w_u, w_d, l).reshape(shp).astype(x.dtype)


MERGE_TN = 512


def _branch_merge_body(*refs):
    y_refs, g_refs, w_refs, o_ref = refs[:N_BRANCH], refs[N_BRANCH:2 * N_BRANCH], refs[2 * N_BRANCH:3 * N_BRANCH], refs[-1]
    acc = 0.0
    for y_ref, g_ref, w_ref in zip(y_refs, g_refs, w_refs):
        acc = acc + jax.nn.sigmoid(g_ref[...]) * _dot(y_ref[...], w_ref[0, 0])
    o_ref[...] = acc.astype(o_ref.dtype)


def _out_ln_body(m_ref, w_ref, x_ref, g_ref, b_ref, o_ref, *, alpha, eps):
    z = alpha * x_ref[...] + jnp.dot(m_ref[...], w_ref[...], preferred_element_type=jnp.float32)
    mu = jnp.mean(z, -1, keepdims=True)
    var = jnp.mean(jnp.square(z - mu), -1, keepdims=True)
    o_ref[...] = (z - mu) * lax.rsqrt(var + eps) * g_ref[...] + b_ref[...]


def _merge_out_ln(ys, gates, x, w_branch, w_out, ln_g, ln_b, l, alpha, tm=1024):
    B, T, D = x.shape
    M = B * T
    tm = min(tm, M)
    tn = MERGE_TN
    nj = D // tn
    row = lambda t: t.reshape(M, t.shape[-1])
    merged = pl.pallas_call(
        _branch_merge_body,
        grid=(pl.cdiv(M, tm), nj),
        in_specs=[pl.BlockSpec((tm, W_MIX), lambda i, j: (i, 0))] * N_BRANCH
                 + [pl.BlockSpec((tm, tn), lambda i, j, b=b: (i, b * nj + j)) for b in range(N_BRANCH)]
                 + [pl.BlockSpec((1, 1, W_MIX, tn), lambda i, j, b=b: (l, b, 0, j)) for b in range(N_BRANCH)],
        out_specs=pl.BlockSpec((tm, tn), lambda i, j: (i, j)),
        out_shape=jax.ShapeDtypeStruct((M, D), jnp.bfloat16),
        compiler_params=pltpu.CompilerParams(dimension_semantics=("arbitrary", "arbitrary"),
                                             vmem_limit_bytes=VMEM_LIMIT_BYTES),
        name="branch_merge",
    )(*[row(y) for y in ys], *([row(gates)] * N_BRANCH), *([w_branch] * N_BRANCH))
    tm2 = min(tm // 2, M)
    out = pl.pallas_call(
        functools.partial(_out_ln_body, alpha=alpha, eps=1e-5),
        grid=(pl.cdiv(M, tm2),),
        in_specs=[pl.BlockSpec((tm2, D), lambda i: (i, 0)),
                  pl.BlockSpec((D, D), lambda i: (0, 0)),
                  pl.BlockSpec((tm2, D), lambda i: (i, 0)),
                  pl.BlockSpec((1, D), lambda i: (0, 0)),
                  pl.BlockSpec((1, D), lambda i: (0, 0))],
        out_specs=pl.BlockSpec((tm2, D), lambda i: (i, 0)),
        out_shape=jax.ShapeDtypeStruct((M, D), jnp.float32),
        compiler_params=pltpu.CompilerParams(dimension_semantics=("arbitrary",),
                                             vmem_limit_bytes=VMEM_LIMIT_BYTES),
        name="out_ln",
    )(merged, w_out.astype(jnp.bfloat16), row(x), ln_g.reshape(1, D), ln_b.reshape(1, D))
    return out.reshape(B, T, D)


def _layer(x, W, l, st, dsa_fn, is_prompt, alpha):
    B, T, _ = x.shape
    (a_u, b_qkv, b_a, b_b, b_z, c_q, c_k, c_v, c_qi, c_ki, c_wi,
     d_qkv, d_i, d_f, d_o, gates) = _proj_in(x, W['w_in'][l])
    rwkv_S0, rwkv_shift0, gdn_S0, gdn_conv0, mC0, mn0, mm0 = st
    y_a, rwkv_S, rwkv_shift = _rwkv7(
        a_u, rwkv_S0, rwkv_shift0, W['rwkv_mu'][l], W['rwkv_w_rkv'][l], W['rwkv_w0'][l], W['rwkv_w1'][l],
        W['rwkv_w2'][l], W['rwkv_a0'][l], W['rwkv_a1'][l], W['rwkv_a2'][l], W['rwkv_g1'][l], W['rwkv_g2'][l],
        W['rwkv_k_k'][l], W['rwkv_k_a'][l], W['rwkv_r_k'][l], W['rwkv_lnx_g'][l], W['rwkv_lnx_b'][l])
    y_b, gdn_S, gdn_conv = _gdn(b_qkv, b_a, b_b, b_z, gdn_S0, gdn_conv0, W['gdn_conv_w'][l], W['gdn_A_log'][l],
                                W['gdn_dt_bias'][l], W['gdn_norm_g'][l], is_prompt)
    q, k, v, qi, ki, wi = _dsa_inputs(c_q, c_k, c_v, c_qi, c_ki, c_wi, W['idx_ln_g'][l], W['idx_ln_b'][l])
    y_c = dsa_fn(q, k, v, qi, ki, wi).reshape(B, T, W_MIX)
    y_d, mC, mn, mm = _mlstm(d_qkv, d_i, d_f, d_o, mC0, mn0, mm0, W['mlstm_b_i'][l], W['mlstm_b_f'][l],
                             W['mlstm_norm_g'][l], is_prompt)
    x = _merge_out_ln((y_a, y_b, y_c, y_d), gates, x, W['w_branch'], W['w_out'][l], W['ln1_g'][l], W['ln1_b'][l],
                      l, alpha)
    ffn = _moe(x, W['moe_w_group'][l], W['moe_w_expert'][l], W['moe_w_gate'], W['moe_w_up'], W['moe_w_down'], l)
    x = _layer_norm(alpha * x + ffn, W['ln2_g'][l], W['ln2_b'][l])
    return x, (k, v, ki, rwkv_S, rwkv_shift, gdn_S, gdn_conv, mC, mn, mm)


def kernel(x_prompt, x_sample, cache_k, cache_v, cache_idx_k, page_table, state_rwkv_S, state_rwkv_shift,
           state_gdn_S, state_gdn_conv, state_mlstm_C, state_mlstm_n, state_mlstm_m, meta_tokens, ln_in_g,
           ln_in_b, w_in, rwkv_mu, rwkv_w_rkv, rwkv_w0, rwkv_w1, rwkv_w2, rwkv_a0, rwkv_a1, rwkv_a2, rwkv_g1,
           rwkv_g2, rwkv_k_k, rwkv_k_a, rwkv_r_k, rwkv_lnx_g, rwkv_lnx_b, gdn_conv_w, gdn_A_log, gdn_dt_bias,
           gdn_norm_g, idx_ln_g, idx_ln_b, mlstm_b_i, mlstm_b_f, mlstm_norm_g, w_branch, w_out, ln1_g, ln1_b,
           ln2_g, ln2_b, moe_w_group, moe_w_expert, moe_w_gate, moe_w_up, moe_w_down):
    W = dict(w_in=w_in, rwkv_mu=rwkv_mu, rwkv_w_rkv=rwkv_w_rkv, rwkv_w0=rwkv_w0, rwkv_w1=rwkv_w1,
             rwkv_w2=rwkv_w2, rwkv_a0=rwkv_a0, rwkv_a1=rwkv_a1, rwkv_a2=rwkv_a2, rwkv_g1=rwkv_g1,
             rwkv_g2=rwkv_g2, rwkv_k_k=rwkv_k_k, rwkv_k_a=rwkv_k_a, rwkv_r_k=rwkv_r_k, rwkv_lnx_g=rwkv_lnx_g,
             rwkv_lnx_b=rwkv_lnx_b, gdn_conv_w=gdn_conv_w, gdn_A_log=gdn_A_log, gdn_dt_bias=gdn_dt_bias,
             gdn_norm_g=gdn_norm_g, idx_ln_g=idx_ln_g, idx_ln_b=idx_ln_b, mlstm_b_i=mlstm_b_i,
             mlstm_b_f=mlstm_b_f, mlstm_norm_g=mlstm_norm_g, w_branch=w_branch, w_out=w_out, ln1_g=ln1_g,
             ln1_b=ln1_b, ln2_g=ln2_g, ln2_b=ln2_b, moe_w_group=moe_w_group, moe_w_expert=moe_w_expert,
             moe_w_gate=moe_w_gate, moe_w_up=moe_w_up, moe_w_down=moe_w_down)
    f32 = jnp.float32
    depth = w_in.shape[0]
    alpha = (2 * depth) ** 0.25
    B = x_prompt.shape[0]
    meta = jnp.broadcast_to(meta_tokens.astype(x_prompt.dtype)[None], (B, N_META, D_MODEL))
    hp = _layer_norm(jnp.concatenate([meta, x_prompt], axis=1), ln_in_g, ln_in_b)
    hs = _layer_norm(x_sample, ln_in_g, ln_in_b)
    zero_state = (jnp.zeros((B, RWKV_HEADS, RWKV_HEAD_DIM, RWKV_HEAD_DIM), f32),
                  jnp.zeros((B, W_MIX), f32),
                  jnp.zeros((B, GDN_HEADS, GDN_HEAD_DIM, GDN_HEAD_DIM), f32),
                  jnp.zeros((B, CONV_W - 1, 3 * W_MIX), f32),
                  jnp.zeros((B, MLSTM_HEADS, MLSTM_HEAD_DIM, MLSTM_HEAD_DIM), f32),
                  jnp.zeros((B, MLSTM_HEADS, MLSTM_HEAD_DIM), f32),
                  jnp.zeros((B, MLSTM_HEADS), f32))
    new_p, new_s = [], []
    for l in range(depth):
        hp, sp = _layer(hp, W, l, zero_state, _dsa_prompt_pallas, True, alpha)
        dsa_s = functools.partial(_dsa_sample_pallas, cache_k=cache_k, cache_v=cache_v, cache_idx_k=cache_idx_k,
                                  page_table=page_table, l=l)
        st_s = (state_rwkv_S[l], state_rwkv_shift[l], state_gdn_S[l], state_gdn_conv[l],
                state_mlstm_C[l], state_mlstm_n[l], state_mlstm_m[l])
        hs, ss = _layer(hs, W, l, st_s, dsa_s, False, alpha)
        new_p.append(sp)
        new_s.append(ss)

    def stack(rows, j):
        return jnp.stack([r[j] for r in rows])

    k_p, v_p, ik_p, rS_p, rsh_p, gS_p, gc_p, mC_p, mn_p, mm_p = (stack(new_p, j) for j in range(10))
    k_s, v_s, ik_s, rS_s, rsh_s, gS_s, gc_s, mC_s, mn_s, mm_s = (stack(new_s, j) for j in range(10))
    y_prompt = hp[:, N_META:]
    y_sample = hs
    return (y_prompt, y_sample, k_p, v_p, ik_p, k_s, v_s, ik_s, rS_p, rS_s, rsh_p, rsh_s,
            gS_p, gS_s, gc_p, gc_s, mC_p, mC_s, mn_p, mn_s, mm_p, mm_s)
```

```python
import math, functools
import jax, jax.numpy as jnp
from jax import lax
import numpy as np
from jax.experimental import pallas as pl
from jax.experimental.pallas import tpu as pltpu

D_MODEL = 2048
N_META = 16
N_BRANCH = 4
W_MIX = D_MODEL // 4
RWKV_HEAD_DIM = 64
RWKV_HEADS = W_MIX // RWKV_HEAD_DIM
RWKV_GN_EPS = 64e-5
GDN_HEADS = 4
GDN_HEAD_DIM = W_MIX // GDN_HEADS
CONV_W = 4
GDN_CHUNK = 64
ATT_HEADS = 4
ATT_HEAD_DIM = W_MIX // ATT_HEADS
IDX_HEADS = 8
IDX_DIM = 64
TOPK_MAX = 256
MLSTM_HEADS = 4
MLSTM_HEAD_DIM = W_MIX // MLSTM_HEADS
MLSTM_CHUNK = 64
N_GROUPS = 4
EXPERTS_PER_GROUP = 8
N_EXPERTS = N_GROUPS * EXPERTS_PER_GROUP
TOP_K_INNER = 2
D_EXPERT = D_MODEL // 4
MOE_BLOCK = 128
PAGE_SIZE = 128

VMEM_LIMIT_BYTES = 56 * 1024 * 1024


def _mm_body(x_ref, w_ref, o_ref):
    o_ref[...] = jnp.dot(x_ref[...].astype(jnp.bfloat16), w_ref[...].astype(jnp.bfloat16),
                         preferred_element_type=jnp.float32)


def _mm(x, w, tm=1024, tn=1024):
    M, K = x.shape
    N = w.shape[1]
    tm = min(tm, M)
    tn = N if N <= 3 * W_MIX else tn
    return pl.pallas_call(
        _mm_body,
        grid=(pl.cdiv(N, tn), pl.cdiv(M, tm)),
        in_specs=[pl.BlockSpec((tm, K), lambda j, i: (i, 0)),
                  pl.BlockSpec((K, tn), lambda j, i: (0, j))],
        out_specs=pl.BlockSpec((tm, tn), lambda j, i: (i, j)),
        out_shape=jax.ShapeDtypeStruct((M, N), jnp.float32),
        compiler_params=pltpu.CompilerParams(dimension_semantics=("arbitrary", "arbitrary"),
                                             vmem_limit_bytes=VMEM_LIMIT_BYTES),
        name="proj_mm",
    )(x, w)


def _mm_group_body(x_ref, w_ref, *o_refs, offs):
    if w_ref.shape[1] <= LANES:
        acc = jnp.dot(x_ref[...], w_ref[...], preferred_element_type=jnp.float32)
        for o_ref, (a, b) in zip(o_refs, offs):
            o_ref[...] = acc[:, a:b]
        return
    for o_ref, (a, b) in zip(o_refs, offs):
        o_ref[...] = jnp.dot(x_ref[...], w_ref[:, a:b], preferred_element_type=jnp.float32)


def _mm_group(x, w, sizes, tm=512):
    M, K = x.shape
    tm = min(tm, M)
    ends = np.cumsum(sizes).tolist()
    offs = tuple(zip([0] + ends[:-1], ends))
    return pl.pallas_call(
        functools.partial(_mm_group_body, offs=offs),
        grid=(pl.cdiv(M, tm),),
        in_specs=[pl.BlockSpec((tm, K), lambda i: (i, 0)),
                  pl.BlockSpec((K, ends[-1]), lambda i: (0, 0))],
        out_specs=[pl.BlockSpec((tm, s), lambda i: (i, 0)) for s in sizes],
        out_shape=[jax.ShapeDtypeStruct((M, s), jnp.float32) for s in sizes],
        compiler_params=pltpu.CompilerParams(dimension_semantics=("arbitrary",),
                                             vmem_limit_bytes=VMEM_LIMIT_BYTES),
        name="proj_mm_group",
    )(x, w)


LANES = 128
INT_MIN = -2 ** 31
DSA_TQ = 128
DSA_SEGMENTS = 4


def _round_bf16(x):
    return x.astype(jnp.bfloat16).astype(jnp.float32)


def _topk_mask(isc, valid, key_ref, k_top):
    rows, t_pad = key_ref.shape
    bits = lax.bitcast_convert_type(isc, jnp.int32)
    key = bits ^ ((bits >> 31) & 0x7FFFFFFF)
    key_ref[...] = jnp.where(valid, key, INT_MIN)

    def count_ge(c):
        return jnp.sum(jnp.where(key_ref[...] >= c, 1.0, 0.0), axis=1, keepdims=True)

    kf = float(k_top)
    thr = jnp.where(count_ge(jnp.zeros((rows, 1), jnp.int32)) >= kf, 0, INT_MIN).astype(jnp.int32)

    def bit_step(j, thr):
        cand = thr | jnp.left_shift(jnp.int32(1), 30 - j)
        return jnp.where(count_ge(cand) >= kf, cand, thr)

    thr = lax.fori_loop(0, 31, bit_step, thr)

    key = key_ref[...]
    gt = key > thr
    eq = key == thr
    need = kf - jnp.sum(jnp.where(gt, 1.0, 0.0), axis=1, keepdims=True)
    upper = (lax.broadcasted_iota(jnp.int32, (LANES, LANES), 0)
             < lax.broadcasted_iota(jnp.int32, (LANES, LANES), 1)).astype(jnp.bfloat16)
    eqf = jnp.where(eq, 1.0, 0.0)
    off = jnp.zeros((rows, 1), jnp.float32)
    pre = []
    for c in range(t_pad // LANES):
        e_c = eqf[:, c * LANES:(c + 1) * LANES]
        pre.append(jnp.dot(e_c.astype(jnp.bfloat16), upper, preferred_element_type=jnp.float32) + off)
        off = off + jnp.sum(e_c, axis=1, keepdims=True)
    prefix = jnp.concatenate(pre, axis=1)
    return valid & (gt | (eq & (prefix < need)))


def _dsa_prompt_body(q_ref, qi_ref, wi_ref, kT_ref, v_ref, kiT_ref, o_ref, key_ref, *, k_top, first_block):
    tq, t_pad = key_ref.shape
    i = pl.program_id(1) + first_block
    row = lax.broadcasted_iota(jnp.int32, (tq, t_pad), 0) + i * tq
    col = lax.broadcasted_iota(jnp.int32, (tq, t_pad), 1)
    valid = col <= row

    wi = _round_bf16(wi_ref[0])
    isc = jnp.zeros((tq, t_pad), jnp.float32)
    for h in range(IDX_HEADS):
        sc = jnp.dot(qi_ref[0, h].astype(jnp.bfloat16), kiT_ref[0],
                     preferred_element_type=jnp.float32) * IDX_DIM ** -0.5
        isc = isc + _round_bf16(jnp.maximum(sc, 0.0)) * wi[:, h:h + 1]

    sel = _topk_mask(isc, valid, key_ref, k_top)

    dist = (row - col).astype(jnp.float32)
    for h in range(ATT_HEADS):
        hs = slice(h * ATT_HEAD_DIM, (h + 1) * ATT_HEAD_DIM)
        slope = 2.0 ** (-8.0 * (h + 1) / ATT_HEADS)
        lg = jnp.dot(q_ref[0, :, hs].astype(jnp.bfloat16), kT_ref[0, hs, :],
                     preferred_element_type=jnp.float32) * ATT_HEAD_DIM ** -0.5 - slope * dist
        lg = jnp.where(sel, lg, -jnp.inf)
        e = jnp.exp(lg - jnp.max(lg, axis=1, keepdims=True))
        p = e * (1.0 / jnp.sum(e, axis=1, keepdims=True))
        o_ref[0, :, hs] = jnp.dot(p.astype(jnp.bfloat16), v_ref[0, :, hs], preferred_element_type=jnp.float32)


def _dsa_prompt_pallas(q, k, v, qi, ki, wi):
    B, T = q.shape[:2]
    k_top = min(TOPK_MAX, T // 4)
    tq = DSA_TQ
    t_pad = pl.cdiv(T, LANES) * LANES
    padt = lambda t: jnp.pad(t, [(0, 0), (0, t_pad - T)] + [(0, 0)] * (t.ndim - 2))
    q2 = padt(q.reshape(B, T, W_MIX))
    kT = jnp.swapaxes(padt(k.reshape(B, T, W_MIX)), 1, 2).astype(jnp.bfloat16)
    v2 = padt(v.reshape(B, T, W_MIX)).astype(jnp.bfloat16)
    qi2 = jnp.swapaxes(padt(qi), 1, 2)
    kiT = jnp.swapaxes(padt(ki), 1, 2).astype(jnp.bfloat16)
    wi2 = padt(wi)
    n_q = t_pad // tq
    bounds = sorted({-(-n_q * s // DSA_SEGMENTS) for s in range(DSA_SEGMENTS + 1)})
    outs = []
    for i0, i1 in zip(bounds[:-1], bounds[1:]):
        tk = i1 * tq
        outs.append(pl.pallas_call(
            functools.partial(_dsa_prompt_body, k_top=k_top, first_block=i0),
            grid=(B, i1 - i0),
            in_specs=[pl.BlockSpec((1, tq, W_MIX), lambda b, i, i0=i0: (b, i + i0, 0)),
                      pl.BlockSpec((1, IDX_HEADS, tq, IDX_DIM), lambda b, i, i0=i0: (b, 0, i + i0, 0)),
                      pl.BlockSpec((1, tq, IDX_HEADS), lambda b, i, i0=i0: (b, i + i0, 0)),
                      pl.BlockSpec((1, W_MIX, tk), lambda b, i: (b, 0, 0)),
                      pl.BlockSpec((1, tk, W_MIX), lambda b, i: (b, 0, 0)),
                      pl.BlockSpec((1, IDX_DIM, tk), lambda b, i: (b, 0, 0))],
            out_specs=pl.BlockSpec((1, tq, W_MIX), lambda b, i: (b, i, 0)),
            out_shape=jax.ShapeDtypeStruct((B, (i1 - i0) * tq, W_MIX), jnp.float32),
            scratch_shapes=[pltpu.VMEM((tq, tk), jnp.int32)],
            compiler_params=pltpu.CompilerParams(dimension_semantics=("arbitrary", "arbitrary"),
                                                 vmem_limit_bytes=VMEM_LIMIT_BYTES),
            name="dsa_prompt",
        )(q2, qi2, wi2, kT, v2, kiT))
    out = jnp.concatenate(outs, axis=1)
    return out[:, :T].reshape(B, T, ATT_HEADS, ATT_HEAD_DIM)


SUBLANES = 8


def _dsa_sample_body(pt_ref, q_ref, kn_ref, vn_ref, qi_ref, kin_ref, wi_ref, *rest, n_pages, k_top):
    ki_pages, k_pages, v_pages = rest[:n_pages], rest[n_pages:2 * n_pages], rest[2 * n_pages:3 * n_pages]
    o_ref, key_ref = rest[3 * n_pages], rest[3 * n_pages + 1]
    R = SUBLANES
    HP = ATT_HEADS * PAGE_SIZE
    t_pad = key_ref.shape[1]
    past_len = n_pages * PAGE_SIZE

    qi = qi_ref[0].astype(jnp.bfloat16)
    wi = _round_bf16(wi_ref[0])

    def idx_tile(ki_page):
        sc = _dot_nt(qi, ki_page) * IDX_DIM ** -0.5
        return jnp.sum(_round_bf16(jnp.maximum(sc, 0.0)) * wi, axis=0, keepdims=True)

    first_row = lax.broadcasted_iota(jnp.int32, (PAGE_SIZE, 1), 0) == 0
    kin_page = jnp.where(first_row, kin_ref[0], 0.0)
    isc = jnp.concatenate([idx_tile(ki_pages[j][0, 0]) for j in range(n_pages)] + [idx_tile(kin_page)], axis=1)
    col = lax.broadcasted_iota(jnp.int32, (R, t_pad), 1)
    sel = _topk_mask(jnp.broadcast_to(isc, (R, t_pad)), col <= past_len, key_ref, k_top)

    expand = (lax.broadcasted_iota(jnp.int32, (PAGE_SIZE, HP), 1) // ATT_HEADS
              == lax.broadcasted_iota(jnp.int32, (PAGE_SIZE, HP), 0)).astype(jnp.bfloat16)
    sel_rows = jnp.concatenate([jnp.where(sel[:, j * PAGE_SIZE:(j + 1) * PAGE_SIZE], 1.0, 0.0)
                                for j in range(n_pages + 1)], axis=0).astype(jnp.bfloat16)
    sel_hp = jnp.dot(sel_rows, expand, preferred_element_type=jnp.float32)

    lane = lax.broadcasted_iota(jnp.int32, (R, HP), 1)
    hrow = lax.broadcasted_iota(jnp.int32, (R, HP), 0) % ATT_HEADS
    own = lane % ATT_HEADS == hrow
    slope = sum(jnp.where(hrow == h, 2.0 ** (-8.0 * (h + 1) / ATT_HEADS), 0.0) for h in range(ATT_HEADS))
    q8 = q_ref[0].astype(jnp.bfloat16)
    zeros = jnp.zeros((HP - R, ATT_HEAD_DIM), jnp.float32)
    k_all = [k_pages[j][0, 0] for j in range(n_pages)] + [jnp.concatenate([kn_ref[0], zeros], axis=0)]
    v_all = [v_pages[j][0, 0] for j in range(n_pages)] + [jnp.concatenate([vn_ref[0], zeros], axis=0)]
    tiles = []
    for j, kp in enumerate(k_all):
        dist = (past_len - j * PAGE_SIZE - lane // ATT_HEADS).astype(jnp.float32)
        lg = _dot_nt(q8, kp) * ATT_HEAD_DIM ** -0.5 - slope * dist
        tiles.append(jnp.where(own & (sel_hp[j * R:(j + 1) * R] > 0.5), lg, -jnp.inf))
    lg = jnp.concatenate(tiles, axis=1)
    e = jnp.exp(lg - jnp.max(lg, axis=1, keepdims=True))
    p = (e * (1.0 / jnp.sum(e, axis=1, keepdims=True))).astype(jnp.bfloat16)
    acc = jnp.zeros((R, ATT_HEAD_DIM), jnp.float32)
    for j, vp in enumerate(v_all):
        acc = acc + _dot(p[:, j * HP:(j + 1) * HP], vp)
    o_ref[0] = acc[:ATT_HEADS]


def _dsa_sample_pallas(q, k, v, qi, ki, wi, cache_k, cache_v, cache_idx_k, page_table, l):
    B, Q = q.shape[:2]
    assert Q == 1 and PAGE_SIZE == LANES
    n_pages = page_table.shape[1]
    n_pool = cache_k.shape[1]
    k_top = min(TOPK_MAX, (n_pages * PAGE_SIZE + Q) // 4)
    t_pad = (n_pages + 1) * PAGE_SIZE
    HP = ATT_HEADS * PAGE_SIZE
    head_spec = pl.BlockSpec((1, SUBLANES, ATT_HEAD_DIM), lambda b, pt: (b, 0, 0))
    page_spec = lambda rows, w, j: pl.BlockSpec((1, 1, rows, w), lambda b, pt: (l, pt[b, j], 0, 0))
    grid_spec = pltpu.PrefetchScalarGridSpec(
        num_scalar_prefetch=1,
        grid=(B,),
        in_specs=[head_spec, head_spec, head_spec,
                  pl.BlockSpec((1, IDX_HEADS, IDX_DIM), lambda b, pt: (b, 0, 0)),
                  pl.BlockSpec((1, 1, IDX_DIM), lambda b, pt: (b, 0, 0)),
                  pl.BlockSpec((1, IDX_HEADS, 1), lambda b, pt: (b, 0, 0))]
                 + [page_spec(PAGE_SIZE, IDX_DIM, j) for j in range(n_pages)]
                 + [page_spec(HP, ATT_HEAD_DIM, j) for j in range(n_pages)] * 2,
        out_specs=pl.BlockSpec((1, ATT_HEADS, ATT_HEAD_DIM), lambda b, pt: (b, 0, 0)),
        scratch_shapes=[pltpu.VMEM((SUBLANES, t_pad), jnp.int32)])
    heads = lambda t: jnp.pad(t.reshape(B, ATT_HEADS, ATT_HEAD_DIM), [(0, 0), (0, SUBLANES - ATT_HEADS), (0, 0)])
    pool = lambda c: c.reshape(c.shape[0], n_pool, HP, ATT_HEAD_DIM)
    out = pl.pallas_call(
        functools.partial(_dsa_sample_body, n_pages=n_pages, k_top=k_top),
        grid_spec=grid_spec,
        out_shape=jax.ShapeDtypeStruct((B, ATT_HEADS, ATT_HEAD_DIM), jnp.float32),
        compiler_params=pltpu.CompilerParams(dimension_semantics=("arbitrary",),
                                             vmem_limit_bytes=VMEM_LIMIT_BYTES),
        name="dsa_sample",
    )(page_table, heads(q), heads(k), heads(v),
      qi.reshape(B, IDX_HEADS, IDX_DIM), ki, wi.reshape(B, IDX_HEADS, 1),
      *([cache_idx_k] * n_pages), *([pool(cache_k)] * n_pages), *([pool(cache_v)] * n_pages))
    return out.reshape(B, 1, ATT_HEADS, ATT_HEAD_DIM)


RWKV_GROUP = 8
RWKV_BB = 2


def _rwkv_scan_body(r_ref, d_ref, k_ref, v_ref, kk_ref, a_ref, s0_ref, y_ref, sT_ref, S_scr, Sb_scr):
    BB, tc = r_ref.shape[:2]
    P, N = S_scr.shape[1:3]
    G = RWKV_GROUP
    chains = [(b, p) for b in range(BB) for p in range(P)]
    c = pl.program_id(1)

    @pl.when(c == 0)
    def _():
        S_scr[...] = s0_ref[...]
        Sb_scr[...] = s0_ref[...].astype(jnp.bfloat16)

    lo_half = lax.broadcasted_iota(jnp.int32, (1, LANES), 1) < N
    sub = lax.broadcasted_iota(jnp.int32, (G, N), 0)
    sub2 = lax.broadcasted_iota(jnp.int32, (G, LANES), 0)
    r_mask = ((sub2 == 0) & lo_half) | ((sub2 == 1) & jnp.logical_not(lo_half))
    ii = lax.broadcasted_iota(jnp.int32, (N, LANES), 0)
    jj = lax.broadcasted_iota(jnp.int32, (N, LANES), 1)
    diagA, diagB = jj == ii, jj == ii + N

    def group(g, carry):
        t0 = pl.multiple_of(g * G, G)
        ld = lambda ref, ch: ref[ch[0], pl.ds(t0, G), ch[1] * LANES:(ch[1] + 1) * LANES]
        kk8 = [ld(kk_ref, ch) for ch in chains]
        beta8 = [kk8[i] * ld(a_ref, ch) for i, ch in enumerate(chains)]
        d8 = [ld(d_ref, ch) for ch in chains]
        k8 = [ld(k_ref, ch) for ch in chains]
        r8 = [ld(r_ref, ch) for ch in chains]
        v8 = [ld(v_ref, ch) for ch in chains]
        kkA = [jnp.where(lo_half, x, 0.0).astype(jnp.bfloat16) for x in kk8]
        kkB = [jnp.where(lo_half, 0.0, x).astype(jnp.bfloat16) for x in kk8]
        yA = [jnp.zeros((G, N), jnp.float32) for _ in chains]
        yB = [jnp.zeros((G, N), jnp.float32) for _ in chains]

        def y_step(t, i):
            b, p = chains[i]
            r2 = jnp.where(r_mask, r8[i][t:t + 1], 0.0).astype(jnp.bfloat16)
            y2 = _dot_nt(r2, Sb_scr[b, p])
            yA[i] = jnp.where(sub == t, jnp.broadcast_to(y2[0:1], (G, N)), yA[i])
            yB[i] = jnp.where(sub == t, jnp.broadcast_to(y2[1:2], (G, N)), yB[i])

        for t in range(G):
            row = lambda x: x[t:t + 1]
            sa_b = []
            for i, (b, p) in enumerate(chains):
                w1 = jnp.concatenate([jnp.broadcast_to(row(kkA[i]), (N, LANES)),
                                      jnp.broadcast_to(row(kkB[i]), (N, LANES))], axis=0)
                sa_b.append(_dot_nt(Sb_scr[b, p], w1))
            if t > 0:
                for i in range(len(chains)):
                    y_step(t - 1, i)
            for i, (b, p) in enumerate(chains):
                vA = jnp.sum(jnp.where(diagA, row(v8[i]), 0.0), axis=1, keepdims=True)
                vB = jnp.sum(jnp.where(diagB, row(v8[i]), 0.0), axis=1, keepdims=True)
                v_b = jnp.where(lo_half, vA, vB)
                S = S_scr[b, p] * row(d8[i]) - sa_b[i] * row(beta8[i]) + v_b * row(k8[i])
                S_scr[b, p] = S
                Sb_scr[b, p] = S.astype(jnp.bfloat16)
        for i, (b, p) in enumerate(chains):
            y_step(G - 1, i)
            y_ref[b, pl.ds(t0, G), p * LANES:(p + 1) * LANES] = jnp.concatenate([yA[i], yB[i]], axis=1)
        return carry

    lax.fori_loop(0, tc // G, group, 0)

    @pl.when(c == pl.num_programs(1) - 1)
    def _():
        sT_ref[...] = S_scr[...]


def _rwkv_scan(r, d, k, v, kk, a, S0):
    B, T, W = r.shape
    H, N = S0.shape[1:3]
    P = H // 2
    G = RWKV_GROUP
    bb = RWKV_BB if B % RWKV_BB == 0 else 1
    t_pad = pl.cdiv(T, G) * G
    tc = t_pad
    for cand in range(min(t_pad, 512), G - 1, -1):
        if t_pad % cand == 0 and cand % G == 0:
            tc = cand
            break
    pad = lambda x, c=0.0: jnp.pad(x, [(0, 0), (0, t_pad - T), (0, 0)], constant_values=c) if t_pad != T else x
    S2 = S0.reshape(B, P, 2, N, N).transpose(0, 1, 3, 2, 4).reshape(B, P, N, 2 * N)
    seq_spec = pl.BlockSpec((bb, tc, W), lambda b, c: (b, c, 0))
    st_spec = pl.BlockSpec((bb, P, N, 2 * N), lambda b, c: (b, 0, 0, 0))
    y, S = pl.pallas_call(
        _rwkv_scan_body,
        grid=(B // bb, t_pad // tc),
        in_specs=[seq_spec] * 6 + [st_spec],
        out_specs=[seq_spec, st_spec],
        out_shape=[jax.ShapeDtypeStruct((B, t_pad, W), jnp.float32),
                   jax.ShapeDtypeStruct((B, P, N, 2 * N), jnp.float32)],
        scratch_shapes=[pltpu.VMEM((bb, P, N, 2 * N), jnp.float32), pltpu.VMEM((bb, P, N, 2 * N), jnp.bfloat16)],
        compiler_params=pltpu.CompilerParams(dimension_semantics=("arbitrary", "arbitrary"),
                                             vmem_limit_bytes=VMEM_LIMIT_BYTES),
        name="rwkv_scan",
    )(pad(r), pad(d, 1.0), pad(k), pad(v), pad(kk), pad(a), S2)
    S = S.reshape(B, P, N, 2, N).transpose(0, 1, 3, 2, 4).reshape(B, H, N, N)
    return y[:, :T], S


def _bf(x):
    return x.astype(jnp.bfloat16)


def _dot(a, b):
    return jnp.dot(_bf(a), _bf(b), preferred_element_type=jnp.float32)


def _dot_nt(a, b):
    return lax.dot_general(_bf(a), _bf(b), (((1,), (1,)), ((), ())), preferred_element_type=jnp.float32)


def _dot_f32(a, b):
    return jnp.dot(a, b, precision=lax.Precision.HIGHEST, preferred_element_type=jnp.float32)


GDN_BB = 2


def _gdn_chunk_body(q_ref, k_ref, v_ref, g_ref, beta_ref, grow_ref, s0_ref, o_ref, sT_ref, S_scr):
    BB, L = q_ref.shape[:2]
    H, D = S_scr.shape[1:3]
    chains = [(b, h) for b in range(BB) for h in range(H)]
    c = pl.program_id(1)

    @pl.when(c == 0)
    def _():
        S_scr[...] = s0_ref[...]

    ii = lax.broadcasted_iota(jnp.int32, (L, L), 0)
    jj = lax.broadcasted_iota(jnp.int32, (L, L), 1)
    tri, stri = jj <= ii, jj < ii
    eye = jnp.where(ii == jj, 1.0, 0.0)
    every = lambda f: [f(i) for i in range(len(chains))]
    hs = lambda h: slice(h * D, (h + 1) * D)
    q = [q_ref[b, :, hs(h)] for b, h in chains]
    k = [k_ref[b, :, hs(h)] for b, h in chains]
    g_col = [g_ref[b, :, h:h + 1] for b, h in chains]
    b_col = [beta_ref[b, :, h:h + 1] for b, h in chains]
    g_row = [grow_ref[b, 0, h:h + 1, :] for b, h in chains]
    gc_col = every(lambda i: jnp.sum(jnp.where(tri, g_row[i], 0.0), axis=1, keepdims=True))
    gc_row = every(lambda i: jnp.sum(jnp.where(ii <= jj, g_col[i], 0.0), axis=0, keepdims=True))
    dec = every(lambda i: jnp.exp(jnp.where(tri, gc_col[i] - gc_row[i], -jnp.inf)))
    kb = every(lambda i: k[i] * b_col[i])
    egc = every(lambda i: jnp.exp(gc_col[i]))
    A = every(lambda i: jnp.where(stri, _dot_nt(kb[i], k[i]) * dec[i], 0.0))
    attn = every(lambda i: jnp.where(tri, _dot_nt(q[i], k[i]) * dec[i], 0.0))
    rhs = [jnp.concatenate([v_ref[b, :, hs(h)] * b_col[i], kb[i] * egc[i]], axis=1) for i, (b, h) in enumerate(chains)]
    inv = every(lambda i: eye - A[i])
    P = every(lambda i: _dot_f32(A[i], A[i]))
    n_lvl = int(math.log2(L)) - 1
    for lvl in range(n_lvl):
        inv = every(lambda i: inv[i] + _dot_f32(inv[i], P[i]))
        if lvl + 1 < n_lvl:
            P = every(lambda i: _dot_f32(P[i], P[i]))
    sol = every(lambda i: _dot_f32(inv[i], rhs[i]))
    S = [S_scr[b, h] for b, h in chains]
    v_new = every(lambda i: sol[i][:, :D] - _dot(sol[i][:, D:], S[i]))
    o_state = every(lambda i: _dot(q[i] * egc[i], S[i]))
    g_last = every(lambda i: gc_col[i][L - 1:L, :])
    k_sc = every(lambda i: (k[i] * jnp.exp(g_last[i] - gc_col[i])).T)
    for i, (b, h) in enumerate(chains):
        o_ref[b, :, hs(h)] = o_state[i] + _dot(attn[i], v_new[i])
    for i, (b, h) in enumerate(chains):
        S_scr[b, h] = S[i] * jnp.exp(g_last[i]) + _dot(k_sc[i], v_new[i])

    @pl.when(c == pl.num_programs(1) - 1)
    def _():
        sT_ref[...] = S_scr[...]


def _gdn_chunks(q, k, v, g, beta, S0, L):
    B, T, W = q.shape
    H, D = S0.shape[1:3]
    n = T // L
    bb = GDN_BB if B % GDN_BB == 0 else 1
    g_row = jnp.swapaxes(g.reshape(B, n, L, H), 2, 3)
    seq = pl.BlockSpec((bb, L, W), lambda b, c: (b, c, 0))
    gate = pl.BlockSpec((bb, L, H), lambda b, c: (b, c, 0))
    st = pl.BlockSpec((bb, H, D, D), lambda b, c: (b, 0, 0, 0))
    return pl.pallas_call(
        _gdn_chunk_body,
        grid=(B // bb, n),
        in_specs=[seq, seq, seq, gate, gate, pl.BlockSpec((bb, 1, H, L), lambda b, c: (b, c, 0, 0)), st],
        out_specs=[seq, st],
        out_shape=[jax.ShapeDtypeStruct((B, T, W), jnp.float32),
                   jax.ShapeDtypeStruct((B, H, D, D), jnp.float32)],
        scratch_shapes=[pltpu.VMEM((bb, H, D, D), jnp.float32)],
        compiler_params=pltpu.CompilerParams(dimension_semantics=("arbitrary", "arbitrary"),
                                             vmem_limit_bytes=VMEM_LIMIT_BYTES),
        name="gdn_chunks",
    )(q, k, v, g, beta, g_row, S0)


def _layer_norm(x, g, b, eps=1e-5):
    xf = x.astype(jnp.float32)
    mu = jnp.mean(xf, -1, keepdims=True)
    var = jnp.mean(jnp.square(xf - mu), -1, keepdims=True)
    return ((xf - mu) * lax.rsqrt(var + eps) * g + b).astype(x.dtype)


def _rms_norm(x, g, eps=1e-6):
    xf = x.astype(jnp.float32)
    return (xf * lax.rsqrt(jnp.mean(xf * xf, -1, keepdims=True) + eps) * g).astype(x.dtype)


def _l2norm(x, eps=1e-6):
    xf = x.astype(jnp.float32)
    return xf * lax.rsqrt(jnp.sum(xf * xf, -1, keepdims=True) + eps)


def _proj_in(x, w_in):
    B, T, _ = x.shape
    sizes = [W_MIX,
             3 * W_MIX, GDN_HEADS, GDN_HEADS, W_MIX,
             W_MIX, W_MIX, W_MIX, IDX_HEADS * IDX_DIM, IDX_DIM, IDX_HEADS,
             3 * W_MIX, MLSTM_HEADS, MLSTM_HEADS, W_MIX,
             N_BRANCH * D_MODEL]
    offs = np.concatenate([[0], np.cumsum(sizes)]).tolist()
    xb = x.reshape(B * T, D_MODEL).astype(jnp.bfloat16)
    outs = [None] * len(sizes)

    def grouped(idx):
        w = jnp.concatenate([w_in[:, offs[i]:offs[i + 1]] for i in idx], axis=1).astype(jnp.bfloat16)
        for i, y in zip(idx, _mm_group(xb, w, [sizes[i] for i in idx])):
            outs[i] = y.reshape(B, T, sizes[i])

    grouped([i for i, s in enumerate(sizes) if s < LANES])
    grouped([i for i, s in enumerate(sizes) if s == W_MIX])
    for i, s in enumerate(sizes):
        if s > W_MIX:
            outs[i] = _mm(xb, w_in[:, offs[i]:offs[i + 1]]).reshape(B, T, s)
    return outs


def _short_conv(u, buf, w):
    T = u.shape[1]
    ext = jnp.concatenate([buf.astype(jnp.float32), u.astype(jnp.float32)], axis=1)
    y = sum(ext[:, j:j + T] * w[j] for j in range(CONV_W))
    return jax.nn.silu(y), ext[:, ext.shape[1] - (CONV_W - 1):]


def _rwkv7(u, S0, shift0, mu, w_rkv, w0, w1, w2, a0, a1, a2, g1, g2, k_k, k_a, r_k, lnx_g, lnx_b):
    B, T, _ = u.shape
    uf = u.astype(jnp.float32)
    prev = jnp.concatenate([shift0[:, None].astype(jnp.float32), uf[:, :-1]], axis=1)
    xx = prev - uf
    xr, xw, xk, xv, xa, xg = (uf + xx * mu[j] for j in range(6))
    r = xr @ w_rkv[0]
    k = xk @ w_rkv[1]
    v = xv @ w_rkv[2]
    w = -jax.nn.softplus(-(w0 + jnp.tanh(xw @ w1) @ w2)) - 0.5
    a = jax.nn.sigmoid(a0 + (xa @ a1) @ a2)
    g = jax.nn.sigmoid(xg @ g1) @ g2
    hd = lambda t: t.reshape(B, T, RWKV_HEADS, RWKV_HEAD_DIM)
    kk = _l2norm(hd(k * k_k))
    k = hd(k * (1.0 + (a - 1.0) * k_a))
    r, v, a = hd(r), hd(v), hd(a)
    decay = jnp.exp(-jnp.exp(hd(w)))

    flat = lambda t: t.reshape(B, T, W_MIX)
    y, S = _rwkv_scan(flat(r), flat(decay), flat(k), flat(v), flat(kk), flat(a), S0.astype(jnp.float32))
    y = hd(y)
    mean = jnp.mean(y, -1, keepdims=True)
    var = jnp.mean(jnp.square(y - mean), -1, keepdims=True)
    y = ((y - mean) * lax.rsqrt(var + RWKV_GN_EPS)).reshape(B, T, W_MIX) * lnx_g + lnx_b
    bonus = jnp.sum(r * k * r_k, -1, keepdims=True) * v
    y = (y + bonus.reshape(B, T, W_MIX)) * g
    return y.astype(u.dtype), S, u[:, -1]


def _gdn_chunk(S, xs):
    q, k, v, g, beta = xs
    L = q.shape[2]
    tri = jnp.tril(jnp.ones((L, L), bool))
    stri = jnp.tril(jnp.ones((L, L), bool), -1)
    gc = jnp.cumsum(g, axis=-1)
    dec = jnp.exp(jnp.where(tri, gc[..., :, None] - gc[..., None, :], -jnp.inf))
    kb = k * beta[..., None]
    A = jnp.where(stri, jnp.einsum('bhid,bhjd->bhij', kb, k) * dec, 0.0)
    rhs = jnp.concatenate([v * beta[..., None], kb * jnp.exp(gc)[..., None]], axis=-1)
    sol = lax.linalg.triangular_solve(A + jnp.eye(L, dtype=A.dtype), rhs, left_side=True,
                                      lower=True, unit_diagonal=True)
    u_, w_ = sol[..., :GDN_HEAD_DIM], sol[..., GDN_HEAD_DIM:]
    v_new = u_ - jnp.einsum('bhld,bhde->bhle', w_, S)
    attn = jnp.where(tri, jnp.einsum('bhid,bhjd->bhij', q, k) * dec, 0.0)
    o = (jnp.einsum('bhld,bhde->bhle', q * jnp.exp(gc)[..., None], S)
         + jnp.einsum('bhij,bhje->bhie', attn, v_new))
    g_last = gc[..., -1]
    S = (S * jnp.exp(g_last)[..., None, None]
         + jnp.einsum('bhld,bhle->bhde', k * jnp.exp(g_last[..., None] - gc)[..., None], v_new))
    return S, o


def _gdn(qkv, a_pre, b_pre, z, S0, buf0, conv_w, A_log, dt_bias, norm_g, is_prompt):
    B, T, _ = qkv.shape
    y, new_buf = _short_conv(qkv, buf0, conv_w)
    q, k, v = jnp.split(y, 3, axis=-1)
    hd4 = lambda t: t.reshape(B, T, GDN_HEADS, GDN_HEAD_DIM)
    q = _l2norm(hd4(q)) * GDN_HEAD_DIM ** -0.5
    k = _l2norm(hd4(k))
    g = -jnp.exp(A_log.astype(jnp.float32)) * jax.nn.softplus(a_pre.astype(jnp.float32) + dt_bias)
    beta = jax.nn.sigmoid(b_pre.astype(jnp.float32))
    if is_prompt:
        pad = (-N_META) % GDN_CHUNK
        fp = lambda t: jnp.pad(t.reshape(B, T, -1), [(0, 0), (pad, 0), (0, 0)])
        o, S = _gdn_chunks(fp(q), fp(k), fp(v), fp(g), fp(beta), S0.astype(jnp.float32), GDN_CHUNK)
        o = hd4(o[:, pad:])
    else:
        hd = lambda t: t.transpose(0, 2, 1, 3)
        S, o = _gdn_chunk(S0.astype(jnp.float32), (hd(q), hd(k), hd(hd4(v)), g.transpose(0, 2, 1),
                                                    beta.transpose(0, 2, 1)))
        o = o.transpose(0, 2, 1, 3)
    o = _rms_norm(o, norm_g)
    o = o * jax.nn.silu(z.astype(jnp.float32).reshape(B, T, GDN_HEADS, GDN_HEAD_DIM))
    return o.reshape(B, T, W_MIX).astype(qkv.dtype), S, new_buf


def _dsa_inputs(c_q, c_k, c_v, c_qi, c_ki, c_wi, ln_g, ln_b):
    B, T, _ = c_q.shape
    hd = lambda t: t.reshape(B, T, ATT_HEADS, ATT_HEAD_DIM)
    qi = c_qi.reshape(B, T, IDX_HEADS, IDX_DIM)
    ki = _layer_norm(c_ki, ln_g, ln_b)
    wi = c_wi * IDX_HEADS ** -0.5
    return hd(c_q), hd(c_k), hd(c_v), qi, ki, wi


def _mlstm_chunk(state, xs):
    C, n, m = state
    q, k, v, li, lf = xs
    L = q.shape[2]
    tri = jnp.tril(jnp.ones((L, L), bool))
    b = jnp.cumsum(lf, axis=-1)
    a = li - b
    m_t = b + jnp.maximum(m[..., None], lax.cummax(a, axis=2))
    dmat = jnp.exp(jnp.where(tri, b[..., :, None] + a[..., None, :] - m_t[..., :, None], -jnp.inf))
    inter = jnp.exp(b + m[..., None] - m_t)
    s = jnp.einsum('bhtd,bhsd->bhts', q, k) * dmat
    num = inter[..., None] * jnp.einsum('bhtd,bhde->bhte', q, C) + jnp.einsum('bhts,bhse->bhte', s, v)
    den = inter * jnp.einsum('bhtd,bhd->bht', q, n) + jnp.sum(s, axis=-1)
    h = num / jnp.maximum(jnp.abs(den), jnp.exp(-m_t))[..., None]
    m_new = m_t[..., -1]
    wend = jnp.exp(b[..., -1:] + a - m_new[..., None])
    carry = jnp.exp(b[..., -1] + m - m_new)
    C = carry[..., None, None] * C + jnp.einsum('bhs,bhsd,bhse->bhde', wend, k, v)
    n = carry[..., None] * n + jnp.einsum('bhs,bhsd->bhd', wend, k)
    return (C, n, m_new), h


MLSTM_BB = 2


def _mlstm_chunk_body(qkv_ref, li_ref, lf_ref, lirow_ref, lfrow_ref, c0_ref, n0_ref, m0_ref,
                      h_ref, cT_ref, nT_ref, mT_ref, C_scr, n_scr, m_scr):
    BB, L = qkv_ref.shape[:2]
    H, D = C_scr.shape[1:3]
    W = H * D
    chains = [(b, h) for b in range(BB) for h in range(H)]
    c = pl.program_id(1)

    @pl.when(c == 0)
    def _():
        C_scr[...] = c0_ref[...]
        n_scr[...] = n0_ref[...]
        m_scr[...] = m0_ref[...]

    ii = lax.broadcasted_iota(jnp.int32, (L, L), 0)
    jj = lax.broadcasted_iota(jnp.int32, (L, L), 1)
    tri = jj <= ii
    every = lambda f: [f(i) for i in range(len(chains))]
    col_of = lambda ref: [ref[b, :, h:h + 1] for b, h in chains]
    row_of = lambda ref: [ref[b, 0, h:h + 1, :] for b, h in chains]
    q = [qkv_ref[b, :, h * D:(h + 1) * D] for b, h in chains]
    k = [qkv_ref[b, :, W + h * D:W + (h + 1) * D] * D ** -0.5 for b, h in chains]
    v = [qkv_ref[b, :, 2 * W + h * D:2 * W + (h + 1) * D] for b, h in chains]
    li_c, lf_c, li_r, lf_r = col_of(li_ref), col_of(lf_ref), row_of(lirow_ref), row_of(lfrow_ref)
    C = [C_scr[b, h] for b, h in chains]
    n = [n_scr[b, h] for b, h in chains]
    m = [m_scr[b, h][:, :1] for b, h in chains]
    b_c = every(lambda i: jnp.sum(jnp.where(tri, lf_r[i], 0.0), axis=1, keepdims=True))
    b_r = every(lambda i: jnp.sum(jnp.where(ii <= jj, lf_c[i], 0.0), axis=0, keepdims=True))
    a_c = every(lambda i: li_c[i] - b_c[i])
    a_r = every(lambda i: li_r[i] - b_r[i])
    cmax = every(lambda i: jnp.max(jnp.where(tri, a_r[i], -jnp.inf), axis=1, keepdims=True))
    m_t = every(lambda i: b_c[i] + jnp.maximum(m[i], cmax[i]))
    dmat = every(lambda i: jnp.exp(jnp.where(tri, b_c[i] + a_r[i] - m_t[i], -jnp.inf)))
    inter = every(lambda i: jnp.exp(b_c[i] + m[i] - m_t[i]))
    s = every(lambda i: _dot_nt(q[i], k[i]) * dmat[i])
    qC = every(lambda i: _dot(q[i], C[i]))
    sv = every(lambda i: _dot(s[i], v[i]))
    qn = every(lambda i: jnp.sum(_round_bf16(q[i]) * _round_bf16(n[i]), axis=1, keepdims=True))
    den = every(lambda i: inter[i] * qn[i] + jnp.sum(s[i], axis=1, keepdims=True))
    m_new = every(lambda i: m_t[i][L - 1:L, :])
    b_last = every(lambda i: b_c[i][L - 1:L, :])
    wend = every(lambda i: jnp.exp(b_last[i] + a_c[i] - m_new[i]))
    carry = every(lambda i: jnp.exp(b_last[i] + m[i] - m_new[i]))
    wk = every(lambda i: wend[i] * k[i])
    for i, (b, h) in enumerate(chains):
        num = inter[i] * qC[i] + sv[i]
        h_ref[b, :, h * D:(h + 1) * D] = num / jnp.maximum(jnp.abs(den[i]), jnp.exp(-m_t[i]))
    for i, (b, h) in enumerate(chains):
        C_scr[b, h] = carry[i] * C[i] + _dot(wk[i].T, v[i])
        n_scr[b, h] = carry[i] * n[i] + jnp.sum(_round_bf16(wend[i]) * _round_bf16(k[i]), axis=0, keepdims=True)
        m_scr[b, h] = jnp.broadcast_to(m_new[i], (1, D))

    @pl.when(c == pl.num_programs(1) - 1)
    def _():
        cT_ref[...] = C_scr[...]
        nT_ref[...] = n_scr[...]
        mT_ref[...] = m_scr[...]


def _mlstm_chunks(qkv, li, lf, C0, n0, m0, L):
    B, T, W3 = qkv.shape
    H, D = C0.shape[1:3]
    n_c = T // L
    bb = MLSTM_BB if B % MLSTM_BB == 0 else 1
    rows = lambda t: jnp.swapaxes(t.reshape(B, n_c, L, H), 2, 3)
    gate = pl.BlockSpec((bb, L, H), lambda b, c: (b, c, 0))
    gate_row = pl.BlockSpec((bb, 1, H, L), lambda b, c: (b, c, 0, 0))
    st_c = pl.BlockSpec((bb, H, D, D), lambda b, c: (b, 0, 0, 0))
    st_v = pl.BlockSpec((bb, H, 1, D), lambda b, c: (b, 0, 0, 0))
    vec = jax.ShapeDtypeStruct((B, H, 1, D), jnp.float32)
    h, C, n, m = pl.pallas_call(
        _mlstm_chunk_body,
        grid=(B // bb, n_c),
        in_specs=[pl.BlockSpec((bb, L, W3), lambda b, c: (b, c, 0)), gate, gate, gate_row, gate_row,
                  st_c, st_v, st_v],
        out_specs=[pl.BlockSpec((bb, L, H * D), lambda b, c: (b, c, 0)), st_c, st_v, st_v],
        out_shape=[jax.ShapeDtypeStruct((B, T, H * D), jnp.float32),
                   jax.ShapeDtypeStruct((B, H, D, D), jnp.float32), vec, vec],
        scratch_shapes=[pltpu.VMEM((bb, H, D, D), jnp.float32), pltpu.VMEM((bb, H, 1, D), jnp.float32),
                        pltpu.VMEM((bb, H, 1, D), jnp.float32)],
        compiler_params=pltpu.CompilerParams(dimension_semantics=("arbitrary", "arbitrary"),
                                             vmem_limit_bytes=VMEM_LIMIT_BYTES),
        name="mlstm_chunks",
    )(qkv, li, lf, rows(li), rows(lf), C0, n0[:, :, None, :], jnp.broadcast_to(m0[:, :, None, None], (B, H, 1, D)))
    return h, C, n[:, :, 0, :], m[:, :, 0, 0]


MLSTM_PAD_LOG_I = -1e30


def _mlstm(qkv, i_pre, f_pre, o_pre, C0, n0, m0, b_i, b_f, norm_g, is_prompt):
    B, T, _ = qkv.shape
    if is_prompt:
        pad = (-N_META) % MLSTM_CHUNK
        fp = lambda t, c=0.0: jnp.pad(t, [(0, 0), (pad, 0), (0, 0)], constant_values=c)
        li = fp(i_pre.astype(jnp.float32) + b_i, MLSTM_PAD_LOG_I)
        lf = fp(jax.nn.log_sigmoid(f_pre.astype(jnp.float32) + b_f))
        h, C, n, m = _mlstm_chunks(fp(qkv.astype(jnp.float32)), li, lf, C0.astype(jnp.float32),
                                   n0.astype(jnp.float32), m0.astype(jnp.float32), MLSTM_CHUNK)
        h = _rms_norm(h[:, pad:].reshape(B, T, MLSTM_HEADS, MLSTM_HEAD_DIM), norm_g).reshape(B, T, W_MIX)
        h = jax.nn.sigmoid(o_pre.astype(jnp.float32)) * h
        return h.astype(qkv.dtype), C, n, m
    q, k, v = jnp.split(qkv.astype(jnp.float32), 3, axis=-1)
    hd = lambda t: t.reshape(B, T, MLSTM_HEADS, MLSTM_HEAD_DIM).transpose(0, 2, 1, 3)
    q, k, v = hd(q), hd(k) * MLSTM_HEAD_DIM ** -0.5, hd(v)
    li = (i_pre.astype(jnp.float32) + b_i).transpose(0, 2, 1)
    lf = jax.nn.log_sigmoid(f_pre.astype(jnp.float32) + b_f).transpose(0, 2, 1)
    st0 = (C0.astype(jnp.float32), n0.astype(jnp.float32), m0.astype(jnp.float32))
    (C, n, m), h = _mlstm_chunk(st0, (q, k, v, li, lf))
    h = _rms_norm(h.transpose(0, 2, 1, 3), norm_g).reshape(B, T, W_MIX)
    h = jax.nn.sigmoid(o_pre.astype(jnp.float32)) * h
    return h.astype(qkv.dtype), C, n, m


def _moe_experts_body(blk_exp_ref, n_used_ref, x_ref, wg_ref, wu_ref, wd_ref, o_ref, wg_bf, wu_bf, wd_bf):
    i = pl.program_id(0)
    new_expert = (i == 0) | (blk_exp_ref[i] != blk_exp_ref[jnp.maximum(i - 1, 0)])

    @pl.when(new_expert)
    def _():
        wg_bf[...] = wg_ref[0, 0].astype(jnp.bfloat16)
        wu_bf[...] = wu_ref[0, 0].astype(jnp.bfloat16)
        wd_bf[...] = wd_ref[0, 0].astype(jnp.bfloat16)

    @pl.when(i < n_used_ref[0])
    def _():
        xb = x_ref[...].astype(jnp.bfloat16)
        hg = jnp.dot(xb, wg_bf[...], preferred_element_type=jnp.float32)
        hu = jnp.dot(xb, wu_bf[...], preferred_element_type=jnp.float32)
        act = (hg * jax.nn.sigmoid(hg)) * hu
        o_ref[...] = jnp.dot(act.astype(jnp.bfloat16), wd_bf[...],
                             preferred_element_type=jnp.float32).astype(o_ref.dtype)

    @pl.when(i >= n_used_ref[0])
    def _():
        o_ref[...] = jnp.zeros_like(o_ref)


def _moe_experts(xb, blk_exp, n_used, w_g, w_u, w_d, l):
    R, D = xb.shape
    F = w_g.shape[-1]
    n_blocks = R // MOE_BLOCK
    grid_spec = pltpu.PrefetchScalarGridSpec(
        num_scalar_prefetch=2,
        grid=(n_blocks,),
        in_specs=[pl.BlockSpec((MOE_BLOCK, D), lambda i, be, nu: (i, 0)),
                  pl.BlockSpec((1, 1, D, F), lambda i, be, nu: (l, be[i], 0, 0)),
                  pl.BlockSpec((1, 1, D, F), lambda i, be, nu: (l, be[i], 0, 0)),
                  pl.BlockSpec((1, 1, F, D), lambda i, be, nu: (l, be[i], 0, 0))],
        out_specs=pl.BlockSpec((MOE_BLOCK, D), lambda i, be, nu: (i, 0)),
        scratch_shapes=[pltpu.VMEM((D, F), jnp.bfloat16), pltpu.VMEM((D, F), jnp.bfloat16),
                        pltpu.VMEM((F, D), jnp.bfloat16)])
    return pl.pallas_call(
        _moe_experts_body,
        grid_spec=grid_spec,
        out_shape=jax.ShapeDtypeStruct((R, D), jnp.bfloat16),
        compiler_params=pltpu.CompilerParams(dimension_semantics=("arbitrary",),
                                             vmem_limit_bytes=VMEM_LIMIT_BYTES),
        name="moe_experts",
    )(blk_exp, n_used, xb, w_g, w_u, w_d)


def _expert_dispatch(xt, eid, gate, w_g, w_u, w_d, l):
    N = xt.shape[0]
    A = N * TOP_K_INNER
    e_flat = eid.reshape(-1)
    tok = jnp.arange(A, dtype=jnp.int32) // TOP_K_INNER
    onehot = (e_flat[:, None] == jnp.arange(N_EXPERTS, dtype=e_flat.dtype)[None, :]).astype(jnp.int32)
    running = jnp.cumsum(onehot, axis=0)
    counts = running[-1]
    padded = (counts + MOE_BLOCK - 1) // MOE_BLOCK * MOE_BLOCK
    pad_end = jnp.cumsum(padded)
    pad_start = pad_end - padded
    slot = jnp.sum(onehot * (pad_start[None, :] + running - 1), axis=1)
    n_blocks = -(-A // MOE_BLOCK) + N_EXPERTS
    slot_tok = jnp.full((n_blocks * MOE_BLOCK,), N, jnp.int32).at[slot].set(tok)
    blk_exp = jnp.minimum(jnp.searchsorted(pad_end, jnp.arange(n_blocks) * MOE_BLOCK, side='right'), N_EXPERTS - 1)
    x_pad = jnp.concatenate([xt, jnp.zeros((1, xt.shape[1]), xt.dtype)], axis=0).astype(jnp.bfloat16)
    n_used = (pad_end[-1:] // MOE_BLOCK).astype(jnp.int32)
    yb = _moe_experts(x_pad[slot_tok], blk_exp.astype(jnp.int32), n_used, w_g, w_u, w_d, l)
    slot = slot.reshape(N, TOP_K_INNER)
    return sum(yb[slot[:, j]].astype(jnp.float32) * _round_bf16(gate[:, j:j + 1]) for j in range(TOP_K_INNER))


def _moe(x, w_group, w_expert, w_g, w_u, w_d, l):
    shp = x.shape
    xt = x.reshape(-1, D_MODEL)
    N = xt.shape[0]
    xf = xt.astype(jnp.float32)
    pg = jax.nn.softmax(xf @ w_group.astype(jnp.float32), axis=-1)
    p_top, g_sel = lax.top_k(pg, 1)
    le = (xf @ w_expert.astype(jnp.float32)).reshape(N, N_GROUPS, EXPERTS_PER_GROUP)
    le_g = le[:, N_GROUPS - 1]
    for grp in range(N_GROUPS - 2, -1, -1):
        le_g = jnp.where(g_sel == grp, le[:, grp], le_g)
    v2, j2 = lax.top_k(le_g, TOP_K_INNER)
    gate = jax.nn.softmax(v2, axis=-1) * p_top
    eid = g_sel * EXPERTS_PER_GROUP + j2
    return _expert_dispatch(xt, eid, gate, w_g, w_u, w_d, l).reshape(shp).astype(x.dtype)


MERGE_TN = 512


def _branch_merge_body(*refs):
    y_refs, g_refs, w_refs, o_ref = refs[:N_BRANCH], refs[N_BRANCH:2 * N_BRANCH], refs[2 * N_BRANCH:3 * N_BRANCH], refs[-1]
    acc = 0.0
    for y_ref, g_ref, w_ref in zip(y_refs, g_refs, w_refs):
        acc = acc + jax.nn.sigmoid(g_ref[...]) * _dot(y_ref[...], w_ref[0, 0])
    o_ref[...] = acc.astype(o_ref.dtype)


def _out_ln_body(m_ref, w_ref, x_ref, g_ref, b_ref, o_ref, *, alpha, eps):
    z = alpha * x_ref[...] + jnp.dot(m_ref[...], w_ref[...], preferred_element_type=jnp.float32)
    mu = jnp.mean(z, -1, keepdims=True)
    var = jnp.mean(jnp.square(z - mu), -1, keepdims=True)
    o_ref[...] = (z - mu) * lax.rsqrt(var + eps) * g_ref[...] + b_ref[...]


def _merge_out_ln(ys, gates, x, w_branch, w_out, ln_g, ln_b, l, alpha, tm=1024):
    B, T, D = x.shape
    M = B * T
    tm = min(tm, M)
    tn = MERGE_TN
    nj = D // tn
    row = lambda t: t.reshape(M, t.shape[-1])
    merged = pl.pallas_call(
        _branch_merge_body,
        grid=(pl.cdiv(M, tm), nj),
        in_specs=[pl.BlockSpec((tm, W_MIX), lambda i, j: (i, 0))] * N_BRANCH
                 + [pl.BlockSpec((tm, tn), lambda i, j, b=b: (i, b * nj + j)) for b in range(N_BRANCH)]
                 + [pl.BlockSpec((1, 1, W_MIX, tn), lambda i, j, b=b: (l, b, 0, j)) for b in range(N_BRANCH)],
        out_specs=pl.BlockSpec((tm, tn), lambda i, j: (i, j)),
        out_shape=jax.ShapeDtypeStruct((M, D), jnp.bfloat16),
        compiler_params=pltpu.CompilerParams(dimension_semantics=("arbitrary", "arbitrary"),
                                             vmem_limit_bytes=VMEM_LIMIT_BYTES),
        name="branch_merge",
    )(*[row(y) for y in ys], *([row(gates)] * N_BRANCH), *([w_branch] * N_BRANCH))
    tm2 = min(tm // 2, M)
    out = pl.pallas_call(
        functools.partial(_out_ln_body, alpha=alpha, eps=1e-5),
        grid=(pl.cdiv(M, tm2),),
        in_specs=[pl.BlockSpec((tm2, D), lambda i: (i, 0)),
                  pl.BlockSpec((D, D), lambda i: (0, 0)),
                  pl.BlockSpec((tm2, D), lambda i: (i, 0)),
                  pl.BlockSpec((1, D), lambda i: (0, 0)),
                  pl.BlockSpec((1, D), lambda i: (0, 0))],
        out_specs=pl.BlockSpec((tm2, D), lambda i: (i, 0)),
        out_shape=jax.ShapeDtypeStruct((M, D), jnp.float32),
        compiler_params=pltpu.CompilerParams(dimension_semantics=("arbitrary",),
                                             vmem_limit_bytes=VMEM_LIMIT_BYTES),
        name="out_ln",
    )(merged, w_out.astype(jnp.bfloat16), row(x), ln_g.reshape(1, D), ln_b.reshape(1, D))
    return out.reshape(B, T, D)


def _layer(x, W, l, st, dsa_fn, is_prompt, alpha):
    B, T, _ = x.shape
    (a_u, b_qkv, b_a, b_b, b_z, c_q, c_k, c_v, c_qi, c_ki, c_wi,
     d_qkv, d_i, d_f, d_o, gates) = _proj_in(x, W['w_in'][l])
    rwkv_S0, rwkv_shift0, gdn_S0, gdn_conv0, mC0, mn0, mm0 = st
    y_a, rwkv_S, rwkv_shift = _rwkv7(
        a_u, rwkv_S0, rwkv_shift0, W['rwkv_mu'][l], W['rwkv_w_rkv'][l], W['rwkv_w0'][l], W['rwkv_w1'][l],
        W['rwkv_w2'][l], W['rwkv_a0'][l], W['rwkv_a1'][l], W['rwkv_a2'][l], W['rwkv_g1'][l], W['rwkv_g2'][l],
        W['rwkv_k_k'][l], W['rwkv_k_a'][l], W['rwkv_r_k'][l], W['rwkv_lnx_g'][l], W['rwkv_lnx_b'][l])
    y_b, gdn_S, gdn_conv = _gdn(b_qkv, b_a, b_b, b_z, gdn_S0, gdn_conv0, W['gdn_conv_w'][l], W['gdn_A_log'][l],
                                W['gdn_dt_bias'][l], W['gdn_norm_g'][l], is_prompt)
    q, k, v, qi, ki, wi = _dsa_inputs(c_q, c_k, c_v, c_qi, c_ki, c_wi, W['idx_ln_g'][l], W['idx_ln_b'][l])
    y_c = dsa_fn(q, k, v, qi, ki, wi).reshape(B, T, W_MIX)
    y_d, mC, mn, mm = _mlstm(d_qkv, d_i, d_f, d_o, mC0, mn0, mm0, W['mlstm_b_i'][l], W['mlstm_b_f'][l],
                             W['mlstm_norm_g'][l], is_prompt)
    x = _merge_out_ln((y_a, y_b, y_c, y_d), gates, x, W['w_branch'], W['w_out'][l], W['ln1_g'][l], W['ln1_b'][l],
                      l, alpha)
    ffn = _moe(x, W['moe_w_group'][l], W['moe_w_expert'][l], W['moe_w_gate'], W['moe_w_up'], W['moe_w_down'], l)
    x = _layer_norm(alpha * x + ffn, W['ln2_g'][l], W['ln2_b'][l])
    return x, (k, v, ki, rwkv_S, rwkv_shift, gdn_S, gdn_conv, mC, mn, mm)


def kernel(x_prompt, x_sample, cache_k, cache_v, cache_idx_k, page_table, state_rwkv_S, state_rwkv_shift,
           state_gdn_S, state_gdn_conv, state_mlstm_C, state_mlstm_n, state_mlstm_m, meta_tokens, ln_in_g,
           ln_in_b, w_in, rwkv_mu, rwkv_w_rkv, rwkv_w0, rwkv_w1, rwkv_w2, rwkv_a0, rwkv_a1, rwkv_a2, rwkv_g1,
           rwkv_g2, rwkv_k_k, rwkv_k_a, rwkv_r_k, rwkv_lnx_g, rwkv_lnx_b, gdn_conv_w, gdn_A_log, gdn_dt_bias,
           gdn_norm_g, idx_ln_g, idx_ln_b, mlstm_b_i, mlstm_b_f, mlstm_norm_g, w_branch, w_out, ln1_g, ln1_b,
           ln2_g, ln2_b, moe_w_group, moe_w_expert, moe_w_gate, moe_w_up, moe_w_down):
    W = dict(w_in=w_in, rwkv_mu=rwkv_mu, rwkv_w_rkv=rwkv_w_rkv, rwkv_w0=rwkv_w0, rwkv_w1=rwkv_w1,
             rwkv_w2=rwkv_w2, rwkv_a0=rwkv_a0, rwkv_a1=rwkv_a1, rwkv_a2=rwkv_a2, rwkv_g1=rwkv_g1,
             rwkv_g2=rwkv_g2, rwkv_k_k=rwkv_k_k, rwkv_k_a=rwkv_k_a, rwkv_r_k=rwkv_r_k, rwkv_lnx_g=rwkv_lnx_g,
             rwkv_lnx_b=rwkv_lnx_b, gdn_conv_w=gdn_conv_w, gdn_A_log=gdn_A_log, gdn_dt_bias=gdn_dt_bias,
             gdn_norm_g=gdn_norm_g, idx_ln_g=idx_ln_g, idx_ln_b=idx_ln_b, mlstm_b_i=mlstm_b_i,
             mlstm_b_f=mlstm_b_f, mlstm_norm_g=mlstm_norm_g, w_branch=w_branch, w_out=w_out, ln1_g=ln1_g,
             ln1_b=ln1_b, ln2_g=ln2_g, ln2_b=ln2_b, moe_w_group=moe_w_group, moe_w_expert=moe_w_expert,
             moe_w_gate=moe_w_gate, moe_w_up=moe_w_up, moe_w_down=moe_w_down)
    f32 = jnp.float32
    depth = w_in.shape[0]
    alpha = (2 * depth) ** 0.25
    B = x_prompt.shape[0]
    meta = jnp.broadcast_to(meta_tokens.astype(x_prompt.dtype)[None], (B, N_META, D_MODEL))
    hp = _layer_norm(jnp.concatenate([meta, x_prompt], axis=1), ln_in_g, ln_in_b)
    hs = _layer_norm(x_sample, ln_in_g, ln_in_b)
    zero_state = (jnp.zeros((B, RWKV_HEADS, RWKV_HEAD_DIM, RWKV_HEAD_DIM), f32),
                  jnp.zeros((B, W_MIX), f32),
                  jnp.zeros((B, GDN_HEADS, GDN_HEAD_DIM, GDN_HEAD_DIM), f32),
                  jnp.zeros((B, CONV_W - 1, 3 * W_MIX), f32),
                  jnp.zeros((B, MLSTM_HEADS, MLSTM_HEAD_DIM, MLSTM_HEAD_DIM), f32),
                  jnp.zeros((B, MLSTM_HEADS, MLSTM_HEAD_DIM), f32),
                  jnp.zeros((B, MLSTM_HEADS), f32))
    new_p, new_s = [], []
    for l in range(depth):
        hp, sp = _layer(hp, W, l, zero_state, _dsa_prompt_pallas, True, alpha)
        dsa_s = functools.partial(_dsa_sample_pallas, cache_k=cache_k, cache_v=cache_v, cache_idx_k=cache_idx_k,
                                  page_table=page_table, l=l)
        st_s = (state_rwkv_S[l], state_rwkv_shift[l], state_gdn_S[l], state_gdn_conv[l],
                state_mlstm_C[l], state_mlstm_n[l], state_mlstm_m[l])
        hs, ss = _layer(hs, W, l, st_s, dsa_s, False, alpha)
        new_p.append(sp)
        new_s.append(ss)

    def stack(rows, j):
        return jnp.stack([r[j] for r in rows])

    k_p, v_p, ik_p, rS_p, rsh_p, gS_p, gc_p, mC_p, mn_p, mm_p = (stack(new_p, j) for j in range(10))
    k_s, v_s, ik_s, rS_s, rsh_s, gS_s, gc_s, mC_s, mn_s, mm_s = (stack(new_s, j) for j in range(10))
    y_prompt = hp[:, N_META:]
    y_sample = hs
    return (y_prompt, y_sample, k_p, v_p, ik_p, k_s, v_s, ik_s, rS_p, rS_s, rsh_p, rsh_s,
            gS_p, gS_s, gc_p, gc_s, mC_p, mC_s, mn_p, mn_s, mm_p, mm_s)
```
